```python
import math
import jax
import jax.numpy as jnp
from jax import lax
import numpy as np

D_MODEL = 1024
BATCH = 8
SEQ = 4096
DEPTH = 1

CTX_LEN = 256
GRID_W = 64

MIX_W = 2 * D_MODEL
SSD_HEADDIM = 64
SSD_W = 3 * MIX_W // 4
SSD_HEADS = SSD_W // SSD_HEADDIM
SSD_GROUPS = 4
SSD_HPG = SSD_HEADS // SSD_GROUPS
SSD_STATE = 128
SSD_CONV = 5
SSD_CHUNK = 128
CONV_CH = SSD_W + 2 * SSD_GROUPS * SSD_STATE
XBCDT_COLS = CONV_CH + 2 * SSD_HEADS
POOL_W = MIX_W - SSD_W
POOL_WINDOWS = (2, 4, 8, 16)
POOL_GROUPS = len(POOL_WINDOWS)
POOL_GW = POOL_W // POOL_GROUPS
IN_COLS = XBCDT_COLS + SSD_W + POOL_W
N_EXPERTS = 32
TOP_K = 4
D_FF = D_MODEL
SWIGLU_ALPHA = 1.702
SWIGLU_LIMIT = 7.0
MOE_BLOCK = 128
RMS_EPS = 1e-6

kernel_name = "hybrid_ssd_pool_moe_prefix_dit"


def rmsnorm(h, w):
    hf = h.astype(jnp.float32)
    hf = hf * lax.rsqrt(jnp.mean(hf * hf, axis=-1, keepdims=True) + RMS_EPS)
    return (hf * w.astype(jnp.float32)).astype(h.dtype)


def modulate(h, shift, scale):
    return h * (1 + scale) + shift


def dwconv_centred(u, w, b):
    ch = u.shape[-1]
    y = lax.conv_general_dilated(
        u, w[:, None, :].astype(u.dtype), window_strides=(1,),
        padding=[(SSD_CONV // 2, SSD_CONV // 2)],
        dimension_numbers=("NWC", "WIO", "NWC"), feature_group_count=ch)
    return y + b.astype(u.dtype)


def segsum(a):
    t = a.shape[-1]
    cs = jnp.cumsum(a, axis=-1)
    diff = cs[..., :, None] - cs[..., None, :]
    return jnp.where(np.tril(np.ones((t, t), dtype=bool)), diff, -jnp.inf)


def ssd_direction_inputs(xbc, dt_raw, dt_bias, a_log, d):
    dtr = dt_raw[..., d, :]
    if d == 1:
        xbc = xbc[:, ::-1]
        dtr = dtr[:, ::-1]
    b, L, _ = xbc.shape
    gn = SSD_GROUPS * SSD_STATE
    xh = xbc[..., :SSD_W].reshape(b, L, SSD_GROUPS, SSD_HPG, SSD_HEADDIM).astype(jnp.float32)
    bm = xbc[..., SSD_W:SSD_W + gn].reshape(b, L, SSD_GROUPS, SSD_STATE).astype(jnp.float32)
    cm = xbc[..., SSD_W + gn:].reshape(b, L, SSD_GROUPS, SSD_STATE).astype(jnp.float32)
    dt = jax.nn.softplus(dtr.astype(jnp.float32) + dt_bias[d].astype(jnp.float32))
    dt = dt.reshape(b, L, SSD_GROUPS, SSD_HPG)
    log_a = -jnp.exp(a_log[d].astype(jnp.float32)).reshape(SSD_GROUPS, SSD_HPG) * dt
    return xh * dt[..., None], log_a, bm, cm


def ssd_chunked(xs, log_a, bm, cm, init_state):
    b, L, G, E, P = xs.shape
    n = bm.shape[-1]
    nc = L // SSD_CHUNK
    xs = xs.reshape(b, nc, SSD_CHUNK, G, E, P)
    bm = bm.reshape(b, nc, SSD_CHUNK, G, n)
    cm = cm.reshape(b, nc, SSD_CHUNK, G, n)
    a = jnp.moveaxis(log_a.reshape(b, nc, SSD_CHUNK, G, E), 2, -1)
    a_cs = jnp.cumsum(a, axis=-1)
    lmat = jnp.exp(segsum(a))
    cb = jnp.einsum("bclgn,bcsgn->bcgls", cm, bm)
    y_diag = jnp.einsum("bcgls,bcgels,bcsgep->bclgep", cb, lmat, xs)
    decay_states = jnp.exp(a_cs[..., -1:] - a_cs)
    states = jnp.einsum("bclgn,bcgel,bclgep->bcgepn", bm, decay_states, xs)
    states = jnp.concatenate([init_state[:, None], states], axis=1)
    chunk_end = jnp.pad(a_cs[..., -1], ((0, 0), (1, 0), (0, 0), (0, 0)))
    decay_chunk = jnp.exp(segsum(jnp.moveaxis(chunk_end, 1, -1)))
    new_states = jnp.einsum("bgezy,bygepn->bzgepn", decay_chunk, states)
    prev_states, final_state = new_states[:, :-1], new_states[:, -1]
    y_off = jnp.einsum("bclgn,bcgepn,bcgel->bclgep", cm, prev_states, jnp.exp(a_cs))
    return (y_diag + y_off).reshape(b, L, G, E, P), final_state


def ssd_final_state(xbc, dt_raw, dt_bias, a_log, d):
    xs, log_a, bm, _ = ssd_direction_inputs(xbc, dt_raw, dt_bias, a_log, d)
    a_cs = jnp.cumsum(log_a, axis=1)
    w = jnp.exp(a_cs[:, -1:] - a_cs)
    return jnp.einsum("blgn,blge,blgep->bgepn", bm, w, xs)


def ssd_bidir(xbc, dt_raw, dt_bias, a_log, d_skip, init_states):
    b, L, _ = xbc.shape
    xs_f, la_f, bm_f, cm_f = ssd_direction_inputs(xbc, dt_raw, dt_bias, a_log, 0)
    y_f, fin_f = ssd_chunked(xs_f, la_f, bm_f, cm_f, init_states[0])
    xs_b, la_b, bm_b, cm_b = ssd_direction_inputs(xbc, dt_raw, dt_bias, a_log, 1)
    y_b, fin_b = ssd_chunked(xs_b, la_b, bm_b, cm_b, init_states[1])
    x_heads = xbc[..., :SSD_W].reshape(b, L, SSD_HEADS, SSD_HEADDIM).astype(jnp.float32)
    y = (y_f + y_b[:, ::-1]).reshape(b, L, SSD_W) \
        + (x_heads * d_skip.astype(jnp.float32)[:, None]).reshape(b, L, SSD_W)
    return y.astype(xbc.dtype), fin_f, fin_b


def centred_mean_minus_self(u, window):
    wl = u.shape[1]
    cs = jnp.pad(jnp.cumsum(u.astype(jnp.float32), axis=1), ((0, 0), (1, 0), (0, 0)))
    t = np.arange(wl)
    lo = np.clip(t - window // 2, 0, wl)
    hi = np.clip(t + window - window // 2, 0, wl)
    cnt = (hi - lo).astype(np.float32)
    mean = (cs[:, hi] - cs[:, lo]) / cnt[None, :, None]
    return mean.astype(u.dtype) - u


def pool_mixer(u, pool_w, pool_scale):
    n, wl, _ = u.shape
    ug = u.reshape(n, wl, POOL_GROUPS, POOL_GW)
    pooled = jnp.stack([centred_mean_minus_self(ug[:, :, g], w)
                        for g, w in enumerate(POOL_WINDOWS)], axis=2)
    y = jnp.einsum("nwgi,gio->nwgo", pooled, pool_w)
    return y.reshape(n, wl, POOL_W) * pool_scale


def mixer_out(y_ssd, z, y_pool, ssd_norm_w, w_out):
    y_ssd = rmsnorm(y_ssd * jax.nn.silu(z), ssd_norm_w)
    return jnp.concatenate([y_ssd, y_pool], axis=-1) @ w_out


def moe_ffn(h, router_w, router_b, w1, b1, w2, b2):
    shp = h.shape
    xt = h.reshape(-1, shp[-1])
    t = xt.shape[0]
    logits = (xt @ router_w + router_b).astype(jnp.float32)
    top_val, top_idx = lax.top_k(logits, TOP_K)
    gates = jax.nn.softmax(top_val, axis=-1)
    e_flat = top_idx.reshape(-1).astype(jnp.int32)
    tok_flat = jnp.repeat(jnp.arange(t, dtype=jnp.int32), TOP_K)
    g_flat = gates.reshape(-1)
    n_assign = t * TOP_K
    order = jnp.argsort(e_flat)
    e_sorted = e_flat[order]
    counts = jnp.bincount(e_flat, length=N_EXPERTS).astype(jnp.int32)
    padded = (counts + MOE_BLOCK - 1) // MOE_BLOCK * MOE_BLOCK
    pad_end = jnp.cumsum(padded)
    pad_start = pad_end - padded
    grp_start = jnp.cumsum(counts) - counts
    rank = jnp.arange(n_assign, dtype=jnp.int32) - grp_start[e_sorted]
    dest = pad_start[e_sorted] + rank
    n_blocks = -(-n_assign // MOE_BLOCK) + N_EXPERTS
    n_slots = n_blocks * MOE_BLOCK
    slot_tok = jnp.full((n_slots,), t, jnp.int32).at[dest].set(tok_flat[order])
    slot_gate = jnp.zeros((n_slots,), jnp.float32).at[dest].set(g_flat[order])
    block_start = jnp.arange(n_blocks, dtype=jnp.int32) * MOE_BLOCK
    block_exp = jnp.minimum(jnp.searchsorted(pad_end, block_start, side="right"),
                            N_EXPERTS - 1).astype(jnp.int32)
    x_pad = jnp.concatenate([xt, jnp.zeros((1, shp[-1]), xt.dtype)], axis=0)

    def run_block(args):
        toks, e = args
        hb = x_pad[toks] @ w1[e] + b1[e]
        g_part = jnp.minimum(hb[..., ::2], SWIGLU_LIMIT)
        u_part = jnp.clip(hb[..., 1::2], -SWIGLU_LIMIT, SWIGLU_LIMIT)
        act = g_part * jax.nn.sigmoid(SWIGLU_ALPHA * g_part) * (u_part + 1)
        return act @ w2[e] + b2[e]

    y_slots = lax.map(run_block, (slot_tok.reshape(n_blocks, MOE_BLOCK), block_exp))
    y_slots = y_slots.reshape(n_slots, shp[-1]) * slot_gate[:, None].astype(xt.dtype)
    out = jnp.zeros((t + 1, shp[-1]), xt.dtype).at[slot_tok].add(y_slots)[:t]
    return out.reshape(shp)


def setup_inputs(seed: int = 0) -> dict:
    key = jax.random.key(seed)
    ks = jax.random.split(key, 25)
    f32 = jnp.float32

    def nrm(k, shape, s):
        return jax.random.normal(k, shape, f32) * s

    dt0 = jnp.exp(jax.random.uniform(ks[10], (DEPTH, 2, SSD_HEADS), f32,
                                     math.log(1e-3), math.log(1e-1)))
    return {
        "x": nrm(ks[0], (BATCH, SEQ, D_MODEL), 1.0),
        "c": nrm(ks[1], (BATCH, D_MODEL), 1.0),
        "ctx": nrm(ks[2], (BATCH, CTX_LEN, D_MODEL), 1.0),
        "c_ctx": nrm(ks[3], (D_MODEL,), 1.0),
        "w_mod": nrm(ks[4], (DEPTH, D_MODEL, 6 * D_MODEL), 0.5 * D_MODEL ** -0.5),
        "b_mod": nrm(ks[5], (DEPTH, 6 * D_MODEL), 0.01),
        "norm1_w": 1.0 + nrm(ks[6], (DEPTH, D_MODEL), 0.02),
        "norm2_w": 1.0 + nrm(ks[7], (DEPTH, D_MODEL), 0.02),
        "w_in": nrm(ks[8], (DEPTH, D_MODEL, IN_COLS), D_MODEL ** -0.5),
        "conv_w": nrm(ks[9], (DEPTH, SSD_CONV, CONV_CH), SSD_CONV ** -0.5),
        "conv_b": nrm(ks[11], (DEPTH, CONV_CH), 0.01),
        "dt_bias": dt0 + jnp.log(-jnp.expm1(-dt0)),
        "a_log": jnp.log(jax.random.uniform(ks[12], (DEPTH, 2, SSD_HEADS), f32, 1.0, 16.0)),
        "d_skip": 1.0 + nrm(ks[13], (DEPTH, SSD_HEADS), 0.1),
        "ssd_norm_w": 1.0 + nrm(ks[14], (DEPTH, SSD_W), 0.02),
        "pool_w": nrm(ks[15], (DEPTH, POOL_GROUPS, POOL_GW, POOL_GW), POOL_GW ** -0.5),
        "pool_scale": 1.0 + nrm(ks[16], (DEPTH, POOL_W), 0.1),
        "w_out": nrm(ks[17], (DEPTH, MIX_W, D_MODEL), MIX_W ** -0.5),
        "router_w": nrm(ks[18], (DEPTH, D_MODEL, N_EXPERTS), D_MODEL ** -0.5),
        "router_b": nrm(ks[19], (DEPTH, N_EXPERTS), 0.01),
        "w1": nrm(ks[20], (DEPTH, N_EXPERTS, D_MODEL, 2 * D_FF), D_MODEL ** -0.5),
        "b1": nrm(ks[21], (DEPTH, N_EXPERTS, 2 * D_FF), 0.01),
        "w2": nrm(ks[22], (DEPTH, N_EXPERTS, D_FF, D_MODEL), D_FF ** -0.5),
        "b2": nrm(ks[23], (DEPTH, N_EXPERTS, D_MODEL), 0.01),
        "final_norm_w": 1.0 + nrm(ks[24], (D_MODEL,), 0.02),
    }


def reference(x, c, ctx, c_ctx, w_mod, b_mod, norm1_w, norm2_w, w_in, conv_w, conv_b,
              dt_bias, a_log, d_skip, ssd_norm_w, pool_w, pool_scale, w_out,
              router_w, router_b, w1, b1, w2, b2, final_norm_w):
    b, L, _ = x.shape
    rows = L // GRID_W
    zero_state = jnp.zeros((b, SSD_GROUPS, SSD_HPG, SSD_HEADDIM, SSD_STATE), jnp.float32)
    for l in range(DEPTH):
        last = l == DEPTH - 1
        mod = (jax.nn.silu(c) @ w_mod[l] + b_mod[l])[:, None, :]
        mod_c = jax.nn.silu(c_ctx) @ w_mod[l] + b_mod[l]
        sh1, sc1, g1, sh2, sc2, g2 = jnp.split(mod, 6, axis=-1)
        csh1, csc1, cg1, csh2, csc2, cg2 = jnp.split(mod_c, 6, axis=-1)

        hc = modulate(rmsnorm(ctx, norm1_w[l]), csh1, csc1)
        pc = hc @ w_in[l][:, :(XBCDT_COLS if last else IN_COLS)]
        xbc_c = jax.nn.silu(dwconv_centred(pc[..., :CONV_CH], conv_w[l], conv_b[l]))
        dt_c = pc[..., CONV_CH:XBCDT_COLS].reshape(b, -1, 2, SSD_HEADS)
        if last:
            init_states = (ssd_final_state(xbc_c, dt_c, dt_bias[l], a_log[l], 0),
                           ssd_final_state(xbc_c, dt_c, dt_bias[l], a_log[l], 1))
            ctx_next = ctx
        else:
            y_c, fin_cf, fin_cb = ssd_bidir(xbc_c, dt_c, dt_bias[l], a_log[l], d_skip[l],
                                            (zero_state, zero_state))
            init_states = (fin_cf, fin_cb)
            yp_c = pool_mixer(pc[..., XBCDT_COLS + SSD_W:], pool_w[l], pool_scale[l])
            ctx_next = ctx + cg1 * mixer_out(y_c, pc[..., XBCDT_COLS:XBCDT_COLS + SSD_W],
                                             yp_c, ssd_norm_w[l], w_out[l])
            ctx_next = ctx_next + cg2 * moe_ffn(
                modulate(rmsnorm(ctx_next, norm2_w[l]), csh2, csc2),
                router_w[l], router_b[l], w1[l], b1[l], w2[l], b2[l])

        hx = modulate(rmsnorm(x, norm1_w[l]), sh1, sc1)
        p = hx @ w_in[l]
        xbc = jax.nn.silu(dwconv_centred(p[..., :CONV_CH], conv_w[l], conv_b[l]))
        dt_x = p[..., CONV_CH:XBCDT_COLS].reshape(b, L, 2, SSD_HEADS)
        y_x, _, _ = ssd_bidir(xbc, dt_x, dt_bias[l], a_log[l], d_skip[l], init_states)
        u_pool = p[..., XBCDT_COLS + SSD_W:].reshape(b * rows, GRID_W, POOL_W)
        y_pool = pool_mixer(u_pool, pool_w[l], pool_scale[l]).reshape(b, L, POOL_W)
        x = x + g1 * mixer_out(y_x, p[..., XBCDT_COLS:XBCDT_COLS + SSD_W], y_pool,
                               ssd_norm_w[l], w_out[l])
        x = x + g2 * moe_ffn(modulate(rmsnorm(x, norm2_w[l]), sh2, sc2),
                             router_w[l], router_b[l], w1[l], b1[l], w2[l], b2[l])
        ctx = ctx_next
    return rmsnorm(x, final_norm_w)
```

```python
import functools

import numpy as np
import jax
import jax.numpy as jnp
from jax import lax
from jax.experimental import pallas as pl
from jax.experimental.pallas import tpu as pltpu

F32 = jnp.float32
BF16 = jnp.bfloat16

SSD_HEADDIM = 64
SSD_GROUPS = 4
SSD_HPG = 6
SSD_HEADS = SSD_GROUPS * SSD_HPG
SSD_STATE = 128
SSD_CONV = 5
SSD_CHUNK = 128
SSD_W = SSD_HEADS * SSD_HEADDIM
GROUP_W = SSD_HPG * SSD_HEADDIM
CONV_CH = SSD_W + 2 * SSD_GROUPS * SSD_STATE
POOL_WINDOWS = (2, 4, 8, 16)
POOL_GW = 128
POOL_W = POOL_GW * len(POOL_WINDOWS)
GRID_W = 64
N_EXPERTS = 32
TOP_K = 4
SWIGLU_ALPHA = 1.702
SWIGLU_LIMIT = 7.0
RMS_EPS = 1e-6

LANES = 128
SUBLANES = 8
VMEM_LIMIT_BYTES = 56 * 1024 * 1024

MOD_TN = 1024
PROJ_TM = 512
MIX_TM = 512
POOL_TM = 256
MOE_BLK = 256
ROW_TM = 256
NEG_BIG = -1e30


def _sigmoid(x):
    return 1.0 / (1.0 + jnp.exp(-x))


def _split2(a):
    hi = a.astype(BF16)
    mid = (a - hi.astype(F32)).astype(BF16)
    return hi, mid


def _split3(a):
    hi = a.astype(BF16)
    r = a - hi.astype(F32)
    mid = r.astype(BF16)
    lo = (r - mid.astype(F32)).astype(BF16)
    return hi, mid, lo


def _dot(a, b):
    return jnp.dot(a, b, preferred_element_type=F32)


def _dot_exact_rhs(a_f32, b_bf16, parts):
    pieces = _split3(a_f32) if parts == 3 else _split2(a_f32)
    out = _dot(pieces[0], b_bf16)
    for p in pieces[1:]:
        out = out + _dot(p, b_bf16)
    return out


def _dot_hi(a_f32, b_f32):
    a0, a1, a2 = _split3(a_f32)
    b0, b1, b2 = _split3(b_f32)
    out = _dot(a0, b0)
    out = out + _dot(a0, b1) + _dot(a1, b0)
    out = out + _dot(a1, b1) + _dot(a0, b2) + _dot(a2, b0)
    return out


def _cparams(sem):
    return pltpu.CompilerParams(dimension_semantics=sem, vmem_limit_bytes=VMEM_LIMIT_BYTES)


def _mod_kernel(c_ref, w_ref, b_ref, o_ref):
    c = c_ref[...]
    s = c * _sigmoid(c)
    o_ref[...] = _dot_hi(s, w_ref[...]) + b_ref[...]


def _modulation(cc, w_mod, b_mod):
    rows, d = cc.shape
    n = w_mod.shape[1]
    return pl.pallas_call(
        _mod_kernel,
        grid=(n // MOD_TN,),
        in_specs=[pl.BlockSpec((rows, d), lambda j: (0, 0)),
                  pl.BlockSpec((d, MOD_TN), lambda j: (0, j)),
                  pl.BlockSpec((1, MOD_TN), lambda j: (0, j))],
        out_specs=pl.BlockSpec((rows, MOD_TN), lambda j: (0, j)),
        out_shape=jax.ShapeDtypeStruct((rows, n), F32),
        compiler_params=_cparams(("arbitrary",)),
    )(cc, w_mod, b_mod.reshape(1, n))


def _inproj_kernel(x_ref, nw_ref, sh_ref, sc_ref, wx_ref, wd_ref, wz_ref, wp_ref,
                   xbc_ref, dt_ref, z_ref, pool_ref):
    x = x_ref[0]
    ms = jnp.mean(x * x, axis=-1, keepdims=True)
    h = x * lax.rsqrt(ms + RMS_EPS) * nw_ref[...]
    h = h * (1.0 + sc_ref[0]) + sh_ref[0]
    hb = h.astype(BF16)
    xbc_ref[0] = _dot(hb, wx_ref[...])
    dt_ref[0] = _dot(hb, wd_ref[...])
    z_ref[0] = _dot(hb, wz_ref[...])
    pool_ref[0] = _dot(hb, wp_ref[...])


def _inproj(x, norm_w, shift, scale, wx, wd, wz, wp, tm):
    b, l, d = x.shape
    full = lambda a: pl.BlockSpec(a.shape, lambda i, j: (0, 0))
    tok = lambda n: pl.BlockSpec((1, tm, n), lambda i, j: (i, j, 0))
    per_b = pl.BlockSpec((1, 1, d), lambda i, j: (i, 0, 0))
    return pl.pallas_call(
        _inproj_kernel,
        grid=(b, l // tm),
        in_specs=[tok(d), full(norm_w), per_b, per_b, full(wx), full(wd), full(wz), full(wp)],
        out_specs=[tok(CONV_CH), tok(2 * LANES), tok(SSD_W), tok(POOL_W)],
        out_shape=[jax.ShapeDtypeStruct((b, l, CONV_CH), F32),
                   jax.ShapeDtypeStruct((b, l, 2 * LANES), F32),
                   jax.ShapeDtypeStruct((b, l, SSD_W), F32),
                   jax.ShapeDtypeStruct((b, l, POOL_W), F32)],
        compiler_params=_cparams(("arbitrary", "arbitrary")),
    )(x, norm_w, shift, scale, wx, wd, wz, wp)


def _ssd_kernel(nc, xbc_ref, prev_ref, next_ref, dt_ref, z_ref, cw_ref, cb_ref, dtb_ref, alog_ref,
                dsk_ref, nw_ref, exp_ref, init_ref, y_ref, fin_ref,
                ext_ref, act_ref, ybuf_ref, yf_ref, st_ref):
    ch = SSD_CHUNK
    ph = pl.program_id(1)
    c = pl.program_id(2)
    ci = c + ph * (nc - 1 - 2 * c)

    @pl.when(c == 0)
    def _():
        st_ref[...] = init_ref[0, ph]

    ext_ref[0:SUBLANES] = jnp.where(ci > 0, prev_ref[0], 0.0)
    ext_ref[SUBLANES:SUBLANES + ch] = xbc_ref[0]
    ext_ref[SUBLANES + ch:2 * SUBLANES + ch] = jnp.where(ci < nc - 1, next_ref[0], 0.0)
    base = SUBLANES - SSD_CONV // 2
    for j in range(CONV_CH // LANES):
        cols = slice(j * LANES, (j + 1) * LANES)
        acc = cb_ref[:, cols] + cw_ref[0:1, cols] * ext_ref[base:base + ch, cols]
        for k in range(1, SSD_CONV):
            acc = acc + cw_ref[k:k + 1, cols] * ext_ref[base + k:base + k + ch, cols]
        act_ref[:, cols] = acc * _sigmoid(acc)

    dtr = dt_ref[0] + dtb_ref[0]
    dtv = jnp.maximum(dtr, 0.0) + jnp.log1p(jnp.exp(-jnp.abs(dtr)))
    a = dtv * (-jnp.exp(alog_ref[0]))
    row = lax.broadcasted_iota(jnp.int32, (ch, ch), 0)
    col = lax.broadcasted_iota(jnp.int32, (ch, ch), 1)
    tmask = (row - col) * (1 - 2 * ph) >= 0
    tri = jnp.where(tmask, 1.0, 0.0).astype(BF16)
    cs = _dot_exact_rhs_left(tri, a)
    tot = jnp.where(ph == 0, cs[ch - 1:ch, :], cs[0:1, :])
    cs_t = cs.T
    e_cs = jnp.exp(cs)
    e_dec = jnp.exp(tot - cs)
    e_tot = jnp.exp(tot)
    expand = exp_ref[...]
    dt_x = _dot_exact_rhs(dtv, expand, 2)
    ecs_x = _dot_exact_rhs(e_cs, expand, 2)
    edec_x = _dot_exact_rhs(e_dec, expand, 2)
    etot_x = _dot_exact_rhs(jnp.broadcast_to(e_tot, (SUBLANES, LANES)), expand, 2)[0:1]

    lane = lax.broadcasted_iota(jnp.int32, (ch, LANES), 1)
    for g in range(SSD_GROUPS):
        gs = slice(g * GROUP_W, (g + 1) * GROUP_W)
        b_bf = act_ref[:, SSD_W + g * SSD_STATE:SSD_W + (g + 1) * SSD_STATE].astype(BF16)
        c_bf = act_ref[:, SSD_W + (SSD_GROUPS + g) * SSD_STATE:
                       SSD_W + (SSD_GROUPS + g + 1) * SSD_STATE].astype(BF16)
        cb = lax.dot_general(c_bf, b_bf, (((1,), (1,)), ((), ())), preferred_element_type=F32)
        xs = act_ref[:, gs] * dt_x[:, gs]
        xs_bf = xs.astype(BF16)
        s_prev = st_ref[g]
        y_off = _dot(c_bf, s_prev.astype(BF16)) * ecs_x[:, gs]
        x_dec = (xs * edec_x[:, gs]).astype(BF16)
        st_ref[g] = s_prev * etot_x[:, gs] + lax.dot_general(
            b_bf, x_dec, (((0,), (0,)), ((), ())), preferred_element_type=F32)
        for q in range(SSD_HPG // 2):
            lmats = []
            for h in (g * SSD_HPG + 2 * q, g * SSD_HPG + 2 * q + 1):
                diff = cs[:, h:h + 1] - cs_t[h:h + 1, :]
                dec = jnp.exp(jnp.where(tmask, diff, NEG_BIG))
                lmats.append((dec * cb).astype(BF16))
            xp = xs_bf[:, q * LANES:(q + 1) * LANES]
            zero = jnp.zeros_like(xp)
            rhs = jnp.concatenate([jnp.where(lane < SSD_HEADDIM, xp, zero),
                                   jnp.where(lane >= SSD_HEADDIM, xp, zero)], axis=0)
            y_diag = _dot(jnp.concatenate(lmats, axis=1), rhs)
            ps = slice(g * GROUP_W + q * LANES, g * GROUP_W + (q + 1) * LANES)
            ybuf_ref[:, ps] = y_diag + y_off[:, q * LANES:(q + 1) * LANES]

    @pl.when(ph == 0)
    def _():
        yf_ref[ci] = ybuf_ref[...]

    @pl.when(ph == 1)
    def _():
        yt = yf_ref[ci] + ybuf_ref[...] + act_ref[:, 0:SSD_W] * dsk_ref[...]
        zz = z_ref[0]
        gt = yt * (zz * _sigmoid(zz))
        ms = jnp.mean(gt * gt, axis=-1, keepdims=True)
        y_ref[0] = (gt * lax.rsqrt(ms + RMS_EPS) * nw_ref[...]).astype(y_ref.dtype)

    @pl.when(c == nc - 1)
    def _():
        fin_ref[0, ph] = st_ref[...]


def _dot_exact_rhs_left(sel_bf16, a_f32):
    hi, mid, lo = _split3(a_f32)
    return _dot(sel_bf16, hi) + _dot(sel_bf16, mid) + _dot(sel_bf16, lo)


def _ssd(xbc, dt, z, conv_w8, conv_b, dt_bias, a_log, d_skip_x, norm_w, expand, init):
    b, l, _ = xbc.shape
    ch = SSD_CHUNK
    nc = l // ch
    rows8 = ch // SUBLANES

    def cidx(ph, c):
        return c + ph * (nc - 1 - 2 * c)

    def out_idx(ph, c):
        return jnp.where(ph == 0, nc - 1, nc - 1 - c)

    full2 = lambda a: pl.BlockSpec(a.shape, lambda i, ph, c: (0, 0))
    st_spec = pl.BlockSpec((1, 2, SSD_GROUPS, SSD_STATE, GROUP_W), lambda i, ph, c: (i, 0, 0, 0, 0))
    in_specs = [
        pl.BlockSpec((1, ch, CONV_CH), lambda i, ph, c: (i, cidx(ph, c), 0)),
        pl.BlockSpec((1, SUBLANES, CONV_CH),
                     lambda i, ph, c: (i, jnp.maximum(cidx(ph, c) * rows8 - 1, 0), 0)),
        pl.BlockSpec((1, SUBLANES, CONV_CH),
                     lambda i, ph, c: (i, jnp.minimum((cidx(ph, c) + 1) * rows8, l // SUBLANES - 1), 0)),
        pl.BlockSpec((1, ch, LANES), lambda i, ph, c: (i, cidx(ph, c), ph)),
        pl.BlockSpec((1, ch, SSD_W), lambda i, ph, c: (i, out_idx(ph, c), 0)),
        full2(conv_w8), full2(conv_b),
        pl.BlockSpec((1, 1, LANES), lambda i, ph, c: (ph, 0, 0)),
        pl.BlockSpec((1, 1, LANES), lambda i, ph, c: (ph, 0, 0)),
        full2(d_skip_x), full2(norm_w), full2(expand), st_spec,
    ]
    out_specs = [pl.BlockSpec((1, ch, SSD_W), lambda i, ph, c: (i, out_idx(ph, c), 0)), st_spec]
    return pl.pallas_call(
        functools.partial(_ssd_kernel, nc),
        grid=(b, 2, nc),
        in_specs=in_specs,
        out_specs=out_specs,
        out_shape=[jax.ShapeDtypeStruct((b, l, SSD_W), BF16),
                   jax.ShapeDtypeStruct((b, 2, SSD_GROUPS, SSD_STATE, GROUP_W), F32)],
        scratch_shapes=[pltpu.VMEM((ch + 2 * SUBLANES, CONV_CH), F32),
                        pltpu.VMEM((ch, CONV_CH), F32),
                        pltpu.VMEM((ch, SSD_W), F32),
                        pltpu.VMEM((nc, ch, SSD_W), F32),
                        pltpu.VMEM((SSD_GROUPS, SSD_STATE, GROUP_W), F32)],
        compiler_params=_cparams(("arbitrary", "arbitrary", "arbitrary")),
    )(xbc, xbc, xbc, dt, z, conv_w8, conv_b, dt_bias, a_log, d_skip_x, norm_w, expand, init)


def _mix_kernel(y_ref, u_ref, x_ref, pa_ref, pcnt_ref, pw_ref, psc_ref, wo_ref, g1_ref,
                nw_ref, sh_ref, sc_ref, rw_ref, rb_ref, tri_ref,
                x1_ref, h_ref, rt_ref, cnt_ref, base_ref):
    tm = x_ref.shape[1]
    first = (pl.program_id(0) == 0) & (pl.program_id(1) == 0)

    @pl.when(first)
    def _():
        base_ref[...] = jnp.zeros_like(base_ref)

    pooled = []
    for g in range(len(POOL_WINDOWS)):
        parts = []
        for r in range(tm // POOL_TM):
            u = u_ref[0, r * POOL_TM:(r + 1) * POOL_TM, g * POOL_GW:(g + 1) * POOL_GW]
            wsum = _dot_exact_rhs_left(pa_ref[g], u)
            parts.append((wsum / pcnt_ref[g] - u).astype(BF16))
        pg = jnp.concatenate(parts, axis=0)
        pooled.append((_dot(pg, pw_ref[g]) * psc_ref[:, g * POOL_GW:(g + 1) * POOL_GW]).astype(BF16))
    y_pool = jnp.concatenate(pooled, axis=1)

    mix = _dot(y_ref[0], wo_ref[0:SSD_W, :]) + _dot(y_pool, wo_ref[SSD_W:SSD_W + POOL_W, :])
    x1 = x_ref[0] + g1_ref[0] * mix
    x1_ref[0] = x1

    ms = jnp.mean(x1 * x1, axis=-1, keepdims=True)
    h = x1 * lax.rsqrt(ms + RMS_EPS) * nw_ref[...]
    h = h * (1.0 + sc_ref[0]) + sh_ref[0]
    h_ref[0] = h

    h0, h1 = _split2(h)
    r0, r1 = _split2(rw_ref[...])
    logits = _dot(h0, r0) + _dot(h0, r1) + _dot(h1, r0) + rb_ref[...]
    lane = lax.broadcasted_iota(jnp.int32, (tm, LANES), 1)
    lane_f = lane.astype(F32)
    work = jnp.where(lane < N_EXPERTS, logits, NEG_BIG)
    vals, hots = [], []
    for _ in range(TOP_K):
        m = jnp.max(work, axis=-1, keepdims=True)
        first_idx = jnp.min(jnp.where(work == m, lane_f, float(LANES)), axis=-1, keepdims=True)
        hot = lane_f == first_idx
        vals.append(m)
        hots.append(hot)
        work = jnp.where(hot, 2.0 * NEG_BIG, work)
    exps = [jnp.exp(v - vals[0]) for v in vals]
    denom = exps[0] + exps[1] + exps[2] + exps[3]

    onehot = jnp.zeros((tm, LANES), F32)
    for hot in hots:
        onehot = onehot + jnp.where(hot, 1.0, 0.0)
    before = _dot(tri_ref[...], onehot.astype(BF16)) + base_ref[0:1, :]
    packed = jnp.zeros((tm, LANES), F32)
    for k in range(TOP_K):
        idx_k = jnp.sum(jnp.where(hots[k], lane_f, 0.0), axis=-1, keepdims=True)
        rank_k = jnp.sum(jnp.where(hots[k], before, 0.0), axis=-1, keepdims=True)
        packed = jnp.where(lane == k, idx_k, packed)
        packed = jnp.where(lane == TOP_K + k, rank_k, packed)
        packed = jnp.where(lane == 2 * TOP_K + k, exps[k] / denom, packed)
    rt_ref[0] = packed
    new_base = base_ref[0:1, :] + jnp.sum(onehot, axis=0, keepdims=True)
    base_ref[...] = jnp.broadcast_to(new_base, base_ref.shape)
    cnt_ref[...] = jnp.broadcast_to(new_base, cnt_ref.shape)


def _mix(y, u, x, pool_a, pool_cnt, pool_w, pool_scale, w_out, g1, norm_w, shift, scale,
         router_w, router_b, tri):
    b, l, d = x.shape
    tm = MIX_TM
    tok = lambda n: pl.BlockSpec((1, tm, n), lambda i, j: (i, j, 0))
    per_b = pl.BlockSpec((1, 1, d), lambda i, j: (i, 0, 0))
    full = lambda a: pl.BlockSpec(a.shape, lambda i, j: (0,) * a.ndim)
    cnt_spec = pl.BlockSpec((SUBLANES, LANES), lambda i, j: (0, 0))
    return pl.pallas_call(
        _mix_kernel,
        grid=(b, l // tm),
        in_specs=[tok(SSD_W), tok(POOL_W), tok(d), full(pool_a), full(pool_cnt), full(pool_w),
                  full(pool_scale), full(w_out), per_b, full(norm_w), per_b, per_b,
                  full(router_w), full(router_b), full(tri)],
        out_specs=[tok(d), tok(d), tok(LANES), cnt_spec],
        out_shape=[jax.ShapeDtypeStruct((b, l, d), F32),
                   jax.ShapeDtypeStruct((b, l, d), F32),
                   jax.ShapeDtypeStruct((b, l, LANES), F32),
                   jax.ShapeDtypeStruct((SUBLANES, LANES), F32)],
        scratch_shapes=[pltpu.VMEM((SUBLANES, LANES), F32)],
        compiler_params=_cparams(("arbitrary", "arbitrary")),
    )(y, u, x, pool_a, pool_cnt, pool_w, pool_scale, w_out, g1, norm_w, shift, scale,
      router_w, router_b, tri)


def _dispatch_kernel(fill_start_ref, fill_len_ref, dest_ref, h_ref, xs_ref, zero_ref, sem, fill_sem):
    tm = h_ref.shape[0]
    n_slots = xs_ref.shape[0]

    def body(t, carry):
        for k in range(TOP_K):
            d = dest_ref[t * TOP_K + k]
            pltpu.make_async_copy(h_ref.at[pl.ds(t, 1)], xs_ref.at[pl.ds(d, 1)], sem).start()
        return carry

    lax.fori_loop(0, tm, body, 0)

    @pl.when(pl.program_id(0) == pl.num_programs(0) - 1)
    def _():
        zero_ref[...] = jnp.zeros_like(zero_ref)
        bits = [1 << s for s in range(MOE_BLK.bit_length() - 2, SUBLANES.bit_length() - 2, -1)]

        def pad_copies(e, wait):
            start = fill_start_ref[e]
            n = fill_len_ref[e]
            head = jnp.minimum((-start) & (SUBLANES - 1), n)
            for r in range(SUBLANES - 1):
                @pl.when(r < head)
                def _():
                    cp = pltpu.make_async_copy(zero_ref.at[pl.ds(0, 1)],
                                               xs_ref.at[pl.ds(start + r, 1)], fill_sem)
                    cp.wait() if wait else cp.start()
            start = start + head
            n = n - head
            for bit in bits:
                @pl.when((n & bit) != 0)
                def _():
                    off = pl.multiple_of(start + (n & ~(2 * bit - 1)), SUBLANES)
                    cp = pltpu.make_async_copy(zero_ref.at[pl.ds(0, bit)],
                                               xs_ref.at[pl.ds(off, bit)], fill_sem)
                    cp.wait() if wait else cp.start()

        def tail_copy(j, wait):
            off = pl.multiple_of(fill_start_ref[N_EXPERTS] + j * MOE_BLK, MOE_BLK)
            cp = pltpu.make_async_copy(zero_ref, xs_ref.at[pl.ds(off, MOE_BLK)], fill_sem)
            cp.wait() if wait else cp.start()

        n_tail = (n_slots - fill_start_ref[N_EXPERTS]) // MOE_BLK
        for wait in (False, True):
            lax.fori_loop(0, N_EXPERTS, lambda e, cr: (pad_copies(e, wait), cr)[1], 0)
            lax.fori_loop(0, n_tail, lambda j, cr: (tail_copy(j, wait), cr)[1], 0)

    for _ in range(TOP_K):
        pltpu.make_async_copy(h_ref, xs_ref.at[pl.ds(0, tm)], sem).wait()


def _dispatch(fill_start, fill_len, dest_flat, h, n_slots):
    t, d = h.shape
    tm = ROW_TM
    grid_spec = pltpu.PrefetchScalarGridSpec(
        num_scalar_prefetch=2,
        grid=(t // tm,),
        in_specs=[pl.BlockSpec((tm * TOP_K,), lambda i, fs, fl: (i,), memory_space=pltpu.SMEM),
                  pl.BlockSpec((tm, d), lambda i, fs, fl: (i, 0))],
        out_specs=pl.BlockSpec(memory_space=pl.ANY),
        scratch_shapes=[pltpu.VMEM((MOE_BLK, d), F32),
                        pltpu.SemaphoreType.DMA(()), pltpu.SemaphoreType.DMA(())],
    )
    return pl.pallas_call(
        _dispatch_kernel,
        grid_spec=grid_spec,
        out_shape=jax.ShapeDtypeStruct((n_slots, d), F32),
        compiler_params=_cparams(("arbitrary",)),
    )(fill_start, fill_len, dest_flat, h)


def _expert_kernel(be_ref, nused_ref, xs_ref, w1_ref, b1_ref, w2_ref, b2_ref, perm_ref, y_ref,
                   w1s_ref, w2s_ref):
    i = pl.program_id(0)
    prev = be_ref[jnp.maximum(i - 1, 0)]
    changed = (i == 0) | (be_ref[i] != prev)
    dff2 = w1_ref.shape[2]
    tile = 2 * LANES

    @pl.when(changed & (i < nused_ref[0]))
    def _():
        for j in range(dff2 // tile):
            wj = w1_ref[0, :, j * tile:(j + 1) * tile].astype(BF16)
            w1s_ref[:, j * tile:(j + 1) * tile] = _dot(wj, perm_ref[...]).astype(BF16)
        w2s_ref[...] = w2_ref[0].astype(BF16)

    @pl.when(i < nused_ref[0])
    def _():
        x = xs_ref[...].astype(BF16)
        acts = []
        for j in range(dff2 // tile):
            hb = _dot(x, w1s_ref[:, j * tile:(j + 1) * tile]) + b1_ref[0, :, j * tile:(j + 1) * tile]
            gp = jnp.minimum(hb[:, 0:LANES], SWIGLU_LIMIT)
            up = jnp.clip(hb[:, LANES:tile], -SWIGLU_LIMIT, SWIGLU_LIMIT)
            acts.append((gp * _sigmoid(SWIGLU_ALPHA * gp) * (up + 1.0)).astype(BF16))
        act = jnp.concatenate(acts, axis=1)
        y_ref[...] = _dot(act, w2s_ref[...]) + b2_ref[0]

    @pl.when(i >= nused_ref[0])
    def _():
        y_ref[...] = jnp.zeros_like(y_ref)


def _experts(block_exp, n_used, xs, w1, b1p, w2, b2, perm, n_blocks):
    blk = MOE_BLK
    d = xs.shape[1]
    dff2 = w1.shape[2]
    dff = w2.shape[1]

    def x_idx(i, be, nu):
        return (jnp.minimum(i, nu[0] - 1), 0)


    grid_spec = pltpu.PrefetchScalarGridSpec(
        num_scalar_prefetch=2,
        grid=(n_blocks,),
        in_specs=[pl.BlockSpec((blk, d), x_idx),
                  pl.BlockSpec((1, d, dff2), lambda i, be, nu: (be[i], 0, 0)),
                  pl.BlockSpec((1, 1, dff2), lambda i, be, nu: (be[i], 0, 0)),
                  pl.BlockSpec((1, dff, d), lambda i, be, nu: (be[i], 0, 0)),
                  pl.BlockSpec((1, 1, d), lambda i, be, nu: (be[i], 0, 0)),
                  pl.BlockSpec(perm.shape, lambda i, be, nu: (0, 0))],
        out_specs=pl.BlockSpec((blk, d), lambda i, be, nu: (i, 0)),
        scratch_shapes=[pltpu.VMEM((d, dff2), BF16), pltpu.VMEM((dff, d), BF16)],
    )
    return pl.pallas_call(
        _expert_kernel,
        grid_spec=grid_spec,
        out_shape=jax.ShapeDtypeStruct((n_blocks * blk, d), F32),
        compiler_params=_cparams(("arbitrary",)),
    )(block_exp, n_used, xs, w1, b1p, w2, b2, perm)


def _combine_kernel(dest_ref, x1_ref, rt_ref, g2_ref, fnw_ref, ys_ref, o_ref, buf_ref, sem):
    tm = x1_ref.shape[1]

    def body(t, carry):
        for k in range(TOP_K):
            d = dest_ref[t * TOP_K + k]
            pltpu.make_async_copy(ys_ref.at[pl.ds(d, 1)], buf_ref.at[k, pl.ds(t, 1)], sem).start()
        return carry

    lax.fori_loop(0, tm, body, 0)
    for k in range(TOP_K):
        pltpu.make_async_copy(ys_ref.at[pl.ds(0, tm)], buf_ref.at[k], sem).wait()
    rt = rt_ref[0]
    moe = buf_ref[0] * rt[:, 2 * TOP_K:2 * TOP_K + 1]
    for k in range(1, TOP_K):
        moe = moe + buf_ref[k] * rt[:, 2 * TOP_K + k:2 * TOP_K + k + 1]
    x2 = x1_ref[0] + g2_ref[0] * moe
    ms = jnp.mean(x2 * x2, axis=-1, keepdims=True)
    o_ref[0] = x2 * lax.rsqrt(ms + RMS_EPS) * fnw_ref[...]


def _combine(dest_flat, x1, rt, g2, fnw, ys):
    b, l, d = x1.shape
    tm = ROW_TM
    per_l = l // tm
    tok = lambda n: pl.BlockSpec((1, tm, n), lambda i, j: (i, j, 0))
    return pl.pallas_call(
        _combine_kernel,
        grid=(b, per_l),
        in_specs=[pl.BlockSpec((tm * TOP_K,), lambda i, j: (i * per_l + j,), memory_space=pltpu.SMEM),
                  tok(d), tok(LANES),
                  pl.BlockSpec((1, 1, d), lambda i, j: (i, 0, 0)),
                  pl.BlockSpec(fnw.shape, lambda i, j: (0, 0)),
                  pl.BlockSpec(memory_space=pl.ANY)],
        out_specs=tok(d),
        out_shape=jax.ShapeDtypeStruct((b, l, d), F32),
        scratch_shapes=[pltpu.VMEM((TOP_K, tm, d), F32), pltpu.SemaphoreType.DMA(())],
        compiler_params=_cparams(("arbitrary", "arbitrary")),
    )(dest_flat, x1, rt, g2, fnw, ys)


def _pool_constants():
    i = np.arange(POOL_TM)[:, None]
    j = np.arange(POOL_TM)[None, :]
    same_row = (i // GRID_W) == (j // GRID_W)
    mats, cnts = [], []
    for w in POOL_WINDOWS:
        band = same_row & (j - i >= -(w // 2)) & (j - i < w - w // 2)
        mats.append(band)
        cnts.append(np.broadcast_to(band.sum(axis=1, keepdims=True), (POOL_TM, POOL_GW)))
    return (jnp.asarray(np.stack(mats), BF16), jnp.asarray(np.stack(cnts), F32))


def _head_expand():
    e = np.zeros((LANES, SSD_W), np.float32)
    for h in range(SSD_HEADS):
        e[h, h * SSD_HEADDIM:(h + 1) * SSD_HEADDIM] = 1.0
    return jnp.asarray(e, BF16)


def _deinterleave_perm():
    n = 2 * LANES
    p = np.zeros((n, n), np.float32)
    for k in range(LANES):
        p[2 * k, k] = 1.0
        p[2 * k + 1, LANES + k] = 1.0
    return jnp.asarray(p, BF16)


def _strict_lower(n):
    return jnp.asarray(np.tril(np.ones((n, n), np.float32), -1), BF16)


def _pad_lanes(a, n):
    return jnp.pad(a, [(0, 0)] * (a.ndim - 1) + [(0, n - a.shape[-1])])


def kernel(x, c, ctx, c_ctx, w_mod, b_mod, norm1_w, norm2_w, w_in, conv_w, conv_b, dt_bias, a_log,
           d_skip, ssd_norm_w, pool_w, pool_scale, w_out, router_w, router_b, w1, b1, w2, b2,
           final_norm_w):
    depth = w_mod.shape[0]
    assert depth == 1, "single-layer problem"
    b, l, d = x.shape
    lc = ctx.shape[1]
    xbcdt = CONV_CH + 2 * SSD_HEADS

    mod_rows = 2 * SUBLANES
    cc = jnp.zeros((mod_rows, d), F32).at[0:b].set(c).at[b].set(c_ctx)
    mod = _modulation(cc, w_mod[0], b_mod[0])
    sh1, sc1, g1, sh2, sc2, g2 = [m.reshape(b, 1, d) for m in jnp.split(mod[0:b], 6, axis=-1)]
    csh1, csc1 = [jnp.broadcast_to(m.reshape(1, 1, d), (b, 1, d))
                  for m in jnp.split(mod[b:b + 1], 6, axis=-1)[0:2]]

    wi = w_in[0]
    wx = wi[:, 0:CONV_CH].astype(BF16)
    wd = jnp.concatenate([_pad_lanes(wi[:, CONV_CH:CONV_CH + SSD_HEADS], LANES),
                          _pad_lanes(wi[:, CONV_CH + SSD_HEADS:xbcdt], LANES)], axis=1).astype(BF16)
    wz = wi[:, xbcdt:xbcdt + SSD_W].astype(BF16)
    wp = wi[:, xbcdt + SSD_W:].astype(BF16)
    n1 = norm1_w[0].reshape(1, d)
    conv_w8 = jnp.pad(conv_w[0], ((0, SUBLANES - SSD_CONV), (0, 0)))
    conv_b1 = conv_b[0].reshape(1, CONV_CH)
    dtb = _pad_lanes(dt_bias[0], LANES).reshape(2, 1, LANES)
    alog = _pad_lanes(a_log[0], LANES).reshape(2, 1, LANES)
    dsk = jnp.repeat(d_skip[0], SSD_HEADDIM).reshape(1, SSD_W)
    snw = ssd_norm_w[0].reshape(1, SSD_W)
    expand = _head_expand()

    xbc_c, dt_c, z_c, _ = _inproj(ctx, n1, csh1, csc1, wx, wd, wz, wp, min(PROJ_TM, lc))
    zero_state = jnp.zeros((b, 2, SSD_GROUPS, SSD_STATE, GROUP_W), F32)
    _, ctx_states = _ssd(xbc_c, dt_c, z_c, conv_w8, conv_b1, dtb, alog, dsk, snw, expand, zero_state)

    xbc, dt, z, u_pool = _inproj(x, n1, sh1, sc1, wx, wd, wz, wp, PROJ_TM)
    y_ssd, _ = _ssd(xbc, dt, z, conv_w8, conv_b1, dtb, alog, dsk, snw, expand, ctx_states)

    pool_a, pool_cnt = _pool_constants()
    x1, h2, rt, cnt = _mix(
        y_ssd, u_pool, x, pool_a, pool_cnt, pool_w[0].astype(BF16), pool_scale[0].reshape(1, POOL_W),
        w_out[0].astype(BF16), g1, norm2_w[0].reshape(1, d), sh2, sc2,
        _pad_lanes(router_w[0], LANES), _pad_lanes(router_b[0].reshape(1, N_EXPERTS), LANES),
        _strict_lower(MIX_TM))

    t = b * l
    rt2 = rt.reshape(t, LANES)
    e_idx = rt2[:, 0:TOP_K].astype(jnp.int32)
    rank = rt2[:, TOP_K:2 * TOP_K].astype(jnp.int32)
    counts = cnt[0, 0:N_EXPERTS].astype(jnp.int32)
    padded = (counts + MOE_BLK - 1) // MOE_BLK * MOE_BLK
    pad_end = jnp.cumsum(padded)
    pad_start = pad_end - padded
    dest = (pad_start[e_idx] + rank).reshape(t * TOP_K)
    n_blocks = (t * TOP_K) // MOE_BLK + N_EXPERTS
    n_used = (pad_end[-1] // MOE_BLK).astype(jnp.int32).reshape(1)
    blk_start = jnp.minimum(jnp.arange(n_blocks, dtype=jnp.int32), n_used[0] - 1) * MOE_BLK
    block_exp = jnp.minimum(jnp.searchsorted(pad_end, blk_start, side="right"),
                            N_EXPERTS - 1).astype(jnp.int32)

    fill_start = jnp.concatenate([pad_start + counts, pad_end[-1:]]).astype(jnp.int32)
    fill_len = jnp.concatenate([padded - counts, jnp.zeros((1,), jnp.int32)]).astype(jnp.int32)
    xs = _dispatch(fill_start, fill_len, dest, h2.reshape(t, d), n_blocks * MOE_BLK)
    dff2 = w1.shape[-1]
    b1p = jnp.concatenate(
        [b1[0].reshape(N_EXPERTS, dff2 // (2 * LANES), LANES, 2)[..., 0],
         b1[0].reshape(N_EXPERTS, dff2 // (2 * LANES), LANES, 2)[..., 1]], axis=-1
    ).reshape(N_EXPERTS, 1, dff2)
    ys = _experts(block_exp, n_used, xs, w1[0], b1p, w2[0], b2[0].reshape(N_EXPERTS, 1, d),
                  _deinterleave_perm(), n_blocks)
    return _combine(dest, x1, rt, g2, final_norm_w.reshape(1, d), ys)
```

```python
import functools

import numpy as np
import jax
import jax.numpy as jnp
from jax import lax
from jax.experimental import pallas as pl
from jax.experimental.pallas import tpu as pltpu

F32 = jnp.float32
BF16 = jnp.bfloat16

SSD_HEADDIM = 64
SSD_GROUPS = 4
SSD_HPG = 6
SSD_HEADS = SSD_GROUPS * SSD_HPG
SSD_STATE = 128
SSD_CONV = 5
SSD_CHUNK = 128
SSD_W = SSD_HEADS * SSD_HEADDIM
GROUP_W = SSD_HPG * SSD_HEADDIM
CONV_CH = SSD_W + 2 * SSD_GROUPS * SSD_STATE
POOL_WINDOWS = (2, 4, 8, 16)
POOL_GW = 128
POOL_W = POOL_GW * len(POOL_WINDOWS)
GRID_W = 64
N_EXPERTS = 32
TOP_K = 4
SWIGLU_ALPHA = 1.702
SWIGLU_LIMIT = 7.0
RMS_EPS = 1e-6

LANES = 128
SUBLANES = 8
VMEM_LIMIT_BYTES = 56 * 1024 * 1024

MOD_TN = 1024
PROJ_TM = 512
MIX_TM = 512
POOL_TM = 256
MOE_BLK = 512
ROW_TM = 256
NEG_BIG = -1e30


def _sigmoid(x):
    return 0.5 * jnp.tanh(0.5 * x) + 0.5


def _split2(a):
    hi = a.astype(BF16)
    mid = (a - hi.astype(F32)).astype(BF16)
    return hi, mid


def _split3(a):
    hi = a.astype(BF16)
    r = a - hi.astype(F32)
    mid = r.astype(BF16)
    lo = (r - mid.astype(F32)).astype(BF16)
    return hi, mid, lo


def _dot(a, b):
    return jnp.dot(a, b, preferred_element_type=F32)


def _dot_exact_rhs(a_f32, b_bf16, parts):
    pieces = _split3(a_f32) if parts == 3 else _split2(a_f32)
    out = _dot(pieces[0], b_bf16)
    for p in pieces[1:]:
        out = out + _dot(p, b_bf16)
    return out


def _dot_hi(a_f32, b_f32):
    a0, a1, a2 = _split3(a_f32)
    b0, b1, b2 = _split3(b_f32)
    out = _dot(a0, b0)
    out = out + _dot(a0, b1) + _dot(a1, b0)
    out = out + _dot(a1, b1) + _dot(a0, b2) + _dot(a2, b0)
    return out


def _cparams(sem):
    return pltpu.CompilerParams(dimension_semantics=sem, vmem_limit_bytes=VMEM_LIMIT_BYTES)


def _mod_kernel(c_ref, w_ref, b_ref, o_ref):
    c = c_ref[...]
    s = c * _sigmoid(c)
    o_ref[...] = _dot_hi(s, w_ref[...]) + b_ref[...]


def _modulation(cc, w_mod, b_mod):
    rows, d = cc.shape
    n = w_mod.shape[1]
    return pl.pallas_call(
        _mod_kernel,
        grid=(n // MOD_TN,),
        in_specs=[pl.BlockSpec((rows, d), lambda j: (0, 0)),
                  pl.BlockSpec((d, MOD_TN), lambda j: (0, j)),
                  pl.BlockSpec((1, MOD_TN), lambda j: (0, j))],
        out_specs=pl.BlockSpec((rows, MOD_TN), lambda j: (0, j)),
        out_shape=jax.ShapeDtypeStruct((rows, n), F32),
        compiler_params=_cparams(("arbitrary",)),
    )(cc, w_mod, b_mod.reshape(1, n))


def _inproj_kernel(x_ref, nw_ref, sh_ref, sc_ref, wx_ref, wd_ref, wz_ref, wp_ref,
                   xbc_ref, dt_ref, z_ref, pool_ref):
    x = x_ref[0]
    ms = jnp.mean(x * x, axis=-1, keepdims=True)
    h = x * lax.rsqrt(ms + RMS_EPS) * nw_ref[...]
    h = h * (1.0 + sc_ref[0]) + sh_ref[0]
    hb = h.astype(BF16)
    xbc_ref[0] = _dot(hb, wx_ref[...])
    dt_ref[0] = _dot(hb, wd_ref[...])
    z_ref[0] = _dot(hb, wz_ref[...])
    pool_ref[0] = _dot(hb, wp_ref[...])


def _inproj(x, norm_w, shift, scale, wx, wd, wz, wp, tm):
    b, l, d = x.shape
    full = lambda a: pl.BlockSpec(a.shape, lambda i, j: (0, 0))
    tok = lambda n: pl.BlockSpec((1, tm, n), lambda i, j: (i, j, 0))
    per_b = pl.BlockSpec((1, 1, d), lambda i, j: (i, 0, 0))
    return pl.pallas_call(
        _inproj_kernel,
        grid=(b, l // tm),
        in_specs=[tok(d), full(norm_w), per_b, per_b, full(wx), full(wd), full(wz), full(wp)],
        out_specs=[tok(CONV_CH), tok(2 * LANES), tok(SSD_W), tok(POOL_W)],
        out_shape=[jax.ShapeDtypeStruct((b, l, CONV_CH), F32),
                   jax.ShapeDtypeStruct((b, l, 2 * LANES), F32),
                   jax.ShapeDtypeStruct((b, l, SSD_W), F32),
                   jax.ShapeDtypeStruct((b, l, POOL_W), F32)],
        compiler_params=_cparams(("arbitrary", "arbitrary")),
    )(x, norm_w, shift, scale, wx, wd, wz, wp)


def _ssd_kernel(nc, xbc_ref, prev_ref, next_ref, dt_ref, z_ref, cw_ref, cb_ref, dtb_ref, alog_ref,
                dsk_ref, nw_ref, exp_ref, init_ref, y_ref, fin_ref,
                act_ref, ybuf_ref, yf_ref, st_ref):
    ch = SSD_CHUNK
    ph = pl.program_id(1)
    c = pl.program_id(2)
    ci = c + ph * (nc - 1 - 2 * c)

    @pl.when(c == 0)
    def _():
        st_ref[...] = init_ref[0, ph]

    has_prev = ci > 0
    has_next = ci < nc - 1
    row8 = lax.broadcasted_iota(jnp.int32, (SUBLANES, LANES), 0)
    half = SSD_CONV // 2
    for j in range(CONV_CH // LANES):
        cols = slice(j * LANES, (j + 1) * LANES)
        tiles = ([jnp.where(has_prev, prev_ref[0, :, cols], 0.0)]
                 + [xbc_ref[0, i * SUBLANES:(i + 1) * SUBLANES, cols] for i in range(ch // SUBLANES)]
                 + [jnp.where(has_next, next_ref[0, :, cols], 0.0)])
        n_t = ch // SUBLANES
        acc = [cb_ref[:, cols] + cw_ref[half:half + 1, cols] * tiles[i + 1] for i in range(n_t)]
        for s in range(1, half + 1):
            rot = [pltpu.roll(tl, s, axis=0) for tl in tiles[0:n_t + 1]]
            wk = cw_ref[half - s:half - s + 1, cols]
            for i in range(n_t):
                acc[i] = acc[i] + wk * jnp.where(row8 < s, rot[i], rot[i + 1])
            rot = [pltpu.roll(tl, SUBLANES - s, axis=0) for tl in tiles[1:n_t + 2]]
            wk = cw_ref[half + s:half + s + 1, cols]
            for i in range(n_t):
                acc[i] = acc[i] + wk * jnp.where(row8 >= SUBLANES - s, rot[i + 1], rot[i])
        for i in range(n_t):
            hx = 0.5 * acc[i]
            act_ref[i * SUBLANES:(i + 1) * SUBLANES, cols] = hx * jnp.tanh(hx) + hx

    dtr = dt_ref[0] + dtb_ref[0]
    dtv = jnp.maximum(dtr, 0.0) + jnp.log1p(jnp.exp(-jnp.abs(dtr)))
    a = dtv * (-jnp.exp(alog_ref[0]))
    row = lax.broadcasted_iota(jnp.int32, (ch, ch), 0)
    col = lax.broadcasted_iota(jnp.int32, (ch, ch), 1)
    tmask = (row - col) * (1 - 2 * ph) >= 0
    tri = jnp.where(tmask, 1.0, 0.0).astype(BF16)
    cs = _dot_exact_rhs_left(tri, a)
    tot = jnp.where(ph == 0, cs[ch - 1:ch, :], cs[0:1, :])
    cs_t = cs.T
    e_cs = jnp.exp(cs)
    e_dec = jnp.exp(tot - cs)
    e_tot = jnp.exp(tot)
    expand = exp_ref[...]
    dt_x = _dot_exact_rhs(dtv, expand, 2)
    ecs_x = _dot_exact_rhs(e_cs, expand, 2)
    edec_x = _dot_exact_rhs(e_dec, expand, 2)
    etot_x = _dot_exact_rhs(jnp.broadcast_to(e_tot, (SUBLANES, LANES)), expand, 2)[0:1]

    lane = lax.broadcasted_iota(jnp.int32, (ch, LANES), 1)
    for g in range(SSD_GROUPS):
        gs = slice(g * GROUP_W, (g + 1) * GROUP_W)
        b_bf = act_ref[:, SSD_W + g * SSD_STATE:SSD_W + (g + 1) * SSD_STATE].astype(BF16)
        c_bf = act_ref[:, SSD_W + (SSD_GROUPS + g) * SSD_STATE:
                       SSD_W + (SSD_GROUPS + g + 1) * SSD_STATE].astype(BF16)
        cb = lax.dot_general(c_bf, b_bf, (((1,), (1,)), ((), ())), preferred_element_type=F32)
        xs = act_ref[:, gs] * dt_x[:, gs]
        xs_bf = xs.astype(BF16)
        s_prev = st_ref[g]
        y_off = _dot(c_bf, s_prev.astype(BF16)) * ecs_x[:, gs]
        x_dec = (xs * edec_x[:, gs]).astype(BF16)
        st_ref[g] = s_prev * etot_x[:, gs] + lax.dot_general(
            b_bf, x_dec, (((0,), (0,)), ((), ())), preferred_element_type=F32)
        for q in range(SSD_HPG // 2):
            lmats = []
            for h in (g * SSD_HPG + 2 * q, g * SSD_HPG + 2 * q + 1):
                diff = cs[:, h:h + 1] - cs_t[h:h + 1, :]
                dec = jnp.exp(jnp.where(tmask, diff, NEG_BIG))
                lmats.append((dec * cb).astype(BF16))
            xp = xs_bf[:, q * LANES:(q + 1) * LANES]
            zero = jnp.zeros_like(xp)
            rhs = jnp.concatenate([jnp.where(lane < SSD_HEADDIM, xp, zero),
                                   jnp.where(lane >= SSD_HEADDIM, xp, zero)], axis=0)
            y_diag = _dot(jnp.concatenate(lmats, axis=1), rhs)
            ps = slice(g * GROUP_W + q * LANES, g * GROUP_W + (q + 1) * LANES)
            ybuf_ref[:, ps] = y_diag + y_off[:, q * LANES:(q + 1) * LANES]

    @pl.when(ph == 0)
    def _():
        yf_ref[ci] = ybuf_ref[...]

    @pl.when(ph == 1)
    def _():
        yt = yf_ref[ci] + ybuf_ref[...] + act_ref[:, 0:SSD_W] * dsk_ref[...]
        zz = z_ref[0]
        gt = yt * (zz * _sigmoid(zz))
        ms = jnp.mean(gt * gt, axis=-1, keepdims=True)
        y_ref[0] = (gt * lax.rsqrt(ms + RMS_EPS) * nw_ref[...]).astype(y_ref.dtype)

    @pl.when(c == nc - 1)
    def _():
        fin_ref[0, ph] = st_ref[...]


def _dot_exact_rhs_left(sel_bf16, a_f32):
    hi, mid, lo = _split3(a_f32)
    return _dot(sel_bf16, hi) + _dot(sel_bf16, mid) + _dot(sel_bf16, lo)


def _ssd(xbc, dt, z, conv_w8, conv_b, dt_bias, a_log, d_skip_x, norm_w, expand, init):
    b, l, _ = xbc.shape
    ch = SSD_CHUNK
    nc = l // ch
    rows8 = ch // SUBLANES

    def cidx(ph, c):
        return c + ph * (nc - 1 - 2 * c)

    def out_idx(ph, c):
        return jnp.where(ph == 0, nc - 1, nc - 1 - c)

    full2 = lambda a: pl.BlockSpec(a.shape, lambda i, ph, c: (0, 0))
    st_spec = pl.BlockSpec((1, 2, SSD_GROUPS, SSD_STATE, GROUP_W), lambda i, ph, c: (i, 0, 0, 0, 0))
    in_specs = [
        pl.BlockSpec((1, ch, CONV_CH), lambda i, ph, c: (i, cidx(ph, c), 0)),
        pl.BlockSpec((1, SUBLANES, CONV_CH),
                     lambda i, ph, c: (i, jnp.maximum(cidx(ph, c) * rows8 - 1, 0), 0)),
        pl.BlockSpec((1, SUBLANES, CONV_CH),
                     lambda i, ph, c: (i, jnp.minimum((cidx(ph, c) + 1) * rows8, l // SUBLANES - 1), 0)),
        pl.BlockSpec((1, ch, LANES), lambda i, ph, c: (i, cidx(ph, c), ph)),
        pl.BlockSpec((1, ch, SSD_W), lambda i, ph, c: (i, out_idx(ph, c), 0)),
        full2(conv_w8), full2(conv_b),
        pl.BlockSpec((1, 1, LANES), lambda i, ph, c: (ph, 0, 0)),
        pl.BlockSpec((1, 1, LANES), lambda i, ph, c: (ph, 0, 0)),
        full2(d_skip_x), full2(norm_w), full2(expand), st_spec,
    ]
    out_specs = [pl.BlockSpec((1, ch, SSD_W), lambda i, ph, c: (i, out_idx(ph, c), 0)), st_spec]
    return pl.pallas_call(
        functools.partial(_ssd_kernel, nc),
        grid=(b, 2, nc),
        in_specs=in_specs,
        out_specs=out_specs,
        out_shape=[jax.ShapeDtypeStruct((b, l, SSD_W), BF16),
                   jax.ShapeDtypeStruct((b, 2, SSD_GROUPS, SSD_STATE, GROUP_W), F32)],
        scratch_shapes=[pltpu.VMEM((ch, CONV_CH), F32),
                        pltpu.VMEM((ch, SSD_W), F32),
                        pltpu.VMEM((nc, ch, SSD_W), F32),
                        pltpu.VMEM((SSD_GROUPS, SSD_STATE, GROUP_W), F32)],
        compiler_params=_cparams(("arbitrary", "arbitrary", "arbitrary")),
    )(xbc, xbc, xbc, dt, z, conv_w8, conv_b, dt_bias, a_log, d_skip_x, norm_w, expand, init)


def _mix_kernel(y_ref, u_ref, x_ref, pa_ref, pcnt_ref, pw_ref, psc_ref, wo_ref, g1_ref,
                nw_ref, sh_ref, sc_ref, rw_ref, rb_ref, tri_ref,
                x1_ref, h_ref, rt_ref, cnt_ref, base_ref):
    tm = x_ref.shape[1]
    first = (pl.program_id(0) == 0) & (pl.program_id(1) == 0)

    @pl.when(first)
    def _():
        base_ref[...] = jnp.zeros_like(base_ref)

    pooled = []
    for g in range(len(POOL_WINDOWS)):
        parts = []
        for r in range(tm // POOL_TM):
            u = u_ref[0, r * POOL_TM:(r + 1) * POOL_TM, g * POOL_GW:(g + 1) * POOL_GW]
            wsum = _dot_exact_rhs_left(pa_ref[g], u)
            parts.append((wsum / pcnt_ref[g] - u).astype(BF16))
        pg = jnp.concatenate(parts, axis=0)
        pooled.append((_dot(pg, pw_ref[g]) * psc_ref[:, g * POOL_GW:(g + 1) * POOL_GW]).astype(BF16))
    y_pool = jnp.concatenate(pooled, axis=1)

    mix = _dot(y_ref[0], wo_ref[0:SSD_W, :]) + _dot(y_pool, wo_ref[SSD_W:SSD_W + POOL_W, :])
    x1 = x_ref[0] + g1_ref[0] * mix
    x1_ref[0] = x1

    ms = jnp.mean(x1 * x1, axis=-1, keepdims=True)
    h = x1 * lax.rsqrt(ms + RMS_EPS) * nw_ref[...]
    h = h * (1.0 + sc_ref[0]) + sh_ref[0]
    h_ref[0] = h

    h0, h1 = _split2(h)
    r0, r1 = _split2(rw_ref[...])
    logits = _dot(h0, r0) + _dot(h0, r1) + _dot(h1, r0) + rb_ref[...]
    lane = lax.broadcasted_iota(jnp.int32, (tm, LANES), 1)
    lane_f = lane.astype(F32)
    work = jnp.where(lane < N_EXPERTS, logits, NEG_BIG)
    vals, hots = [], []
    for _ in range(TOP_K):
        m = jnp.max(work, axis=-1, keepdims=True)
        first_idx = jnp.min(jnp.where(work == m, lane_f, float(LANES)), axis=-1, keepdims=True)
        hot = lane_f == first_idx
        vals.append(m)
        hots.append(hot)
        work = jnp.where(hot, 2.0 * NEG_BIG, work)
    exps = [jnp.exp(v - vals[0]) for v in vals]
    denom = exps[0] + exps[1] + exps[2] + exps[3]

    onehot = jnp.zeros((tm, LANES), F32)
    for hot in hots:
        onehot = onehot + jnp.where(hot, 1.0, 0.0)
    before = _dot(tri_ref[...], onehot.astype(BF16)) + base_ref[0:1, :]
    packed = jnp.zeros((tm, LANES), F32)
    for k in range(TOP_K):
        idx_k = jnp.sum(jnp.where(hots[k], lane_f, 0.0), axis=-1, keepdims=True)
        rank_k = jnp.sum(jnp.where(hots[k], before, 0.0), axis=-1, keepdims=True)
        packed = jnp.where(lane == k, idx_k, packed)
        packed = jnp.where(lane == TOP_K + k, rank_k, packed)
        packed = jnp.where(lane == 2 * TOP_K + k, exps[k] / denom, packed)
    rt_ref[0] = packed
    new_base = base_ref[0:1, :] + jnp.sum(onehot, axis=0, keepdims=True)
    base_ref[...] = jnp.broadcast_to(new_base, base_ref.shape)
    cnt_ref[...] = jnp.broadcast_to(new_base, cnt_ref.shape)


def _mix(y, u, x, pool_a, pool_cnt, pool_w, pool_scale, w_out, g1, norm_w, shift, scale,
         router_w, router_b, tri):
    b, l, d = x.shape
    tm = MIX_TM
    tok = lambda n: pl.BlockSpec((1, tm, n), lambda i, j: (i, j, 0))
    per_b = pl.BlockSpec((1, 1, d), lambda i, j: (i, 0, 0))
    full = lambda a: pl.BlockSpec(a.shape, lambda i, j: (0,) * a.ndim)
    cnt_spec = pl.BlockSpec((SUBLANES, LANES), lambda i, j: (0, 0))
    return pl.pallas_call(
        _mix_kernel,
        grid=(b, l // tm),
        in_specs=[tok(SSD_W), tok(POOL_W), tok(d), full(pool_a), full(pool_cnt), full(pool_w),
                  full(pool_scale), full(w_out), per_b, full(norm_w), per_b, per_b,
                  full(router_w), full(router_b), full(tri)],
        out_specs=[tok(d), tok(d), tok(LANES), cnt_spec],
        out_shape=[jax.ShapeDtypeStruct((b, l, d), F32),
                   jax.ShapeDtypeStruct((b, l, d), F32),
                   jax.ShapeDtypeStruct((b, l, LANES), F32),
                   jax.ShapeDtypeStruct((SUBLANES, LANES), F32)],
        scratch_shapes=[pltpu.VMEM((SUBLANES, LANES), F32)],
        compiler_params=_cparams(("arbitrary", "arbitrary")),
    )(y, u, x, pool_a, pool_cnt, pool_w, pool_scale, w_out, g1, norm_w, shift, scale,
      router_w, router_b, tri)


def _dispatch_kernel(fill_start_ref, fill_len_ref, dest_ref, h_ref, xs_ref, zero_ref, sem, fill_sem):
    tm = h_ref.shape[0]
    n_slots = xs_ref.shape[0]

    def body(t, carry):
        for k in range(TOP_K):
            d = dest_ref[t * TOP_K + k]
            pltpu.make_async_copy(h_ref.at[pl.ds(t, 1)], xs_ref.at[pl.ds(d, 1)], sem).start()
        return carry

    lax.fori_loop(0, tm, body, 0)

    @pl.when(pl.program_id(0) == pl.num_programs(0) - 1)
    def _():
        zero_ref[...] = jnp.zeros_like(zero_ref)
        bits = [1 << s for s in range(MOE_BLK.bit_length() - 2, SUBLANES.bit_length() - 2, -1)]

        def pad_copies(e, wait):
            start = fill_start_ref[e]
            n = fill_len_ref[e]
            head = jnp.minimum((-start) & (SUBLANES - 1), n)
            for r in range(SUBLANES - 1):
                @pl.when(r < head)
                def _():
                    cp = pltpu.make_async_copy(zero_ref.at[pl.ds(0, 1)],
                                               xs_ref.at[pl.ds(start + r, 1)], fill_sem)
                    cp.wait() if wait else cp.start()
            start = start + head
            n = n - head
            for bit in bits:
                @pl.when((n & bit) != 0)
                def _():
                    off = pl.multiple_of(start + (n & ~(2 * bit - 1)), SUBLANES)
                    cp = pltpu.make_async_copy(zero_ref.at[pl.ds(0, bit)],
                                               xs_ref.at[pl.ds(off, bit)], fill_sem)
                    cp.wait() if wait else cp.start()

        def tail_copy(j, wait):
            off = pl.multiple_of(fill_start_ref[N_EXPERTS] + j * MOE_BLK, MOE_BLK)
            cp = pltpu.make_async_copy(zero_ref, xs_ref.at[pl.ds(off, MOE_BLK)], fill_sem)
            cp.wait() if wait else cp.start()

        n_tail = (n_slots - fill_start_ref[N_EXPERTS]) // MOE_BLK
        for wait in (False, True):
            lax.fori_loop(0, N_EXPERTS, lambda e, cr: (pad_copies(e, wait), cr)[1], 0)
            lax.fori_loop(0, n_tail, lambda j, cr: (tail_copy(j, wait), cr)[1], 0)

    for _ in range(TOP_K):
        pltpu.make_async_copy(h_ref, xs_ref.at[pl.ds(0, tm)], sem).wait()


def _dispatch(fill_start, fill_len, dest_flat, h, n_slots):
    t, d = h.shape
    tm = ROW_TM
    grid_spec = pltpu.PrefetchScalarGridSpec(
        num_scalar_prefetch=2,
        grid=(t // tm,),
        in_specs=[pl.BlockSpec((tm * TOP_K,), lambda i, fs, fl: (i,), memory_space=pltpu.SMEM),
                  pl.BlockSpec((tm, d), lambda i, fs, fl: (i, 0))],
        out_specs=pl.BlockSpec(memory_space=pl.ANY),
        scratch_shapes=[pltpu.VMEM((MOE_BLK, d), F32),
                        pltpu.SemaphoreType.DMA(()), pltpu.SemaphoreType.DMA(())],
    )
    return pl.pallas_call(
        _dispatch_kernel,
        grid_spec=grid_spec,
        out_shape=jax.ShapeDtypeStruct((n_slots, d), F32),
        compiler_params=_cparams(("arbitrary",)),
    )(fill_start, fill_len, dest_flat, h)


def _expert_kernel(be_ref, nused_ref, xs_ref, w1_ref, b1_ref, w2_ref, b2_ref, perm_ref, y_ref,
                   w1s_ref, w2s_ref):
    i = pl.program_id(0)
    prev = be_ref[jnp.maximum(i - 1, 0)]
    changed = (i == 0) | (be_ref[i] != prev)
    dff2 = w1_ref.shape[2]
    tile = 2 * LANES

    @pl.when(changed & (i < nused_ref[0]))
    def _():
        for j in range(dff2 // tile):
            wj = w1_ref[0, :, j * tile:(j + 1) * tile].astype(BF16)
            w1s_ref[:, j * tile:(j + 1) * tile] = _dot(wj, perm_ref[...]).astype(BF16)
        w2s_ref[...] = w2_ref[0].astype(BF16)

    @pl.when(i < nused_ref[0])
    def _():
        x = xs_ref[...].astype(BF16)
        acts = []
        for j in range(dff2 // tile):
            hb = _dot(x, w1s_ref[:, j * tile:(j + 1) * tile]) + b1_ref[0, :, j * tile:(j + 1) * tile]
            gp = jnp.minimum(hb[:, 0:LANES], SWIGLU_LIMIT)
            up = jnp.clip(hb[:, LANES:tile], -SWIGLU_LIMIT, SWIGLU_LIMIT)
            acts.append((gp * _sigmoid(SWIGLU_ALPHA * gp) * (up + 1.0)).astype(BF16))
        act = jnp.concatenate(acts, axis=1)
        y_ref[...] = _dot(act, w2s_ref[...]) + b2_ref[0]

    @pl.when(i >= nused_ref[0])
    def _():
        y_ref[...] = jnp.zeros_like(y_ref)


def _experts(block_exp, n_used, xs, w1, b1p, w2, b2, perm, n_blocks):
    blk = MOE_BLK
    d = xs.shape[1]
    dff2 = w1.shape[2]
    dff = w2.shape[1]

    def x_idx(i, be, nu):
        return (jnp.minimum(i, nu[0] - 1), 0)


    grid_spec = pltpu.PrefetchScalarGridSpec(
        num_scalar_prefetch=2,
        grid=(n_blocks,),
        in_specs=[pl.BlockSpec((blk, d), x_idx),
                  pl.BlockSpec((1, d, dff2), lambda i, be, nu: (be[i], 0, 0)),
                  pl.BlockSpec((1, 1, dff2), lambda i, be, nu: (be[i], 0, 0)),
                  pl.BlockSpec((1, dff, d), lambda i, be, nu: (be[i], 0, 0)),
                  pl.BlockSpec((1, 1, d), lambda i, be, nu: (be[i], 0, 0)),
                  pl.BlockSpec(perm.shape, lambda i, be, nu: (0, 0))],
        out_specs=pl.BlockSpec((blk, d), lambda i, be, nu: (i, 0)),
        scratch_shapes=[pltpu.VMEM((d, dff2), BF16), pltpu.VMEM((dff, d), BF16)],
    )
    return pl.pallas_call(
        _expert_kernel,
        grid_spec=grid_spec,
        out_shape=jax.ShapeDtypeStruct((n_blocks * blk, d), F32),
        compiler_params=_cparams(("arbitrary",)),
    )(block_exp, n_used, xs, w1, b1p, w2, b2, perm)


def _combine_kernel(dest_ref, x1_ref, rt_ref, g2_ref, fnw_ref, ys_ref, o_ref, buf_ref, sem):
    tm = x1_ref.shape[1]

    def body(t, carry):
        for k in range(TOP_K):
            d = dest_ref[t * TOP_K + k]
            pltpu.make_async_copy(ys_ref.at[pl.ds(d, 1)], buf_ref.at[k, pl.ds(t, 1)], sem).start()
        return carry

    lax.fori_loop(0, tm, body, 0)
    for k in range(TOP_K):
        pltpu.make_async_copy(ys_ref.at[pl.ds(0, tm)], buf_ref.at[k], sem).wait()
    rt = rt_ref[0]
    moe = buf_ref[0] * rt[:, 2 * TOP_K:2 * TOP_K + 1]
    for k in range(1, TOP_K):
        moe = moe + buf_ref[k] * rt[:, 2 * TOP_K + k:2 * TOP_K + k + 1]
    x2 = x1_ref[0] + g2_ref[0] * moe
    ms = jnp.mean(x2 * x2, axis=-1, keepdims=True)
    o_ref[0] = x2 * lax.rsqrt(ms + RMS_EPS) * fnw_ref[...]


def _combine(dest_flat, x1, rt, g2, fnw, ys):
    b, l, d = x1.shape
    tm = ROW_TM
    per_l = l // tm
    tok = lambda n: pl.BlockSpec((1, tm, n), lambda i, j: (i, j, 0))
    return pl.pallas_call(
        _combine_kernel,
        grid=(b, per_l),
        in_specs=[pl.BlockSpec((tm * TOP_K,), lambda i, j: (i * per_l + j,), memory_space=pltpu.SMEM),
                  tok(d), tok(LANES),
                  pl.BlockSpec((1, 1, d), lambda i, j: (i, 0, 0)),
                  pl.BlockSpec(fnw.shape, lambda i, j: (0, 0)),
                  pl.BlockSpec(memory_space=pl.ANY)],
        out_specs=tok(d),
        out_shape=jax.ShapeDtypeStruct((b, l, d), F32),
        scratch_shapes=[pltpu.VMEM((TOP_K, tm, d), F32), pltpu.SemaphoreType.DMA(())],
        compiler_params=_cparams(("arbitrary", "arbitrary")),
    )(dest_flat, x1, rt, g2, fnw, ys)


def _pool_constants():
    i = np.arange(POOL_TM)[:, None]
    j = np.arange(POOL_TM)[None, :]
    same_row = (i // GRID_W) == (j // GRID_W)
    mats, cnts = [], []
    for w in POOL_WINDOWS:
        band = same_row & (j - i >= -(w // 2)) & (j - i < w - w // 2)
        mats.append(band)
        cnts.append(np.broadcast_to(band.sum(axis=1, keepdims=True), (POOL_TM, POOL_GW)))
    return (jnp.asarray(np.stack(mats), BF16), jnp.asarray(np.stack(cnts), F32))


def _head_expand():
    e = np.zeros((LANES, SSD_W), np.float32)
    for h in range(SSD_HEADS):
        e[h, h * SSD_HEADDIM:(h + 1) * SSD_HEADDIM] = 1.0
    return jnp.asarray(e, BF16)


def _deinterleave_perm():
    n = 2 * LANES
    p = np.zeros((n, n), np.float32)
    for k in range(LANES):
        p[2 * k, k] = 1.0
        p[2 * k + 1, LANES + k] = 1.0
    return jnp.asarray(p, BF16)


def _strict_lower(n):
    return jnp.asarray(np.tril(np.ones((n, n), np.float32), -1), BF16)


def _pad_lanes(a, n):
    return jnp.pad(a, [(0, 0)] * (a.ndim - 1) + [(0, n - a.shape[-1])])


def kernel(x, c, ctx, c_ctx, w_mod, b_mod, norm1_w, norm2_w, w_in, conv_w, conv_b, dt_bias, a_log,
           d_skip, ssd_norm_w, pool_w, pool_scale, w_out, router_w, router_b, w1, b1, w2, b2,
           final_norm_w):
    depth = w_mod.shape[0]
    assert depth == 1, "single-layer problem"
    b, l, d = x.shape
    lc = ctx.shape[1]
    xbcdt = CONV_CH + 2 * SSD_HEADS

    mod_rows = 2 * SUBLANES
    cc = jnp.zeros((mod_rows, d), F32).at[0:b].set(c).at[b].set(c_ctx)
    mod = _modulation(cc, w_mod[0], b_mod[0])
    sh1, sc1, g1, sh2, sc2, g2 = [m.reshape(b, 1, d) for m in jnp.split(mod[0:b], 6, axis=-1)]
    csh1, csc1 = [jnp.broadcast_to(m.reshape(1, 1, d), (b, 1, d))
                  for m in jnp.split(mod[b:b + 1], 6, axis=-1)[0:2]]

    wi = w_in[0]
    wx = wi[:, 0:CONV_CH].astype(BF16)
    wd = jnp.concatenate([_pad_lanes(wi[:, CONV_CH:CONV_CH + SSD_HEADS], LANES),
                          _pad_lanes(wi[:, CONV_CH + SSD_HEADS:xbcdt], LANES)], axis=1).astype(BF16)
    wz = wi[:, xbcdt:xbcdt + SSD_W].astype(BF16)
    wp = wi[:, xbcdt + SSD_W:].astype(BF16)
    n1 = norm1_w[0].reshape(1, d)
    conv_w8 = jnp.pad(conv_w[0], ((0, SUBLANES - SSD_CONV), (0, 0)))
    conv_b1 = conv_b[0].reshape(1, CONV_CH)
    dtb = _pad_lanes(dt_bias[0], LANES).reshape(2, 1, LANES)
    alog = _pad_lanes(a_log[0], LANES).reshape(2, 1, LANES)
    dsk = jnp.repeat(d_skip[0], SSD_HEADDIM).reshape(1, SSD_W)
    snw = ssd_norm_w[0].reshape(1, SSD_W)
    expand = _head_expand()

    xbc_c, dt_c, z_c, _ = _inproj(ctx, n1, csh1, csc1, wx, wd, wz, wp, min(PROJ_TM, lc))
    zero_state = jnp.zeros((b, 2, SSD_GROUPS, SSD_STATE, GROUP_W), F32)
    _, ctx_states = _ssd(xbc_c, dt_c, z_c, conv_w8, conv_b1, dtb, alog, dsk, snw, expand, zero_state)

    xbc, dt, z, u_pool = _inproj(x, n1, sh1, sc1, wx, wd, wz, wp, PROJ_TM)
    y_ssd, _ = _ssd(xbc, dt, z, conv_w8, conv_b1, dtb, alog, dsk, snw, expand, ctx_states)

    pool_a, pool_cnt = _pool_constants()
    x1, h2, rt, cnt = _mix(
        y_ssd, u_pool, x, pool_a, pool_cnt, pool_w[0].astype(BF16), pool_scale[0].reshape(1, POOL_W),
        w_out[0].astype(BF16), g1, norm2_w[0].reshape(1, d), sh2, sc2,
        _pad_lanes(router_w[0], LANES), _pad_lanes(router_b[0].reshape(1, N_EXPERTS), LANES),
        _strict_lower(MIX_TM))

    t = b * l
    rt2 = rt.reshape(t, LANES)
    e_idx = rt2[:, 0:TOP_K].astype(jnp.int32)
    rank = rt2[:, TOP_K:2 * TOP_K].astype(jnp.int32)
    counts = cnt[0, 0:N_EXPERTS].astype(jnp.int32)
    padded = (counts + MOE_BLK - 1) // MOE_BLK * MOE_BLK
    pad_end = jnp.cumsum(padded)
    pad_start = pad_end - padded
    dest = (pad_start[e_idx] + rank).reshape(t * TOP_K)
    n_blocks = (t * TOP_K) // MOE_BLK + N_EXPERTS
    n_used = (pad_end[-1] // MOE_BLK).astype(jnp.int32).reshape(1)
    blk_start = jnp.minimum(jnp.arange(n_blocks, dtype=jnp.int32), n_used[0] - 1) * MOE_BLK
    block_exp = jnp.minimum(jnp.sum(blk_start[:, None] >= pad_end[None, :], axis=1),
                            N_EXPERTS - 1).astype(jnp.int32)

    fill_start = jnp.concatenate([pad_start + counts, pad_end[-1:]]).astype(jnp.int32)
    fill_len = jnp.concatenate([padded - counts, jnp.zeros((1,), jnp.int32)]).astype(jnp.int32)
    xs = _dispatch(fill_start, fill_len, dest, h2.reshape(t, d), n_blocks * MOE_BLK)
    dff2 = w1.shape[-1]
    b1p = jnp.concatenate(
        [b1[0].reshape(N_EXPERTS, dff2 // (2 * LANES), LANES, 2)[..., 0],
         b1[0].reshape(N_EXPERTS, dff2 // (2 * LANES), LANES, 2)[..., 1]], axis=-1
    ).reshape(N_EXPERTS, 1, dff2)
    ys = _experts(block_exp, n_used, xs, w1[0], b1p, w2[0], b2[0].reshape(N_EXPERTS, 1, d),
                  _deinterleave_perm(), n_blocks)
    return _combine(dest, x1, rt, g2, final_norm_w.reshape(1, d), ys)
```

```python
import functools

import numpy as np
import jax
import jax.numpy as jnp
from jax import lax
from jax.experimental import pallas as pl
from jax.experimental.pallas import tpu as pltpu

F32 = jnp.float32
BF16 = jnp.bfloat16

SSD_HEADDIM = 64
SSD_GROUPS = 4
SSD_HPG = 6
SSD_HEADS = SSD_GROUPS * SSD_HPG
SSD_STATE = 128
SSD_CONV = 5
SSD_CHUNK = 128
SSD_W = SSD_HEADS * SSD_HEADDIM
GROUP_W = SSD_HPG * SSD_HEADDIM
CONV_CH = SSD_W + 2 * SSD_GROUPS * SSD_STATE
POOL_WINDOWS = (2, 4, 8, 16)
POOL_GW = 128
POOL_W = POOL_GW * len(POOL_WINDOWS)
GRID_W = 64
N_EXPERTS = 32
TOP_K = 4
SWIGLU_ALPHA = 1.702
SWIGLU_LIMIT = 7.0
RMS_EPS = 1e-6

LANES = 128
SUBLANES = 8
VMEM_LIMIT_BYTES = 56 * 1024 * 1024

MOD_TN = 1024
PROJ_TM = 512
MIX_TM = 512
POOL_TM = 256
MOE_BLK = 512
ROW_TM = 512
NEG_BIG = -1e30


def _sigmoid(x):
    return 0.5 * jnp.tanh(0.5 * x) + 0.5


def _split2(a):
    hi = a.astype(BF16)
    mid = (a - hi.astype(F32)).astype(BF16)
    return hi, mid


def _split3(a):
    hi = a.astype(BF16)
    r = a - hi.astype(F32)
    mid = r.astype(BF16)
    lo = (r - mid.astype(F32)).astype(BF16)
    return hi, mid, lo


def _dot(a, b):
    return jnp.dot(a, b, preferred_element_type=F32)


def _dot_exact_rhs(a_f32, b_bf16, parts):
    pieces = _split3(a_f32) if parts == 3 else _split2(a_f32)
    out = _dot(pieces[0], b_bf16)
    for p in pieces[1:]:
        out = out + _dot(p, b_bf16)
    return out


def _dot_hi(a_f32, b_f32):
    a0, a1, a2 = _split3(a_f32)
    b0, b1, b2 = _split3(b_f32)
    out = _dot(a0, b0)
    out = out + _dot(a0, b1) + _dot(a1, b0)
    out = out + _dot(a1, b1) + _dot(a0, b2) + _dot(a2, b0)
    return out


def _cparams(sem):
    return pltpu.CompilerParams(dimension_semantics=sem, vmem_limit_bytes=VMEM_LIMIT_BYTES)


def _mod_kernel(c_ref, w_ref, b_ref, o_ref):
    c = c_ref[...]
    s = c * _sigmoid(c)
    o_ref[...] = _dot_hi(s, w_ref[...]) + b_ref[...]


def _modulation(cc, w_mod, b_mod):
    rows, d = cc.shape
    n = w_mod.shape[1]
    return pl.pallas_call(
        _mod_kernel,
        grid=(n // MOD_TN,),
        in_specs=[pl.BlockSpec((rows, d), lambda j: (0, 0)),
                  pl.BlockSpec((d, MOD_TN), lambda j: (0, j)),
                  pl.BlockSpec((1, MOD_TN), lambda j: (0, j))],
        out_specs=pl.BlockSpec((rows, MOD_TN), lambda j: (0, j)),
        out_shape=jax.ShapeDtypeStruct((rows, n), F32),
        compiler_params=_cparams(("arbitrary",)),
    )(cc, w_mod, b_mod.reshape(1, n))


def _inproj_kernel(x_ref, nw_ref, sh_ref, sc_ref, wx_ref, wd_ref, wz_ref, wp_ref,
                   xbc_ref, dt_ref, z_ref, pool_ref):
    x = x_ref[0]
    ms = jnp.mean(x * x, axis=-1, keepdims=True)
    h = x * lax.rsqrt(ms + RMS_EPS) * nw_ref[...]
    h = h * (1.0 + sc_ref[0]) + sh_ref[0]
    hb = h.astype(BF16)
    xbc_ref[0] = _dot(hb, wx_ref[...])
    dt_ref[0] = _dot(hb, wd_ref[...])
    z_ref[0] = _dot(hb, wz_ref[...])
    pool_ref[0] = _dot(hb, wp_ref[...])


def _inproj(x, norm_w, shift, scale, wx, wd, wz, wp, tm):
    b, l, d = x.shape
    full = lambda a: pl.BlockSpec(a.shape, lambda i, j: (0, 0))
    tok = lambda n: pl.BlockSpec((1, tm, n), lambda i, j: (i, j, 0))
    per_b = pl.BlockSpec((1, 1, d), lambda i, j: (i, 0, 0))
    return pl.pallas_call(
        _inproj_kernel,
        grid=(b, l // tm),
        in_specs=[tok(d), full(norm_w), per_b, per_b, full(wx), full(wd), full(wz), full(wp)],
        out_specs=[tok(CONV_CH), tok(2 * LANES), tok(SSD_W), tok(POOL_W)],
        out_shape=[jax.ShapeDtypeStruct((b, l, CONV_CH), F32),
                   jax.ShapeDtypeStruct((b, l, 2 * LANES), F32),
                   jax.ShapeDtypeStruct((b, l, SSD_W), F32),
                   jax.ShapeDtypeStruct((b, l, POOL_W), F32)],
        compiler_params=_cparams(("arbitrary", "arbitrary")),
    )(x, norm_w, shift, scale, wx, wd, wz, wp)


def _ssd_kernel(nc, xbc_ref, prev_ref, next_ref, dt_ref, z_ref, cw_ref, cb_ref, dtb_ref, alog_ref,
                dsk_ref, nw_ref, exp_ref, init_ref, y_ref, fin_ref,
                act_ref, ybuf_ref, yf_ref, st_ref):
    ch = SSD_CHUNK
    ph = pl.program_id(1)
    c = pl.program_id(2)
    ci = c + ph * (nc - 1 - 2 * c)

    @pl.when(c == 0)
    def _():
        st_ref[...] = init_ref[0, ph]

    @pl.when(ph == 0)
    def _():
        _conv_silu(nc, ci, xbc_ref, prev_ref, next_ref, cw_ref, cb_ref, act_ref)

    _ssd_scan(nc, ph, c, ci, dt_ref, z_ref, dtb_ref, alog_ref, dsk_ref, nw_ref, exp_ref, y_ref, fin_ref,
              act_ref, ybuf_ref, yf_ref, st_ref)


def _conv_silu(nc, ci, xbc_ref, prev_ref, next_ref, cw_ref, cb_ref, act_ref):
    ch = SSD_CHUNK
    has_prev = ci > 0
    has_next = ci < nc - 1
    row8 = lax.broadcasted_iota(jnp.int32, (SUBLANES, LANES), 0)
    half = SSD_CONV // 2
    for j in range(CONV_CH // LANES):
        cols = slice(j * LANES, (j + 1) * LANES)
        tiles = ([jnp.where(has_prev, prev_ref[0, :, cols], 0.0)]
                 + [xbc_ref[0, i * SUBLANES:(i + 1) * SUBLANES, cols] for i in range(ch // SUBLANES)]
                 + [jnp.where(has_next, next_ref[0, :, cols], 0.0)])
        n_t = ch // SUBLANES
        acc = [cb_ref[:, cols] + cw_ref[half:half + 1, cols] * tiles[i + 1] for i in range(n_t)]
        for s in range(1, half + 1):
            rot = [pltpu.roll(tl, s, axis=0) for tl in tiles[0:n_t + 1]]
            wk = cw_ref[half - s:half - s + 1, cols]
            for i in range(n_t):
                acc[i] = acc[i] + wk * jnp.where(row8 < s, rot[i], rot[i + 1])
            rot = [pltpu.roll(tl, SUBLANES - s, axis=0) for tl in tiles[1:n_t + 2]]
            wk = cw_ref[half + s:half + s + 1, cols]
            for i in range(n_t):
                acc[i] = acc[i] + wk * jnp.where(row8 >= SUBLANES - s, rot[i + 1], rot[i])
        silu = [0.5 * v * jnp.tanh(0.5 * v) + 0.5 * v for v in acc]
        act_ref[ci, :, cols] = jnp.concatenate(silu, axis=0).astype(BF16)


def _ssd_scan(nc, ph, c, ci, dt_ref, z_ref, dtb_ref, alog_ref, dsk_ref, nw_ref, exp_ref, y_ref, fin_ref,
              act_ref, ybuf_ref, yf_ref, st_ref):
    ch = SSD_CHUNK
    dtr = dt_ref[0] + dtb_ref[0]
    dtv = jnp.maximum(dtr, 0.0) + jnp.log1p(jnp.exp(-jnp.abs(dtr)))
    a = dtv * (-jnp.exp(alog_ref[0]))
    row = lax.broadcasted_iota(jnp.int32, (ch, ch), 0)
    col = lax.broadcasted_iota(jnp.int32, (ch, ch), 1)
    tmask = (row - col) * (1 - 2 * ph) >= 0
    tri = jnp.where(tmask, 1.0, 0.0).astype(BF16)
    cs = _dot_exact_rhs_left(tri, a)
    tot = jnp.where(ph == 0, cs[ch - 1:ch, :], cs[0:1, :])
    cs_t = cs.T
    e_cs = jnp.exp(cs)
    e_dec = jnp.exp(tot - cs)
    e_tot = jnp.exp(tot)
    expand = exp_ref[...]
    dt_x = _dot_exact_rhs(dtv, expand, 2)
    ecs_x = _dot_exact_rhs(e_cs, expand, 2)
    edec_x = _dot_exact_rhs(e_dec, expand, 2)
    etot_x = _dot_exact_rhs(jnp.broadcast_to(e_tot, (SUBLANES, LANES)), expand, 2)[0:1]

    lane = lax.broadcasted_iota(jnp.int32, (ch, LANES), 1)
    for g in range(SSD_GROUPS):
        gs = slice(g * GROUP_W, (g + 1) * GROUP_W)
        b_bf = act_ref[ci, :, SSD_W + g * SSD_STATE:SSD_W + (g + 1) * SSD_STATE]
        c_bf = act_ref[ci, :, SSD_W + (SSD_GROUPS + g) * SSD_STATE:
                       SSD_W + (SSD_GROUPS + g + 1) * SSD_STATE]
        cb = lax.dot_general(c_bf, b_bf, (((1,), (1,)), ((), ())), preferred_element_type=F32)
        xs = act_ref[ci, :, gs].astype(F32) * dt_x[:, gs]
        xs_bf = xs.astype(BF16)
        s_prev = st_ref[g]
        y_off = _dot(c_bf, s_prev.astype(BF16)) * ecs_x[:, gs]
        x_dec = (xs * edec_x[:, gs]).astype(BF16)
        st_ref[g] = s_prev * etot_x[:, gs] + lax.dot_general(
            b_bf, x_dec, (((0,), (0,)), ((), ())), preferred_element_type=F32)
        for q in range(SSD_HPG // 2):
            lmats = []
            for h in (g * SSD_HPG + 2 * q, g * SSD_HPG + 2 * q + 1):
                diff = cs[:, h:h + 1] - cs_t[h:h + 1, :]
                dec = jnp.exp(jnp.where(tmask, diff, NEG_BIG))
                lmats.append((dec * cb).astype(BF16))
            xp = xs_bf[:, q * LANES:(q + 1) * LANES]
            zero = jnp.zeros_like(xp)
            rhs = jnp.concatenate([jnp.where(lane < SSD_HEADDIM, xp, zero),
                                   jnp.where(lane >= SSD_HEADDIM, xp, zero)], axis=0)
            y_diag = _dot(jnp.concatenate(lmats, axis=1), rhs)
            ps = slice(g * GROUP_W + q * LANES, g * GROUP_W + (q + 1) * LANES)
            ybuf_ref[:, ps] = y_diag + y_off[:, q * LANES:(q + 1) * LANES]

    @pl.when(ph == 0)
    def _():
        yf_ref[ci] = ybuf_ref[...].astype(BF16)

    @pl.when(ph == 1)
    def _():
        yt = (yf_ref[ci].astype(F32) + ybuf_ref[...]
              + act_ref[ci, :, 0:SSD_W].astype(F32) * dsk_ref[...])
        zz = z_ref[0]
        gt = yt * (zz * _sigmoid(zz))
        ms = jnp.mean(gt * gt, axis=-1, keepdims=True)
        y_ref[0] = (gt * lax.rsqrt(ms + RMS_EPS) * nw_ref[...]).astype(y_ref.dtype)

    @pl.when(c == nc - 1)
    def _():
        fin_ref[0, ph] = st_ref[...]


def _dot_exact_rhs_left(sel_bf16, a_f32):
    hi, mid, lo = _split3(a_f32)
    return _dot(sel_bf16, hi) + _dot(sel_bf16, mid) + _dot(sel_bf16, lo)


def _ssd(xbc, dt, z, conv_w8, conv_b, dt_bias, a_log, d_skip_x, norm_w, expand, init):
    b, l, _ = xbc.shape
    ch = SSD_CHUNK
    nc = l // ch
    rows8 = ch // SUBLANES

    def cidx(ph, c):
        return c + ph * (nc - 1 - 2 * c)

    def xidx(ph, c):
        return jnp.where(ph == 0, c, nc - 1)

    def out_idx(ph, c):
        return jnp.where(ph == 0, nc - 1, nc - 1 - c)

    full2 = lambda a: pl.BlockSpec(a.shape, lambda i, ph, c: (0, 0))
    st_spec = pl.BlockSpec((1, 2, SSD_GROUPS, SSD_STATE, GROUP_W), lambda i, ph, c: (i, 0, 0, 0, 0))
    in_specs = [
        pl.BlockSpec((1, ch, CONV_CH), lambda i, ph, c: (i, xidx(ph, c), 0)),
        pl.BlockSpec((1, SUBLANES, CONV_CH),
                     lambda i, ph, c: (i, jnp.maximum(xidx(ph, c) * rows8 - 1, 0), 0)),
        pl.BlockSpec((1, SUBLANES, CONV_CH),
                     lambda i, ph, c: (i, jnp.minimum((xidx(ph, c) + 1) * rows8, l // SUBLANES - 1), 0)),
        pl.BlockSpec((1, ch, LANES), lambda i, ph, c: (i, cidx(ph, c), ph)),
        pl.BlockSpec((1, ch, SSD_W), lambda i, ph, c: (i, out_idx(ph, c), 0)),
        full2(conv_w8), full2(conv_b),
        pl.BlockSpec((1, 1, LANES), lambda i, ph, c: (ph, 0, 0)),
        pl.BlockSpec((1, 1, LANES), lambda i, ph, c: (ph, 0, 0)),
        full2(d_skip_x), full2(norm_w), full2(expand), st_spec,
    ]
    out_specs = [pl.BlockSpec((1, ch, SSD_W), lambda i, ph, c: (i, out_idx(ph, c), 0)), st_spec]
    return pl.pallas_call(
        functools.partial(_ssd_kernel, nc),
        grid=(b, 2, nc),
        in_specs=in_specs,
        out_specs=out_specs,
        out_shape=[jax.ShapeDtypeStruct((b, l, SSD_W), BF16),
                   jax.ShapeDtypeStruct((b, 2, SSD_GROUPS, SSD_STATE, GROUP_W), F32)],
        scratch_shapes=[pltpu.VMEM((nc, ch, CONV_CH), BF16),
                        pltpu.VMEM((ch, SSD_W), F32),
                        pltpu.VMEM((nc, ch, SSD_W), BF16),
                        pltpu.VMEM((SSD_GROUPS, SSD_STATE, GROUP_W), F32)],
        compiler_params=_cparams(("arbitrary", "arbitrary", "arbitrary")),
    )(xbc, xbc, xbc, dt, z, conv_w8, conv_b, dt_bias, a_log, d_skip_x, norm_w, expand, init)


def _mix_kernel(y_ref, u_ref, x_ref, pa_ref, pcnt_ref, pw_ref, psc_ref, wo_ref, g1_ref,
                nw_ref, sh_ref, sc_ref, rw_ref, rb_ref, tri_ref,
                x1_ref, h_ref, rt_ref, cnt_ref, base_ref):
    tm = x_ref.shape[1]
    first = (pl.program_id(0) == 0) & (pl.program_id(1) == 0)

    @pl.when(first)
    def _():
        base_ref[...] = jnp.zeros_like(base_ref)

    pooled = []
    for g in range(len(POOL_WINDOWS)):
        parts = []
        for r in range(tm // POOL_TM):
            u = u_ref[0, r * POOL_TM:(r + 1) * POOL_TM, g * POOL_GW:(g + 1) * POOL_GW]
            wsum = _dot_exact_rhs_left(pa_ref[g], u)
            parts.append((wsum / pcnt_ref[g] - u).astype(BF16))
        pg = jnp.concatenate(parts, axis=0)
        pooled.append((_dot(pg, pw_ref[g]) * psc_ref[:, g * POOL_GW:(g + 1) * POOL_GW]).astype(BF16))
    y_pool = jnp.concatenate(pooled, axis=1)

    mix = _dot(y_ref[0], wo_ref[0:SSD_W, :]) + _dot(y_pool, wo_ref[SSD_W:SSD_W + POOL_W, :])
    x1 = x_ref[0] + g1_ref[0] * mix
    x1_ref[0] = x1

    ms = jnp.mean(x1 * x1, axis=-1, keepdims=True)
    h = x1 * lax.rsqrt(ms + RMS_EPS) * nw_ref[...]
    h = h * (1.0 + sc_ref[0]) + sh_ref[0]
    h_ref[0] = h

    h0, h1 = _split2(h)
    r0, r1 = _split2(rw_ref[...])
    logits = _dot(h0, r0) + _dot(h0, r1) + _dot(h1, r0) + rb_ref[...]
    lane = lax.broadcasted_iota(jnp.int32, (tm, LANES), 1)
    lane_f = lane.astype(F32)
    work = jnp.where(lane < N_EXPERTS, logits, NEG_BIG)
    vals, hots = [], []
    for _ in range(TOP_K):
        m = jnp.max(work, axis=-1, keepdims=True)
        first_idx = jnp.min(jnp.where(work == m, lane_f, float(LANES)), axis=-1, keepdims=True)
        hot = lane_f == first_idx
        vals.append(m)
        hots.append(hot)
        work = jnp.where(hot, 2.0 * NEG_BIG, work)
    exps = [jnp.exp(v - vals[0]) for v in vals]
    denom = exps[0] + exps[1] + exps[2] + exps[3]

    onehot = jnp.zeros((tm, LANES), F32)
    for hot in hots:
        onehot = onehot + jnp.where(hot, 1.0, 0.0)
    before = _dot(tri_ref[...], onehot.astype(BF16)) + base_ref[0:1, :]
    packed = jnp.zeros((tm, LANES), F32)
    for k in range(TOP_K):
        idx_k = jnp.sum(jnp.where(hots[k], lane_f, 0.0), axis=-1, keepdims=True)
        rank_k = jnp.sum(jnp.where(hots[k], before, 0.0), axis=-1, keepdims=True)
        packed = jnp.where(lane == k, idx_k, packed)
        packed = jnp.where(lane == TOP_K + k, rank_k, packed)
        packed = jnp.where(lane == 2 * TOP_K + k, exps[k] / denom, packed)
    rt_ref[0] = packed
    new_base = base_ref[0:1, :] + jnp.sum(onehot, axis=0, keepdims=True)
    base_ref[...] = jnp.broadcast_to(new_base, base_ref.shape)
    cnt_ref[...] = jnp.broadcast_to(new_base, cnt_ref.shape)


def _mix(y, u, x, pool_a, pool_cnt, pool_w, pool_scale, w_out, g1, norm_w, shift, scale,
         router_w, router_b, tri):
    b, l, d = x.shape
    tm = MIX_TM
    tok = lambda n: pl.BlockSpec((1, tm, n), lambda i, j: (i, j, 0))
    per_b = pl.BlockSpec((1, 1, d), lambda i, j: (i, 0, 0))
    full = lambda a: pl.BlockSpec(a.shape, lambda i, j: (0,) * a.ndim)
    cnt_spec = pl.BlockSpec((SUBLANES, LANES), lambda i, j: (0, 0))
    return pl.pallas_call(
        _mix_kernel,
        grid=(b, l // tm),
        in_specs=[tok(SSD_W), tok(POOL_W), tok(d), full(pool_a), full(pool_cnt), full(pool_w),
                  full(pool_scale), full(w_out), per_b, full(norm_w), per_b, per_b,
                  full(router_w), full(router_b), full(tri)],
        out_specs=[tok(d), tok(d), tok(LANES), cnt_spec],
        out_shape=[jax.ShapeDtypeStruct((b, l, d), F32),
                   jax.ShapeDtypeStruct((b, l, d), F32),
                   jax.ShapeDtypeStruct((b, l, LANES), F32),
                   jax.ShapeDtypeStruct((SUBLANES, LANES), F32)],
        scratch_shapes=[pltpu.VMEM((SUBLANES, LANES), F32)],
        compiler_params=_cparams(("arbitrary", "arbitrary")),
    )(y, u, x, pool_a, pool_cnt, pool_w, pool_scale, w_out, g1, norm_w, shift, scale,
      router_w, router_b, tri)


def _dispatch_kernel(fill_start_ref, fill_len_ref, dest_ref, h_ref, xs_ref, zero_ref, sem, fill_sem):
    tm = h_ref.shape[0]
    n_slots = xs_ref.shape[0]

    def body(t, carry):
        for k in range(TOP_K):
            d = dest_ref[t * TOP_K + k]
            pltpu.make_async_copy(h_ref.at[pl.ds(t, 1)], xs_ref.at[pl.ds(d, 1)],
                                  sem).start(priority=k % 2)
        return carry

    lax.fori_loop(0, tm, body, 0)

    @pl.when(pl.program_id(0) == pl.num_programs(0) - 1)
    def _():
        zero_ref[...] = jnp.zeros_like(zero_ref)
        bits = [1 << s for s in range(MOE_BLK.bit_length() - 2, SUBLANES.bit_length() - 2, -1)]

        def pad_copies(e, wait):
            start = fill_start_ref[e]
            n = fill_len_ref[e]
            head = jnp.minimum((-start) & (SUBLANES - 1), n)
            for r in range(SUBLANES - 1):
                @pl.when(r < head)
                def _():
                    cp = pltpu.make_async_copy(zero_ref.at[pl.ds(0, 1)],
                                               xs_ref.at[pl.ds(start + r, 1)], fill_sem)
                    cp.wait() if wait else cp.start()
            start = start + head
            n = n - head
            for bit in bits:
                @pl.when((n & bit) != 0)
                def _():
                    off = pl.multiple_of(start + (n & ~(2 * bit - 1)), SUBLANES)
                    cp = pltpu.make_async_copy(zero_ref.at[pl.ds(0, bit)],
                                               xs_ref.at[pl.ds(off, bit)], fill_sem)
                    cp.wait() if wait else cp.start()

        def tail_copy(j, wait):
            off = pl.multiple_of(fill_start_ref[N_EXPERTS] + j * MOE_BLK, MOE_BLK)
            cp = pltpu.make_async_copy(zero_ref, xs_ref.at[pl.ds(off, MOE_BLK)], fill_sem)
            cp.wait() if wait else cp.start()

        n_tail = (n_slots - fill_start_ref[N_EXPERTS]) // MOE_BLK
        for wait in (False, True):
            lax.fori_loop(0, N_EXPERTS, lambda e, cr: (pad_copies(e, wait), cr)[1], 0)
            lax.fori_loop(0, n_tail, lambda j, cr: (tail_copy(j, wait), cr)[1], 0)

    for _ in range(TOP_K):
        pltpu.make_async_copy(h_ref, xs_ref.at[pl.ds(0, tm)], sem).wait()


def _dispatch(fill_start, fill_len, dest_flat, h, n_slots):
    t, d = h.shape
    tm = ROW_TM
    grid_spec = pltpu.PrefetchScalarGridSpec(
        num_scalar_prefetch=2,
        grid=(t // tm,),
        in_specs=[pl.BlockSpec((tm * TOP_K,), lambda i, fs, fl: (i,), memory_space=pltpu.SMEM),
                  pl.BlockSpec((tm, d), lambda i, fs, fl: (i, 0))],
        out_specs=pl.BlockSpec(memory_space=pl.ANY),
        scratch_shapes=[pltpu.VMEM((MOE_BLK, d), F32),
                        pltpu.SemaphoreType.DMA(()), pltpu.SemaphoreType.DMA(())],
    )
    return pl.pallas_call(
        _dispatch_kernel,
        grid_spec=grid_spec,
        out_shape=jax.ShapeDtypeStruct((n_slots, d), F32),
        compiler_params=_cparams(("arbitrary",)),
    )(fill_start, fill_len, dest_flat, h)


def _expert_kernel(be_ref, nused_ref, xs_ref, w1_ref, b1_ref, w2_ref, b2_ref, perm_ref, y_ref,
                   w1s_ref, w2s_ref):
    i = pl.program_id(0)
    prev = be_ref[jnp.maximum(i - 1, 0)]
    changed = (i == 0) | (be_ref[i] != prev)
    dff2 = w1_ref.shape[2]
    tile = 2 * LANES

    @pl.when(changed & (i < nused_ref[0]))
    def _():
        for j in range(dff2 // tile):
            wj = w1_ref[0, :, j * tile:(j + 1) * tile].astype(BF16)
            w1s_ref[:, j * tile:(j + 1) * tile] = _dot(wj, perm_ref[...]).astype(BF16)
        w2s_ref[...] = w2_ref[0].astype(BF16)

    @pl.when(i < nused_ref[0])
    def _():
        x = xs_ref[...].astype(BF16)
        acts = []
        for j in range(dff2 // tile):
            hb = _dot(x, w1s_ref[:, j * tile:(j + 1) * tile]) + b1_ref[0, :, j * tile:(j + 1) * tile]
            gp = jnp.minimum(hb[:, 0:LANES], SWIGLU_LIMIT)
            up = jnp.clip(hb[:, LANES:tile], -SWIGLU_LIMIT, SWIGLU_LIMIT)
            acts.append((gp * _sigmoid(SWIGLU_ALPHA * gp) * (up + 1.0)).astype(BF16))
        act = jnp.concatenate(acts, axis=1)
        y_ref[...] = _dot(act, w2s_ref[...]) + b2_ref[0]

    @pl.when(i >= nused_ref[0])
    def _():
        y_ref[...] = jnp.zeros_like(y_ref)


def _experts(block_exp, n_used, xs, w1, b1p, w2, b2, perm, n_blocks):
    blk = MOE_BLK
    d = xs.shape[1]
    dff2 = w1.shape[2]
    dff = w2.shape[1]

    def x_idx(i, be, nu):
        return (jnp.minimum(i, nu[0] - 1), 0)


    grid_spec = pltpu.PrefetchScalarGridSpec(
        num_scalar_prefetch=2,
        grid=(n_blocks,),
        in_specs=[pl.BlockSpec((blk, d), x_idx),
                  pl.BlockSpec((1, d, dff2), lambda i, be, nu: (be[i], 0, 0)),
                  pl.BlockSpec((1, 1, dff2), lambda i, be, nu: (be[i], 0, 0)),
                  pl.BlockSpec((1, dff, d), lambda i, be, nu: (be[i], 0, 0)),
                  pl.BlockSpec((1, 1, d), lambda i, be, nu: (be[i], 0, 0)),
                  pl.BlockSpec(perm.shape, lambda i, be, nu: (0, 0))],
        out_specs=pl.BlockSpec((blk, d), lambda i, be, nu: (i, 0)),
        scratch_shapes=[pltpu.VMEM((d, dff2), BF16), pltpu.VMEM((dff, d), BF16)],
    )
    return pl.pallas_call(
        _expert_kernel,
        grid_spec=grid_spec,
        out_shape=jax.ShapeDtypeStruct((n_blocks * blk, d), F32),
        compiler_params=_cparams(("arbitrary",)),
    )(block_exp, n_used, xs, w1, b1p, w2, b2, perm)


def _combine_kernel(dest_ref, dest_next_ref, x1_ref, rt_ref, g2_ref, fnw_ref, ys_ref, o_ref,
                    buf_ref, sem):
    tm = x1_ref.shape[1]
    step = pl.program_id(0) * pl.num_programs(1) + pl.program_id(1)
    n_steps = pl.num_programs(0) * pl.num_programs(1)
    slot = step % 2

    def gather(idx_ref, into):
        def body(t, carry):
            for k in range(TOP_K):
                d = idx_ref[t * TOP_K + k]
                pltpu.make_async_copy(ys_ref.at[pl.ds(d, 1)], buf_ref.at[into, k, pl.ds(t, 1)],
                                      sem.at[into]).start(priority=k % 2)
            return carry
        lax.fori_loop(0, tm, body, 0)

    @pl.when(step == 0)
    def _():
        gather(dest_ref, 0)

    @pl.when(step + 1 < n_steps)
    def _():
        gather(dest_next_ref, 1 - slot)

    for k in range(TOP_K):
        pltpu.make_async_copy(ys_ref.at[pl.ds(0, tm)], buf_ref.at[slot, k], sem.at[slot]).wait()
    rt = rt_ref[0]
    moe = buf_ref[slot, 0] * rt[:, 2 * TOP_K:2 * TOP_K + 1]
    for k in range(1, TOP_K):
        moe = moe + buf_ref[slot, k] * rt[:, 2 * TOP_K + k:2 * TOP_K + k + 1]
    x2 = x1_ref[0] + g2_ref[0] * moe
    ms = jnp.mean(x2 * x2, axis=-1, keepdims=True)
    o_ref[0] = x2 * lax.rsqrt(ms + RMS_EPS) * fnw_ref[...]


def _combine(dest_flat, x1, rt, g2, fnw, ys):
    b, l, d = x1.shape
    tm = ROW_TM
    per_l = l // tm
    tok = lambda n: pl.BlockSpec((1, tm, n), lambda i, j: (i, j, 0))
    last = b * per_l - 1
    return pl.pallas_call(
        _combine_kernel,
        grid=(b, per_l),
        in_specs=[pl.BlockSpec((tm * TOP_K,), lambda i, j: (i * per_l + j,), memory_space=pltpu.SMEM),
                  pl.BlockSpec((tm * TOP_K,), lambda i, j: (jnp.minimum(i * per_l + j + 1, last),),
                               memory_space=pltpu.SMEM),
                  tok(d), tok(LANES),
                  pl.BlockSpec((1, 1, d), lambda i, j: (i, 0, 0)),
                  pl.BlockSpec(fnw.shape, lambda i, j: (0, 0)),
                  pl.BlockSpec(memory_space=pl.ANY)],
        out_specs=tok(d),
        out_shape=jax.ShapeDtypeStruct((b, l, d), F32),
        scratch_shapes=[pltpu.VMEM((2, TOP_K, tm, d), F32), pltpu.SemaphoreType.DMA((2,))],
        compiler_params=_cparams(("arbitrary", "arbitrary")),
    )(dest_flat, dest_flat, x1, rt, g2, fnw, ys)


def _pool_constants():
    i = np.arange(POOL_TM)[:, None]
    j = np.arange(POOL_TM)[None, :]
    same_row = (i // GRID_W) == (j // GRID_W)
    mats, cnts = [], []
    for w in POOL_WINDOWS:
        band = same_row & (j - i >= -(w // 2)) & (j - i < w - w // 2)
        mats.append(band)
        cnts.append(np.broadcast_to(band.sum(axis=1, keepdims=True), (POOL_TM, POOL_GW)))
    return (jnp.asarray(np.stack(mats), BF16), jnp.asarray(np.stack(cnts), F32))


def _head_expand():
    e = np.zeros((LANES, SSD_W), np.float32)
    for h in range(SSD_HEADS):
        e[h, h * SSD_HEADDIM:(h + 1) * SSD_HEADDIM] = 1.0
    return jnp.asarray(e, BF16)


def _deinterleave_perm():
    n = 2 * LANES
    p = np.zeros((n, n), np.float32)
    for k in range(LANES):
        p[2 * k, k] = 1.0
        p[2 * k + 1, LANES + k] = 1.0
    return jnp.asarray(p, BF16)


def _strict_lower(n):
    return jnp.asarray(np.tril(np.ones((n, n), np.float32), -1), BF16)


def _pad_lanes(a, n):
    return jnp.pad(a, [(0, 0)] * (a.ndim - 1) + [(0, n - a.shape[-1])])


def kernel(x, c, ctx, c_ctx, w_mod, b_mod, norm1_w, norm2_w, w_in, conv_w, conv_b, dt_bias, a_log,
           d_skip, ssd_norm_w, pool_w, pool_scale, w_out, router_w, router_b, w1, b1, w2, b2,
           final_norm_w):
    depth = w_mod.shape[0]
    assert depth == 1, "single-layer problem"
    b, l, d = x.shape
    lc = ctx.shape[1]
    xbcdt = CONV_CH + 2 * SSD_HEADS

    mod_rows = 2 * SUBLANES
    cc = jnp.zeros((mod_rows, d), F32).at[0:b].set(c).at[b].set(c_ctx)
    mod = _modulation(cc, w_mod[0], b_mod[0])
    sh1, sc1, g1, sh2, sc2, g2 = [m.reshape(b, 1, d) for m in jnp.split(mod[0:b], 6, axis=-1)]
    csh1, csc1 = [jnp.broadcast_to(m.reshape(1, 1, d), (b, 1, d))
                  for m in jnp.split(mod[b:b + 1], 6, axis=-1)[0:2]]

    wi = w_in[0]
    wx = wi[:, 0:CONV_CH].astype(BF16)
    wd = jnp.concatenate([_pad_lanes(wi[:, CONV_CH:CONV_CH + SSD_HEADS], LANES),
                          _pad_lanes(wi[:, CONV_CH + SSD_HEADS:xbcdt], LANES)], axis=1).astype(BF16)
    wz = wi[:, xbcdt:xbcdt + SSD_W].astype(BF16)
    wp = wi[:, xbcdt + SSD_W:].astype(BF16)
    n1 = norm1_w[0].reshape(1, d)
    conv_w8 = jnp.pad(conv_w[0], ((0, SUBLANES - SSD_CONV), (0, 0)))
    conv_b1 = conv_b[0].reshape(1, CONV_CH)
    dtb = _pad_lanes(dt_bias[0], LANES).reshape(2, 1, LANES)
    alog = _pad_lanes(a_log[0], LANES).reshape(2, 1, LANES)
    dsk = jnp.repeat(d_skip[0], SSD_HEADDIM).reshape(1, SSD_W)
    snw = ssd_norm_w[0].reshape(1, SSD_W)
    expand = _head_expand()

    xbc_c, dt_c, z_c, _ = _inproj(ctx, n1, csh1, csc1, wx, wd, wz, wp, min(PROJ_TM, lc))
    zero_state = jnp.zeros((b, 2, SSD_GROUPS, SSD_STATE, GROUP_W), F32)
    _, ctx_states = _ssd(xbc_c, dt_c, z_c, conv_w8, conv_b1, dtb, alog, dsk, snw, expand, zero_state)

    xbc, dt, z, u_pool = _inproj(x, n1, sh1, sc1, wx, wd, wz, wp, PROJ_TM)
    y_ssd, _ = _ssd(xbc, dt, z, conv_w8, conv_b1, dtb, alog, dsk, snw, expand, ctx_states)

    pool_a, pool_cnt = _pool_constants()
    x1, h2, rt, cnt = _mix(
        y_ssd, u_pool, x, pool_a, pool_cnt, pool_w[0].astype(BF16), pool_scale[0].reshape(1, POOL_W),
        w_out[0].astype(BF16), g1, norm2_w[0].reshape(1, d), sh2, sc2,
        _pad_lanes(router_w[0], LANES), _pad_lanes(router_b[0].reshape(1, N_EXPERTS), LANES),
        _strict_lower(MIX_TM))

    t = b * l
    rt2 = rt.reshape(t, LANES)
    e_idx = rt2[:, 0:TOP_K].astype(jnp.int32)
    rank = rt2[:, TOP_K:2 * TOP_K].astype(jnp.int32)
    counts = cnt[0, 0:N_EXPERTS].astype(jnp.int32)
    padded = (counts + MOE_BLK - 1) // MOE_BLK * MOE_BLK
    pad_end = jnp.cumsum(padded)
    pad_start = pad_end - padded
    dest = (pad_start[e_idx] + rank).reshape(t * TOP_K)
    n_blocks = (t * TOP_K) // MOE_BLK + N_EXPERTS
    n_used = (pad_end[-1] // MOE_BLK).astype(jnp.int32).reshape(1)
    blk_start = jnp.minimum(jnp.arange(n_blocks, dtype=jnp.int32), n_used[0] - 1) * MOE_BLK
    block_exp = jnp.minimum(jnp.sum(blk_start[:, None] >= pad_end[None, :], axis=1),
                            N_EXPERTS - 1).astype(jnp.int32)

    fill_start = jnp.concatenate([pad_start + counts, pad_end[-1:]]).astype(jnp.int32)
    fill_len = jnp.concatenate([padded - counts, jnp.zeros((1,), jnp.int32)]).astype(jnp.int32)
    xs = _dispatch(fill_start, fill_len, dest, h2.reshape(t, d), n_blocks * MOE_BLK)
    dff2 = w1.shape[-1]
    b1p = jnp.concatenate(
        [b1[0].reshape(N_EXPERTS, dff2 // (2 * LANES), LANES, 2)[..., 0],
         b1[0].reshape(N_EXPERTS, dff2 // (2 * LANES), LANES, 2)[..., 1]], axis=-1
    ).reshape(N_EXPERTS, 1, dff2)
    ys = _experts(block_exp, n_used, xs, w1[0], b1p, w2[0], b2[0].reshape(N_EXPERTS, 1, d),
                  _deinterleave_perm(), n_blocks)
    return _combine(dest, x1, rt, g2, final_norm_w.reshape(1, d), ys)
```

```python
import functools

import numpy as np
import jax
import jax.numpy as jnp
from jax import lax
from jax.experimental import pallas as pl
from jax.experimental.pallas import tpu as pltpu

F32 = jnp.float32
BF16 = jnp.bfloat16

SSD_HEADDIM = 64
SSD_GROUPS = 4
SSD_HPG = 6
SSD_HEADS = SSD_GROUPS * SSD_HPG
SSD_STATE = 128
SSD_CONV = 5
SSD_CHUNK = 128
SSD_W = SSD_HEADS * SSD_HEADDIM
GROUP_W = SSD_HPG * SSD_HEADDIM
CONV_CH = SSD_W + 2 * SSD_GROUPS * SSD_STATE
POOL_WINDOWS = (2, 4, 8, 16)
POOL_GW = 128
POOL_W = POOL_GW * len(POOL_WINDOWS)
GRID_W = 64
N_EXPERTS = 32
TOP_K = 4
SWIGLU_ALPHA = 1.702
SWIGLU_LIMIT = 7.0
RMS_EPS = 1e-6

LANES = 128
SUBLANES = 8
VMEM_LIMIT_BYTES = 56 * 1024 * 1024

MOD_TN = 1024
PROJ_TM = 512
MIX_TM = 512
POOL_TM = 256
MOE_BLK = 512
DISP_TM = 256
NEG_BIG = -1e30

GATE_LANE = 2 * TOP_K
PACK_W = 512
XS_W = PACK_W + LANES
STAGE_CHUNK = 256
_STAGE_NEED = TOP_K * DISP_TM + N_EXPERTS * 2 * (SUBLANES - 1) + SUBLANES
STAGE_ROWS = -(-_STAGE_NEED // STAGE_CHUNK) * STAGE_CHUNK
RUN_BITS = (32, 16, 8, 4, 2, 1)
RUN_SMALL_BIT = 4
WAIT_BITS = (128, 64, 32, 16, 8, 4, 2, 1)


def _sigmoid(x):
    return 0.5 * jnp.tanh(0.5 * x) + 0.5


def _split2(a):
    hi = a.astype(BF16)
    mid = (a - hi.astype(F32)).astype(BF16)
    return hi, mid


def _split3(a):
    hi = a.astype(BF16)
    r = a - hi.astype(F32)
    mid = r.astype(BF16)
    lo = (r - mid.astype(F32)).astype(BF16)
    return hi, mid, lo


def _dot(a, b):
    return jnp.dot(a, b, preferred_element_type=F32)


def _dot_exact_rhs(a_f32, b_bf16, parts):
    pieces = _split3(a_f32) if parts == 3 else _split2(a_f32)
    out = _dot(pieces[0], b_bf16)
    for p in pieces[1:]:
        out = out + _dot(p, b_bf16)
    return out


def _dot_hi(a_f32, b_f32):
    a0, a1, a2 = _split3(a_f32)
    b0, b1, b2 = _split3(b_f32)
    out = _dot(a0, b0)
    out = out + _dot(a0, b1) + _dot(a1, b0)
    out = out + _dot(a1, b1) + _dot(a0, b2) + _dot(a2, b0)
    return out


def _cparams(sem):
    return pltpu.CompilerParams(dimension_semantics=sem, vmem_limit_bytes=VMEM_LIMIT_BYTES)


def _mod_kernel(c_ref, w_ref, b_ref, o_ref):
    c = c_ref[...]
    s = c * _sigmoid(c)
    o_ref[...] = _dot_hi(s, w_ref[...]) + b_ref[...]


def _modulation(cc, w_mod, b_mod):
    rows, d = cc.shape
    n = w_mod.shape[1]
    return pl.pallas_call(
        _mod_kernel,
        grid=(n // MOD_TN,),
        in_specs=[pl.BlockSpec((rows, d), lambda j: (0, 0)),
                  pl.BlockSpec((d, MOD_TN), lambda j: (0, j)),
                  pl.BlockSpec((1, MOD_TN), lambda j: (0, j))],
        out_specs=pl.BlockSpec((rows, MOD_TN), lambda j: (0, j)),
        out_shape=jax.ShapeDtypeStruct((rows, n), F32),
        compiler_params=_cparams(("arbitrary",)),
    )(cc, w_mod, b_mod.reshape(1, n))


def _inproj_kernel(x_ref, nw_ref, sh_ref, sc_ref, wx_ref, wd_ref, wz_ref, wp_ref,
                   xbc_ref, dt_ref, z_ref, pool_ref):
    x = x_ref[0]
    ms = jnp.mean(x * x, axis=-1, keepdims=True)
    h = x * lax.rsqrt(ms + RMS_EPS) * nw_ref[...]
    h = h * (1.0 + sc_ref[0]) + sh_ref[0]
    hb = h.astype(BF16)
    xbc_ref[0] = _dot(hb, wx_ref[...])
    dt_ref[0] = _dot(hb, wd_ref[...])
    z_ref[0] = _dot(hb, wz_ref[...])
    pool_ref[0] = _dot(hb, wp_ref[...])


def _inproj(x, norm_w, shift, scale, wx, wd, wz, wp, tm):
    b, l, d = x.shape
    full = lambda a: pl.BlockSpec(a.shape, lambda i, j: (0, 0))
    tok = lambda n: pl.BlockSpec((1, tm, n), lambda i, j: (i, j, 0))
    per_b = pl.BlockSpec((1, 1, d), lambda i, j: (i, 0, 0))
    return pl.pallas_call(
        _inproj_kernel,
        grid=(b, l // tm),
        in_specs=[tok(d), full(norm_w), per_b, per_b, full(wx), full(wd), full(wz), full(wp)],
        out_specs=[tok(CONV_CH), tok(2 * LANES), tok(SSD_W), tok(POOL_W)],
        out_shape=[jax.ShapeDtypeStruct((b, l, CONV_CH), F32),
                   jax.ShapeDtypeStruct((b, l, 2 * LANES), F32),
                   jax.ShapeDtypeStruct((b, l, SSD_W), F32),
                   jax.ShapeDtypeStruct((b, l, POOL_W), F32)],
        compiler_params=_cparams(("arbitrary", "arbitrary")),
    )(x, norm_w, shift, scale, wx, wd, wz, wp)


def _ssd_kernel(nc, xbc_ref, prev_ref, next_ref, dt_ref, z_ref, cw_ref, cb_ref, dtb_ref, alog_ref,
                dsk_ref, nw_ref, exp_ref, init_ref, y_ref, fin_ref,
                act_ref, ybuf_ref, yf_ref, st_ref):
    ch = SSD_CHUNK
    ph = pl.program_id(1)
    c = pl.program_id(2)
    ci = c + ph * (nc - 1 - 2 * c)

    @pl.when(c == 0)
    def _():
        st_ref[...] = init_ref[0, ph]

    @pl.when(ph == 0)
    def _():
        _conv_silu(nc, ci, xbc_ref, prev_ref, next_ref, cw_ref, cb_ref, act_ref)

    _ssd_scan(nc, ph, c, ci, dt_ref, z_ref, dtb_ref, alog_ref, dsk_ref, nw_ref, exp_ref, y_ref, fin_ref,
              act_ref, ybuf_ref, yf_ref, st_ref)


def _conv_silu(nc, ci, xbc_ref, prev_ref, next_ref, cw_ref, cb_ref, act_ref):
    ch = SSD_CHUNK
    has_prev = ci > 0
    has_next = ci < nc - 1
    row8 = lax.broadcasted_iota(jnp.int32, (SUBLANES, LANES), 0)
    half = SSD_CONV // 2
    for j in range(CONV_CH // LANES):
        cols = slice(j * LANES, (j + 1) * LANES)
        tiles = ([jnp.where(has_prev, prev_ref[0, :, cols], 0.0)]
                 + [xbc_ref[0, i * SUBLANES:(i + 1) * SUBLANES, cols] for i in range(ch // SUBLANES)]
                 + [jnp.where(has_next, next_ref[0, :, cols], 0.0)])
        n_t = ch // SUBLANES
        acc = [cb_ref[:, cols] + cw_ref[half:half + 1, cols] * tiles[i + 1] for i in range(n_t)]
        for s in range(1, half + 1):
            rot = [pltpu.roll(tl, s, axis=0) for tl in tiles[0:n_t + 1]]
            wk = cw_ref[half - s:half - s + 1, cols]
            for i in range(n_t):
                acc[i] = acc[i] + wk * jnp.where(row8 < s, rot[i], rot[i + 1])
            rot = [pltpu.roll(tl, SUBLANES - s, axis=0) for tl in tiles[1:n_t + 2]]
            wk = cw_ref[half + s:half + s + 1, cols]
            for i in range(n_t):
                acc[i] = acc[i] + wk * jnp.where(row8 >= SUBLANES - s, rot[i + 1], rot[i])
        silu = [0.5 * v * jnp.tanh(0.5 * v) + 0.5 * v for v in acc]
        act_ref[ci, :, cols] = jnp.concatenate(silu, axis=0).astype(BF16)


def _ssd_scan(nc, ph, c, ci, dt_ref, z_ref, dtb_ref, alog_ref, dsk_ref, nw_ref, exp_ref, y_ref, fin_ref,
              act_ref, ybuf_ref, yf_ref, st_ref):
    ch = SSD_CHUNK
    dtr = dt_ref[0] + dtb_ref[0]
    dtv = jnp.maximum(dtr, 0.0) + jnp.log1p(jnp.exp(-jnp.abs(dtr)))
    a = dtv * (-jnp.exp(alog_ref[0]))
    row = lax.broadcasted_iota(jnp.int32, (ch, ch), 0)
    col = lax.broadcasted_iota(jnp.int32, (ch, ch), 1)
    tmask = (row - col) * (1 - 2 * ph) >= 0
    tri = jnp.where(tmask, 1.0, 0.0).astype(BF16)
    cs = _dot_exact_rhs_left(tri, a)
    tot = jnp.where(ph == 0, cs[ch - 1:ch, :], cs[0:1, :])
    cs_t = cs.T
    e_cs = jnp.exp(cs)
    e_dec = jnp.exp(tot - cs)
    e_tot = jnp.exp(tot)
    expand = exp_ref[...]
    dt_x = _dot_exact_rhs(dtv, expand, 2)
    ecs_x = _dot_exact_rhs(e_cs, expand, 2)
    edec_x = _dot_exact_rhs(e_dec, expand, 2)
    etot_x = _dot_exact_rhs(jnp.broadcast_to(e_tot, (SUBLANES, LANES)), expand, 2)[0:1]

    lane = lax.broadcasted_iota(jnp.int32, (ch, LANES), 1)
    for g in range(SSD_GROUPS):
        gs = slice(g * GROUP_W, (g + 1) * GROUP_W)
        b_bf = act_ref[ci, :, SSD_W + g * SSD_STATE:SSD_W + (g + 1) * SSD_STATE]
        c_bf = act_ref[ci, :, SSD_W + (SSD_GROUPS + g) * SSD_STATE:
                       SSD_W + (SSD_GROUPS + g + 1) * SSD_STATE]
        cb = lax.dot_general(c_bf, b_bf, (((1,), (1,)), ((), ())), preferred_element_type=F32)
        xs = act_ref[ci, :, gs].astype(F32) * dt_x[:, gs]
        xs_bf = xs.astype(BF16)
        s_prev = st_ref[g]
        y_off = _dot(c_bf, s_prev.astype(BF16)) * ecs_x[:, gs]
        x_dec = (xs * edec_x[:, gs]).astype(BF16)
        st_ref[g] = s_prev * etot_x[:, gs] + lax.dot_general(
            b_bf, x_dec, (((0,), (0,)), ((), ())), preferred_element_type=F32)
        for q in range(SSD_HPG // 2):
            lmats = []
            for h in (g * SSD_HPG + 2 * q, g * SSD_HPG + 2 * q + 1):
                diff = cs[:, h:h + 1] - cs_t[h:h + 1, :]
                dec = jnp.exp(jnp.where(tmask, diff, NEG_BIG))
                lmats.append((dec * cb).astype(BF16))
            xp = xs_bf[:, q * LANES:(q + 1) * LANES]
            zero = jnp.zeros_like(xp)
            rhs = jnp.concatenate([jnp.where(lane < SSD_HEADDIM, xp, zero),
                                   jnp.where(lane >= SSD_HEADDIM, xp, zero)], axis=0)
            y_diag = _dot(jnp.concatenate(lmats, axis=1), rhs)
            ps = slice(g * GROUP_W + q * LANES, g * GROUP_W + (q + 1) * LANES)
            ybuf_ref[:, ps] = y_diag + y_off[:, q * LANES:(q + 1) * LANES]

    @pl.when(ph == 0)
    def _():
        yf_ref[ci] = ybuf_ref[...].astype(BF16)

    @pl.when(ph == 1)
    def _():
        yt = (yf_ref[ci].astype(F32) + ybuf_ref[...]
              + act_ref[ci, :, 0:SSD_W].astype(F32) * dsk_ref[...])
        zz = z_ref[0]
        gt = yt * (zz * _sigmoid(zz))
        ms = jnp.mean(gt * gt, axis=-1, keepdims=True)
        y_ref[0] = (gt * lax.rsqrt(ms + RMS_EPS) * nw_ref[...]).astype(y_ref.dtype)

    @pl.when(c == nc - 1)
    def _():
        fin_ref[0, ph] = st_ref[...]


def _dot_exact_rhs_left(sel_bf16, a_f32):
    hi, mid, lo = _split3(a_f32)
    return _dot(sel_bf16, hi) + _dot(sel_bf16, mid) + _dot(sel_bf16, lo)


def _ssd(xbc, dt, z, conv_w8, conv_b, dt_bias, a_log, d_skip_x, norm_w, expand, init):
    b, l, _ = xbc.shape
    ch = SSD_CHUNK
    nc = l // ch
    rows8 = ch // SUBLANES

    def cidx(ph, c):
        return c + ph * (nc - 1 - 2 * c)

    def xidx(ph, c):
        return jnp.where(ph == 0, c, nc - 1)

    def out_idx(ph, c):
        return jnp.where(ph == 0, nc - 1, nc - 1 - c)

    full2 = lambda a: pl.BlockSpec(a.shape, lambda i, ph, c: (0, 0))
    st_spec = pl.BlockSpec((1, 2, SSD_GROUPS, SSD_STATE, GROUP_W), lambda i, ph, c: (i, 0, 0, 0, 0))
    in_specs = [
        pl.BlockSpec((1, ch, CONV_CH), lambda i, ph, c: (i, xidx(ph, c), 0)),
        pl.BlockSpec((1, SUBLANES, CONV_CH),
                     lambda i, ph, c: (i, jnp.maximum(xidx(ph, c) * rows8 - 1, 0), 0)),
        pl.BlockSpec((1, SUBLANES, CONV_CH),
                     lambda i, ph, c: (i, jnp.minimum((xidx(ph, c) + 1) * rows8, l // SUBLANES - 1), 0)),
        pl.BlockSpec((1, ch, LANES), lambda i, ph, c: (i, cidx(ph, c), ph)),
        pl.BlockSpec((1, ch, SSD_W), lambda i, ph, c: (i, out_idx(ph, c), 0)),
        full2(conv_w8), full2(conv_b),
        pl.BlockSpec((1, 1, LANES), lambda i, ph, c: (ph, 0, 0)),
        pl.BlockSpec((1, 1, LANES), lambda i, ph, c: (ph, 0, 0)),
        full2(d_skip_x), full2(norm_w), full2(expand), st_spec,
    ]
    out_specs = [pl.BlockSpec((1, ch, SSD_W), lambda i, ph, c: (i, out_idx(ph, c), 0)), st_spec]
    return pl.pallas_call(
        functools.partial(_ssd_kernel, nc),
        grid=(b, 2, nc),
        in_specs=in_specs,
        out_specs=out_specs,
        out_shape=[jax.ShapeDtypeStruct((b, l, SSD_W), BF16),
                   jax.ShapeDtypeStruct((b, 2, SSD_GROUPS, SSD_STATE, GROUP_W), F32)],
        scratch_shapes=[pltpu.VMEM((nc, ch, CONV_CH), BF16),
                        pltpu.VMEM((ch, SSD_W), F32),
                        pltpu.VMEM((nc, ch, SSD_W), BF16),
                        pltpu.VMEM((SSD_GROUPS, SSD_STATE, GROUP_W), F32)],
        compiler_params=_cparams(("arbitrary", "arbitrary", "arbitrary")),
    )(xbc, xbc, xbc, dt, z, conv_w8, conv_b, dt_bias, a_log, d_skip_x, norm_w, expand, init)


def _mix_kernel(y_ref, u_ref, x_ref, pa_ref, pcnt_ref, pw_ref, psc_ref, wo_ref, g1_ref,
                nw_ref, sh_ref, sc_ref, rw_ref, rb_ref,
                x1_ref, h_ref, rt_ref, tcnt_ref):
    tm = x_ref.shape[1]

    pooled = []
    for g in range(len(POOL_WINDOWS)):
        parts = []
        for r in range(tm // POOL_TM):
            u = u_ref[0, r * POOL_TM:(r + 1) * POOL_TM, g * POOL_GW:(g + 1) * POOL_GW]
            wsum = _dot_exact_rhs_left(pa_ref[g], u)
            parts.append((wsum / pcnt_ref[g] - u).astype(BF16))
        pg = jnp.concatenate(parts, axis=0)
        pooled.append((_dot(pg, pw_ref[g]) * psc_ref[:, g * POOL_GW:(g + 1) * POOL_GW]).astype(BF16))
    y_pool = jnp.concatenate(pooled, axis=1)

    mix = _dot(y_ref[0], wo_ref[0:SSD_W, :]) + _dot(y_pool, wo_ref[SSD_W:SSD_W + POOL_W, :])
    x1 = x_ref[0] + g1_ref[0] * mix
    x1_ref[0] = x1

    ms = jnp.mean(x1 * x1, axis=-1, keepdims=True)
    h = x1 * lax.rsqrt(ms + RMS_EPS) * nw_ref[...]
    h = h * (1.0 + sc_ref[0]) + sh_ref[0]
    h_ref[0] = h

    h0, h1 = _split2(h)
    r0, r1 = _split2(rw_ref[...])
    logits = _dot(h0, r0) + _dot(h0, r1) + _dot(h1, r0) + rb_ref[...]
    lane = lax.broadcasted_iota(jnp.int32, (tm, LANES), 1)
    lane_f = lane.astype(F32)
    work = jnp.where(lane < N_EXPERTS, logits, NEG_BIG)
    vals, hots = [], []
    for _ in range(TOP_K):
        m = jnp.max(work, axis=-1, keepdims=True)
        first_idx = jnp.min(jnp.where(work == m, lane_f, float(LANES)), axis=-1, keepdims=True)
        hot = lane_f == first_idx
        vals.append(m)
        hots.append(hot)
        work = jnp.where(hot, 2.0 * NEG_BIG, work)
    exps = [jnp.exp(v - vals[0]) for v in vals]
    denom = exps[0] + exps[1] + exps[2] + exps[3]

    onehot = jnp.zeros((tm, LANES), F32)
    for hot in hots:
        onehot = onehot + jnp.where(hot, 1.0, 0.0)
    packed = jnp.zeros((tm, LANES), F32)
    for k in range(TOP_K):
        idx_k = jnp.sum(jnp.where(hots[k], lane_f, 0.0), axis=-1, keepdims=True)
        packed = jnp.where(lane == k, idx_k, packed)
        packed = jnp.where(lane == GATE_LANE + k, exps[k] / denom, packed)
    rt_ref[0] = packed
    row8 = lax.broadcasted_iota(jnp.int32, (SUBLANES, LANES), 0)
    tcnt = jnp.zeros((SUBLANES, LANES), F32)
    for r in range(tm // DISP_TM):
        sub = jnp.sum(onehot[r * DISP_TM:(r + 1) * DISP_TM], axis=0, keepdims=True)
        tcnt = jnp.where(row8 == r, sub, tcnt)
    tcnt_ref[0] = tcnt


def _mix(y, u, x, pool_a, pool_cnt, pool_w, pool_scale, w_out, g1, norm_w, shift, scale,
         router_w, router_b):
    b, l, d = x.shape
    tm = MIX_TM
    per_l = l // tm
    tok = lambda n: pl.BlockSpec((1, tm, n), lambda i, j: (i, j, 0))
    per_b = pl.BlockSpec((1, 1, d), lambda i, j: (i, 0, 0))
    full = lambda a: pl.BlockSpec(a.shape, lambda i, j: (0,) * a.ndim)
    return pl.pallas_call(
        _mix_kernel,
        grid=(b, per_l),
        in_specs=[tok(SSD_W), tok(POOL_W), tok(d), full(pool_a), full(pool_cnt), full(pool_w),
                  full(pool_scale), full(w_out), per_b, full(norm_w), per_b, per_b,
                  full(router_w), full(router_b)],
        out_specs=[tok(d), tok(d), tok(LANES),
                   pl.BlockSpec((1, SUBLANES, LANES), lambda i, j: (i * per_l + j, 0, 0))],
        out_shape=[jax.ShapeDtypeStruct((b, l, d), F32),
                   jax.ShapeDtypeStruct((b, l, d), F32),
                   jax.ShapeDtypeStruct((b, l, LANES), F32),
                   jax.ShapeDtypeStruct((b * per_l, SUBLANES, LANES), F32)],
        compiler_params=_cparams(("arbitrary", "arbitrary")),
    )(y, u, x, pool_a, pool_cnt, pool_w, pool_scale, w_out, g1, norm_w, shift, scale,
      router_w, router_b)


def _group_copies(n_groups, src_ref, src_group, dst_ref, dst_group, sem, wait):
    def pieces(bits):
        for bit in bits:
            @pl.when((n_groups & bit) != 0)
            def _():
                done = n_groups & ~(2 * bit - 1)
                src = src_ref.at[pl.ds(pl.multiple_of((src_group + done) * SUBLANES, SUBLANES),
                                       bit * SUBLANES)]
                dst = dst_ref.at[pl.ds(pl.multiple_of((dst_group + done) * SUBLANES, SUBLANES),
                                       bit * SUBLANES)]
                cp = pltpu.make_async_copy(src, dst, sem)
                cp.wait() if wait else cp.start()

    split = RUN_BITS.index(RUN_SMALL_BIT)
    pl.when(n_groups >= 2 * RUN_SMALL_BIT)(lambda: pieces(RUN_BITS[:split]))
    pieces(RUN_BITS[split:])


def _wait_groups(n_groups, src_ref, dst_ref, sem):
    for bit in WAIT_BITS:
        @pl.when((n_groups & bit) != 0)
        def _():
            rows = bit * SUBLANES
            pltpu.make_async_copy(src_ref.at[pl.ds(0, rows)], dst_ref.at[pl.ds(0, rows)], sem).wait()


def _stage_positions(rt, base_row, tri):
    tm = rt.shape[0]
    lane_f = lax.broadcasted_iota(jnp.int32, (tm, LANES), 1).astype(F32)
    hots = [lane_f == rt[:, k:k + 1] for k in range(TOP_K)]
    onehot = jnp.zeros((tm, LANES), F32)
    for hot in hots:
        onehot = onehot + jnp.where(hot, 1.0, 0.0)
    pos = _dot(tri, onehot.astype(BF16)) + base_row
    return [jnp.sum(jnp.where(hot, pos, 0.0), axis=-1, keepdims=True) for hot in hots]


def _dispatch_kernel(o_ref, f_ref, fl_ref, rem_ref, hg_ref, tot_ref, fs_ref, flen_ref,
                     h_ref, rt_ref, sb_ref, tri_ref, xs_ref,
                     stg_ref, tails_ref, zero_ref, sem, fill_sem):
    i = pl.program_id(0)
    tm = h_ref.shape[0]
    n_slots = xs_ref.shape[0]
    base = i * N_EXPERTS
    hi_mask = jnp.int32(-65536)

    @pl.when(i == 0)
    def _():
        tails_ref[...] = jnp.zeros_like(tails_ref)

    rt = rt_ref[...]
    pos = _stage_positions(rt, sb_ref[0], tri_ref[...])
    lane = lax.broadcasted_iota(jnp.int32, (tm, LANES), 1)
    cols = jnp.zeros((tm, LANES), F32)
    for k in range(TOP_K):
        cols = jnp.where(lane == k, pos[k], cols)
        cols = jnp.where(lane == TOP_K + k, rt[:, GATE_LANE + k:GATE_LANE + k + 1], cols)
    rows = cols.T

    hb = h_ref[...].astype(BF16)
    half = PACK_W
    for rc in range(STAGE_ROWS // STAGE_CHUNK):
        r_lo = rc * STAGE_CHUNK
        row_f = (lax.broadcasted_iota(jnp.int32, (STAGE_CHUNK, tm), 0) + r_lo).astype(F32)
        sel = jnp.zeros((STAGE_CHUNK, tm), F32)
        gsel = jnp.zeros((STAGE_CHUNK, tm), F32)
        for k in range(TOP_K):
            hit = row_f == rows[k:k + 1, :]
            sel = jnp.where(hit, 1.0, sel)
            gsel = jnp.where(hit, rows[TOP_K + k:TOP_K + k + 1, :], gsel)
        gate = jnp.sum(gsel, axis=-1, keepdims=True)
        moved = _dot(sel.astype(BF16), hb)
        lo = pltpu.bitcast(moved[:, 0:half], jnp.int32)
        hi = pltpu.bitcast(moved[:, half:2 * half], jnp.int32)
        stg_ref[r_lo:r_lo + STAGE_CHUNK, 0:half] = lax.shift_right_logical(lo, 16) | (hi & hi_mask)
        stg_ref[r_lo:r_lo + STAGE_CHUNK, half:half + LANES] = pltpu.bitcast(
            jnp.broadcast_to(gate, (STAGE_CHUNK, LANES)), jnp.int32)

    def carry_tails(e, cr):
        first = pl.multiple_of(o_ref[base + e] * SUBLANES, SUBLANES)
        stg_ref[pl.ds(first, SUBLANES), :] = stg_ref[pl.ds(first, SUBLANES), :] | tails_ref[e]
        part = pl.multiple_of((o_ref[base + e] + f_ref[base + e]) * SUBLANES, SUBLANES)
        tails_ref[e] = jnp.where(rem_ref[base + e] > 0, stg_ref[pl.ds(part, SUBLANES), :], 0)
        return cr

    lax.fori_loop(0, N_EXPERTS, carry_tails, 0)

    def run_copies(e, cr):
        _group_copies(fl_ref[base + e], stg_ref, o_ref[base + e], xs_ref, hg_ref[base + e], sem, False)
        return cr

    lax.fori_loop(0, N_EXPERTS, run_copies, 0)

    @pl.when(i == pl.num_programs(0) - 1)
    def _():
        zero_ref[...] = jnp.zeros_like(zero_ref)
        tail_group = fs_ref[N_EXPERTS]
        n_tail = (n_slots // SUBLANES - tail_group) // (MOE_BLK // SUBLANES)

        def tail_copy(j, wait):
            off = pl.multiple_of(tail_group * SUBLANES + j * MOE_BLK, MOE_BLK)
            cp = pltpu.make_async_copy(zero_ref, xs_ref.at[pl.ds(off, MOE_BLK)], fill_sem)
            cp.wait() if wait else cp.start()

        for wait in (False, True):
            def pad_body(e, cr):
                _group_copies(flen_ref[e], zero_ref, 0, xs_ref, fs_ref[e], fill_sem, wait)
                return cr
            lax.fori_loop(0, N_EXPERTS, pad_body, 0)
            lax.fori_loop(0, n_tail, lambda j, cr: (tail_copy(j, wait), cr)[1], 0)

    _wait_groups(tot_ref[i], stg_ref, xs_ref, sem)


def _dispatch(tables, fill_groups, fill_len_groups, h, rt, stage_base, n_slots):
    t, d = h.shape
    tm = DISP_TM
    grid_spec = pltpu.PrefetchScalarGridSpec(
        num_scalar_prefetch=len(tables) + 2,
        grid=(t // tm,),
        in_specs=[pl.BlockSpec((tm, d), lambda i, *_: (i, 0)),
                  pl.BlockSpec((tm, LANES), lambda i, *_: (i, 0)),
                  pl.BlockSpec((1, 1, LANES), lambda i, *_: (i, 0, 0)),
                  pl.BlockSpec((tm, tm), lambda i, *_: (0, 0))],
        out_specs=pl.BlockSpec(memory_space=pl.ANY),
        scratch_shapes=[pltpu.VMEM((STAGE_ROWS, XS_W), jnp.int32),
                        pltpu.VMEM((N_EXPERTS, SUBLANES, XS_W), jnp.int32),
                        pltpu.VMEM((MOE_BLK, XS_W), jnp.int32),
                        pltpu.SemaphoreType.DMA(()), pltpu.SemaphoreType.DMA(())],
    )
    return pl.pallas_call(
        _dispatch_kernel,
        grid_spec=grid_spec,
        out_shape=jax.ShapeDtypeStruct((n_slots, XS_W), jnp.int32),
        compiler_params=_cparams(("arbitrary",)),
    )(*tables, fill_groups, fill_len_groups, h, rt, stage_base, _strict_lower(tm))


def _expert_kernel(be_ref, nused_ref, xs_ref, w1_ref, b1_ref, w2_ref, b2_ref, perm_ref, y_ref,
                   w1s_ref, w2s_ref):
    i = pl.program_id(0)
    prev = be_ref[jnp.maximum(i - 1, 0)]
    changed = (i == 0) | (be_ref[i] != prev)
    dff2 = w1_ref.shape[2]
    tile = 2 * LANES

    @pl.when(changed & (i < nused_ref[0]))
    def _():
        for j in range(dff2 // tile):
            wj = w1_ref[0, :, j * tile:(j + 1) * tile].astype(BF16)
            w1s_ref[:, j * tile:(j + 1) * tile] = _dot(wj, perm_ref[...]).astype(BF16)
        w2s_ref[...] = w2_ref[0].astype(BF16)

    @pl.when(i < nused_ref[0])
    def _():
        hi_mask = jnp.int32(-65536)
        words = xs_ref[:, 0:PACK_W]
        x = jnp.concatenate(
            [pltpu.bitcast(lax.shift_left(words, 16), F32).astype(BF16),
             pltpu.bitcast(words & hi_mask, F32).astype(BF16)], axis=1)
        gate = pltpu.bitcast(xs_ref[:, PACK_W:PACK_W + 1], F32)
        acts = []
        for j in range(dff2 // tile):
            hb = _dot(x, w1s_ref[:, j * tile:(j + 1) * tile]) + b1_ref[0, :, j * tile:(j + 1) * tile]
            gp = jnp.minimum(hb[:, 0:LANES], SWIGLU_LIMIT)
            up = jnp.clip(hb[:, LANES:tile], -SWIGLU_LIMIT, SWIGLU_LIMIT)
            acts.append((gp * _sigmoid(SWIGLU_ALPHA * gp) * (up + 1.0)).astype(BF16))
        act = jnp.concatenate(acts, axis=1)
        y = (_dot(act, w2s_ref[...]) + b2_ref[0]) * gate
        lo = pltpu.bitcast(y[:, 0:PACK_W].astype(BF16).astype(F32), jnp.int32)
        hi = pltpu.bitcast(y[:, PACK_W:2 * PACK_W].astype(BF16).astype(F32), jnp.int32)
        y_ref[...] = lax.shift_right_logical(lo, 16) | (hi & hi_mask)

    @pl.when(i >= nused_ref[0])
    def _():
        y_ref[...] = jnp.zeros_like(y_ref)


def _experts(block_exp, n_used, xs, w1, b1p, w2, b2, perm, n_blocks):
    blk = MOE_BLK
    d = w1.shape[1]
    dff2 = w1.shape[2]
    dff = w2.shape[1]

    def x_idx(i, be, nu):
        return (jnp.minimum(i, nu[0] - 1), 0)

    grid_spec = pltpu.PrefetchScalarGridSpec(
        num_scalar_prefetch=2,
        grid=(n_blocks,),
        in_specs=[pl.BlockSpec((blk, XS_W), x_idx),
                  pl.BlockSpec((1, d, dff2), lambda i, be, nu: (be[i], 0, 0)),
                  pl.BlockSpec((1, 1, dff2), lambda i, be, nu: (be[i], 0, 0)),
                  pl.BlockSpec((1, dff, d), lambda i, be, nu: (be[i], 0, 0)),
                  pl.BlockSpec((1, 1, d), lambda i, be, nu: (be[i], 0, 0)),
                  pl.BlockSpec(perm.shape, lambda i, be, nu: (0, 0))],
        out_specs=pl.BlockSpec((blk, PACK_W), lambda i, be, nu: (i, 0)),
        scratch_shapes=[pltpu.VMEM((d, dff2), BF16), pltpu.VMEM((dff, d), BF16)],
    )
    return pl.pallas_call(
        _expert_kernel,
        grid_spec=grid_spec,
        out_shape=jax.ShapeDtypeStruct((n_blocks * blk, PACK_W), jnp.int32),
        compiler_params=_cparams(("arbitrary",)),
    )(block_exp, n_used, xs, w1, b1p, w2, b2, perm)


def _combine_kernel(o_ref, g_ref, hg_ref, tot_ref, x1_ref, rt_ref, sb_ref, tri_ref, g2_ref, fnw_ref,
                    ys_ref, out_ref, stg_ref, sem):
    tm = x1_ref.shape[1]
    step = pl.program_id(0) * pl.num_programs(1) + pl.program_id(1)
    n_steps = pl.num_programs(0) * pl.num_programs(1)
    slot = step % 2
    hi_mask = jnp.int32(-65536)

    def fetch(tile, into):
        def body(e, cr):
            _group_copies(g_ref[tile * N_EXPERTS + e], ys_ref, hg_ref[tile * N_EXPERTS + e],
                          stg_ref.at[into], o_ref[tile * N_EXPERTS + e], sem.at[into], False)
            return cr
        lax.fori_loop(0, N_EXPERTS, body, 0)

    @pl.when(step == 0)
    def _():
        stg_ref[...] = jnp.zeros_like(stg_ref)
        fetch(0, 0)

    @pl.when(step + 1 < n_steps)
    def _():
        fetch(step + 1, 1 - slot)

    _wait_groups(tot_ref[step], ys_ref, stg_ref.at[slot], sem.at[slot])

    pos = _stage_positions(rt_ref[0], sb_ref[0], tri_ref[...])
    col_f = lax.broadcasted_iota(jnp.int32, (tm, STAGE_ROWS), 1).astype(F32)
    sel = jnp.zeros((tm, STAGE_ROWS), F32)
    for k in range(TOP_K):
        sel = jnp.where(col_f == pos[k], 1.0, sel)
    sel = sel.astype(BF16)
    words = stg_ref[slot]
    lo = pltpu.bitcast(lax.shift_left(words, 16), F32).astype(BF16)
    hi = pltpu.bitcast(words & hi_mask, F32).astype(BF16)
    moe = jnp.concatenate([_dot(sel, lo), _dot(sel, hi)], axis=1)
    x2 = x1_ref[0] + g2_ref[0] * moe
    ms = jnp.mean(x2 * x2, axis=-1, keepdims=True)
    out_ref[0] = x2 * lax.rsqrt(ms + RMS_EPS) * fnw_ref[...]


def _combine(tables, x1, rt, stage_base, g2, fnw, ys):
    b, l, d = x1.shape
    tm = DISP_TM
    per_l = l // tm
    tok = lambda n: pl.BlockSpec((1, tm, n), lambda i, j, *_: (i, j, 0))
    grid_spec = pltpu.PrefetchScalarGridSpec(
        num_scalar_prefetch=len(tables),
        grid=(b, per_l),
        in_specs=[tok(d), tok(LANES),
                  pl.BlockSpec((1, 1, LANES), lambda i, j, *_: (i * per_l + j, 0, 0)),
                  pl.BlockSpec((tm, tm), lambda i, j, *_: (0, 0)),
                  pl.BlockSpec((1, 1, d), lambda i, j, *_: (i, 0, 0)),
                  pl.BlockSpec(fnw.shape, lambda i, j, *_: (0, 0)),
                  pl.BlockSpec(memory_space=pl.ANY)],
        out_specs=tok(d),
        scratch_shapes=[pltpu.VMEM((2, STAGE_ROWS, PACK_W), jnp.int32),
                        pltpu.SemaphoreType.DMA((2,))],
    )
    return pl.pallas_call(
        _combine_kernel,
        grid_spec=grid_spec,
        out_shape=jax.ShapeDtypeStruct((b, l, d), F32),
        compiler_params=_cparams(("arbitrary", "arbitrary")),
    )(*tables, x1, rt, stage_base, _strict_lower(tm), g2, fnw, ys)


def _pool_constants():
    i = np.arange(POOL_TM)[:, None]
    j = np.arange(POOL_TM)[None, :]
    same_row = (i // GRID_W) == (j // GRID_W)
    mats, cnts = [], []
    for w in POOL_WINDOWS:
        band = same_row & (j - i >= -(w // 2)) & (j - i < w - w // 2)
        mats.append(band)
        cnts.append(np.broadcast_to(band.sum(axis=1, keepdims=True), (POOL_TM, POOL_GW)))
    return (jnp.asarray(np.stack(mats), BF16), jnp.asarray(np.stack(cnts), F32))


def _head_expand():
    e = np.zeros((LANES, SSD_W), np.float32)
    for h in range(SSD_HEADS):
        e[h, h * SSD_HEADDIM:(h + 1) * SSD_HEADDIM] = 1.0
    return jnp.asarray(e, BF16)


def _deinterleave_perm():
    n = 2 * LANES
    p = np.zeros((n, n), np.float32)
    for k in range(LANES):
        p[2 * k, k] = 1.0
        p[2 * k + 1, LANES + k] = 1.0
    return jnp.asarray(p, BF16)


def _strict_lower(n):
    return jnp.asarray(np.tril(np.ones((n, n), np.float32), -1), BF16)


def _pad_lanes(a, n):
    return jnp.pad(a, [(0, 0)] * (a.ndim - 1) + [(0, n - a.shape[-1])])


def kernel(x, c, ctx, c_ctx, w_mod, b_mod, norm1_w, norm2_w, w_in, conv_w, conv_b, dt_bias, a_log,
           d_skip, ssd_norm_w, pool_w, pool_scale, w_out, router_w, router_b, w1, b1, w2, b2,
           final_norm_w):
    depth = w_mod.shape[0]
    assert depth == 1, "single-layer problem"
    b, l, d = x.shape
    lc = ctx.shape[1]
    xbcdt = CONV_CH + 2 * SSD_HEADS

    mod_rows = 2 * SUBLANES
    cc = jnp.zeros((mod_rows, d), F32).at[0:b].set(c).at[b].set(c_ctx)
    mod = _modulation(cc, w_mod[0], b_mod[0])
    sh1, sc1, g1, sh2, sc2, g2 = [m.reshape(b, 1, d) for m in jnp.split(mod[0:b], 6, axis=-1)]
    csh1, csc1 = [jnp.broadcast_to(m.reshape(1, 1, d), (b, 1, d))
                  for m in jnp.split(mod[b:b + 1], 6, axis=-1)[0:2]]

    wi = w_in[0]
    wx = wi[:, 0:CONV_CH].astype(BF16)
    wd = jnp.concatenate([_pad_lanes(wi[:, CONV_CH:CONV_CH + SSD_HEADS], LANES),
                          _pad_lanes(wi[:, CONV_CH + SSD_HEADS:xbcdt], LANES)], axis=1).astype(BF16)
    wz = wi[:, xbcdt:xbcdt + SSD_W].astype(BF16)
    wp = wi[:, xbcdt + SSD_W:].astype(BF16)
    n1 = norm1_w[0].reshape(1, d)
    conv_w8 = jnp.pad(conv_w[0], ((0, SUBLANES - SSD_CONV), (0, 0)))
    conv_b1 = conv_b[0].reshape(1, CONV_CH)
    dtb = _pad_lanes(dt_bias[0], LANES).reshape(2, 1, LANES)
    alog = _pad_lanes(a_log[0], LANES).reshape(2, 1, LANES)
    dsk = jnp.repeat(d_skip[0], SSD_HEADDIM).reshape(1, SSD_W)
    snw = ssd_norm_w[0].reshape(1, SSD_W)
    expand = _head_expand()

    xbc_c, dt_c, z_c, _ = _inproj(ctx, n1, csh1, csc1, wx, wd, wz, wp, min(PROJ_TM, lc))
    zero_state = jnp.zeros((b, 2, SSD_GROUPS, SSD_STATE, GROUP_W), F32)
    _, ctx_states = _ssd(xbc_c, dt_c, z_c, conv_w8, conv_b1, dtb, alog, dsk, snw, expand, zero_state)

    xbc, dt, z, u_pool = _inproj(x, n1, sh1, sc1, wx, wd, wz, wp, PROJ_TM)
    y_ssd, _ = _ssd(xbc, dt, z, conv_w8, conv_b1, dtb, alog, dsk, snw, expand, ctx_states)

    pool_a, pool_cnt = _pool_constants()
    x1, h2, rt, tcnt = _mix(
        y_ssd, u_pool, x, pool_a, pool_cnt, pool_w[0].astype(BF16), pool_scale[0].reshape(1, POOL_W),
        w_out[0].astype(BF16), g1, norm2_w[0].reshape(1, d), sh2, sc2,
        _pad_lanes(router_w[0], LANES), _pad_lanes(router_b[0].reshape(1, N_EXPERTS), LANES))

    i32 = jnp.int32
    t = b * l
    n_tiles = t // DISP_TM
    tc = tcnt[:, 0:MIX_TM // DISP_TM, 0:N_EXPERTS].reshape(n_tiles, N_EXPERTS).astype(i32)
    counts = jnp.sum(tc, axis=0)
    run_start = jnp.cumsum(tc, axis=0) - tc
    padded = (counts + MOE_BLK - 1) // MOE_BLK * MOE_BLK
    pad_end = jnp.cumsum(padded)
    pad_start = pad_end - padded
    carried = run_start % SUBLANES
    span = carried + tc
    groups = (span + SUBLANES - 1) // SUBLANES
    full = span // SUBLANES
    stage_group = jnp.cumsum(groups, axis=1) - groups
    slot_group = (pad_start[None, :] + run_start - carried) // SUBLANES
    flush = full.at[n_tiles - 1].set(groups[n_tiles - 1])
    flat = lambda a: a.reshape(n_tiles * N_EXPERTS).astype(i32)
    stage_base = _pad_lanes((stage_group * SUBLANES + carried).astype(F32), LANES)
    stage_base = stage_base.reshape(n_tiles, 1, LANES)
    n_blocks = (t * TOP_K) // MOE_BLK + N_EXPERTS
    n_used = (pad_end[-1] // MOE_BLK).astype(i32).reshape(1)
    blk_start = jnp.minimum(jnp.arange(n_blocks, dtype=i32), n_used[0] - 1) * MOE_BLK
    block_exp = jnp.minimum(jnp.sum(blk_start[:, None] >= pad_end[None, :], axis=1),
                            N_EXPERTS - 1).astype(i32)
    written = (counts + SUBLANES - 1) // SUBLANES * SUBLANES
    fill_group = (jnp.concatenate([pad_start + written, pad_end[-1:]]) // SUBLANES).astype(i32)
    fill_groups = (jnp.concatenate([padded - written, jnp.zeros((1,), i32)]) // SUBLANES).astype(i32)

    xs = _dispatch((flat(stage_group), flat(full), flat(flush), flat(span % SUBLANES), flat(slot_group),
                    jnp.sum(flush, axis=1).astype(i32)),
                   fill_group, fill_groups, h2.reshape(t, d), rt.reshape(t, LANES), stage_base,
                   n_blocks * MOE_BLK)
    dff2 = w1.shape[-1]
    b1p = jnp.concatenate(
        [b1[0].reshape(N_EXPERTS, dff2 // (2 * LANES), LANES, 2)[..., 0],
         b1[0].reshape(N_EXPERTS, dff2 // (2 * LANES), LANES, 2)[..., 1]], axis=-1
    ).reshape(N_EXPERTS, 1, dff2)
    ys = _experts(block_exp, n_used, xs, w1[0], b1p, w2[0], b2[0].reshape(N_EXPERTS, 1, d),
                  _deinterleave_perm(), n_blocks)
    return _combine((flat(stage_group), flat(groups), flat(slot_group),
                     jnp.sum(groups, axis=1).astype(i32)), x1, rt, stage_base, g2,
                    final_norm_w.reshape(1, d), ys)
```

```python
import functools

import numpy as np
import jax
import jax.numpy as jnp
from jax import lax
from jax.experimental import pallas as pl
from jax.experimental.pallas import tpu as pltpu

F32 = jnp.float32
BF16 = jnp.bfloat16

SSD_HEADDIM = 64
SSD_GROUPS = 4
SSD_HPG = 6
SSD_HEADS = SSD_GROUPS * SSD_HPG
SSD_STATE = 128
SSD_CONV = 5
SSD_CHUNK = 128
SSD_W = SSD_HEADS * SSD_HEADDIM
GROUP_W = SSD_HPG * SSD_HEADDIM
CONV_CH = SSD_W + 2 * SSD_GROUPS * SSD_STATE
POOL_WINDOWS = (2, 4, 8, 16)
POOL_GW = 128
POOL_W = POOL_GW * len(POOL_WINDOWS)
GRID_W = 64
N_EXPERTS = 32
TOP_K = 4
SWIGLU_ALPHA = 1.702
SWIGLU_LIMIT = 7.0
RMS_EPS = 1e-6
LOG2_E = 1.4426950408889634

LANES = 128
SUBLANES = 8
VMEM_LIMIT_BYTES = 56 * 1024 * 1024

MOD_TN = 1024
PROJ_TM = 512
MIX_TM = 512
POOL_TM = 256
MOE_BLK = 512
DISP_TM = 256
NEG_BIG = -1e30

GATE_LANE = 2 * TOP_K
PACK_W = 512
XS_W = PACK_W + LANES
STAGE_CHUNK = 256
_STAGE_NEED = TOP_K * DISP_TM + N_EXPERTS * 2 * (SUBLANES - 1) + SUBLANES
STAGE_ROWS = -(-_STAGE_NEED // STAGE_CHUNK) * STAGE_CHUNK
RUN_BITS = (32, 16, 8, 4, 2, 1)
RUN_SMALL_BIT = 4
WAIT_BITS = (128, 64, 32, 16, 8, 4, 2, 1)


def _sigmoid(x):
    return 0.5 * jnp.tanh(0.5 * x) + 0.5


def _split2(a):
    hi = a.astype(BF16)
    mid = (a - hi.astype(F32)).astype(BF16)
    return hi, mid


def _split3(a):
    hi = a.astype(BF16)
    r = a - hi.astype(F32)
    mid = r.astype(BF16)
    lo = (r - mid.astype(F32)).astype(BF16)
    return hi, mid, lo


def _dot(a, b):
    return jnp.dot(a, b, preferred_element_type=F32)


def _dot_exact_rhs(a_f32, b_bf16, parts):
    pieces = _split3(a_f32) if parts == 3 else _split2(a_f32)
    out = _dot(pieces[0], b_bf16)
    for p in pieces[1:]:
        out = out + _dot(p, b_bf16)
    return out


def _dot_hi(a_f32, b_f32):
    a0, a1, a2 = _split3(a_f32)
    b0, b1, b2 = _split3(b_f32)
    out = _dot(a0, b0)
    out = out + _dot(a0, b1) + _dot(a1, b0)
    out = out + _dot(a1, b1) + _dot(a0, b2) + _dot(a2, b0)
    return out


def _cparams(sem):
    return pltpu.CompilerParams(dimension_semantics=sem, vmem_limit_bytes=VMEM_LIMIT_BYTES)


def _mod_kernel(c_ref, w_ref, b_ref, o_ref):
    c = c_ref[...]
    s = c * _sigmoid(c)
    o_ref[...] = _dot_hi(s, w_ref[...]) + b_ref[...]


def _modulation(cc, w_mod, b_mod):
    rows, d = cc.shape
    n = w_mod.shape[1]
    return pl.pallas_call(
        _mod_kernel,
        grid=(n // MOD_TN,),
        in_specs=[pl.BlockSpec((rows, d), lambda j: (0, 0)),
                  pl.BlockSpec((d, MOD_TN), lambda j: (0, j)),
                  pl.BlockSpec((1, MOD_TN), lambda j: (0, j))],
        out_specs=pl.BlockSpec((rows, MOD_TN), lambda j: (0, j)),
        out_shape=jax.ShapeDtypeStruct((rows, n), F32),
        compiler_params=_cparams(("arbitrary",)),
    )(cc, w_mod, b_mod.reshape(1, n))


def _inproj_kernel(x_ref, nw_ref, sh_ref, sc_ref, wx_ref, wd_ref, wz_ref, wp_ref,
                   xbc_ref, dt_ref, z_ref, pool_ref):
    x = x_ref[0]
    ms = jnp.mean(x * x, axis=-1, keepdims=True)
    h = x * lax.rsqrt(ms + RMS_EPS) * nw_ref[...]
    h = h * (1.0 + sc_ref[0]) + sh_ref[0]
    hb = h.astype(BF16)
    xbc_ref[0] = _dot(hb, wx_ref[...])
    dt_ref[0] = _dot(hb, wd_ref[...])
    z_ref[0] = _dot(hb, wz_ref[...])
    pool_ref[0] = _dot(hb, wp_ref[...])


def _inproj(x, norm_w, shift, scale, wx, wd, wz, wp, tm):
    b, l, d = x.shape
    full = lambda a: pl.BlockSpec(a.shape, lambda i, j: (0, 0))
    tok = lambda n: pl.BlockSpec((1, tm, n), lambda i, j: (i, j, 0))
    per_b = pl.BlockSpec((1, 1, d), lambda i, j: (i, 0, 0))
    return pl.pallas_call(
        _inproj_kernel,
        grid=(b, l // tm),
        in_specs=[tok(d), full(norm_w), per_b, per_b, full(wx), full(wd), full(wz), full(wp)],
        out_specs=[tok(CONV_CH), tok(2 * LANES), tok(SSD_W), tok(POOL_W)],
        out_shape=[jax.ShapeDtypeStruct((b, l, CONV_CH), F32),
                   jax.ShapeDtypeStruct((b, l, 2 * LANES), F32),
                   jax.ShapeDtypeStruct((b, l, SSD_W), F32),
                   jax.ShapeDtypeStruct((b, l, POOL_W), F32)],
        compiler_params=_cparams(("arbitrary", "arbitrary")),
    )(x, norm_w, shift, scale, wx, wd, wz, wp)


def _ssd_kernel(nc, xbc_ref, prev_ref, next_ref, dt_ref, z_ref, cw_ref, cb_ref, dtb_ref, alog_ref,
                dsk_ref, nw_ref, exp_ref, init_ref, y_ref, fin_ref,
                act_ref, ybuf_ref, yf_ref, st_ref):
    ch = SSD_CHUNK
    ph = pl.program_id(1)
    c = pl.program_id(2)
    ci = c + ph * (nc - 1 - 2 * c)

    @pl.when(c == 0)
    def _():
        st_ref[...] = init_ref[0, ph]

    @pl.when(ph == 0)
    def _():
        _conv_silu(nc, ci, xbc_ref, prev_ref, next_ref, cw_ref, cb_ref, act_ref)

    _ssd_scan(nc, ph, c, ci, dt_ref, z_ref, dtb_ref, alog_ref, dsk_ref, nw_ref, exp_ref, y_ref, fin_ref,
              act_ref, ybuf_ref, yf_ref, st_ref)


def _conv_silu(nc, ci, xbc_ref, prev_ref, next_ref, cw_ref, cb_ref, act_ref):
    ch = SSD_CHUNK
    has_prev = ci > 0
    has_next = ci < nc - 1
    row8 = lax.broadcasted_iota(jnp.int32, (SUBLANES, LANES), 0)
    half = SSD_CONV // 2
    for j in range(CONV_CH // LANES):
        cols = slice(j * LANES, (j + 1) * LANES)
        tiles = ([jnp.where(has_prev, prev_ref[0, :, cols], 0.0)]
                 + [xbc_ref[0, i * SUBLANES:(i + 1) * SUBLANES, cols] for i in range(ch // SUBLANES)]
                 + [jnp.where(has_next, next_ref[0, :, cols], 0.0)])
        n_t = ch // SUBLANES
        acc = [cb_ref[:, cols] + cw_ref[half:half + 1, cols] * tiles[i + 1] for i in range(n_t)]
        for s in range(1, half + 1):
            rot = [pltpu.roll(tl, s, axis=0) for tl in tiles[0:n_t + 1]]
            wk = cw_ref[half - s:half - s + 1, cols]
            for i in range(n_t):
                acc[i] = acc[i] + wk * jnp.where(row8 < s, rot[i], rot[i + 1])
            rot = [pltpu.roll(tl, SUBLANES - s, axis=0) for tl in tiles[1:n_t + 2]]
            wk = cw_ref[half + s:half + s + 1, cols]
            for i in range(n_t):
                acc[i] = acc[i] + wk * jnp.where(row8 >= SUBLANES - s, rot[i + 1], rot[i])
        silu = [0.5 * v * jnp.tanh(0.5 * v) + 0.5 * v for v in acc]
        act_ref[ci, :, cols] = jnp.concatenate(silu, axis=0).astype(BF16)


def _ssd_scan(nc, ph, c, ci, dt_ref, z_ref, dtb_ref, alog_ref, dsk_ref, nw_ref, exp_ref, y_ref, fin_ref,
              act_ref, ybuf_ref, yf_ref, st_ref):
    ch = SSD_CHUNK
    dtr = dt_ref[0] + dtb_ref[0]
    small = jnp.exp(-jnp.abs(dtr))
    one_plus = 1.0 + small
    log1p_small = jnp.where(one_plus == 1.0, small, jnp.log(one_plus) * (small / (one_plus - 1.0)))
    dtv = jnp.maximum(dtr, 0.0) + log1p_small
    a = dtv * (-jnp.exp(alog_ref[0]) * LOG2_E)
    row = lax.broadcasted_iota(jnp.int32, (ch, ch), 0)
    col = lax.broadcasted_iota(jnp.int32, (ch, ch), 1)
    tmask = (row - col) * (1 - 2 * ph) >= 0
    tri = jnp.where(tmask, 1.0, 0.0).astype(BF16)
    cs = _dot_exact_rhs_left(tri, a)
    tot = jnp.where(ph == 0, cs[ch - 1:ch, :], cs[0:1, :])
    cs_t = cs.T
    e_cs = jnp.exp2(cs)
    e_dec = jnp.exp2(tot - cs)
    e_tot = jnp.exp2(tot)
    expand = exp_ref[...]
    ecs_x = _dot(e_cs.astype(BF16), expand)
    wdec_x = _dot((dtv * e_dec).astype(BF16), expand).astype(BF16)
    etot_x = _dot_exact_rhs(jnp.broadcast_to(e_tot, (SUBLANES, LANES)), expand, 2)[0:1]
    dt_t = dtv.T

    lane = lax.broadcasted_iota(jnp.int32, (ch, LANES), 1)
    for g in range(SSD_GROUPS):
        gs = slice(g * GROUP_W, (g + 1) * GROUP_W)
        b_bf = act_ref[ci, :, SSD_W + g * SSD_STATE:SSD_W + (g + 1) * SSD_STATE]
        c_bf = act_ref[ci, :, SSD_W + (SSD_GROUPS + g) * SSD_STATE:
                       SSD_W + (SSD_GROUPS + g + 1) * SSD_STATE]
        cb = lax.dot_general(c_bf, b_bf, (((1,), (1,)), ((), ())), preferred_element_type=F32)
        x_bf = act_ref[ci, :, gs]
        s_prev = st_ref[g]
        y_off = _dot(c_bf, s_prev.astype(BF16)) * ecs_x[:, gs]
        x_dec = x_bf * wdec_x[:, gs]
        st_ref[g] = s_prev * etot_x[:, gs] + lax.dot_general(
            b_bf, x_dec, (((0,), (0,)), ((), ())), preferred_element_type=F32)
        for q in range(SSD_HPG // 2):
            lmats = []
            for h in (g * SSD_HPG + 2 * q, g * SSD_HPG + 2 * q + 1):
                diff = cs[:, h:h + 1] - cs_t[h:h + 1, :]
                dec = jnp.exp2(jnp.where(tmask, diff, NEG_BIG))
                lmats.append((dec * cb * dt_t[h:h + 1, :]).astype(BF16))
            xp = x_bf[:, q * LANES:(q + 1) * LANES]
            zero = jnp.zeros_like(xp)
            rhs = jnp.concatenate([jnp.where(lane < SSD_HEADDIM, xp, zero),
                                   jnp.where(lane >= SSD_HEADDIM, xp, zero)], axis=0)
            y_diag = _dot(jnp.concatenate(lmats, axis=1), rhs)
            ps = slice(g * GROUP_W + q * LANES, g * GROUP_W + (q + 1) * LANES)
            ybuf_ref[:, ps] = y_diag + y_off[:, q * LANES:(q + 1) * LANES]

    @pl.when(ph == 0)
    def _():
        yf_ref[ci] = ybuf_ref[...].astype(BF16)

    @pl.when(ph == 1)
    def _():
        yt = (yf_ref[ci].astype(F32) + ybuf_ref[...]
              + act_ref[ci, :, 0:SSD_W].astype(F32) * dsk_ref[...])
        zz = z_ref[0]
        gt = yt * (zz * _sigmoid(zz))
        ms = jnp.mean(gt * gt, axis=-1, keepdims=True)
        y_ref[0] = (gt * lax.rsqrt(ms + RMS_EPS) * nw_ref[...]).astype(y_ref.dtype)

    @pl.when(c == nc - 1)
    def _():
        fin_ref[0, ph] = st_ref[...]


def _dot_exact_rhs_left(sel_bf16, a_f32):
    hi, mid, lo = _split3(a_f32)
    return _dot(sel_bf16, hi) + _dot(sel_bf16, mid) + _dot(sel_bf16, lo)


def _ssd(xbc, dt, z, conv_w8, conv_b, dt_bias, a_log, d_skip_x, norm_w, expand, init):
    b, l, _ = xbc.shape
    ch = SSD_CHUNK
    nc = l // ch
    rows8 = ch // SUBLANES

    def cidx(ph, c):
        return c + ph * (nc - 1 - 2 * c)

    def xidx(ph, c):
        return jnp.where(ph == 0, c, nc - 1)

    def out_idx(ph, c):
        return jnp.where(ph == 0, nc - 1, nc - 1 - c)

    full2 = lambda a: pl.BlockSpec(a.shape, lambda i, ph, c: (0, 0))
    st_spec = pl.BlockSpec((1, 2, SSD_GROUPS, SSD_STATE, GROUP_W), lambda i, ph, c: (i, 0, 0, 0, 0))
    in_specs = [
        pl.BlockSpec((1, ch, CONV_CH), lambda i, ph, c: (i, xidx(ph, c), 0)),
        pl.BlockSpec((1, SUBLANES, CONV_CH),
                     lambda i, ph, c: (i, jnp.maximum(xidx(ph, c) * rows8 - 1, 0), 0)),
        pl.BlockSpec((1, SUBLANES, CONV_CH),
                     lambda i, ph, c: (i, jnp.minimum((xidx(ph, c) + 1) * rows8, l // SUBLANES - 1), 0)),
        pl.BlockSpec((1, ch, LANES), lambda i, ph, c: (i, cidx(ph, c), ph)),
        pl.BlockSpec((1, ch, SSD_W), lambda i, ph, c: (i, out_idx(ph, c), 0)),
        full2(conv_w8), full2(conv_b),
        pl.BlockSpec((1, 1, LANES), lambda i, ph, c: (ph, 0, 0)),
        pl.BlockSpec((1, 1, LANES), lambda i, ph, c: (ph, 0, 0)),
        full2(d_skip_x), full2(norm_w), full2(expand), st_spec,
    ]
    out_specs = [pl.BlockSpec((1, ch, SSD_W), lambda i, ph, c: (i, out_idx(ph, c), 0)), st_spec]
    return pl.pallas_call(
        functools.partial(_ssd_kernel, nc),
        grid=(b, 2, nc),
        in_specs=in_specs,
        out_specs=out_specs,
        out_shape=[jax.ShapeDtypeStruct((b, l, SSD_W), BF16),
                   jax.ShapeDtypeStruct((b, 2, SSD_GROUPS, SSD_STATE, GROUP_W), F32)],
        scratch_shapes=[pltpu.VMEM((nc, ch, CONV_CH), BF16),
                        pltpu.VMEM((ch, SSD_W), F32),
                        pltpu.VMEM((nc, ch, SSD_W), BF16),
                        pltpu.VMEM((SSD_GROUPS, SSD_STATE, GROUP_W), F32)],
        compiler_params=_cparams(("arbitrary", "arbitrary", "arbitrary")),
    )(xbc, xbc, xbc, dt, z, conv_w8, conv_b, dt_bias, a_log, d_skip_x, norm_w, expand, init)


def _mix_kernel(y_ref, u_ref, x_ref, pa_ref, pcnt_ref, pw_ref, psc_ref, wo_ref, g1_ref,
                nw_ref, sh_ref, sc_ref, rw_ref, rb_ref,
                x1_ref, h_ref, rt_ref, tcnt_ref):
    tm = x_ref.shape[1]

    pooled = []
    for g in range(len(POOL_WINDOWS)):
        parts = []
        for r in range(tm // POOL_TM):
            u = u_ref[0, r * POOL_TM:(r + 1) * POOL_TM, g * POOL_GW:(g + 1) * POOL_GW]
            wsum = _dot_exact_rhs_left(pa_ref[g], u)
            parts.append((wsum / pcnt_ref[g] - u).astype(BF16))
        pg = jnp.concatenate(parts, axis=0)
        pooled.append((_dot(pg, pw_ref[g]) * psc_ref[:, g * POOL_GW:(g + 1) * POOL_GW]).astype(BF16))
    y_pool = jnp.concatenate(pooled, axis=1)

    mix = _dot(y_ref[0], wo_ref[0:SSD_W, :]) + _dot(y_pool, wo_ref[SSD_W:SSD_W + POOL_W, :])
    x1 = x_ref[0] + g1_ref[0] * mix
    x1_ref[0] = x1

    ms = jnp.mean(x1 * x1, axis=-1, keepdims=True)
    h = x1 * lax.rsqrt(ms + RMS_EPS) * nw_ref[...]
    h = h * (1.0 + sc_ref[0]) + sh_ref[0]
    h_ref[0] = h

    h0, h1 = _split2(h)
    r0, r1 = _split2(rw_ref[...])
    logits = _dot(h0, r0) + _dot(h0, r1) + _dot(h1, r0) + rb_ref[...]
    lane = lax.broadcasted_iota(jnp.int32, (tm, LANES), 1)
    lane_f = lane.astype(F32)
    work = jnp.where(lane < N_EXPERTS, logits, NEG_BIG)
    vals, hots = [], []
    for _ in range(TOP_K):
        m = jnp.max(work, axis=-1, keepdims=True)
        first_idx = jnp.min(jnp.where(work == m, lane_f, float(LANES)), axis=-1, keepdims=True)
        hot = lane_f == first_idx
        vals.append(m)
        hots.append(hot)
        work = jnp.where(hot, 2.0 * NEG_BIG, work)
    exps = [jnp.exp(v - vals[0]) for v in vals]
    denom = exps[0] + exps[1] + exps[2] + exps[3]

    onehot = jnp.zeros((tm, LANES), F32)
    for hot in hots:
        onehot = onehot + jnp.where(hot, 1.0, 0.0)
    packed = jnp.zeros((tm, LANES), F32)
    for k in range(TOP_K):
        idx_k = jnp.sum(jnp.where(hots[k], lane_f, 0.0), axis=-1, keepdims=True)
        packed = jnp.where(lane == k, idx_k, packed)
        packed = jnp.where(lane == GATE_LANE + k, exps[k] / denom, packed)
    rt_ref[0] = packed
    row8 = lax.broadcasted_iota(jnp.int32, (SUBLANES, LANES), 0)
    tcnt = jnp.zeros((SUBLANES, LANES), F32)
    for r in range(tm // DISP_TM):
        sub = jnp.sum(onehot[r * DISP_TM:(r + 1) * DISP_TM], axis=0, keepdims=True)
        tcnt = jnp.where(row8 == r, sub, tcnt)
    tcnt_ref[0] = tcnt


def _mix(y, u, x, pool_a, pool_cnt, pool_w, pool_scale, w_out, g1, norm_w, shift, scale,
         router_w, router_b):
    b, l, d = x.shape
    tm = MIX_TM
    per_l = l // tm
    tok = lambda n: pl.BlockSpec((1, tm, n), lambda i, j: (i, j, 0))
    per_b = pl.BlockSpec((1, 1, d), lambda i, j: (i, 0, 0))
    full = lambda a: pl.BlockSpec(a.shape, lambda i, j: (0,) * a.ndim)
    return pl.pallas_call(
        _mix_kernel,
        grid=(b, per_l),
        in_specs=[tok(SSD_W), tok(POOL_W), tok(d), full(pool_a), full(pool_cnt), full(pool_w),
                  full(pool_scale), full(w_out), per_b, full(norm_w), per_b, per_b,
                  full(router_w), full(router_b)],
        out_specs=[tok(d), tok(d), tok(LANES),
                   pl.BlockSpec((1, SUBLANES, LANES), lambda i, j: (i * per_l + j, 0, 0))],
        out_shape=[jax.ShapeDtypeStruct((b, l, d), F32),
                   jax.ShapeDtypeStruct((b, l, d), F32),
                   jax.ShapeDtypeStruct((b, l, LANES), F32),
                   jax.ShapeDtypeStruct((b * per_l, SUBLANES, LANES), F32)],
        compiler_params=_cparams(("arbitrary", "arbitrary")),
    )(y, u, x, pool_a, pool_cnt, pool_w, pool_scale, w_out, g1, norm_w, shift, scale,
      router_w, router_b)


def _group_copies(n_groups, src_ref, src_group, dst_ref, dst_group, sem, wait):
    def pieces(bits):
        for bit in bits:
            @pl.when((n_groups & bit) != 0)
            def _():
                done = n_groups & ~(2 * bit - 1)
                src = src_ref.at[pl.ds(pl.multiple_of((src_group + done) * SUBLANES, SUBLANES),
                                       bit * SUBLANES)]
                dst = dst_ref.at[pl.ds(pl.multiple_of((dst_group + done) * SUBLANES, SUBLANES),
                                       bit * SUBLANES)]
                cp = pltpu.make_async_copy(src, dst, sem)
                cp.wait() if wait else cp.start()

    split = RUN_BITS.index(RUN_SMALL_BIT)
    pl.when(n_groups >= 2 * RUN_SMALL_BIT)(lambda: pieces(RUN_BITS[:split]))
    pieces(RUN_BITS[split:])


def _wait_groups(n_groups, src_ref, dst_ref, sem):
    for bit in WAIT_BITS:
        @pl.when((n_groups & bit) != 0)
        def _():
            rows = bit * SUBLANES
            pltpu.make_async_copy(src_ref.at[pl.ds(0, rows)], dst_ref.at[pl.ds(0, rows)], sem).wait()


def _stage_positions(rt, base_row, tri):
    tm = rt.shape[0]
    lane_f = lax.broadcasted_iota(jnp.int32, (tm, LANES), 1).astype(F32)
    hots = [lane_f == rt[:, k:k + 1] for k in range(TOP_K)]
    onehot = jnp.zeros((tm, LANES), F32)
    for hot in hots:
        onehot = onehot + jnp.where(hot, 1.0, 0.0)
    pos = _dot(tri, onehot.astype(BF16)) + base_row
    return [jnp.sum(jnp.where(hot, pos, 0.0), axis=-1, keepdims=True) for hot in hots]


def _dispatch_kernel(o_ref, f_ref, fl_ref, rem_ref, hg_ref, tot_ref, fs_ref, flen_ref,
                     h_ref, rt_ref, sb_ref, tri_ref, xs_ref,
                     stg2_ref, tails_ref, zero_ref, sem2, fill_sem):
    i = pl.program_id(0)
    tm = h_ref.shape[0]
    n_slots = xs_ref.shape[0]
    base = i * N_EXPERTS
    hi_mask = jnp.int32(-65536)
    slot = i % 2
    stg_ref = stg2_ref.at[slot]
    sem = sem2.at[slot]

    @pl.when(i == 0)
    def _():
        tails_ref[...] = jnp.zeros_like(tails_ref)

    rt = rt_ref[...]
    pos = _stage_positions(rt, sb_ref[0], tri_ref[...])
    lane = lax.broadcasted_iota(jnp.int32, (tm, LANES), 1)
    cols = jnp.zeros((tm, LANES), F32)
    for k in range(TOP_K):
        cols = jnp.where(lane == k, pos[k], cols)
        cols = jnp.where(lane == TOP_K + k, rt[:, GATE_LANE + k:GATE_LANE + k + 1], cols)
    rows = cols.T

    hb = h_ref[...].astype(BF16)
    half = PACK_W
    for rc in range(STAGE_ROWS // STAGE_CHUNK):
        r_lo = rc * STAGE_CHUNK
        row_f = (lax.broadcasted_iota(jnp.int32, (STAGE_CHUNK, tm), 0) + r_lo).astype(F32)
        sel = jnp.zeros((STAGE_CHUNK, tm), F32)
        gsel = jnp.zeros((STAGE_CHUNK, tm), F32)
        for k in range(TOP_K):
            hit = row_f == rows[k:k + 1, :]
            sel = jnp.where(hit, 1.0, sel)
            gsel = jnp.where(hit, rows[TOP_K + k:TOP_K + k + 1, :], gsel)
        gate = jnp.sum(gsel, axis=-1, keepdims=True)
        moved = _dot(sel.astype(BF16), hb)
        lo = pltpu.bitcast(moved[:, 0:half], jnp.int32)
        hi = pltpu.bitcast(moved[:, half:2 * half], jnp.int32)
        stg_ref[r_lo:r_lo + STAGE_CHUNK, 0:half] = lax.shift_right_logical(lo, 16) | (hi & hi_mask)
        stg_ref[r_lo:r_lo + STAGE_CHUNK, half:half + LANES] = pltpu.bitcast(
            jnp.broadcast_to(gate, (STAGE_CHUNK, LANES)), jnp.int32)

    def carry_tails(e, cr):
        first = pl.multiple_of(o_ref[base + e] * SUBLANES, SUBLANES)
        stg_ref[pl.ds(first, SUBLANES), :] = stg_ref[pl.ds(first, SUBLANES), :] | tails_ref[e]
        part = pl.multiple_of((o_ref[base + e] + f_ref[base + e]) * SUBLANES, SUBLANES)
        tails_ref[e] = jnp.where(rem_ref[base + e] > 0, stg_ref[pl.ds(part, SUBLANES), :], 0)
        return cr

    lax.fori_loop(0, N_EXPERTS, carry_tails, 0)

    def run_copies(e, cr):
        _group_copies(fl_ref[base + e], stg_ref, o_ref[base + e], xs_ref, hg_ref[base + e], sem, False)
        return cr

    lax.fori_loop(0, N_EXPERTS, run_copies, 0)

    @pl.when(i == pl.num_programs(0) - 1)
    def _():
        zero_ref[...] = jnp.zeros_like(zero_ref)
        tail_group = fs_ref[N_EXPERTS]
        n_tail = (n_slots // SUBLANES - tail_group) // (MOE_BLK // SUBLANES)

        def tail_copy(j, wait):
            off = pl.multiple_of(tail_group * SUBLANES + j * MOE_BLK, MOE_BLK)
            cp = pltpu.make_async_copy(zero_ref, xs_ref.at[pl.ds(off, MOE_BLK)], fill_sem)
            cp.wait() if wait else cp.start()

        for wait in (False, True):
            def pad_body(e, cr):
                _group_copies(flen_ref[e], zero_ref, 0, xs_ref, fs_ref[e], fill_sem, wait)
                return cr
            lax.fori_loop(0, N_EXPERTS, pad_body, 0)
            lax.fori_loop(0, n_tail, lambda j, cr: (tail_copy(j, wait), cr)[1], 0)

    @pl.when(i > 0)
    def _():
        _wait_groups(tot_ref[jnp.maximum(i - 1, 0)], stg2_ref.at[1 - slot], xs_ref, sem2.at[1 - slot])

    @pl.when(i == pl.num_programs(0) - 1)
    def _():
        _wait_groups(tot_ref[i], stg_ref, xs_ref, sem)


def _dispatch(tables, fill_groups, fill_len_groups, h, rt, stage_base, n_slots):
    t, d = h.shape
    tm = DISP_TM
    grid_spec = pltpu.PrefetchScalarGridSpec(
        num_scalar_prefetch=len(tables) + 2,
        grid=(t // tm,),
        in_specs=[pl.BlockSpec((tm, d), lambda i, *_: (i, 0)),
                  pl.BlockSpec((tm, LANES), lambda i, *_: (i, 0)),
                  pl.BlockSpec((1, 1, LANES), lambda i, *_: (i, 0, 0)),
                  pl.BlockSpec((tm, tm), lambda i, *_: (0, 0))],
        out_specs=pl.BlockSpec(memory_space=pl.ANY),
        scratch_shapes=[pltpu.VMEM((2, STAGE_ROWS, XS_W), jnp.int32),
                        pltpu.VMEM((N_EXPERTS, SUBLANES, XS_W), jnp.int32),
                        pltpu.VMEM((MOE_BLK, XS_W), jnp.int32),
                        pltpu.SemaphoreType.DMA((2,)), pltpu.SemaphoreType.DMA(())],
    )
    return pl.pallas_call(
        _dispatch_kernel,
        grid_spec=grid_spec,
        out_shape=jax.ShapeDtypeStruct((n_slots, XS_W), jnp.int32),
        compiler_params=_cparams(("arbitrary",)),
    )(*tables, fill_groups, fill_len_groups, h, rt, stage_base, _strict_lower(tm))


def _expert_kernel(be_ref, nused_ref, xs_ref, w1_ref, b1_ref, w2_ref, b2_ref, perm_ref, y_ref,
                   w1s_ref, w2s_ref):
    i = pl.program_id(0)
    prev = be_ref[jnp.maximum(i - 1, 0)]
    changed = (i == 0) | (be_ref[i] != prev)
    dff2 = w1_ref.shape[2]
    tile = 2 * LANES

    @pl.when(changed & (i < nused_ref[0]))
    def _():
        for j in range(dff2 // tile):
            wj = w1_ref[0, :, j * tile:(j + 1) * tile].astype(BF16)
            w1s_ref[:, j * tile:(j + 1) * tile] = _dot(wj, perm_ref[...]).astype(BF16)
        w2s_ref[...] = w2_ref[0].astype(BF16)

    @pl.when(i < nused_ref[0])
    def _():
        hi_mask = jnp.int32(-65536)
        words = xs_ref[:, 0:PACK_W]
        x = jnp.concatenate(
            [pltpu.bitcast(lax.shift_left(words, 16), F32).astype(BF16),
             pltpu.bitcast(words & hi_mask, F32).astype(BF16)], axis=1)
        gate = pltpu.bitcast(xs_ref[:, PACK_W:PACK_W + 1], F32)
        acts = []
        for j in range(dff2 // tile):
            hb = _dot(x, w1s_ref[:, j * tile:(j + 1) * tile]) + b1_ref[0, :, j * tile:(j + 1) * tile]
            gp = jnp.minimum(hb[:, 0:LANES], SWIGLU_LIMIT)
            up = jnp.clip(hb[:, LANES:tile], -SWIGLU_LIMIT, SWIGLU_LIMIT)
            acts.append((gp * _sigmoid(SWIGLU_ALPHA * gp) * (up + 1.0)).astype(BF16))
        act = jnp.concatenate(acts, axis=1)
        y = (_dot(act, w2s_ref[...]) + b2_ref[0]) * gate
        lo = pltpu.bitcast(y[:, 0:PACK_W].astype(BF16).astype(F32), jnp.int32)
        hi = pltpu.bitcast(y[:, PACK_W:2 * PACK_W].astype(BF16).astype(F32), jnp.int32)
        y_ref[...] = lax.shift_right_logical(lo, 16) | (hi & hi_mask)

    @pl.when(i >= nused_ref[0])
    def _():
        y_ref[...] = jnp.zeros_like(y_ref)


def _experts(block_exp, n_used, xs, w1, b1p, w2, b2, perm, n_blocks):
    blk = MOE_BLK
    d = w1.shape[1]
    dff2 = w1.shape[2]
    dff = w2.shape[1]

    def x_idx(i, be, nu):
        return (jnp.minimum(i, nu[0] - 1), 0)

    grid_spec = pltpu.PrefetchScalarGridSpec(
        num_scalar_prefetch=2,
        grid=(n_blocks,),
        in_specs=[pl.BlockSpec((blk, XS_W), x_idx),
                  pl.BlockSpec((1, d, dff2), lambda i, be, nu: (be[i], 0, 0)),
                  pl.BlockSpec((1, 1, dff2), lambda i, be, nu: (be[i], 0, 0)),
                  pl.BlockSpec((1, dff, d), lambda i, be, nu: (be[i], 0, 0)),
                  pl.BlockSpec((1, 1, d), lambda i, be, nu: (be[i], 0, 0)),
                  pl.BlockSpec(perm.shape, lambda i, be, nu: (0, 0))],
        out_specs=pl.BlockSpec((blk, PACK_W), lambda i, be, nu: (i, 0)),
        scratch_shapes=[pltpu.VMEM((d, dff2), BF16), pltpu.VMEM((dff, d), BF16)],
    )
    return pl.pallas_call(
        _expert_kernel,
        grid_spec=grid_spec,
        out_shape=jax.ShapeDtypeStruct((n_blocks * blk, PACK_W), jnp.int32),
        compiler_params=_cparams(("arbitrary",)),
    )(block_exp, n_used, xs, w1, b1p, w2, b2, perm)


def _combine_kernel(o_ref, g_ref, hg_ref, tot_ref, x1_ref, rt_ref, sb_ref, tri_ref, g2_ref, fnw_ref,
                    ys_ref, out_ref, stg_ref, sem):
    tm = x1_ref.shape[1]
    step = pl.program_id(0) * pl.num_programs(1) + pl.program_id(1)
    n_steps = pl.num_programs(0) * pl.num_programs(1)
    slot = step % 2
    hi_mask = jnp.int32(-65536)

    def fetch(tile, into):
        def body(e, cr):
            _group_copies(g_ref[tile * N_EXPERTS + e], ys_ref, hg_ref[tile * N_EXPERTS + e],
                          stg_ref.at[into], o_ref[tile * N_EXPERTS + e], sem.at[into], False)
            return cr
        lax.fori_loop(0, N_EXPERTS, body, 0)

    @pl.when(step == 0)
    def _():
        stg_ref[...] = jnp.zeros_like(stg_ref)
        fetch(0, 0)

    @pl.when(step + 1 < n_steps)
    def _():
        fetch(step + 1, 1 - slot)

    _wait_groups(tot_ref[step], ys_ref, stg_ref.at[slot], sem.at[slot])

    pos = _stage_positions(rt_ref[0], sb_ref[0], tri_ref[...])
    col_f = lax.broadcasted_iota(jnp.int32, (tm, STAGE_ROWS), 1).astype(F32)
    sel = jnp.zeros((tm, STAGE_ROWS), F32)
    for k in range(TOP_K):
        sel = jnp.where(col_f == pos[k], 1.0, sel)
    sel = sel.astype(BF16)
    words = stg_ref[slot]
    lo = pltpu.bitcast(lax.shift_left(words, 16), F32).astype(BF16)
    hi = pltpu.bitcast(words & hi_mask, F32).astype(BF16)
    moe = jnp.concatenate([_dot(sel, lo), _dot(sel, hi)], axis=1)
    x2 = x1_ref[0] + g2_ref[0] * moe
    ms = jnp.mean(x2 * x2, axis=-1, keepdims=True)
    out_ref[0] = x2 * lax.rsqrt(ms + RMS_EPS) * fnw_ref[...]


def _combine(tables, x1, rt, stage_base, g2, fnw, ys):
    b, l, d = x1.shape
    tm = DISP_TM
    per_l = l // tm
    tok = lambda n: pl.BlockSpec((1, tm, n), lambda i, j, *_: (i, j, 0))
    grid_spec = pltpu.PrefetchScalarGridSpec(
        num_scalar_prefetch=len(tables),
        grid=(b, per_l),
        in_specs=[tok(d), tok(LANES),
                  pl.BlockSpec((1, 1, LANES), lambda i, j, *_: (i * per_l + j, 0, 0)),
                  pl.BlockSpec((tm, tm), lambda i, j, *_: (0, 0)),
                  pl.BlockSpec((1, 1, d), lambda i, j, *_: (i, 0, 0)),
                  pl.BlockSpec(fnw.shape, lambda i, j, *_: (0, 0)),
                  pl.BlockSpec(memory_space=pl.ANY)],
        out_specs=tok(d),
        scratch_shapes=[pltpu.VMEM((2, STAGE_ROWS, PACK_W), jnp.int32),
                        pltpu.SemaphoreType.DMA((2,))],
    )
    return pl.pallas_call(
        _combine_kernel,
        grid_spec=grid_spec,
        out_shape=jax.ShapeDtypeStruct((b, l, d), F32),
        compiler_params=_cparams(("arbitrary", "arbitrary")),
    )(*tables, x1, rt, stage_base, _strict_lower(tm), g2, fnw, ys)


def _pool_constants():
    i = np.arange(POOL_TM)[:, None]
    j = np.arange(POOL_TM)[None, :]
    same_row = (i // GRID_W) == (j // GRID_W)
    mats, cnts = [], []
    for w in POOL_WINDOWS:
        band = same_row & (j - i >= -(w // 2)) & (j - i < w - w // 2)
        mats.append(band)
        cnts.append(np.broadcast_to(band.sum(axis=1, keepdims=True), (POOL_TM, POOL_GW)))
    return (jnp.asarray(np.stack(mats), BF16), jnp.asarray(np.stack(cnts), F32))


def _head_expand():
    e = np.zeros((LANES, SSD_W), np.float32)
    for h in range(SSD_HEADS):
        e[h, h * SSD_HEADDIM:(h + 1) * SSD_HEADDIM] = 1.0
    return jnp.asarray(e, BF16)


def _deinterleave_perm():
    n = 2 * LANES
    p = np.zeros((n, n), np.float32)
    for k in range(LANES):
        p[2 * k, k] = 1.0
        p[2 * k + 1, LANES + k] = 1.0
    return jnp.asarray(p, BF16)


def _strict_lower(n):
    return jnp.asarray(np.tril(np.ones((n, n), np.float32), -1), BF16)


def _pad_lanes(a, n):
    return jnp.pad(a, [(0, 0)] * (a.ndim - 1) + [(0, n - a.shape[-1])])


def kernel(x, c, ctx, c_ctx, w_mod, b_mod, norm1_w, norm2_w, w_in, conv_w, conv_b, dt_bias, a_log,
           d_skip, ssd_norm_w, pool_w, pool_scale, w_out, router_w, router_b, w1, b1, w2, b2,
           final_norm_w):
    depth = w_mod.shape[0]
    assert depth == 1, "single-layer problem"
    b, l, d = x.shape
    lc = ctx.shape[1]
    xbcdt = CONV_CH + 2 * SSD_HEADS

    mod_rows = 2 * SUBLANES
    cc = jnp.zeros((mod_rows, d), F32).at[0:b].set(c).at[b].set(c_ctx)
    mod = _modulation(cc, w_mod[0], b_mod[0])
    sh1, sc1, g1, sh2, sc2, g2 = [m.reshape(b, 1, d) for m in jnp.split(mod[0:b], 6, axis=-1)]
    csh1, csc1 = [jnp.broadcast_to(m.reshape(1, 1, d), (b, 1, d))
                  for m in jnp.split(mod[b:b + 1], 6, axis=-1)[0:2]]

    wi = w_in[0]
    wx = wi[:, 0:CONV_CH].astype(BF16)
    wd = jnp.concatenate([_pad_lanes(wi[:, CONV_CH:CONV_CH + SSD_HEADS], LANES),
                          _pad_lanes(wi[:, CONV_CH + SSD_HEADS:xbcdt], LANES)], axis=1).astype(BF16)
    wz = wi[:, xbcdt:xbcdt + SSD_W].astype(BF16)
    wp = wi[:, xbcdt + SSD_W:].astype(BF16)
    n1 = norm1_w[0].reshape(1, d)
    conv_w8 = jnp.pad(conv_w[0], ((0, SUBLANES - SSD_CONV), (0, 0)))
    conv_b1 = conv_b[0].reshape(1, CONV_CH)
    dtb = _pad_lanes(dt_bias[0], LANES).reshape(2, 1, LANES)
    alog = _pad_lanes(a_log[0], LANES).reshape(2, 1, LANES)
    dsk = jnp.repeat(d_skip[0], SSD_HEADDIM).reshape(1, SSD_W)
    snw = ssd_norm_w[0].reshape(1, SSD_W)
    expand = _head_expand()

    xbc_c, dt_c, z_c, _ = _inproj(ctx, n1, csh1, csc1, wx, wd, wz, wp, min(PROJ_TM, lc))
    zero_state = jnp.zeros((b, 2, SSD_GROUPS, SSD_STATE, GROUP_W), F32)
    _, ctx_states = _ssd(xbc_c, dt_c, z_c, conv_w8, conv_b1, dtb, alog, dsk, snw, expand, zero_state)

    xbc, dt, z, u_pool = _inproj(x, n1, sh1, sc1, wx, wd, wz, wp, PROJ_TM)
    y_ssd, _ = _ssd(xbc, dt, z, conv_w8, conv_b1, dtb, alog, dsk, snw, expand, ctx_states)

    pool_a, pool_cnt = _pool_constants()
    x1, h2, rt, tcnt = _mix(
        y_ssd, u_pool, x, pool_a, pool_cnt, pool_w[0].astype(BF16), pool_scale[0].reshape(1, POOL_W),
        w_out[0].astype(BF16), g1, norm2_w[0].reshape(1, d), sh2, sc2,
        _pad_lanes(router_w[0], LANES), _pad_lanes(router_b[0].reshape(1, N_EXPERTS), LANES))

    i32 = jnp.int32
    t = b * l
    n_tiles = t // DISP_TM
    tc = tcnt[:, 0:MIX_TM // DISP_TM, 0:N_EXPERTS].reshape(n_tiles, N_EXPERTS).astype(i32)
    counts = jnp.sum(tc, axis=0)
    run_start = jnp.cumsum(tc, axis=0) - tc
    padded = (counts + MOE_BLK - 1) // MOE_BLK * MOE_BLK
    pad_end = jnp.cumsum(padded)
    pad_start = pad_end - padded
    carried = run_start % SUBLANES
    span = carried + tc
    groups = (span + SUBLANES - 1) // SUBLANES
    full = span // SUBLANES
    stage_group = jnp.cumsum(groups, axis=1) - groups
    slot_group = (pad_start[None, :] + run_start - carried) // SUBLANES
    flush = full.at[n_tiles - 1].set(groups[n_tiles - 1])
    flat = lambda a: a.reshape(n_tiles * N_EXPERTS).astype(i32)
    stage_base = _pad_lanes((stage_group * SUBLANES + carried).astype(F32), LANES)
    stage_base = stage_base.reshape(n_tiles, 1, LANES)
    n_blocks = (t * TOP_K) // MOE_BLK + N_EXPERTS
    n_used = (pad_end[-1] // MOE_BLK).astype(i32).reshape(1)
    blk_start = jnp.minimum(jnp.arange(n_blocks, dtype=i32), n_used[0] - 1) * MOE_BLK
    block_exp = jnp.minimum(jnp.sum(blk_start[:, None] >= pad_end[None, :], axis=1),
                            N_EXPERTS - 1).astype(i32)
    written = (counts + SUBLANES - 1) // SUBLANES * SUBLANES
    fill_group = (jnp.concatenate([pad_start + written, pad_end[-1:]]) // SUBLANES).astype(i32)
    fill_groups = (jnp.concatenate([padded - written, jnp.zeros((1,), i32)]) // SUBLANES).astype(i32)

    xs = _dispatch((flat(stage_group), flat(full), flat(flush), flat(span % SUBLANES), flat(slot_group),
                    jnp.sum(flush, axis=1).astype(i32)),
                   fill_group, fill_groups, h2.reshape(t, d), rt.reshape(t, LANES), stage_base,
                   n_blocks * MOE_BLK)
    dff2 = w1.shape[-1]
    b1p = jnp.concatenate(
        [b1[0].reshape(N_EXPERTS, dff2 // (2 * LANES), LANES, 2)[..., 0],
         b1[0].reshape(N_EXPERTS, dff2 // (2 * LANES), LANES, 2)[..., 1]], axis=-1
    ).reshape(N_EXPERTS, 1, dff2)
    ys = _experts(block_exp, n_used, xs, w1[0], b1p, w2[0], b2[0].reshape(N_EXPERTS, 1, d),
                  _deinterleave_perm(), n_blocks)
    return _combine((flat(stage_group), flat(groups), flat(slot_group),
                     jnp.sum(groups, axis=1).astype(i32)), x1, rt, stage_base, g2,
                    final_norm_w.reshape(1, d), ys)
```

```python
import functools

import numpy as np
import jax
import jax.numpy as jnp
from jax import lax
from jax.experimental import pallas as pl
from jax.experimental.pallas import tpu as pltpu

F32 = jnp.float32
BF16 = jnp.bfloat16

SSD_HEADDIM = 64
SSD_GROUPS = 4
SSD_HPG = 6
SSD_HEADS = SSD_GROUPS * SSD_HPG
SSD_STATE = 128
SSD_CONV = 5
SSD_CHUNK = 128
SSD_W = SSD_HEADS * SSD_HEADDIM
GROUP_W = SSD_HPG * SSD_HEADDIM
CONV_CH = SSD_W + 2 * SSD_GROUPS * SSD_STATE
POOL_WINDOWS = (2, 4, 8, 16)
POOL_GW = 128
POOL_W = POOL_GW * len(POOL_WINDOWS)
GRID_W = 64
N_EXPERTS = 32
TOP_K = 4
SWIGLU_ALPHA = 1.702
SWIGLU_LIMIT = 7.0
RMS_EPS = 1e-6
LOG2_E = 1.4426950408889634

LANES = 128
SUBLANES = 8
VMEM_LIMIT_BYTES = 56 * 1024 * 1024

MOD_TN = 1024
PROJ_TM = 512
MIX_TM = 512
POOL_TM = 256
MOE_BLK = 512
DISP_TM = 256
NEG_BIG = -1e30

GATE_LANE = 2 * TOP_K
PACK_W = 512
XS_W = PACK_W + LANES
STAGE_CHUNK = 256
_STAGE_NEED = TOP_K * DISP_TM + N_EXPERTS * 2 * (SUBLANES - 1) + SUBLANES
STAGE_ROWS = -(-_STAGE_NEED // STAGE_CHUNK) * STAGE_CHUNK
RUN_BITS = (32, 16, 8, 4, 2, 1)
RUN_SMALL_BIT = 4
WAIT_BITS = (128, 64, 32, 16, 8, 4, 2, 1)


def _sigmoid(x):
    return 0.5 * jnp.tanh(0.5 * x) + 0.5


def _split2(a):
    hi = a.astype(BF16)
    mid = (a - hi.astype(F32)).astype(BF16)
    return hi, mid


def _split3(a):
    hi = a.astype(BF16)
    r = a - hi.astype(F32)
    mid = r.astype(BF16)
    lo = (r - mid.astype(F32)).astype(BF16)
    return hi, mid, lo


def _dot(a, b):
    return jnp.dot(a, b, preferred_element_type=F32)


def _dot_exact_rhs(a_f32, b_bf16, parts):
    pieces = _split3(a_f32) if parts == 3 else _split2(a_f32)
    out = _dot(pieces[0], b_bf16)
    for p in pieces[1:]:
        out = out + _dot(p, b_bf16)
    return out


def _dot_hi(a_f32, b_f32):
    a0, a1, a2 = _split3(a_f32)
    b0, b1, b2 = _split3(b_f32)
    out = _dot(a0, b0)
    out = out + _dot(a0, b1) + _dot(a1, b0)
    out = out + _dot(a1, b1) + _dot(a0, b2) + _dot(a2, b0)
    return out


def _cparams(sem):
    return pltpu.CompilerParams(dimension_semantics=sem, vmem_limit_bytes=VMEM_LIMIT_BYTES)


def _mod_kernel(c_ref, w_ref, b_ref, o_ref):
    c = c_ref[...]
    s = c * _sigmoid(c)
    o_ref[...] = _dot_hi(s, w_ref[...]) + b_ref[...]


def _modulation(cc, w_mod, b_mod):
    rows, d = cc.shape
    n = w_mod.shape[1]
    return pl.pallas_call(
        _mod_kernel,
        grid=(n // MOD_TN,),
        in_specs=[pl.BlockSpec((rows, d), lambda j: (0, 0)),
                  pl.BlockSpec((d, MOD_TN), lambda j: (0, j)),
                  pl.BlockSpec((1, MOD_TN), lambda j: (0, j))],
        out_specs=pl.BlockSpec((rows, MOD_TN), lambda j: (0, j)),
        out_shape=jax.ShapeDtypeStruct((rows, n), F32),
        compiler_params=_cparams(("arbitrary",)),
    )(cc, w_mod, b_mod.reshape(1, n))


def _inproj_kernel(n_groups, x0_ref, xn_ref, nw_ref, sh0_ref, sc0_ref, shn_ref, scn_ref, *refs):
    w_refs = refs[0:n_groups]
    out_refs = refs[n_groups:2 * n_groups]
    hb_even_ref, hb_odd_ref = refs[2 * n_groups:]
    step = pl.program_id(0) * pl.num_programs(1) + pl.program_id(1)

    def normed(x, sh, sc):
        ms = jnp.mean(x * x, axis=-1, keepdims=True)
        h = x * lax.rsqrt(ms + RMS_EPS) * nw_ref[...]
        return (h * (1.0 + sc) + sh).astype(BF16)

    @pl.when(step == 0)
    def _():
        hb_even_ref[...] = normed(x0_ref[0], sh0_ref[0], sc0_ref[0])

    def project(cur_ref, nxt_ref):
        hb = cur_ref[...]
        for w_ref, out_ref in zip(w_refs, out_refs):
            out_ref[0] = _dot(hb, w_ref[...])
        nxt_ref[...] = normed(xn_ref[0], shn_ref[0], scn_ref[0])

    pl.when(step % 2 == 0)(functools.partial(project, hb_even_ref, hb_odd_ref))
    pl.when(step % 2 == 1)(functools.partial(project, hb_odd_ref, hb_even_ref))


def _inproj(x, norm_w, shift, scale, weights, tm):
    b, l, d = x.shape
    per_l = l // tm
    last = b * per_l - 1

    def nxt(i, j):
        s = jnp.minimum(i * per_l + j + 1, last)
        return s // per_l, s % per_l

    full = lambda a: pl.BlockSpec(a.shape, lambda i, j: (0, 0))
    tok = lambda n: pl.BlockSpec((1, tm, n), lambda i, j: (i, j, 0))
    first_b = pl.BlockSpec((1, 1, d), lambda i, j: (0, 0, 0))
    next_b = pl.BlockSpec((1, 1, d), lambda i, j: (nxt(i, j)[0], 0, 0))
    return pl.pallas_call(
        functools.partial(_inproj_kernel, len(weights)),
        grid=(b, per_l),
        in_specs=[pl.BlockSpec((1, tm, d), lambda i, j: (0, 0, 0)),
                  pl.BlockSpec((1, tm, d), lambda i, j: (*nxt(i, j), 0)),
                  full(norm_w), first_b, first_b, next_b, next_b] + [full(w) for w in weights],
        out_specs=[tok(w.shape[1]) for w in weights],
        out_shape=[jax.ShapeDtypeStruct((b, l, w.shape[1]), F32) for w in weights],
        scratch_shapes=[pltpu.VMEM((tm, d), BF16), pltpu.VMEM((tm, d), BF16)],
        compiler_params=_cparams(("arbitrary", "arbitrary")),
    )(x, x, norm_w, shift, scale, shift, scale, *weights)


def _ssd_kernel(nc, emit_y, *refs):
    if emit_y:
        (xbc_ref, prev_ref, next_ref, dt_ref, z_ref, cw_ref, cb_ref, dtb_ref, alog_ref, dsk_ref,
         nw_ref, exp_ref, init_ref, y_ref, fin_ref, act_ref, ybuf_ref, yf_ref, st_ref) = refs
        y_part = (z_ref, dsk_ref, nw_ref, y_ref, ybuf_ref, yf_ref)
    else:
        (xbc_ref, prev_ref, next_ref, dt_ref, cw_ref, cb_ref, dtb_ref, alog_ref,
         exp_ref, init_ref, fin_ref, act_ref, st_ref) = refs
        y_part = None
    ph = pl.program_id(1)
    c = pl.program_id(2)
    ci = c + ph * (nc - 1 - 2 * c)

    @pl.when(c == 0)
    def _():
        st_ref[...] = init_ref[0, ph]

    @pl.when(ph == 0)
    def _():
        _conv_silu(nc, ci, xbc_ref, prev_ref, next_ref, cw_ref, cb_ref, act_ref)

    _ssd_scan(nc, ph, c, ci, dt_ref, dtb_ref, alog_ref, exp_ref, act_ref, st_ref, fin_ref, y_part)


def _conv_silu(nc, ci, xbc_ref, prev_ref, next_ref, cw_ref, cb_ref, act_ref):
    ch = SSD_CHUNK
    has_prev = ci > 0
    has_next = ci < nc - 1
    row8 = lax.broadcasted_iota(jnp.int32, (SUBLANES, LANES), 0)
    half = SSD_CONV // 2
    for j in range(CONV_CH // LANES):
        cols = slice(j * LANES, (j + 1) * LANES)
        tiles = ([jnp.where(has_prev, prev_ref[0, :, cols], 0.0)]
                 + [xbc_ref[0, i * SUBLANES:(i + 1) * SUBLANES, cols] for i in range(ch // SUBLANES)]
                 + [jnp.where(has_next, next_ref[0, :, cols], 0.0)])
        n_t = ch // SUBLANES
        acc = [cb_ref[:, cols] + cw_ref[half:half + 1, cols] * tiles[i + 1] for i in range(n_t)]
        for s in range(1, half + 1):
            rot = [pltpu.roll(tl, s, axis=0) for tl in tiles[0:n_t + 1]]
            wk = cw_ref[half - s:half - s + 1, cols]
            for i in range(n_t):
                acc[i] = acc[i] + wk * jnp.where(row8 < s, rot[i], rot[i + 1])
            rot = [pltpu.roll(tl, SUBLANES - s, axis=0) for tl in tiles[1:n_t + 2]]
            wk = cw_ref[half + s:half + s + 1, cols]
            for i in range(n_t):
                acc[i] = acc[i] + wk * jnp.where(row8 >= SUBLANES - s, rot[i + 1], rot[i])
        silu = [0.5 * v * jnp.tanh(0.5 * v) + 0.5 * v for v in acc]
        act_ref[ci, :, cols] = jnp.concatenate(silu, axis=0).astype(BF16)


def _ssd_scan(nc, ph, c, ci, dt_ref, dtb_ref, alog_ref, exp_ref, act_ref, st_ref, fin_ref, y_part):
    ch = SSD_CHUNK
    emit_y = y_part is not None
    dtr = dt_ref[0] + dtb_ref[0]
    small = jnp.exp(-jnp.abs(dtr))
    one_plus = 1.0 + small
    log1p_small = jnp.where(one_plus == 1.0, small, jnp.log(one_plus) * (small / (one_plus - 1.0)))
    dtv = jnp.maximum(dtr, 0.0) + log1p_small
    a = dtv * (-jnp.exp(alog_ref[0]) * LOG2_E)
    row = lax.broadcasted_iota(jnp.int32, (ch, ch), 0)
    col = lax.broadcasted_iota(jnp.int32, (ch, ch), 1)
    tmask = (row - col) * (1 - 2 * ph) >= 0
    tri = jnp.where(tmask, 1.0, 0.0).astype(BF16)
    cs = _dot_exact_rhs_left(tri, a)
    tot = jnp.where(ph == 0, cs[ch - 1:ch, :], cs[0:1, :])
    cs_t = cs.T
    e_cs = jnp.exp2(cs)
    e_dec = jnp.exp2(tot - cs)
    e_tot = jnp.exp2(tot)
    expand = exp_ref[...]
    wdec_x = _dot((dtv * e_dec).astype(BF16), expand).astype(BF16)
    etot_x = _dot_exact_rhs(jnp.broadcast_to(e_tot, (SUBLANES, LANES)), expand, 2)[0:1]
    if emit_y:
        z_ref, dsk_ref, nw_ref, y_ref, ybuf_ref, yf_ref = y_part
        ecs_x = _dot(e_cs.astype(BF16), expand)
        dt_t = dtv.T

    lane = lax.broadcasted_iota(jnp.int32, (ch, LANES), 1)
    for g in range(SSD_GROUPS):
        gs = slice(g * GROUP_W, (g + 1) * GROUP_W)
        b_bf = act_ref[ci, :, SSD_W + g * SSD_STATE:SSD_W + (g + 1) * SSD_STATE]
        x_bf = act_ref[ci, :, gs]
        s_prev = st_ref[g]
        x_dec = x_bf * wdec_x[:, gs]
        st_ref[g] = s_prev * etot_x[:, gs] + lax.dot_general(
            b_bf, x_dec, (((0,), (0,)), ((), ())), preferred_element_type=F32)
        if not emit_y:
            continue
        c_bf = act_ref[ci, :, SSD_W + (SSD_GROUPS + g) * SSD_STATE:
                       SSD_W + (SSD_GROUPS + g + 1) * SSD_STATE]
        cb = lax.dot_general(c_bf, b_bf, (((1,), (1,)), ((), ())), preferred_element_type=F32)
        y_off = _dot(c_bf, s_prev.astype(BF16)) * ecs_x[:, gs]
        for q in range(SSD_HPG // 2):
            lmats = []
            for h in (g * SSD_HPG + 2 * q, g * SSD_HPG + 2 * q + 1):
                diff = cs[:, h:h + 1] - cs_t[h:h + 1, :]
                dec = jnp.exp2(jnp.where(tmask, diff, NEG_BIG))
                lmats.append((dec * cb * dt_t[h:h + 1, :]).astype(BF16))
            xp = x_bf[:, q * LANES:(q + 1) * LANES]
            zero = jnp.zeros_like(xp)
            rhs = jnp.concatenate([jnp.where(lane < SSD_HEADDIM, xp, zero),
                                   jnp.where(lane >= SSD_HEADDIM, xp, zero)], axis=0)
            y_diag = _dot(jnp.concatenate(lmats, axis=1), rhs)
            ps = slice(g * GROUP_W + q * LANES, g * GROUP_W + (q + 1) * LANES)
            ybuf_ref[:, ps] = y_diag + y_off[:, q * LANES:(q + 1) * LANES]

    if emit_y:
        @pl.when(ph == 0)
        def _():
            yf_ref[ci] = ybuf_ref[...].astype(BF16)

        @pl.when(ph == 1)
        def _():
            yt = (yf_ref[ci].astype(F32) + ybuf_ref[...]
                  + act_ref[ci, :, 0:SSD_W].astype(F32) * dsk_ref[...])
            zz = z_ref[0]
            gt = yt * (zz * _sigmoid(zz))
            ms = jnp.mean(gt * gt, axis=-1, keepdims=True)
            y_ref[0] = (gt * lax.rsqrt(ms + RMS_EPS) * nw_ref[...]).astype(y_ref.dtype)

    @pl.when(c == nc - 1)
    def _():
        fin_ref[0, ph] = st_ref[...]


def _dot_exact_rhs_left(sel_bf16, a_f32):
    hi, mid, lo = _split3(a_f32)
    return _dot(sel_bf16, hi) + _dot(sel_bf16, mid) + _dot(sel_bf16, lo)


def _ssd(xbc, dt, z, conv_w8, conv_b, dt_bias, a_log, d_skip_x, norm_w, expand, init):
    emit_y = z is not None
    b, l, _ = xbc.shape
    ch = SSD_CHUNK
    nc = l // ch
    rows8 = ch // SUBLANES

    def cidx(ph, c):
        return c + ph * (nc - 1 - 2 * c)

    def xidx(ph, c):
        return jnp.where(ph == 0, c, nc - 1)

    def out_idx(ph, c):
        return jnp.where(ph == 0, nc - 1, nc - 1 - c)

    full2 = lambda a: pl.BlockSpec(a.shape, lambda i, ph, c: (0, 0))
    st_spec = pl.BlockSpec((1, 2, SSD_GROUPS, SSD_STATE, GROUP_W), lambda i, ph, c: (i, 0, 0, 0, 0))
    per_phase = pl.BlockSpec((1, 1, LANES), lambda i, ph, c: (ph, 0, 0))
    y_spec = pl.BlockSpec((1, ch, SSD_W), lambda i, ph, c: (i, out_idx(ph, c), 0))
    operands = [(xbc, pl.BlockSpec((1, ch, CONV_CH), lambda i, ph, c: (i, xidx(ph, c), 0))),
                (xbc, pl.BlockSpec((1, SUBLANES, CONV_CH),
                                   lambda i, ph, c: (i, jnp.maximum(xidx(ph, c) * rows8 - 1, 0), 0))),
                (xbc, pl.BlockSpec((1, SUBLANES, CONV_CH),
                                   lambda i, ph, c: (i, jnp.minimum((xidx(ph, c) + 1) * rows8,
                                                                    l // SUBLANES - 1), 0))),
                (dt, pl.BlockSpec((1, ch, LANES), lambda i, ph, c: (i, cidx(ph, c), ph)))]
    if emit_y:
        operands.append((z, y_spec))
    operands += [(conv_w8, full2(conv_w8)), (conv_b, full2(conv_b)),
                 (dt_bias, per_phase), (a_log, per_phase)]
    if emit_y:
        operands += [(d_skip_x, full2(d_skip_x)), (norm_w, full2(norm_w))]
    operands += [(expand, full2(expand)), (init, st_spec)]
    st_shape = jax.ShapeDtypeStruct((b, 2, SSD_GROUPS, SSD_STATE, GROUP_W), F32)
    scratch = [pltpu.VMEM((nc, ch, CONV_CH), BF16)]
    if emit_y:
        scratch += [pltpu.VMEM((ch, SSD_W), F32),
                    pltpu.VMEM((nc, ch, SSD_W), BF16)]
    scratch.append(pltpu.VMEM((SSD_GROUPS, SSD_STATE, GROUP_W), F32))
    outs = pl.pallas_call(
        functools.partial(_ssd_kernel, nc, emit_y),
        grid=(b, 2, nc),
        in_specs=[spec for _, spec in operands],
        out_specs=[y_spec, st_spec] if emit_y else [st_spec],
        out_shape=[jax.ShapeDtypeStruct((b, l, SSD_W), BF16), st_shape] if emit_y else [st_shape],
        scratch_shapes=scratch,
        compiler_params=_cparams(("arbitrary", "arbitrary", "arbitrary")),
    )(*[a for a, _ in operands])
    return outs if emit_y else (None, outs[0])


def _mix_kernel(y_ref, u_ref, x_ref, pa_ref, pcnt_ref, pw_ref, psc_ref, wo_ref, g1_ref,
                nw_ref, sh_ref, sc_ref, rw_ref, rb_ref,
                x1_ref, h_ref, rt_ref, tcnt_ref):
    tm = x_ref.shape[1]

    pooled = []
    for g in range(len(POOL_WINDOWS)):
        parts = []
        for r in range(tm // POOL_TM):
            u = u_ref[0, r * POOL_TM:(r + 1) * POOL_TM, g * POOL_GW:(g + 1) * POOL_GW]
            both = _dot(pa_ref[g], jnp.concatenate(_split2(u), axis=1))
            wsum = both[:, 0:POOL_GW] + both[:, POOL_GW:2 * POOL_GW]
            parts.append((wsum / pcnt_ref[g] - u).astype(BF16))
        pooled.append(jnp.concatenate(parts, axis=0))
    mapped = []
    for gp in range(len(POOL_WINDOWS) // 2):
        pair = jnp.concatenate(pooled[2 * gp:2 * gp + 2], axis=1)
        mapped.append((_dot(pair, pw_ref[gp])
                       * psc_ref[:, 2 * gp * POOL_GW:(2 * gp + 2) * POOL_GW]).astype(BF16))
    y_pool = jnp.concatenate(mapped, axis=1)

    mix = _dot(y_ref[0], wo_ref[0:SSD_W, :]) + _dot(y_pool, wo_ref[SSD_W:SSD_W + POOL_W, :])
    x1 = x_ref[0] + g1_ref[0] * mix
    x1_ref[0] = x1

    ms = jnp.mean(x1 * x1, axis=-1, keepdims=True)
    h = x1 * lax.rsqrt(ms + RMS_EPS) * nw_ref[...]
    h = h * (1.0 + sc_ref[0]) + sh_ref[0]
    h_ref[0] = h

    h0, h1 = _split2(h)
    rw2 = jnp.concatenate(_split2(rw_ref[...]), axis=1)
    t0 = _dot(h0, rw2)
    t1 = _dot(h1, rw2)
    logits = (t0[:, 0:LANES] + t0[:, LANES:2 * LANES] + t1[:, 0:LANES] + t1[:, LANES:2 * LANES]
              + rb_ref[...])
    lane = lax.broadcasted_iota(jnp.int32, (tm, LANES), 1)
    lane_f = lane.astype(F32)
    work = jnp.where(lane < N_EXPERTS, logits, NEG_BIG)
    vals, hots = [], []
    for _ in range(TOP_K):
        m = jnp.max(work, axis=-1, keepdims=True)
        first_idx = jnp.min(jnp.where(work == m, lane_f, float(LANES)), axis=-1, keepdims=True)
        hot = lane_f == first_idx
        vals.append(m)
        hots.append(hot)
        work = jnp.where(hot, 2.0 * NEG_BIG, work)
    exps = [jnp.exp(v - vals[0]) for v in vals]
    denom = exps[0] + exps[1] + exps[2] + exps[3]

    onehot = jnp.zeros((tm, LANES), F32)
    for hot in hots:
        onehot = onehot + jnp.where(hot, 1.0, 0.0)
    packed = jnp.zeros((tm, LANES), F32)
    for k in range(TOP_K):
        idx_k = jnp.sum(jnp.where(hots[k], lane_f, 0.0), axis=-1, keepdims=True)
        packed = jnp.where(lane == k, idx_k, packed)
        packed = jnp.where(lane == GATE_LANE + k, exps[k] / denom, packed)
    rt_ref[0] = packed
    row8 = lax.broadcasted_iota(jnp.int32, (SUBLANES, LANES), 0)
    tcnt = jnp.zeros((SUBLANES, LANES), F32)
    for r in range(tm // DISP_TM):
        sub = jnp.sum(onehot[r * DISP_TM:(r + 1) * DISP_TM], axis=0, keepdims=True)
        tcnt = jnp.where(row8 == r, sub, tcnt)
    tcnt_ref[0] = tcnt


def _mix(y, u, x, pool_a, pool_cnt, pool_w, pool_scale, w_out, g1, norm_w, shift, scale,
         router_w, router_b):
    b, l, d = x.shape
    tm = MIX_TM
    per_l = l // tm
    tok = lambda n: pl.BlockSpec((1, tm, n), lambda i, j: (i, j, 0))
    per_b = pl.BlockSpec((1, 1, d), lambda i, j: (i, 0, 0))
    full = lambda a: pl.BlockSpec(a.shape, lambda i, j: (0,) * a.ndim)
    return pl.pallas_call(
        _mix_kernel,
        grid=(b, per_l),
        in_specs=[tok(SSD_W), tok(POOL_W), tok(d), full(pool_a), full(pool_cnt), full(pool_w),
                  full(pool_scale), full(w_out), per_b, full(norm_w), per_b, per_b,
                  full(router_w), full(router_b)],
        out_specs=[tok(d), tok(d), tok(LANES),
                   pl.BlockSpec((1, SUBLANES, LANES), lambda i, j: (i * per_l + j, 0, 0))],
        out_shape=[jax.ShapeDtypeStruct((b, l, d), F32),
                   jax.ShapeDtypeStruct((b, l, d), F32),
                   jax.ShapeDtypeStruct((b, l, LANES), F32),
                   jax.ShapeDtypeStruct((b * per_l, SUBLANES, LANES), F32)],
        compiler_params=_cparams(("arbitrary", "arbitrary")),
    )(y, u, x, pool_a, pool_cnt, pool_w, pool_scale, w_out, g1, norm_w, shift, scale,
      router_w, router_b)


def _group_copies(n_groups, src_ref, src_group, dst_ref, dst_group, sem, wait):
    def pieces(bits):
        for bit in bits:
            @pl.when((n_groups & bit) != 0)
            def _():
                done = n_groups & ~(2 * bit - 1)
                src = src_ref.at[pl.ds(pl.multiple_of((src_group + done) * SUBLANES, SUBLANES),
                                       bit * SUBLANES)]
                dst = dst_ref.at[pl.ds(pl.multiple_of((dst_group + done) * SUBLANES, SUBLANES),
                                       bit * SUBLANES)]
                cp = pltpu.make_async_copy(src, dst, sem)
                cp.wait() if wait else cp.start()

    split = RUN_BITS.index(RUN_SMALL_BIT)
    pl.when(n_groups >= 2 * RUN_SMALL_BIT)(lambda: pieces(RUN_BITS[:split]))
    pieces(RUN_BITS[split:])


def _wait_groups(n_groups, src_ref, dst_ref, sem):
    for bit in WAIT_BITS:
        @pl.when((n_groups & bit) != 0)
        def _():
            rows = bit * SUBLANES
            pltpu.make_async_copy(src_ref.at[pl.ds(0, rows)], dst_ref.at[pl.ds(0, rows)], sem).wait()


def _stage_positions(rt, base_row, tri):
    tm = rt.shape[0]
    lane_f = lax.broadcasted_iota(jnp.int32, (tm, LANES), 1).astype(F32)
    hots = [lane_f == rt[:, k:k + 1] for k in range(TOP_K)]
    onehot = jnp.zeros((tm, LANES), F32)
    for hot in hots:
        onehot = onehot + jnp.where(hot, 1.0, 0.0)
    pos = _dot(tri, onehot.astype(BF16)) + base_row
    return [jnp.sum(jnp.where(hot, pos, 0.0), axis=-1, keepdims=True) for hot in hots]


def _dispatch_kernel(o_ref, f_ref, fl_ref, rem_ref, hg_ref, tot_ref, used_ref, fs_ref, flen_ref,
                     h_ref, rt_ref, sb_ref, tri_ref, xs_ref,
                     stg2_ref, tails_ref, zero_ref, sem2, fill_sem):
    i = pl.program_id(0)
    tm = h_ref.shape[0]
    n_slots = xs_ref.shape[0]
    base = i * N_EXPERTS
    hi_mask = jnp.int32(-65536)
    slot = i % 2
    stg_ref = stg2_ref.at[slot]
    sem = sem2.at[slot]

    @pl.when(i == 0)
    def _():
        tails_ref[...] = jnp.zeros_like(tails_ref)
        stg2_ref[...] = jnp.zeros_like(stg2_ref)

    rt = rt_ref[...]
    pos = _stage_positions(rt, sb_ref[0], tri_ref[...])
    lane = lax.broadcasted_iota(jnp.int32, (tm, LANES), 1)
    cols = jnp.zeros((tm, LANES), F32)
    for k in range(TOP_K):
        cols = jnp.where(lane == k, pos[k], cols)
        cols = jnp.where(lane == TOP_K + k, rt[:, GATE_LANE + k:GATE_LANE + k + 1], cols)
    rows = cols.T

    hb = h_ref[...].astype(BF16)
    half = PACK_W
    used_rows = used_ref[i] * SUBLANES

    def stage_chunk(r_lo):
        row_f = (lax.broadcasted_iota(jnp.int32, (STAGE_CHUNK, tm), 0) + r_lo).astype(F32)
        sel = jnp.zeros((STAGE_CHUNK, tm), F32)
        gsel = jnp.zeros((STAGE_CHUNK, tm), F32)
        for k in range(TOP_K):
            hit = row_f == rows[k:k + 1, :]
            sel = jnp.where(hit, 1.0, sel)
            gsel = jnp.where(hit, rows[TOP_K + k:TOP_K + k + 1, :], gsel)
        gate = jnp.sum(gsel, axis=-1, keepdims=True)
        moved = _dot(sel.astype(BF16), hb)
        lo = pltpu.bitcast(moved[:, 0:half], jnp.int32)
        hi = pltpu.bitcast(moved[:, half:2 * half], jnp.int32)
        stg_ref[r_lo:r_lo + STAGE_CHUNK, 0:half] = lax.shift_right_logical(lo, 16) | (hi & hi_mask)
        stg_ref[r_lo:r_lo + STAGE_CHUNK, half:half + LANES] = pltpu.bitcast(
            jnp.broadcast_to(gate, (STAGE_CHUNK, LANES)), jnp.int32)

    for rc in range(STAGE_ROWS // STAGE_CHUNK):
        pl.when(rc * STAGE_CHUNK < used_rows)(functools.partial(stage_chunk, rc * STAGE_CHUNK))

    def carry_tails(e, cr):
        first = pl.multiple_of(o_ref[base + e] * SUBLANES, SUBLANES)
        stg_ref[pl.ds(first, SUBLANES), :] = stg_ref[pl.ds(first, SUBLANES), :] | tails_ref[e]
        part = pl.multiple_of((o_ref[base + e] + f_ref[base + e]) * SUBLANES, SUBLANES)
        tails_ref[e] = jnp.where(rem_ref[base + e] > 0, stg_ref[pl.ds(part, SUBLANES), :], 0)
        return cr

    lax.fori_loop(0, N_EXPERTS, carry_tails, 0)

    def run_copies(e, cr):
        _group_copies(fl_ref[base + e], stg_ref, o_ref[base + e], xs_ref, hg_ref[base + e], sem, False)
        return cr

    lax.fori_loop(0, N_EXPERTS, run_copies, 0)

    @pl.when(i == pl.num_programs(0) - 1)
    def _():
        zero_ref[...] = jnp.zeros_like(zero_ref)
        tail_group = fs_ref[N_EXPERTS]
        n_tail = (n_slots // SUBLANES - tail_group) // (MOE_BLK // SUBLANES)

        def tail_copy(j, wait):
            off = pl.multiple_of(tail_group * SUBLANES + j * MOE_BLK, MOE_BLK)
            cp = pltpu.make_async_copy(zero_ref, xs_ref.at[pl.ds(off, MOE_BLK)], fill_sem)
            cp.wait() if wait else cp.start()

        for wait in (False, True):
            def pad_body(e, cr):
                _group_copies(flen_ref[e], zero_ref, 0, xs_ref, fs_ref[e], fill_sem, wait)
                return cr
            lax.fori_loop(0, N_EXPERTS, pad_body, 0)
            lax.fori_loop(0, n_tail, lambda j, cr: (tail_copy(j, wait), cr)[1], 0)

    @pl.when(i > 0)
    def _():
        _wait_groups(tot_ref[jnp.maximum(i - 1, 0)], stg2_ref.at[1 - slot], xs_ref, sem2.at[1 - slot])

    @pl.when(i == pl.num_programs(0) - 1)
    def _():
        _wait_groups(tot_ref[i], stg_ref, xs_ref, sem)


def _dispatch(tables, fill_groups, fill_len_groups, h, rt, stage_base, n_slots):
    t, d = h.shape
    tm = DISP_TM
    grid_spec = pltpu.PrefetchScalarGridSpec(
        num_scalar_prefetch=len(tables) + 2,
        grid=(t // tm,),
        in_specs=[pl.BlockSpec((tm, d), lambda i, *_: (i, 0)),
                  pl.BlockSpec((tm, LANES), lambda i, *_: (i, 0)),
                  pl.BlockSpec((1, 1, LANES), lambda i, *_: (i, 0, 0)),
                  pl.BlockSpec((tm, tm), lambda i, *_: (0, 0))],
        out_specs=pl.BlockSpec(memory_space=pl.ANY),
        scratch_shapes=[pltpu.VMEM((2, STAGE_ROWS, XS_W), jnp.int32),
                        pltpu.VMEM((N_EXPERTS, SUBLANES, XS_W), jnp.int32),
                        pltpu.VMEM((MOE_BLK, XS_W), jnp.int32),
                        pltpu.SemaphoreType.DMA((2,)), pltpu.SemaphoreType.DMA(())],
    )
    return pl.pallas_call(
        _dispatch_kernel,
        grid_spec=grid_spec,
        out_shape=jax.ShapeDtypeStruct((n_slots, XS_W), jnp.int32),
        compiler_params=_cparams(("arbitrary",)),
    )(*tables, fill_groups, fill_len_groups, h, rt, stage_base, _strict_lower(tm))


def _expert_kernel(be_ref, nused_ref, xs_ref, w1_ref, b1_ref, w2_ref, b2_ref, perm_ref, y_ref,
                   w1s_ref, w2s_ref):
    i = pl.program_id(0)
    prev = be_ref[jnp.maximum(i - 1, 0)]
    changed = (i == 0) | (be_ref[i] != prev)
    dff2 = w1_ref.shape[2]
    tile = 2 * LANES

    @pl.when(changed & (i < nused_ref[0]))
    def _():
        for j in range(dff2 // tile):
            wj = w1_ref[0, :, j * tile:(j + 1) * tile].astype(BF16)
            w1s_ref[:, j * tile:(j + 1) * tile] = _dot(wj, perm_ref[...]).astype(BF16)
        w2s_ref[...] = w2_ref[0].astype(BF16)

    @pl.when(i < nused_ref[0])
    def _():
        hi_mask = jnp.int32(-65536)
        words = xs_ref[:, 0:PACK_W]
        x = jnp.concatenate(
            [pltpu.bitcast(lax.shift_left(words, 16), F32).astype(BF16),
             pltpu.bitcast(words & hi_mask, F32).astype(BF16)], axis=1)
        gate = pltpu.bitcast(xs_ref[:, PACK_W:PACK_W + 1], F32)
        acts = []
        for j in range(dff2 // tile):
            hb = _dot(x, w1s_ref[:, j * tile:(j + 1) * tile]) + b1_ref[0, :, j * tile:(j + 1) * tile]
            gp = jnp.minimum(hb[:, 0:LANES], SWIGLU_LIMIT)
            up = jnp.clip(hb[:, LANES:tile], -SWIGLU_LIMIT, SWIGLU_LIMIT)
            acts.append((gp * _sigmoid(SWIGLU_ALPHA * gp) * (up + 1.0)).astype(BF16))
        act = jnp.concatenate(acts, axis=1)
        y = (_dot(act, w2s_ref[...]) + b2_ref[0]) * gate
        lo = pltpu.bitcast(y[:, 0:PACK_W].astype(BF16).astype(F32), jnp.int32)
        hi = pltpu.bitcast(y[:, PACK_W:2 * PACK_W].astype(BF16).astype(F32), jnp.int32)
        y_ref[...] = lax.shift_right_logical(lo, 16) | (hi & hi_mask)

    @pl.when(i >= nused_ref[0])
    def _():
        y_ref[...] = jnp.zeros_like(y_ref)


def _experts(block_exp, n_used, xs, w1, b1p, w2, b2, perm, n_blocks):
    blk = MOE_BLK
    d = w1.shape[1]
    dff2 = w1.shape[2]
    dff = w2.shape[1]

    def x_idx(i, be, nu):
        return (jnp.minimum(i, nu[0] - 1), 0)

    grid_spec = pltpu.PrefetchScalarGridSpec(
        num_scalar_prefetch=2,
        grid=(n_blocks,),
        in_specs=[pl.BlockSpec((blk, XS_W), x_idx),
                  pl.BlockSpec((1, d, dff2), lambda i, be, nu: (be[i], 0, 0)),
                  pl.BlockSpec((1, 1, dff2), lambda i, be, nu: (be[i], 0, 0)),
                  pl.BlockSpec((1, dff, d), lambda i, be, nu: (be[i], 0, 0)),
                  pl.BlockSpec((1, 1, d), lambda i, be, nu: (be[i], 0, 0)),
                  pl.BlockSpec(perm.shape, lambda i, be, nu: (0, 0))],
        out_specs=pl.BlockSpec((blk, PACK_W), lambda i, be, nu: (i, 0)),
        scratch_shapes=[pltpu.VMEM((d, dff2), BF16), pltpu.VMEM((dff, d), BF16)],
    )
    return pl.pallas_call(
        _expert_kernel,
        grid_spec=grid_spec,
        out_shape=jax.ShapeDtypeStruct((n_blocks * blk, PACK_W), jnp.int32),
        compiler_params=_cparams(("arbitrary",)),
    )(block_exp, n_used, xs, w1, b1p, w2, b2, perm)


def _combine_kernel(o_ref, g_ref, hg_ref, tot_ref, x1_ref, rt_ref, sb_ref, tri_ref, g2_ref, fnw_ref,
                    ys_ref, out_ref, stg_ref, sem):
    tm = x1_ref.shape[1]
    step = pl.program_id(0) * pl.num_programs(1) + pl.program_id(1)
    n_steps = pl.num_programs(0) * pl.num_programs(1)
    slot = step % 2
    hi_mask = jnp.int32(-65536)

    def fetch(tile, into):
        def body(e, cr):
            _group_copies(g_ref[tile * N_EXPERTS + e], ys_ref, hg_ref[tile * N_EXPERTS + e],
                          stg_ref.at[into], o_ref[tile * N_EXPERTS + e], sem.at[into], False)
            return cr
        lax.fori_loop(0, N_EXPERTS, body, 0)

    @pl.when(step == 0)
    def _():
        stg_ref[...] = jnp.zeros_like(stg_ref)
        fetch(0, 0)

    @pl.when(step + 1 < n_steps)
    def _():
        fetch(step + 1, 1 - slot)

    _wait_groups(tot_ref[step], ys_ref, stg_ref.at[slot], sem.at[slot])

    pos = _stage_positions(rt_ref[0], sb_ref[0], tri_ref[...])
    col_f = lax.broadcasted_iota(jnp.int32, (tm, STAGE_ROWS), 1).astype(F32)
    sel = jnp.zeros((tm, STAGE_ROWS), F32)
    for k in range(TOP_K):
        sel = jnp.where(col_f == pos[k], 1.0, sel)
    sel = sel.astype(BF16)
    words = stg_ref[slot]
    lo = pltpu.bitcast(lax.shift_left(words, 16), F32).astype(BF16)
    hi = pltpu.bitcast(words & hi_mask, F32).astype(BF16)
    moe = jnp.concatenate([_dot(sel, lo), _dot(sel, hi)], axis=1)
    x2 = x1_ref[0] + g2_ref[0] * moe
    ms = jnp.mean(x2 * x2, axis=-1, keepdims=True)
    out_ref[0] = x2 * lax.rsqrt(ms + RMS_EPS) * fnw_ref[...]


def _combine(tables, x1, rt, stage_base, g2, fnw, ys):
    b, l, d = x1.shape
    tm = DISP_TM
    per_l = l // tm
    tok = lambda n: pl.BlockSpec((1, tm, n), lambda i, j, *_: (i, j, 0))
    grid_spec = pltpu.PrefetchScalarGridSpec(
        num_scalar_prefetch=len(tables),
        grid=(b, per_l),
        in_specs=[tok(d), tok(LANES),
                  pl.BlockSpec((1, 1, LANES), lambda i, j, *_: (i * per_l + j, 0, 0)),
                  pl.BlockSpec((tm, tm), lambda i, j, *_: (0, 0)),
                  pl.BlockSpec((1, 1, d), lambda i, j, *_: (i, 0, 0)),
                  pl.BlockSpec(fnw.shape, lambda i, j, *_: (0, 0)),
                  pl.BlockSpec(memory_space=pl.ANY)],
        out_specs=tok(d),
        scratch_shapes=[pltpu.VMEM((2, STAGE_ROWS, PACK_W), jnp.int32),
                        pltpu.SemaphoreType.DMA((2,))],
    )
    return pl.pallas_call(
        _combine_kernel,
        grid_spec=grid_spec,
        out_shape=jax.ShapeDtypeStruct((b, l, d), F32),
        compiler_params=_cparams(("arbitrary", "arbitrary")),
    )(*tables, x1, rt, stage_base, _strict_lower(tm), g2, fnw, ys)


def _pool_constants():
    i = np.arange(POOL_TM)[:, None]
    j = np.arange(POOL_TM)[None, :]
    same_row = (i // GRID_W) == (j // GRID_W)
    mats, cnts = [], []
    for w in POOL_WINDOWS:
        band = same_row & (j - i >= -(w // 2)) & (j - i < w - w // 2)
        mats.append(band)
        cnts.append(np.broadcast_to(band.sum(axis=1, keepdims=True), (POOL_TM, POOL_GW)))
    return (jnp.asarray(np.stack(mats), BF16), jnp.asarray(np.stack(cnts), F32))


def _head_expand():
    e = np.zeros((LANES, SSD_W), np.float32)
    for h in range(SSD_HEADS):
        e[h, h * SSD_HEADDIM:(h + 1) * SSD_HEADDIM] = 1.0
    return jnp.asarray(e, BF16)


def _deinterleave_perm():
    n = 2 * LANES
    p = np.zeros((n, n), np.float32)
    for k in range(LANES):
        p[2 * k, k] = 1.0
        p[2 * k + 1, LANES + k] = 1.0
    return jnp.asarray(p, BF16)


def _strict_lower(n):
    return jnp.asarray(np.tril(np.ones((n, n), np.float32), -1), BF16)


def _pad_lanes(a, n):
    return jnp.pad(a, [(0, 0)] * (a.ndim - 1) + [(0, n - a.shape[-1])])


def kernel(x, c, ctx, c_ctx, w_mod, b_mod, norm1_w, norm2_w, w_in, conv_w, conv_b, dt_bias, a_log,
           d_skip, ssd_norm_w, pool_w, pool_scale, w_out, router_w, router_b, w1, b1, w2, b2,
           final_norm_w):
    depth = w_mod.shape[0]
    assert depth == 1, "single-layer problem"
    b, l, d = x.shape
    lc = ctx.shape[1]
    xbcdt = CONV_CH + 2 * SSD_HEADS

    mod_rows = 2 * SUBLANES
    cc = jnp.zeros((mod_rows, d), F32).at[0:b].set(c).at[b].set(c_ctx)
    mod = _modulation(cc, w_mod[0], b_mod[0])
    sh1, sc1, g1, sh2, sc2, g2 = [m.reshape(b, 1, d) for m in jnp.split(mod[0:b], 6, axis=-1)]
    csh1, csc1 = [jnp.broadcast_to(m.reshape(1, 1, d), (b, 1, d))
                  for m in jnp.split(mod[b:b + 1], 6, axis=-1)[0:2]]

    wi = w_in[0]
    wx = wi[:, 0:CONV_CH].astype(BF16)
    wd = jnp.concatenate([_pad_lanes(wi[:, CONV_CH:CONV_CH + SSD_HEADS], LANES),
                          _pad_lanes(wi[:, CONV_CH + SSD_HEADS:xbcdt], LANES)], axis=1).astype(BF16)
    wz = wi[:, xbcdt:xbcdt + SSD_W].astype(BF16)
    wp = wi[:, xbcdt + SSD_W:].astype(BF16)
    n1 = norm1_w[0].reshape(1, d)
    conv_w8 = jnp.pad(conv_w[0], ((0, SUBLANES - SSD_CONV), (0, 0)))
    conv_b1 = conv_b[0].reshape(1, CONV_CH)
    dtb = _pad_lanes(dt_bias[0], LANES).reshape(2, 1, LANES)
    alog = _pad_lanes(a_log[0], LANES).reshape(2, 1, LANES)
    dsk = jnp.repeat(d_skip[0], SSD_HEADDIM).reshape(1, SSD_W)
    snw = ssd_norm_w[0].reshape(1, SSD_W)
    expand = _head_expand()

    xbc_c, dt_c = _inproj(ctx, n1, csh1, csc1, (wx, wd), min(PROJ_TM, lc))
    zero_state = jnp.zeros((b, 2, SSD_GROUPS, SSD_STATE, GROUP_W), F32)
    _, ctx_states = _ssd(xbc_c, dt_c, None, conv_w8, conv_b1, dtb, alog, dsk, snw, expand, zero_state)

    xbc, dt, z, u_pool = _inproj(x, n1, sh1, sc1, (wx, wd, wz, wp), PROJ_TM)
    y_ssd, _ = _ssd(xbc, dt, z, conv_w8, conv_b1, dtb, alog, dsk, snw, expand, ctx_states)

    pool_a, pool_cnt = _pool_constants()
    pw = pool_w[0].astype(BF16)
    zero_gw = jnp.zeros((POOL_GW, POOL_GW), BF16)
    pw_pairs = jnp.stack([jnp.block([[pw[2 * gp], zero_gw], [zero_gw, pw[2 * gp + 1]]])
                          for gp in range(len(POOL_WINDOWS) // 2)])
    x1, h2, rt, tcnt = _mix(
        y_ssd, u_pool, x, pool_a, pool_cnt, pw_pairs, pool_scale[0].reshape(1, POOL_W),
        w_out[0].astype(BF16), g1, norm2_w[0].reshape(1, d), sh2, sc2,
        _pad_lanes(router_w[0], LANES), _pad_lanes(router_b[0].reshape(1, N_EXPERTS), LANES))

    i32 = jnp.int32
    t = b * l
    n_tiles = t // DISP_TM
    tc = tcnt[:, 0:MIX_TM // DISP_TM, 0:N_EXPERTS].reshape(n_tiles, N_EXPERTS).astype(i32)
    counts = jnp.sum(tc, axis=0)
    run_start = jnp.cumsum(tc, axis=0) - tc
    padded = (counts + MOE_BLK - 1) // MOE_BLK * MOE_BLK
    pad_end = jnp.cumsum(padded)
    pad_start = pad_end - padded
    carried = run_start % SUBLANES
    span = carried + tc
    groups = (span + SUBLANES - 1) // SUBLANES
    full = span // SUBLANES
    stage_group = jnp.cumsum(groups, axis=1) - groups
    slot_group = (pad_start[None, :] + run_start - carried) // SUBLANES
    flush = full.at[n_tiles - 1].set(groups[n_tiles - 1])
    flat = lambda a: a.reshape(n_tiles * N_EXPERTS).astype(i32)
    stage_base = _pad_lanes((stage_group * SUBLANES + carried).astype(F32), LANES)
    stage_base = stage_base.reshape(n_tiles, 1, LANES)
    n_blocks = (t * TOP_K) // MOE_BLK + N_EXPERTS
    n_used = (pad_end[-1] // MOE_BLK).astype(i32).reshape(1)
    blk_start = jnp.minimum(jnp.arange(n_blocks, dtype=i32), n_used[0] - 1) * MOE_BLK
    block_exp = jnp.minimum(jnp.sum(blk_start[:, None] >= pad_end[None, :], axis=1),
                            N_EXPERTS - 1).astype(i32)
    written = (counts + SUBLANES - 1) // SUBLANES * SUBLANES
    fill_group = (jnp.concatenate([pad_start + written, pad_end[-1:]]) // SUBLANES).astype(i32)
    fill_groups = (jnp.concatenate([padded - written, jnp.zeros((1,), i32)]) // SUBLANES).astype(i32)

    xs = _dispatch((flat(stage_group), flat(full), flat(flush), flat(span % SUBLANES), flat(slot_group),
                    jnp.sum(flush, axis=1).astype(i32), jnp.sum(groups, axis=1).astype(i32)),
                   fill_group, fill_groups, h2.reshape(t, d), rt.reshape(t, LANES), stage_base,
                   n_blocks * MOE_BLK)
    dff2 = w1.shape[-1]
    b1p = jnp.concatenate(
        [b1[0].reshape(N_EXPERTS, dff2 // (2 * LANES), LANES, 2)[..., 0],
         b1[0].reshape(N_EXPERTS, dff2 // (2 * LANES), LANES, 2)[..., 1]], axis=-1
    ).reshape(N_EXPERTS, 1, dff2)
    ys = _experts(block_exp, n_used, xs, w1[0], b1p, w2[0], b2[0].reshape(N_EXPERTS, 1, d),
                  _deinterleave_perm(), n_blocks)
    return _combine((flat(stage_group), flat(groups), flat(slot_group),
                     jnp.sum(groups, axis=1).astype(i32)), x1, rt, stage_base, g2,
                    final_norm_w.reshape(1, d), ys)
```

```python
import functools

import numpy as np
import jax
import jax.numpy as jnp
from jax import lax
from jax.experimental import pallas as pl
from jax.experimental.pallas import tpu as pltpu

F32 = jnp.float32
BF16 = jnp.bfloat16

SSD_HEADDIM = 64
SSD_GROUPS = 4
SSD_HPG = 6
SSD_HEADS = SSD_GROUPS * SSD_HPG
SSD_STATE = 128
SSD_CONV = 5
SSD_CHUNK = 128
SSD_W = SSD_HEADS * SSD_HEADDIM
GROUP_W = SSD_HPG * SSD_HEADDIM
CONV_CH = SSD_W + 2 * SSD_GROUPS * SSD_STATE
POOL_WINDOWS = (2, 4, 8, 16)
POOL_GW = 128
POOL_W = POOL_GW * len(POOL_WINDOWS)
GRID_W = 64
N_EXPERTS = 32
TOP_K = 4
SWIGLU_ALPHA = 1.702
SWIGLU_LIMIT = 7.0
RMS_EPS = 1e-6
LOG2_E = 1.4426950408889634

LANES = 128
SUBLANES = 8
VMEM_LIMIT_BYTES = 56 * 1024 * 1024

MOD_TN = 1024
PROJ_TM = 512
MIX_TM = 512
POOL_TM = 256
MOE_BLK = 512
DISP_TM = 256
NEG_BIG = -1e30

GATE_LANE = 2 * TOP_K
PACK_W = 512
XS_W = PACK_W + LANES
STAGE_CHUNK = 256
_STAGE_NEED = TOP_K * DISP_TM + N_EXPERTS * 2 * (SUBLANES - 1) + SUBLANES
STAGE_ROWS = -(-_STAGE_NEED // STAGE_CHUNK) * STAGE_CHUNK
RUN_BITS = (32, 16, 8, 4, 2, 1)
RUN_SMALL_BIT = 4
WAIT_BITS = (128, 64, 32, 16, 8, 4, 2, 1)


def _sigmoid(x):
    return 0.5 * jnp.tanh(0.5 * x) + 0.5


def _split2(a):
    hi = a.astype(BF16)
    mid = (a - hi.astype(F32)).astype(BF16)
    return hi, mid


def _split3(a):
    hi = a.astype(BF16)
    r = a - hi.astype(F32)
    mid = r.astype(BF16)
    lo = (r - mid.astype(F32)).astype(BF16)
    return hi, mid, lo


def _dot(a, b):
    return jnp.dot(a, b, preferred_element_type=F32)


def _dot_exact_rhs(a_f32, b_bf16, parts):
    pieces = _split3(a_f32) if parts == 3 else _split2(a_f32)
    out = _dot(pieces[0], b_bf16)
    for p in pieces[1:]:
        out = out + _dot(p, b_bf16)
    return out


def _dot_hi(a_f32, b_f32):
    a0, a1, a2 = _split3(a_f32)
    b0, b1, b2 = _split3(b_f32)
    out = _dot(a0, b0)
    out = out + _dot(a0, b1) + _dot(a1, b0)
    out = out + _dot(a1, b1) + _dot(a0, b2) + _dot(a2, b0)
    return out


def _cparams(sem):
    return pltpu.CompilerParams(dimension_semantics=sem, vmem_limit_bytes=VMEM_LIMIT_BYTES)


def _mod_kernel(c_ref, w_ref, b_ref, o_ref):
    c = c_ref[...]
    s = c * _sigmoid(c)
    o_ref[...] = _dot_hi(s, w_ref[...]) + b_ref[...]


def _modulation(cc, w_mod, b_mod):
    rows, d = cc.shape
    n = w_mod.shape[1]
    return pl.pallas_call(
        _mod_kernel,
        grid=(n // MOD_TN,),
        in_specs=[pl.BlockSpec((rows, d), lambda j: (0, 0)),
                  pl.BlockSpec((d, MOD_TN), lambda j: (0, j)),
                  pl.BlockSpec((1, MOD_TN), lambda j: (0, j))],
        out_specs=pl.BlockSpec((rows, MOD_TN), lambda j: (0, j)),
        out_shape=jax.ShapeDtypeStruct((rows, n), F32),
        compiler_params=_cparams(("arbitrary",)),
    )(cc, w_mod, b_mod.reshape(1, n))


def _inproj_kernel(n_groups, x0_ref, xn_ref, nw_ref, sh0_ref, sc0_ref, shn_ref, scn_ref, *refs):
    w_refs = refs[0:n_groups]
    out_refs = refs[n_groups:2 * n_groups]
    hb_even_ref, hb_odd_ref = refs[2 * n_groups:]
    step = pl.program_id(0) * pl.num_programs(1) + pl.program_id(1)

    def normed(x, sh, sc):
        ms = jnp.mean(x * x, axis=-1, keepdims=True)
        h = x * lax.rsqrt(ms + RMS_EPS) * nw_ref[...]
        return (h * (1.0 + sc) + sh).astype(BF16)

    @pl.when(step == 0)
    def _():
        hb_even_ref[...] = normed(x0_ref[0], sh0_ref[0], sc0_ref[0])

    def project(cur_ref, nxt_ref):
        hb = cur_ref[...]
        for w_ref, out_ref in zip(w_refs, out_refs):
            out_ref[0] = _dot(hb, w_ref[...]).astype(out_ref.dtype)
        nxt_ref[...] = normed(xn_ref[0], shn_ref[0], scn_ref[0])

    pl.when(step % 2 == 0)(functools.partial(project, hb_even_ref, hb_odd_ref))
    pl.when(step % 2 == 1)(functools.partial(project, hb_odd_ref, hb_even_ref))


def _inproj(x, norm_w, shift, scale, weights, out_dtypes, tm):
    b, l, d = x.shape
    per_l = l // tm
    last = b * per_l - 1

    def nxt(i, j):
        s = jnp.minimum(i * per_l + j + 1, last)
        return s // per_l, s % per_l

    full = lambda a: pl.BlockSpec(a.shape, lambda i, j: (0, 0))
    tok = lambda n: pl.BlockSpec((1, tm, n), lambda i, j: (i, j, 0))
    first_b = pl.BlockSpec((1, 1, d), lambda i, j: (0, 0, 0))
    next_b = pl.BlockSpec((1, 1, d), lambda i, j: (nxt(i, j)[0], 0, 0))
    return pl.pallas_call(
        functools.partial(_inproj_kernel, len(weights)),
        grid=(b, per_l),
        in_specs=[pl.BlockSpec((1, tm, d), lambda i, j: (0, 0, 0)),
                  pl.BlockSpec((1, tm, d), lambda i, j: (*nxt(i, j), 0)),
                  full(norm_w), first_b, first_b, next_b, next_b] + [full(w) for w in weights],
        out_specs=[tok(w.shape[1]) for w in weights],
        out_shape=[jax.ShapeDtypeStruct((b, l, w.shape[1]), dt)
                   for w, dt in zip(weights, out_dtypes)],
        scratch_shapes=[pltpu.VMEM((tm, d), BF16), pltpu.VMEM((tm, d), BF16)],
        compiler_params=_cparams(("arbitrary", "arbitrary")),
    )(x, x, norm_w, shift, scale, shift, scale, *weights)


def _ssd_kernel(nc, emit_y, *refs):
    if emit_y:
        (xbc_ref, prev_ref, next_ref, dt_ref, z_ref, cw_ref, cb_ref, dtb_ref, alog_ref, dsk_ref,
         nw_ref, exp_ref, init_ref, y_ref, fin_ref, act_ref, ybuf_ref, yf_ref, st_ref) = refs
        y_part = (z_ref, dsk_ref, nw_ref, y_ref, ybuf_ref, yf_ref)
    else:
        (xbc_ref, prev_ref, next_ref, dt_ref, cw_ref, cb_ref, dtb_ref, alog_ref,
         exp_ref, init_ref, fin_ref, act_ref, st_ref) = refs
        y_part = None
    ph = pl.program_id(1)
    c = pl.program_id(2)
    ci = c + ph * (nc - 1 - 2 * c)

    @pl.when(c == 0)
    def _():
        st_ref[...] = init_ref[0, ph]

    @pl.when(ph == 0)
    def _():
        _conv_silu(nc, ci, xbc_ref, prev_ref, next_ref, cw_ref, cb_ref, act_ref)

    _ssd_scan(nc, ph, c, ci, dt_ref, dtb_ref, alog_ref, exp_ref, act_ref, st_ref, fin_ref, y_part)


def _conv_silu(nc, ci, xbc_ref, prev_ref, next_ref, cw_ref, cb_ref, act_ref):
    ch = SSD_CHUNK
    has_prev = ci > 0
    has_next = ci < nc - 1
    row8 = lax.broadcasted_iota(jnp.int32, (SUBLANES, LANES), 0)
    half = SSD_CONV // 2
    for j in range(CONV_CH // LANES):
        cols = slice(j * LANES, (j + 1) * LANES)
        n_t = ch // SUBLANES
        tiles = [jnp.where(has_prev, prev_ref[0, SUBLANES:2 * SUBLANES, cols].astype(F32), 0.0)]
        for i in range(n_t // 2):
            both = xbc_ref[0, 2 * i * SUBLANES:2 * (i + 1) * SUBLANES, cols].astype(F32)
            tiles += [both[0:SUBLANES], both[SUBLANES:2 * SUBLANES]]
        tiles.append(jnp.where(has_next, next_ref[0, 0:SUBLANES, cols].astype(F32), 0.0))
        acc = [cb_ref[:, cols] + cw_ref[half:half + 1, cols] * tiles[i + 1] for i in range(n_t)]
        for s in range(1, half + 1):
            rot = [pltpu.roll(tl, s, axis=0) for tl in tiles[0:n_t + 1]]
            wk = cw_ref[half - s:half - s + 1, cols]
            for i in range(n_t):
                acc[i] = acc[i] + wk * jnp.where(row8 < s, rot[i], rot[i + 1])
            rot = [pltpu.roll(tl, SUBLANES - s, axis=0) for tl in tiles[1:n_t + 2]]
            wk = cw_ref[half + s:half + s + 1, cols]
            for i in range(n_t):
                acc[i] = acc[i] + wk * jnp.where(row8 >= SUBLANES - s, rot[i + 1], rot[i])
        silu = [0.5 * v * jnp.tanh(0.5 * v) + 0.5 * v for v in acc]
        act_ref[ci, :, cols] = jnp.concatenate(silu, axis=0).astype(BF16)


def _ssd_scan(nc, ph, c, ci, dt_ref, dtb_ref, alog_ref, exp_ref, act_ref, st_ref, fin_ref, y_part):
    ch = SSD_CHUNK
    emit_y = y_part is not None
    dtr = dt_ref[0] + dtb_ref[0]
    small = jnp.exp(-jnp.abs(dtr))
    one_plus = 1.0 + small
    log1p_small = jnp.where(one_plus == 1.0, small, jnp.log(one_plus) * (small / (one_plus - 1.0)))
    dtv = jnp.maximum(dtr, 0.0) + log1p_small
    a = dtv * (-jnp.exp(alog_ref[0]) * LOG2_E)
    row = lax.broadcasted_iota(jnp.int32, (ch, ch), 0)
    col = lax.broadcasted_iota(jnp.int32, (ch, ch), 1)
    tmask = (row - col) * (1 - 2 * ph) >= 0
    tri = jnp.where(tmask, 1.0, 0.0).astype(BF16)
    cs = _dot_exact_rhs_left(tri, a)
    tot = jnp.where(ph == 0, cs[ch - 1:ch, :], cs[0:1, :])
    cs_t = cs.T
    e_cs = jnp.exp2(cs)
    e_dec = jnp.exp2(tot - cs)
    e_tot = jnp.exp2(tot)
    expand = exp_ref[...]
    wdec_x = _dot((dtv * e_dec).astype(BF16), expand).astype(BF16)
    etot_x = _dot_exact_rhs(jnp.broadcast_to(e_tot, (SUBLANES, LANES)), expand, 2)[0:1]
    if emit_y:
        z_ref, dsk_ref, nw_ref, y_ref, ybuf_ref, yf_ref = y_part
        ecs_x = _dot(e_cs.astype(BF16), expand)
        dt_t = dtv.T

    lane = lax.broadcasted_iota(jnp.int32, (ch, LANES), 1)
    for g in range(SSD_GROUPS):
        gs = slice(g * GROUP_W, (g + 1) * GROUP_W)
        b_bf = act_ref[ci, :, SSD_W + g * SSD_STATE:SSD_W + (g + 1) * SSD_STATE]
        x_bf = act_ref[ci, :, gs]
        s_prev = st_ref[g]
        if emit_y:
            c_bf = act_ref[ci, :, SSD_W + (SSD_GROUPS + g) * SSD_STATE:
                           SSD_W + (SSD_GROUPS + g + 1) * SSD_STATE]
            cb = lax.dot_general(c_bf, b_bf, (((1,), (1,)), ((), ())), preferred_element_type=F32)
            y_off = _dot(c_bf, s_prev.astype(BF16)) * ecs_x[:, gs]
        x_dec = x_bf * wdec_x[:, gs]
        st_ref[g] = s_prev * etot_x[:, gs] + lax.dot_general(
            b_bf, x_dec, (((0,), (0,)), ((), ())), preferred_element_type=F32)
        if not emit_y:
            continue
        for q in range(SSD_HPG // 2):
            lmats = []
            for h in (g * SSD_HPG + 2 * q, g * SSD_HPG + 2 * q + 1):
                diff = cs[:, h:h + 1] - cs_t[h:h + 1, :]
                dec = jnp.exp2(jnp.where(tmask, diff, NEG_BIG))
                lmats.append((dec * cb * dt_t[h:h + 1, :]).astype(BF16))
            xp = x_bf[:, q * LANES:(q + 1) * LANES]
            zero = jnp.zeros_like(xp)
            rhs = jnp.concatenate([jnp.where(lane < SSD_HEADDIM, xp, zero),
                                   jnp.where(lane >= SSD_HEADDIM, xp, zero)], axis=0)
            y_diag = _dot(jnp.concatenate(lmats, axis=1), rhs)
            ps = slice(g * GROUP_W + q * LANES, g * GROUP_W + (q + 1) * LANES)
            ybuf_ref[:, ps] = y_diag + y_off[:, q * LANES:(q + 1) * LANES]

    if emit_y:
        @pl.when(ph == 0)
        def _():
            yf_ref[ci] = ybuf_ref[...].astype(BF16)

        @pl.when(ph == 1)
        def _():
            yt = (yf_ref[ci].astype(F32) + ybuf_ref[...]
                  + act_ref[ci, :, 0:SSD_W].astype(F32) * dsk_ref[...])
            zz = z_ref[0].astype(F32)
            gt = yt * (zz * _sigmoid(zz))
            ms = jnp.mean(gt * gt, axis=-1, keepdims=True)
            y_ref[0] = (gt * lax.rsqrt(ms + RMS_EPS) * nw_ref[...]).astype(y_ref.dtype)

    @pl.when(c == nc - 1)
    def _():
        fin_ref[0, ph] = st_ref[...]


def _dot_exact_rhs_left(sel_bf16, a_f32):
    hi, mid, lo = _split3(a_f32)
    return _dot(sel_bf16, hi) + _dot(sel_bf16, mid) + _dot(sel_bf16, lo)


def _ssd(xbc, dt, z, conv_w8, conv_b, dt_bias, a_log, d_skip_x, norm_w, expand, init):
    emit_y = z is not None
    b, l, _ = xbc.shape
    ch = SSD_CHUNK
    nc = l // ch
    halo = 2 * SUBLANES
    per_ch = ch // halo

    def cidx(ph, c):
        return c + ph * (nc - 1 - 2 * c)

    def xidx(ph, c):
        return jnp.where(ph == 0, c, nc - 1)

    def out_idx(ph, c):
        return jnp.where(ph == 0, nc - 1, nc - 1 - c)

    full2 = lambda a: pl.BlockSpec(a.shape, lambda i, ph, c: (0, 0))
    st_spec = pl.BlockSpec((1, 2, SSD_GROUPS, SSD_STATE, GROUP_W), lambda i, ph, c: (i, 0, 0, 0, 0))
    per_phase = pl.BlockSpec((1, 1, LANES), lambda i, ph, c: (ph, 0, 0))
    y_spec = pl.BlockSpec((1, ch, SSD_W), lambda i, ph, c: (i, out_idx(ph, c), 0))
    operands = [(xbc, pl.BlockSpec((1, ch, CONV_CH), lambda i, ph, c: (i, xidx(ph, c), 0))),
                (xbc, pl.BlockSpec((1, halo, CONV_CH),
                                   lambda i, ph, c: (i, jnp.maximum(xidx(ph, c) * per_ch - 1, 0), 0))),
                (xbc, pl.BlockSpec((1, halo, CONV_CH),
                                   lambda i, ph, c: (i, jnp.minimum((xidx(ph, c) + 1) * per_ch,
                                                                    l // halo - 1), 0))),
                (dt, pl.BlockSpec((1, ch, LANES), lambda i, ph, c: (i, cidx(ph, c), ph)))]
    if emit_y:
        operands.append((z, y_spec))
    operands += [(conv_w8, full2(conv_w8)), (conv_b, full2(conv_b)),
                 (dt_bias, per_phase), (a_log, per_phase)]
    if emit_y:
        operands += [(d_skip_x, full2(d_skip_x)), (norm_w, full2(norm_w))]
    operands += [(expand, full2(expand)), (init, st_spec)]
    st_shape = jax.ShapeDtypeStruct((b, 2, SSD_GROUPS, SSD_STATE, GROUP_W), F32)
    scratch = [pltpu.VMEM((nc, ch, CONV_CH), BF16)]
    if emit_y:
        scratch += [pltpu.VMEM((ch, SSD_W), F32),
                    pltpu.VMEM((nc, ch, SSD_W), BF16)]
    scratch.append(pltpu.VMEM((SSD_GROUPS, SSD_STATE, GROUP_W), F32))
    outs = pl.pallas_call(
        functools.partial(_ssd_kernel, nc, emit_y),
        grid=(b, 2, nc),
        in_specs=[spec for _, spec in operands],
        out_specs=[y_spec, st_spec] if emit_y else [st_spec],
        out_shape=[jax.ShapeDtypeStruct((b, l, SSD_W), BF16), st_shape] if emit_y else [st_shape],
        scratch_shapes=scratch,
        compiler_params=_cparams(("arbitrary", "arbitrary", "arbitrary")),
    )(*[a for a, _ in operands])
    return outs if emit_y else (None, outs[0])


def _mix_kernel(y_ref, u_ref, x_ref, pa_ref, pcnt_ref, pw_ref, psc_ref, wo_ref, g1_ref,
                nw_ref, sh_ref, sc_ref, rw_ref, rb_ref,
                x1_ref, h_ref, rt_ref, tcnt_ref):
    tm = x_ref.shape[1]

    pooled = []
    for g in range(len(POOL_WINDOWS)):
        parts = []
        for r in range(tm // POOL_TM):
            u = u_ref[0, r * POOL_TM:(r + 1) * POOL_TM, g * POOL_GW:(g + 1) * POOL_GW]
            wsum = _dot(pa_ref[g], u)
            parts.append((wsum / pcnt_ref[g] - u.astype(F32)).astype(BF16))
        pooled.append(jnp.concatenate(parts, axis=0))
    mapped = []
    for gp in range(len(POOL_WINDOWS) // 2):
        pair = jnp.concatenate(pooled[2 * gp:2 * gp + 2], axis=1)
        mapped.append((_dot(pair, pw_ref[gp])
                       * psc_ref[:, 2 * gp * POOL_GW:(2 * gp + 2) * POOL_GW]).astype(BF16))
    y_pool = jnp.concatenate(mapped, axis=1)

    mix = _dot(y_ref[0], wo_ref[0:SSD_W, :]) + _dot(y_pool, wo_ref[SSD_W:SSD_W + POOL_W, :])
    x1 = x_ref[0] + g1_ref[0] * mix
    x1_ref[0] = x1

    ms = jnp.mean(x1 * x1, axis=-1, keepdims=True)
    h = x1 * lax.rsqrt(ms + RMS_EPS) * nw_ref[...]
    h = h * (1.0 + sc_ref[0]) + sh_ref[0]
    h_ref[0] = h

    h0, h1 = _split2(h)
    rw2 = jnp.concatenate(_split2(rw_ref[...]), axis=1)
    t0 = _dot(h0, rw2)
    t1 = _dot(h1, rw2)
    logits = (t0[:, 0:LANES] + t0[:, LANES:2 * LANES] + t1[:, 0:LANES] + t1[:, LANES:2 * LANES]
              + rb_ref[...])
    lane = lax.broadcasted_iota(jnp.int32, (tm, LANES), 1)
    lane_f = lane.astype(F32)
    work = jnp.where(lane < N_EXPERTS, logits, NEG_BIG)
    vals, hots = [], []
    for _ in range(TOP_K):
        m = jnp.max(work, axis=-1, keepdims=True)
        first_idx = jnp.min(jnp.where(work == m, lane_f, float(LANES)), axis=-1, keepdims=True)
        hot = lane_f == first_idx
        vals.append(m)
        hots.append(hot)
        work = jnp.where(hot, 2.0 * NEG_BIG, work)
    exps = [jnp.exp(v - vals[0]) for v in vals]
    denom = exps[0] + exps[1] + exps[2] + exps[3]

    onehot = jnp.zeros((tm, LANES), F32)
    for hot in hots:
        onehot = onehot + jnp.where(hot, 1.0, 0.0)
    packed = jnp.zeros((tm, LANES), F32)
    for k in range(TOP_K):
        idx_k = jnp.sum(jnp.where(hots[k], lane_f, 0.0), axis=-1, keepdims=True)
        packed = jnp.where(lane == k, idx_k, packed)
        packed = jnp.where(lane == GATE_LANE + k, exps[k] / denom, packed)
    rt_ref[0] = packed
    row8 = lax.broadcasted_iota(jnp.int32, (SUBLANES, LANES), 0)
    tcnt = jnp.zeros((SUBLANES, LANES), F32)
    for r in range(tm // DISP_TM):
        sub = jnp.sum(onehot[r * DISP_TM:(r + 1) * DISP_TM], axis=0, keepdims=True)
        tcnt = jnp.where(row8 == r, sub, tcnt)
    tcnt_ref[0] = tcnt


def _mix(y, u, x, pool_a, pool_cnt, pool_w, pool_scale, w_out, g1, norm_w, shift, scale,
         router_w, router_b):
    b, l, d = x.shape
    tm = MIX_TM
    per_l = l // tm
    tok = lambda n: pl.BlockSpec((1, tm, n), lambda i, j: (i, j, 0))
    per_b = pl.BlockSpec((1, 1, d), lambda i, j: (i, 0, 0))
    full = lambda a: pl.BlockSpec(a.shape, lambda i, j: (0,) * a.ndim)
    return pl.pallas_call(
        _mix_kernel,
        grid=(b, per_l),
        in_specs=[tok(SSD_W), tok(POOL_W), tok(d), full(pool_a), full(pool_cnt), full(pool_w),
                  full(pool_scale), full(w_out), per_b, full(norm_w), per_b, per_b,
                  full(router_w), full(router_b)],
        out_specs=[tok(d), tok(d), tok(LANES),
                   pl.BlockSpec((1, SUBLANES, LANES), lambda i, j: (i * per_l + j, 0, 0))],
        out_shape=[jax.ShapeDtypeStruct((b, l, d), F32),
                   jax.ShapeDtypeStruct((b, l, d), F32),
                   jax.ShapeDtypeStruct((b, l, LANES), F32),
                   jax.ShapeDtypeStruct((b * per_l, SUBLANES, LANES), F32)],
        compiler_params=_cparams(("arbitrary", "arbitrary")),
    )(y, u, x, pool_a, pool_cnt, pool_w, pool_scale, w_out, g1, norm_w, shift, scale,
      router_w, router_b)


def _group_copies(n_groups, src_ref, src_group, dst_ref, dst_group, sem, wait):
    def pieces(bits):
        for bit in bits:
            @pl.when((n_groups & bit) != 0)
            def _():
                done = n_groups & ~(2 * bit - 1)
                src = src_ref.at[pl.ds(pl.multiple_of((src_group + done) * SUBLANES, SUBLANES),
                                       bit * SUBLANES)]
                dst = dst_ref.at[pl.ds(pl.multiple_of((dst_group + done) * SUBLANES, SUBLANES),
                                       bit * SUBLANES)]
                cp = pltpu.make_async_copy(src, dst, sem)
                cp.wait() if wait else cp.start()

    split = RUN_BITS.index(RUN_SMALL_BIT)
    pl.when(n_groups >= 2 * RUN_SMALL_BIT)(lambda: pieces(RUN_BITS[:split]))
    pieces(RUN_BITS[split:])


def _wait_groups(n_groups, src_ref, dst_ref, sem):
    for bit in WAIT_BITS:
        @pl.when((n_groups & bit) != 0)
        def _():
            rows = bit * SUBLANES
            pltpu.make_async_copy(src_ref.at[pl.ds(0, rows)], dst_ref.at[pl.ds(0, rows)], sem).wait()


def _stage_positions(rt, base_row, tri):
    tm = rt.shape[0]
    lane_f = lax.broadcasted_iota(jnp.int32, (tm, LANES), 1).astype(F32)
    hots = [lane_f == rt[:, k:k + 1] for k in range(TOP_K)]
    onehot = jnp.zeros((tm, LANES), F32)
    for hot in hots:
        onehot = onehot + jnp.where(hot, 1.0, 0.0)
    pos = _dot(tri, onehot.astype(BF16)) + base_row
    return [jnp.sum(jnp.where(hot, pos, 0.0), axis=-1, keepdims=True) for hot in hots]


def _dispatch_kernel(o_ref, f_ref, fl_ref, rem_ref, hg_ref, tot_ref, used_ref, fs_ref, flen_ref,
                     h_ref, rt_ref, sb_ref, tri_ref, xs_ref,
                     stg2_ref, tails_ref, zero_ref, sem2, fill_sem):
    i = pl.program_id(0)
    tm = h_ref.shape[0]
    n_slots = xs_ref.shape[0]
    base = i * N_EXPERTS
    hi_mask = jnp.int32(-65536)
    slot = i % 2
    stg_ref = stg2_ref.at[slot]
    sem = sem2.at[slot]

    @pl.when(i == 0)
    def _():
        tails_ref[...] = jnp.zeros_like(tails_ref)
        stg2_ref[...] = jnp.zeros_like(stg2_ref)

    rt = rt_ref[...]
    pos = _stage_positions(rt, sb_ref[0], tri_ref[...])
    lane = lax.broadcasted_iota(jnp.int32, (tm, LANES), 1)
    cols = jnp.zeros((tm, LANES), F32)
    for k in range(TOP_K):
        cols = jnp.where(lane == k, pos[k], cols)
        cols = jnp.where(lane == TOP_K + k, rt[:, GATE_LANE + k:GATE_LANE + k + 1], cols)
    rows = cols.T

    hb = h_ref[...].astype(BF16)
    half = PACK_W
    used_rows = used_ref[i] * SUBLANES

    def stage_chunk(r_lo):
        row_f = (lax.broadcasted_iota(jnp.int32, (STAGE_CHUNK, tm), 0) + r_lo).astype(F32)
        sel = jnp.zeros((STAGE_CHUNK, tm), F32)
        gsel = jnp.zeros((STAGE_CHUNK, tm), F32)
        for k in range(TOP_K):
            hit = row_f == rows[k:k + 1, :]
            sel = jnp.where(hit, 1.0, sel)
            gsel = jnp.where(hit, rows[TOP_K + k:TOP_K + k + 1, :], gsel)
        gate = jnp.sum(gsel, axis=-1, keepdims=True)
        moved = _dot(sel.astype(BF16), hb)
        lo = pltpu.bitcast(moved[:, 0:half], jnp.int32)
        hi = pltpu.bitcast(moved[:, half:2 * half], jnp.int32)
        stg_ref[r_lo:r_lo + STAGE_CHUNK, 0:half] = lax.shift_right_logical(lo, 16) | (hi & hi_mask)
        stg_ref[r_lo:r_lo + STAGE_CHUNK, half:half + LANES] = pltpu.bitcast(
            jnp.broadcast_to(gate, (STAGE_CHUNK, LANES)), jnp.int32)

    for rc in range(STAGE_ROWS // STAGE_CHUNK):
        pl.when(rc * STAGE_CHUNK < used_rows)(functools.partial(stage_chunk, rc * STAGE_CHUNK))

    def carry_tails(e, cr):
        first = pl.multiple_of(o_ref[base + e] * SUBLANES, SUBLANES)
        stg_ref[pl.ds(first, SUBLANES), :] = stg_ref[pl.ds(first, SUBLANES), :] | tails_ref[e]
        part = pl.multiple_of((o_ref[base + e] + f_ref[base + e]) * SUBLANES, SUBLANES)
        tails_ref[e] = jnp.where(rem_ref[base + e] > 0, stg_ref[pl.ds(part, SUBLANES), :], 0)
        return cr

    lax.fori_loop(0, N_EXPERTS, carry_tails, 0)

    def run_copies(e, cr):
        _group_copies(fl_ref[base + e], stg_ref, o_ref[base + e], xs_ref, hg_ref[base + e], sem, False)
        return cr

    lax.fori_loop(0, N_EXPERTS, run_copies, 0)

    @pl.when(i == pl.num_programs(0) - 1)
    def _():
        zero_ref[...] = jnp.zeros_like(zero_ref)
        tail_group = fs_ref[N_EXPERTS]
        n_tail = (n_slots // SUBLANES - tail_group) // (MOE_BLK // SUBLANES)

        def tail_copy(j, wait):
            off = pl.multiple_of(tail_group * SUBLANES + j * MOE_BLK, MOE_BLK)
            cp = pltpu.make_async_copy(zero_ref, xs_ref.at[pl.ds(off, MOE_BLK)], fill_sem)
            cp.wait() if wait else cp.start()

        for wait in (False, True):
            def pad_body(e, cr):
                _group_copies(flen_ref[e], zero_ref, 0, xs_ref, fs_ref[e], fill_sem, wait)
                return cr
            lax.fori_loop(0, N_EXPERTS, pad_body, 0)
            lax.fori_loop(0, n_tail, lambda j, cr: (tail_copy(j, wait), cr)[1], 0)

    @pl.when(i > 0)
    def _():
        _wait_groups(tot_ref[jnp.maximum(i - 1, 0)], stg2_ref.at[1 - slot], xs_ref, sem2.at[1 - slot])

    @pl.when(i == pl.num_programs(0) - 1)
    def _():
        _wait_groups(tot_ref[i], stg_ref, xs_ref, sem)


def _dispatch(tables, fill_groups, fill_len_groups, h, rt, stage_base, n_slots):
    t, d = h.shape
    tm = DISP_TM
    grid_spec = pltpu.PrefetchScalarGridSpec(
        num_scalar_prefetch=len(tables) + 2,
        grid=(t // tm,),
        in_specs=[pl.BlockSpec((tm, d), lambda i, *_: (i, 0)),
                  pl.BlockSpec((tm, LANES), lambda i, *_: (i, 0)),
                  pl.BlockSpec((1, 1, LANES), lambda i, *_: (i, 0, 0)),
                  pl.BlockSpec((tm, tm), lambda i, *_: (0, 0))],
        out_specs=pl.BlockSpec(memory_space=pl.ANY),
        scratch_shapes=[pltpu.VMEM((2, STAGE_ROWS, XS_W), jnp.int32),
                        pltpu.VMEM((N_EXPERTS, SUBLANES, XS_W), jnp.int32),
                        pltpu.VMEM((MOE_BLK, XS_W), jnp.int32),
                        pltpu.SemaphoreType.DMA((2,)), pltpu.SemaphoreType.DMA(())],
    )
    return pl.pallas_call(
        _dispatch_kernel,
        grid_spec=grid_spec,
        out_shape=jax.ShapeDtypeStruct((n_slots, XS_W), jnp.int32),
        compiler_params=_cparams(("arbitrary",)),
    )(*tables, fill_groups, fill_len_groups, h, rt, stage_base, _strict_lower(tm))


def _expert_kernel(be_ref, nused_ref, xs_ref, w1_ref, b1_ref, w2_ref, b2_ref, perm_ref, y_ref,
                   w1s_ref, w2s_ref):
    i = pl.program_id(0)
    prev = be_ref[jnp.maximum(i - 1, 0)]
    changed = (i == 0) | (be_ref[i] != prev)
    dff2 = w1_ref.shape[2]
    tile = 2 * LANES

    @pl.when(changed & (i < nused_ref[0]))
    def _():
        for j in range(dff2 // tile):
            wj = w1_ref[0, :, j * tile:(j + 1) * tile].astype(BF16)
            w1s_ref[:, j * tile:(j + 1) * tile] = _dot(wj, perm_ref[...]).astype(BF16)
        w2s_ref[...] = w2_ref[0].astype(BF16)

    @pl.when(i < nused_ref[0])
    def _():
        hi_mask = jnp.int32(-65536)
        words = xs_ref[:, 0:PACK_W]
        x = jnp.concatenate(
            [pltpu.bitcast(lax.shift_left(words, 16), F32).astype(BF16),
             pltpu.bitcast(words & hi_mask, F32).astype(BF16)], axis=1)
        gate = pltpu.bitcast(xs_ref[:, PACK_W:PACK_W + 1], F32)
        acts = []
        for j in range(dff2 // tile):
            hb = _dot(x, w1s_ref[:, j * tile:(j + 1) * tile]) + b1_ref[0, :, j * tile:(j + 1) * tile]
            gp = jnp.minimum(hb[:, 0:LANES], SWIGLU_LIMIT)
            up = jnp.clip(hb[:, LANES:tile], -SWIGLU_LIMIT, SWIGLU_LIMIT)
            acts.append((gp * _sigmoid(SWIGLU_ALPHA * gp) * (up + 1.0)).astype(BF16))
        act = jnp.concatenate(acts, axis=1)
        y = (_dot(act, w2s_ref[...]) + b2_ref[0]) * gate
        lo = pltpu.bitcast(y[:, 0:PACK_W].astype(BF16).astype(F32), jnp.int32)
        hi = pltpu.bitcast(y[:, PACK_W:2 * PACK_W].astype(BF16).astype(F32), jnp.int32)
        y_ref[...] = lax.shift_right_logical(lo, 16) | (hi & hi_mask)

    @pl.when(i >= nused_ref[0])
    def _():
        y_ref[...] = jnp.zeros_like(y_ref)


def _experts(block_exp, n_used, xs, w1, b1p, w2, b2, perm, n_blocks):
    blk = MOE_BLK
    d = w1.shape[1]
    dff2 = w1.shape[2]
    dff = w2.shape[1]

    def x_idx(i, be, nu):
        return (jnp.minimum(i, nu[0] - 1), 0)

    grid_spec = pltpu.PrefetchScalarGridSpec(
        num_scalar_prefetch=2,
        grid=(n_blocks,),
        in_specs=[pl.BlockSpec((blk, XS_W), x_idx),
                  pl.BlockSpec((1, d, dff2), lambda i, be, nu: (be[i], 0, 0)),
                  pl.BlockSpec((1, 1, dff2), lambda i, be, nu: (be[i], 0, 0)),
                  pl.BlockSpec((1, dff, d), lambda i, be, nu: (be[i], 0, 0)),
                  pl.BlockSpec((1, 1, d), lambda i, be, nu: (be[i], 0, 0)),
                  pl.BlockSpec(perm.shape, lambda i, be, nu: (0, 0))],
        out_specs=pl.BlockSpec((blk, PACK_W), lambda i, be, nu: (i, 0)),
        scratch_shapes=[pltpu.VMEM((d, dff2), BF16), pltpu.VMEM((dff, d), BF16)],
    )
    return pl.pallas_call(
        _expert_kernel,
        grid_spec=grid_spec,
        out_shape=jax.ShapeDtypeStruct((n_blocks * blk, PACK_W), jnp.int32),
        compiler_params=_cparams(("arbitrary",)),
    )(block_exp, n_used, xs, w1, b1p, w2, b2, perm)


def _combine_kernel(o_ref, g_ref, hg_ref, tot_ref, x1_ref, rt_ref, sb_ref, tri_ref, g2_ref, fnw_ref,
                    ys_ref, out_ref, stg_ref, sem):
    tm = x1_ref.shape[1]
    step = pl.program_id(0) * pl.num_programs(1) + pl.program_id(1)
    n_steps = pl.num_programs(0) * pl.num_programs(1)
    slot = step % 2
    hi_mask = jnp.int32(-65536)

    def fetch(tile, into):
        def body(e, cr):
            _group_copies(g_ref[tile * N_EXPERTS + e], ys_ref, hg_ref[tile * N_EXPERTS + e],
                          stg_ref.at[into], o_ref[tile * N_EXPERTS + e], sem.at[into], False)
            return cr
        lax.fori_loop(0, N_EXPERTS, body, 0)

    @pl.when(step == 0)
    def _():
        stg_ref[...] = jnp.zeros_like(stg_ref)
        fetch(0, 0)

    @pl.when(step + 1 < n_steps)
    def _():
        fetch(step + 1, 1 - slot)

    _wait_groups(tot_ref[step], ys_ref, stg_ref.at[slot], sem.at[slot])

    pos = _stage_positions(rt_ref[0], sb_ref[0], tri_ref[...])
    col_f = lax.broadcasted_iota(jnp.int32, (tm, STAGE_ROWS), 1).astype(F32)
    sel = jnp.zeros((tm, STAGE_ROWS), F32)
    for k in range(TOP_K):
        sel = jnp.where(col_f == pos[k], 1.0, sel)
    sel = sel.astype(BF16)
    words = stg_ref[slot]
    lo = pltpu.bitcast(lax.shift_left(words, 16), F32).astype(BF16)
    hi = pltpu.bitcast(words & hi_mask, F32).astype(BF16)
    moe = jnp.concatenate([_dot(sel, lo), _dot(sel, hi)], axis=1)
    x2 = x1_ref[0] + g2_ref[0] * moe
    ms = jnp.mean(x2 * x2, axis=-1, keepdims=True)
    out_ref[0] = x2 * lax.rsqrt(ms + RMS_EPS) * fnw_ref[...]


def _combine(tables, x1, rt, stage_base, g2, fnw, ys):
    b, l, d = x1.shape
    tm = DISP_TM
    per_l = l // tm
    tok = lambda n: pl.BlockSpec((1, tm, n), lambda i, j, *_: (i, j, 0))
    grid_spec = pltpu.PrefetchScalarGridSpec(
        num_scalar_prefetch=len(tables),
        grid=(b, per_l),
        in_specs=[tok(d), tok(LANES),
                  pl.BlockSpec((1, 1, LANES), lambda i, j, *_: (i * per_l + j, 0, 0)),
                  pl.BlockSpec((tm, tm), lambda i, j, *_: (0, 0)),
                  pl.BlockSpec((1, 1, d), lambda i, j, *_: (i, 0, 0)),
                  pl.BlockSpec(fnw.shape, lambda i, j, *_: (0, 0)),
                  pl.BlockSpec(memory_space=pl.ANY)],
        out_specs=tok(d),
        scratch_shapes=[pltpu.VMEM((2, STAGE_ROWS, PACK_W), jnp.int32),
                        pltpu.SemaphoreType.DMA((2,))],
    )
    return pl.pallas_call(
        _combine_kernel,
        grid_spec=grid_spec,
        out_shape=jax.ShapeDtypeStruct((b, l, d), F32),
        compiler_params=_cparams(("arbitrary", "arbitrary")),
    )(*tables, x1, rt, stage_base, _strict_lower(tm), g2, fnw, ys)


def _pool_constants():
    i = np.arange(POOL_TM)[:, None]
    j = np.arange(POOL_TM)[None, :]
    same_row = (i // GRID_W) == (j // GRID_W)
    mats, cnts = [], []
    for w in POOL_WINDOWS:
        band = same_row & (j - i >= -(w // 2)) & (j - i < w - w // 2)
        mats.append(band)
        cnts.append(np.broadcast_to(band.sum(axis=1, keepdims=True), (POOL_TM, POOL_GW)))
    return (jnp.asarray(np.stack(mats), BF16), jnp.asarray(np.stack(cnts), F32))


def _head_expand():
    e = np.zeros((LANES, SSD_W), np.float32)
    for h in range(SSD_HEADS):
        e[h, h * SSD_HEADDIM:(h + 1) * SSD_HEADDIM] = 1.0
    return jnp.asarray(e, BF16)


def _deinterleave_perm():
    n = 2 * LANES
    p = np.zeros((n, n), np.float32)
    for k in range(LANES):
        p[2 * k, k] = 1.0
        p[2 * k + 1, LANES + k] = 1.0
    return jnp.asarray(p, BF16)


def _strict_lower(n):
    return jnp.asarray(np.tril(np.ones((n, n), np.float32), -1), BF16)


def _pad_lanes(a, n):
    return jnp.pad(a, [(0, 0)] * (a.ndim - 1) + [(0, n - a.shape[-1])])


def kernel(x, c, ctx, c_ctx, w_mod, b_mod, norm1_w, norm2_w, w_in, conv_w, conv_b, dt_bias, a_log,
           d_skip, ssd_norm_w, pool_w, pool_scale, w_out, router_w, router_b, w1, b1, w2, b2,
           final_norm_w):
    depth = w_mod.shape[0]
    assert depth == 1, "single-layer problem"
    b, l, d = x.shape
    lc = ctx.shape[1]
    xbcdt = CONV_CH + 2 * SSD_HEADS

    mod_rows = 2 * SUBLANES
    cc = jnp.zeros((mod_rows, d), F32).at[0:b].set(c).at[b].set(c_ctx)
    mod = _modulation(cc, w_mod[0], b_mod[0])
    sh1, sc1, g1, sh2, sc2, g2 = [m.reshape(b, 1, d) for m in jnp.split(mod[0:b], 6, axis=-1)]
    csh1, csc1 = [jnp.broadcast_to(m.reshape(1, 1, d), (b, 1, d))
                  for m in jnp.split(mod[b:b + 1], 6, axis=-1)[0:2]]

    wi = w_in[0]
    wx = wi[:, 0:CONV_CH].astype(BF16)
    wd = jnp.concatenate([_pad_lanes(wi[:, CONV_CH:CONV_CH + SSD_HEADS], LANES),
                          _pad_lanes(wi[:, CONV_CH + SSD_HEADS:xbcdt], LANES)], axis=1).astype(BF16)
    wz = wi[:, xbcdt:xbcdt + SSD_W].astype(BF16)
    wp = wi[:, xbcdt + SSD_W:].astype(BF16)
    n1 = norm1_w[0].reshape(1, d)
    conv_w8 = jnp.pad(conv_w[0], ((0, SUBLANES - SSD_CONV), (0, 0)))
    conv_b1 = conv_b[0].reshape(1, CONV_CH)
    dtb = _pad_lanes(dt_bias[0], LANES).reshape(2, 1, LANES)
    alog = _pad_lanes(a_log[0], LANES).reshape(2, 1, LANES)
    dsk = jnp.repeat(d_skip[0], SSD_HEADDIM).reshape(1, SSD_W)
    snw = ssd_norm_w[0].reshape(1, SSD_W)
    expand = _head_expand()

    xbc_c, dt_c = _inproj(ctx, n1, csh1, csc1, (wx, wd), (BF16, F32), min(PROJ_TM, lc))
    zero_state = jnp.zeros((b, 2, SSD_GROUPS, SSD_STATE, GROUP_W), F32)
    _, ctx_states = _ssd(xbc_c, dt_c, None, conv_w8, conv_b1, dtb, alog, dsk, snw, expand, zero_state)

    xbc, dt, z, u_pool = _inproj(x, n1, sh1, sc1, (wx, wd, wz, wp), (BF16, F32, BF16, BF16), PROJ_TM)
    y_ssd, _ = _ssd(xbc, dt, z, conv_w8, conv_b1, dtb, alog, dsk, snw, expand, ctx_states)

    pool_a, pool_cnt = _pool_constants()
    pw = pool_w[0].astype(BF16)
    zero_gw = jnp.zeros((POOL_GW, POOL_GW), BF16)
    pw_pairs = jnp.stack([jnp.block([[pw[2 * gp], zero_gw], [zero_gw, pw[2 * gp + 1]]])
                          for gp in range(len(POOL_WINDOWS) // 2)])
    x1, h2, rt, tcnt = _mix(
        y_ssd, u_pool, x, pool_a, pool_cnt, pw_pairs, pool_scale[0].reshape(1, POOL_W),
        w_out[0].astype(BF16), g1, norm2_w[0].reshape(1, d), sh2, sc2,
        _pad_lanes(router_w[0], LANES), _pad_lanes(router_b[0].reshape(1, N_EXPERTS), LANES))

    i32 = jnp.int32
    t = b * l
    n_tiles = t // DISP_TM
    tc = tcnt[:, 0:MIX_TM // DISP_TM, 0:N_EXPERTS].reshape(n_tiles, N_EXPERTS).astype(i32)
    counts = jnp.sum(tc, axis=0)
    run_start = jnp.cumsum(tc, axis=0) - tc
    padded = (counts + MOE_BLK - 1) // MOE_BLK * MOE_BLK
    pad_end = jnp.cumsum(padded)
    pad_start = pad_end - padded
    carried = run_start % SUBLANES
    span = carried + tc
    groups = (span + SUBLANES - 1) // SUBLANES
    full = span // SUBLANES
    stage_group = jnp.cumsum(groups, axis=1) - groups
    slot_group = (pad_start[None, :] + run_start - carried) // SUBLANES
    flush = full.at[n_tiles - 1].set(groups[n_tiles - 1])
    flat = lambda a: a.reshape(n_tiles * N_EXPERTS).astype(i32)
    stage_base = _pad_lanes((stage_group * SUBLANES + carried).astype(F32), LANES)
    stage_base = stage_base.reshape(n_tiles, 1, LANES)
    n_blocks = (t * TOP_K) // MOE_BLK + N_EXPERTS
    n_used = (pad_end[-1] // MOE_BLK).astype(i32).reshape(1)
    blk_start = jnp.minimum(jnp.arange(n_blocks, dtype=i32), n_used[0] - 1) * MOE_BLK
    block_exp = jnp.minimum(jnp.sum(blk_start[:, None] >= pad_end[None, :], axis=1),
                            N_EXPERTS - 1).astype(i32)
    written = (counts + SUBLANES - 1) // SUBLANES * SUBLANES
    fill_group = (jnp.concatenate([pad_start + written, pad_end[-1:]]) // SUBLANES).astype(i32)
    fill_groups = (jnp.concatenate([padded - written, jnp.zeros((1,), i32)]) // SUBLANES).astype(i32)

    xs = _dispatch((flat(stage_group), flat(full), flat(flush), flat(span % SUBLANES), flat(slot_group),
                    jnp.sum(flush, axis=1).astype(i32), jnp.sum(groups, axis=1).astype(i32)),
                   fill_group, fill_groups, h2.reshape(t, d), rt.reshape(t, LANES), stage_base,
                   n_blocks * MOE_BLK)
    dff2 = w1.shape[-1]
    b1p = jnp.concatenate(
        [b1[0].reshape(N_EXPERTS, dff2 // (2 * LANES), LANES, 2)[..., 0],
         b1[0].reshape(N_EXPERTS, dff2 // (2 * LANES), LANES, 2)[..., 1]], axis=-1
    ).reshape(N_EXPERTS, 1, dff2)
    ys = _experts(block_exp, n_used, xs, w1[0], b1p, w2[0], b2[0].reshape(N_EXPERTS, 1, d),
                  _deinterleave_perm(), n_blocks)
    return _combine((flat(stage_group), flat(groups), flat(slot_group),
                     jnp.sum(groups, axis=1).astype(i32)), x1, rt, stage_base, g2,
                    final_norm_w.reshape(1, d), ys)
```

```python
import functools

import numpy as np
import jax
import jax.numpy as jnp
from jax import lax
from jax.experimental import pallas as pl
from jax.experimental.pallas import tpu as pltpu

F32 = jnp.float32
BF16 = jnp.bfloat16

SSD_HEADDIM = 64
SSD_GROUPS = 4
SSD_HPG = 6
SSD_HEADS = SSD_GROUPS * SSD_HPG
SSD_STATE = 128
SSD_CONV = 5
SSD_CHUNK = 128
SSD_W = SSD_HEADS * SSD_HEADDIM
GROUP_W = SSD_HPG * SSD_HEADDIM
CONV_CH = SSD_W + 2 * SSD_GROUPS * SSD_STATE
POOL_WINDOWS = (2, 4, 8, 16)
POOL_GW = 128
POOL_W = POOL_GW * len(POOL_WINDOWS)
GRID_W = 64
N_EXPERTS = 32
TOP_K = 4
SWIGLU_ALPHA = 1.702
SWIGLU_LIMIT = 7.0
RMS_EPS = 1e-6
LOG2_E = 1.4426950408889634

LANES = 128
SUBLANES = 8
VMEM_LIMIT_BYTES = 56 * 1024 * 1024

MOD_TN = 1024
PROJ_TM = 512
MIX_TM = 512
POOL_TM = 256
MOE_BLK = 512
DISP_TM = 256
NEG_BIG = -1e30

GATE_LANE = 2 * TOP_K
PACK_W = 512
XS_W = PACK_W + LANES
STAGE_CHUNK = 256
_STAGE_NEED = TOP_K * DISP_TM + N_EXPERTS * 2 * (SUBLANES - 1) + SUBLANES
STAGE_ROWS = -(-_STAGE_NEED // STAGE_CHUNK) * STAGE_CHUNK
RUN_BITS = (32, 16, 8, 4, 2, 1)
RUN_SMALL_BIT = 4
WAIT_BITS = (128, 64, 32, 16, 8, 4, 2, 1)


def _sigmoid(x):
    return 0.5 * jnp.tanh(0.5 * x) + 0.5


def _split2(a):
    hi = a.astype(BF16)
    mid = (a - hi.astype(F32)).astype(BF16)
    return hi, mid


def _split3(a):
    hi = a.astype(BF16)
    r = a - hi.astype(F32)
    mid = r.astype(BF16)
    lo = (r - mid.astype(F32)).astype(BF16)
    return hi, mid, lo


def _dot(a, b):
    return jnp.dot(a, b, preferred_element_type=F32)


def _dot_exact_rhs(a_f32, b_bf16, parts):
    pieces = _split3(a_f32) if parts == 3 else _split2(a_f32)
    out = _dot(pieces[0], b_bf16)
    for p in pieces[1:]:
        out = out + _dot(p, b_bf16)
    return out


def _dot_hi(a_f32, b_f32):
    a0, a1, a2 = _split3(a_f32)
    b0, b1, b2 = _split3(b_f32)
    out = _dot(a0, b0)
    out = out + _dot(a0, b1) + _dot(a1, b0)
    out = out + _dot(a1, b1) + _dot(a0, b2) + _dot(a2, b0)
    return out


def _cparams(sem):
    return pltpu.CompilerParams(dimension_semantics=sem, vmem_limit_bytes=VMEM_LIMIT_BYTES)


def _mod_kernel(c_ref, w_ref, b_ref, o_ref):
    c = c_ref[...]
    s = c * _sigmoid(c)
    o_ref[...] = _dot_hi(s, w_ref[...]) + b_ref[...]


def _modulation(cc, w_mod, b_mod):
    rows, d = cc.shape
    n = w_mod.shape[1]
    return pl.pallas_call(
        _mod_kernel,
        grid=(n // MOD_TN,),
        in_specs=[pl.BlockSpec((rows, d), lambda j: (0, 0)),
                  pl.BlockSpec((d, MOD_TN), lambda j: (0, j)),
                  pl.BlockSpec((1, MOD_TN), lambda j: (0, j))],
        out_specs=pl.BlockSpec((rows, MOD_TN), lambda j: (0, j)),
        out_shape=jax.ShapeDtypeStruct((rows, n), F32),
        compiler_params=_cparams(("arbitrary",)),
    )(cc, w_mod, b_mod.reshape(1, n))


def _inproj_kernel(n_groups, x0_ref, xn_ref, nw_ref, sh0_ref, sc0_ref, shn_ref, scn_ref, *refs):
    w_refs = refs[0:n_groups]
    out_refs = refs[n_groups:2 * n_groups]
    hb_even_ref, hb_odd_ref = refs[2 * n_groups:]
    step = pl.program_id(0) * pl.num_programs(1) + pl.program_id(1)

    def normed(x, sh, sc):
        ms = jnp.mean(x * x, axis=-1, keepdims=True)
        h = x * lax.rsqrt(ms + RMS_EPS) * nw_ref[...]
        return (h * (1.0 + sc) + sh).astype(BF16)

    @pl.when(step == 0)
    def _():
        hb_even_ref[...] = normed(x0_ref[0], sh0_ref[0], sc0_ref[0])

    def project(cur_ref, nxt_ref):
        hb = cur_ref[...]
        for w_ref, out_ref in zip(w_refs, out_refs):
            out_ref[0] = _dot(hb, w_ref[...]).astype(out_ref.dtype)
        nxt_ref[...] = normed(xn_ref[0], shn_ref[0], scn_ref[0])

    pl.when(step % 2 == 0)(functools.partial(project, hb_even_ref, hb_odd_ref))
    pl.when(step % 2 == 1)(functools.partial(project, hb_odd_ref, hb_even_ref))


def _inproj(x, norm_w, shift, scale, weights, out_dtypes, tm):
    b, l, d = x.shape
    per_l = l // tm
    last = b * per_l - 1

    def nxt(i, j):
        s = jnp.minimum(i * per_l + j + 1, last)
        return s // per_l, s % per_l

    full = lambda a: pl.BlockSpec(a.shape, lambda i, j: (0, 0))
    tok = lambda n: pl.BlockSpec((1, tm, n), lambda i, j: (i, j, 0))
    first_b = pl.BlockSpec((1, 1, d), lambda i, j: (0, 0, 0))
    next_b = pl.BlockSpec((1, 1, d), lambda i, j: (nxt(i, j)[0], 0, 0))
    return pl.pallas_call(
        functools.partial(_inproj_kernel, len(weights)),
        grid=(b, per_l),
        in_specs=[pl.BlockSpec((1, tm, d), lambda i, j: (0, 0, 0)),
                  pl.BlockSpec((1, tm, d), lambda i, j: (*nxt(i, j), 0)),
                  full(norm_w), first_b, first_b, next_b, next_b] + [full(w) for w in weights],
        out_specs=[tok(w.shape[1]) for w in weights],
        out_shape=[jax.ShapeDtypeStruct((b, l, w.shape[1]), dt)
                   for w, dt in zip(weights, out_dtypes)],
        scratch_shapes=[pltpu.VMEM((tm, d), BF16), pltpu.VMEM((tm, d), BF16)],
        compiler_params=_cparams(("arbitrary", "arbitrary")),
    )(x, x, norm_w, shift, scale, shift, scale, *weights)


def _ssd_kernel(nc, emit_y, *refs):
    if emit_y:
        (xbc_ref, prev_ref, next_ref, dt_ref, z_ref, cw_ref, cb_ref, dtb_ref, alog_ref, dsk_ref,
         nw_ref, exp_ref, init_ref, y_ref, fin_ref, act_ref, ybuf_ref, yf_ref, st_ref) = refs
        y_part = (z_ref, dsk_ref, nw_ref, y_ref, ybuf_ref, yf_ref)
    else:
        (xbc_ref, prev_ref, next_ref, dt_ref, cw_ref, cb_ref, dtb_ref, alog_ref,
         exp_ref, init_ref, fin_ref, act_ref, st_ref) = refs
        y_part = None
    ph = pl.program_id(1)
    c = pl.program_id(2)
    ci = c + ph * (nc - 1 - 2 * c)

    @pl.when(c == 0)
    def _():
        st_ref[...] = init_ref[0, ph]

    @pl.when(ph == 0)
    def _():
        _conv_silu(nc, ci, xbc_ref, prev_ref, next_ref, cw_ref, cb_ref, act_ref)

    _ssd_scan(nc, ph, c, ci, dt_ref, dtb_ref, alog_ref, exp_ref, act_ref, st_ref, fin_ref, y_part)


def _conv_silu(nc, ci, xbc_ref, prev_ref, next_ref, cw_ref, cb_ref, act_ref):
    ch = SSD_CHUNK
    has_prev = ci > 0
    has_next = ci < nc - 1
    row8 = lax.broadcasted_iota(jnp.int32, (SUBLANES, LANES), 0)
    half = SSD_CONV // 2
    for j in range(CONV_CH // LANES):
        cols = slice(j * LANES, (j + 1) * LANES)
        n_t = ch // SUBLANES
        tiles = [jnp.where(has_prev, prev_ref[0, SUBLANES:2 * SUBLANES, cols].astype(F32), 0.0)]
        for i in range(n_t // 2):
            both = xbc_ref[0, 2 * i * SUBLANES:2 * (i + 1) * SUBLANES, cols].astype(F32)
            tiles += [both[0:SUBLANES], both[SUBLANES:2 * SUBLANES]]
        tiles.append(jnp.where(has_next, next_ref[0, 0:SUBLANES, cols].astype(F32), 0.0))
        acc = [cb_ref[:, cols] + cw_ref[half:half + 1, cols] * tiles[i + 1] for i in range(n_t)]
        for s in range(1, half + 1):
            rot = [pltpu.roll(tl, s, axis=0) for tl in tiles[0:n_t + 1]]
            wk = cw_ref[half - s:half - s + 1, cols]
            for i in range(n_t):
                acc[i] = acc[i] + wk * jnp.where(row8 < s, rot[i], rot[i + 1])
            rot = [pltpu.roll(tl, SUBLANES - s, axis=0) for tl in tiles[1:n_t + 2]]
            wk = cw_ref[half + s:half + s + 1, cols]
            for i in range(n_t):
                acc[i] = acc[i] + wk * jnp.where(row8 >= SUBLANES - s, rot[i + 1], rot[i])
        silu = [0.5 * v * jnp.tanh(0.5 * v) + 0.5 * v for v in acc]
        act_ref[ci, :, cols] = jnp.concatenate(silu, axis=0).astype(BF16)


def _ssd_scan(nc, ph, c, ci, dt_ref, dtb_ref, alog_ref, exp_ref, act_ref, st_ref, fin_ref, y_part):
    ch = SSD_CHUNK
    emit_y = y_part is not None
    dtr = dt_ref[0] + dtb_ref[0]
    small = jnp.exp(-jnp.abs(dtr))
    one_plus = 1.0 + small
    log1p_small = jnp.where(one_plus == 1.0, small, jnp.log(one_plus) * (small / (one_plus - 1.0)))
    dtv = jnp.maximum(dtr, 0.0) + log1p_small
    a = dtv * (-jnp.exp(alog_ref[0]) * LOG2_E)
    row = lax.broadcasted_iota(jnp.int32, (ch, ch), 0)
    col = lax.broadcasted_iota(jnp.int32, (ch, ch), 1)
    tmask = (row - col) * (1 - 2 * ph) >= 0
    tri = jnp.where(tmask, 1.0, 0.0).astype(BF16)
    cs = _dot_exact_rhs_left(tri, a)
    tot = jnp.where(ph == 0, cs[ch - 1:ch, :], cs[0:1, :])
    cs_t = cs.T
    e_cs = jnp.exp2(cs)
    e_dec = jnp.exp2(tot - cs)
    e_tot = jnp.exp2(tot)
    expand = exp_ref[...]
    wdec_x = _dot((dtv * e_dec).astype(BF16), expand).astype(BF16)
    etot_x = _dot_exact_rhs(jnp.broadcast_to(e_tot, (SUBLANES, LANES)), expand, 2)[0:1]
    if emit_y:
        z_ref, dsk_ref, nw_ref, y_ref, ybuf_ref, yf_ref = y_part
        ecs_x = _dot(e_cs.astype(BF16), expand)
        dt_t = dtv.T

    lane = lax.broadcasted_iota(jnp.int32, (ch, LANES), 1)
    for g in range(SSD_GROUPS):
        gs = slice(g * GROUP_W, (g + 1) * GROUP_W)
        b_bf = act_ref[ci, :, SSD_W + g * SSD_STATE:SSD_W + (g + 1) * SSD_STATE]
        x_bf = act_ref[ci, :, gs]
        s_prev = st_ref[g]
        if emit_y:
            c_bf = act_ref[ci, :, SSD_W + (SSD_GROUPS + g) * SSD_STATE:
                           SSD_W + (SSD_GROUPS + g + 1) * SSD_STATE]
            cb = lax.dot_general(c_bf, b_bf, (((1,), (1,)), ((), ())), preferred_element_type=F32)
            y_off = _dot(c_bf, s_prev.astype(BF16)) * ecs_x[:, gs]
        x_dec = x_bf * wdec_x[:, gs]
        st_ref[g] = s_prev * etot_x[:, gs] + lax.dot_general(
            b_bf, x_dec, (((0,), (0,)), ((), ())), preferred_element_type=F32)
        if not emit_y:
            continue
        for q in range(SSD_HPG // 2):
            lmats = []
            for h in (g * SSD_HPG + 2 * q, g * SSD_HPG + 2 * q + 1):
                diff = cs[:, h:h + 1] - cs_t[h:h + 1, :]
                dec = jnp.exp2(jnp.where(tmask, diff, NEG_BIG))
                lmats.append((dec * cb * dt_t[h:h + 1, :]).astype(BF16))
            xp = x_bf[:, q * LANES:(q + 1) * LANES]
            zero = jnp.zeros_like(xp)
            rhs = jnp.concatenate([jnp.where(lane < SSD_HEADDIM, xp, zero),
                                   jnp.where(lane >= SSD_HEADDIM, xp, zero)], axis=0)
            y_diag = _dot(jnp.concatenate(lmats, axis=1), rhs)
            ps = slice(g * GROUP_W + q * LANES, g * GROUP_W + (q + 1) * LANES)
            ybuf_ref[:, ps] = y_diag + y_off[:, q * LANES:(q + 1) * LANES]

    if emit_y:
        @pl.when(ph == 0)
        def _():
            yf_ref[ci] = ybuf_ref[...].astype(BF16)

        @pl.when(ph == 1)
        def _():
            yt = (yf_ref[ci].astype(F32) + ybuf_ref[...]
                  + act_ref[ci, :, 0:SSD_W].astype(F32) * dsk_ref[...])
            zz = z_ref[0].astype(F32)
            gt = yt * (zz * _sigmoid(zz))
            ms = jnp.mean(gt * gt, axis=-1, keepdims=True)
            y_ref[0] = (gt * lax.rsqrt(ms + RMS_EPS) * nw_ref[...]).astype(y_ref.dtype)

    @pl.when(c == nc - 1)
    def _():
        fin_ref[0, ph] = st_ref[...]


def _dot_exact_rhs_left(sel_bf16, a_f32):
    hi, mid, lo = _split3(a_f32)
    return _dot(sel_bf16, hi) + _dot(sel_bf16, mid) + _dot(sel_bf16, lo)


def _ssd(xbc, dt, z, conv_w8, conv_b, dt_bias, a_log, d_skip_x, norm_w, expand, init):
    emit_y = z is not None
    b, l, _ = xbc.shape
    ch = SSD_CHUNK
    nc = l // ch
    halo = 2 * SUBLANES
    per_ch = ch // halo

    def cidx(ph, c):
        return c + ph * (nc - 1 - 2 * c)

    def xidx(ph, c):
        return jnp.where(ph == 0, c, nc - 1)

    def out_idx(ph, c):
        return jnp.where(ph == 0, nc - 1, nc - 1 - c)

    full2 = lambda a: pl.BlockSpec(a.shape, lambda i, ph, c: (0, 0))
    st_spec = pl.BlockSpec((1, 2, SSD_GROUPS, SSD_STATE, GROUP_W), lambda i, ph, c: (i, 0, 0, 0, 0))
    per_phase = pl.BlockSpec((1, 1, LANES), lambda i, ph, c: (ph, 0, 0))
    y_spec = pl.BlockSpec((1, ch, SSD_W), lambda i, ph, c: (i, out_idx(ph, c), 0))
    operands = [(xbc, pl.BlockSpec((1, ch, CONV_CH), lambda i, ph, c: (i, xidx(ph, c), 0))),
                (xbc, pl.BlockSpec((1, halo, CONV_CH),
                                   lambda i, ph, c: (i, jnp.maximum(xidx(ph, c) * per_ch - 1, 0), 0))),
                (xbc, pl.BlockSpec((1, halo, CONV_CH),
                                   lambda i, ph, c: (i, jnp.minimum((xidx(ph, c) + 1) * per_ch,
                                                                    l // halo - 1), 0))),
                (dt, pl.BlockSpec((1, ch, LANES), lambda i, ph, c: (i, cidx(ph, c), ph)))]
    if emit_y:
        operands.append((z, y_spec))
    operands += [(conv_w8, full2(conv_w8)), (conv_b, full2(conv_b)),
                 (dt_bias, per_phase), (a_log, per_phase)]
    if emit_y:
        operands += [(d_skip_x, full2(d_skip_x)), (norm_w, full2(norm_w))]
    operands += [(expand, full2(expand)), (init, st_spec)]
    st_shape = jax.ShapeDtypeStruct((b, 2, SSD_GROUPS, SSD_STATE, GROUP_W), F32)
    scratch = [pltpu.VMEM((nc, ch, CONV_CH), BF16)]
    if emit_y:
        scratch += [pltpu.VMEM((ch, SSD_W), F32),
                    pltpu.VMEM((nc, ch, SSD_W), BF16)]
    scratch.append(pltpu.VMEM((SSD_GROUPS, SSD_STATE, GROUP_W), F32))
    outs = pl.pallas_call(
        functools.partial(_ssd_kernel, nc, emit_y),
        grid=(b, 2, nc),
        in_specs=[spec for _, spec in operands],
        out_specs=[y_spec, st_spec] if emit_y else [st_spec],
        out_shape=[jax.ShapeDtypeStruct((b, l, SSD_W), BF16), st_shape] if emit_y else [st_shape],
        scratch_shapes=scratch,
        compiler_params=_cparams(("arbitrary", "arbitrary", "arbitrary")),
    )(*[a for a, _ in operands])
    return outs if emit_y else (None, outs[0])


def _mix_kernel(y_ref, u_ref, x_ref, pa_ref, pcnt_ref, pw_ref, psc_ref, wo_ref, g1_ref,
                nw_ref, sh_ref, sc_ref, rw_ref, rb_ref,
                x1_ref, h_ref, rt_ref, tcnt_ref):
    tm = x_ref.shape[1]

    pooled = []
    for g in range(len(POOL_WINDOWS)):
        parts = []
        for r in range(tm // POOL_TM):
            u = u_ref[0, r * POOL_TM:(r + 1) * POOL_TM, g * POOL_GW:(g + 1) * POOL_GW]
            wsum = _dot(pa_ref[g], u)
            parts.append((wsum / pcnt_ref[g] - u.astype(F32)).astype(BF16))
        pooled.append(jnp.concatenate(parts, axis=0))
    mapped = []
    for gp in range(len(POOL_WINDOWS) // 2):
        pair = jnp.concatenate(pooled[2 * gp:2 * gp + 2], axis=1)
        mapped.append((_dot(pair, pw_ref[gp])
                       * psc_ref[:, 2 * gp * POOL_GW:(2 * gp + 2) * POOL_GW]).astype(BF16))
    y_pool = jnp.concatenate(mapped, axis=1)

    mix = _dot(y_ref[0], wo_ref[0:SSD_W, :]) + _dot(y_pool, wo_ref[SSD_W:SSD_W + POOL_W, :])
    x1 = x_ref[0] + g1_ref[0] * mix
    x1_ref[0] = x1

    ms = jnp.mean(x1 * x1, axis=-1, keepdims=True)
    h = x1 * lax.rsqrt(ms + RMS_EPS) * nw_ref[...]
    h = h * (1.0 + sc_ref[0]) + sh_ref[0]
    h_ref[0] = h

    h0, h1 = _split2(h)
    rw2 = jnp.concatenate(_split2(rw_ref[...]), axis=1)
    t0 = _dot(h0, rw2)
    t1 = _dot(h1, rw2)
    logits = (t0[:, 0:LANES] + t0[:, LANES:2 * LANES] + t1[:, 0:LANES] + t1[:, LANES:2 * LANES]
              + rb_ref[...])
    lane = lax.broadcasted_iota(jnp.int32, (tm, LANES), 1)
    lane_f = lane.astype(F32)
    work = jnp.where(lane < N_EXPERTS, logits, NEG_BIG)
    vals, hots = [], []
    for _ in range(TOP_K):
        m = jnp.max(work, axis=-1, keepdims=True)
        first_idx = jnp.min(jnp.where(work == m, lane_f, float(LANES)), axis=-1, keepdims=True)
        hot = lane_f == first_idx
        vals.append(m)
        hots.append(hot)
        work = jnp.where(hot, 2.0 * NEG_BIG, work)
    exps = [jnp.exp(v - vals[0]) for v in vals]
    denom = exps[0] + exps[1] + exps[2] + exps[3]

    onehot = jnp.zeros((tm, LANES), F32)
    for hot in hots:
        onehot = onehot + jnp.where(hot, 1.0, 0.0)
    packed = jnp.zeros((tm, LANES), F32)
    for k in range(TOP_K):
        idx_k = jnp.sum(jnp.where(hots[k], lane_f, 0.0), axis=-1, keepdims=True)
        packed = jnp.where(lane == k, idx_k, packed)
        packed = jnp.where(lane == GATE_LANE + k, exps[k] / denom, packed)
    rt_ref[0] = packed
    row8 = lax.broadcasted_iota(jnp.int32, (SUBLANES, LANES), 0)
    tcnt = jnp.zeros((SUBLANES, LANES), F32)
    for r in range(tm // DISP_TM):
        sub = jnp.sum(onehot[r * DISP_TM:(r + 1) * DISP_TM], axis=0, keepdims=True)
        tcnt = jnp.where(row8 == r, sub, tcnt)
    tcnt_ref[0] = tcnt


def _mix(y, u, x, pool_a, pool_cnt, pool_w, pool_scale, w_out, g1, norm_w, shift, scale,
         router_w, router_b):
    b, l, d = x.shape
    tm = MIX_TM
    per_l = l // tm
    tok = lambda n: pl.BlockSpec((1, tm, n), lambda i, j: (i, j, 0))
    per_b = pl.BlockSpec((1, 1, d), lambda i, j: (i, 0, 0))
    full = lambda a: pl.BlockSpec(a.shape, lambda i, j: (0,) * a.ndim)
    return pl.pallas_call(
        _mix_kernel,
        grid=(b, per_l),
        in_specs=[tok(SSD_W), tok(POOL_W), tok(d), full(pool_a), full(pool_cnt), full(pool_w),
                  full(pool_scale), full(w_out), per_b, full(norm_w), per_b, per_b,
                  full(router_w), full(router_b)],
        out_specs=[tok(d), tok(d), tok(LANES),
                   pl.BlockSpec((1, SUBLANES, LANES), lambda i, j: (i * per_l + j, 0, 0))],
        out_shape=[jax.ShapeDtypeStruct((b, l, d), F32),
                   jax.ShapeDtypeStruct((b, l, d), F32),
                   jax.ShapeDtypeStruct((b, l, LANES), F32),
                   jax.ShapeDtypeStruct((b * per_l, SUBLANES, LANES), F32)],
        compiler_params=_cparams(("arbitrary", "arbitrary")),
    )(y, u, x, pool_a, pool_cnt, pool_w, pool_scale, w_out, g1, norm_w, shift, scale,
      router_w, router_b)


def _group_copies(n_groups, src_ref, src_group, dst_ref, dst_group, sem, wait):
    def pieces(bits):
        for bit in bits:
            @pl.when((n_groups & bit) != 0)
            def _():
                done = n_groups & ~(2 * bit - 1)
                src = src_ref.at[pl.ds(pl.multiple_of((src_group + done) * SUBLANES, SUBLANES),
                                       bit * SUBLANES)]
                dst = dst_ref.at[pl.ds(pl.multiple_of((dst_group + done) * SUBLANES, SUBLANES),
                                       bit * SUBLANES)]
                cp = pltpu.make_async_copy(src, dst, sem)
                cp.wait() if wait else cp.start()

    split = RUN_BITS.index(RUN_SMALL_BIT)
    pl.when(n_groups >= 2 * RUN_SMALL_BIT)(lambda: pieces(RUN_BITS[:split]))
    pieces(RUN_BITS[split:])


def _wait_groups(n_groups, src_ref, dst_ref, sem):
    for bit in WAIT_BITS:
        @pl.when((n_groups & bit) != 0)
        def _():
            rows = bit * SUBLANES
            pltpu.make_async_copy(src_ref.at[pl.ds(0, rows)], dst_ref.at[pl.ds(0, rows)], sem).wait()


def _stage_positions(rt_t, base_col, upper):
    tm = rt_t.shape[1]
    expert_f = lax.broadcasted_iota(jnp.int32, (LANES, tm), 0).astype(F32)
    hots = [expert_f == rt_t[k:k + 1, :] for k in range(TOP_K)]
    onehot = jnp.zeros((LANES, tm), F32)
    for hot in hots:
        onehot = onehot + jnp.where(hot, 1.0, 0.0)
    pos = _dot(onehot.astype(BF16), upper) + base_col
    return [jnp.sum(jnp.where(hot, pos, 0.0), axis=0, keepdims=True) for hot in hots]


def _dispatch_kernel(o_ref, f_ref, fl_ref, rem_ref, hg_ref, tot_ref, used_ref, fs_ref, flen_ref,
                     h_ref, rt_ref, sb_ref, tri_ref, xs_ref,
                     stg2_ref, tails_ref, zero_ref, sem2, fill_sem):
    i = pl.program_id(0)
    tm = h_ref.shape[0]
    n_slots = xs_ref.shape[0]
    base = i * N_EXPERTS
    hi_mask = jnp.int32(-65536)
    slot = i % 2
    stg_ref = stg2_ref.at[slot]
    sem = sem2.at[slot]

    @pl.when(i == 0)
    def _():
        tails_ref[...] = jnp.zeros_like(tails_ref)
        stg2_ref[...] = jnp.zeros_like(stg2_ref)

    rt_t = rt_ref[...].T
    pos = _stage_positions(rt_t, sb_ref[0], tri_ref[...])
    gates = [rt_t[GATE_LANE + k:GATE_LANE + k + 1, :] for k in range(TOP_K)]

    hb = h_ref[...].astype(BF16)
    half = PACK_W
    used_rows = used_ref[i] * SUBLANES

    def stage_chunk(r_lo):
        row_f = (lax.broadcasted_iota(jnp.int32, (STAGE_CHUNK, tm), 0) + r_lo).astype(F32)
        sel = jnp.zeros((STAGE_CHUNK, tm), F32)
        gsel = jnp.zeros((STAGE_CHUNK, tm), F32)
        for k in range(TOP_K):
            hit = row_f == pos[k]
            sel = jnp.where(hit, 1.0, sel)
            gsel = jnp.where(hit, gates[k], gsel)
        gate = jnp.sum(gsel, axis=-1, keepdims=True)
        moved = _dot(sel.astype(BF16), hb)
        lo = pltpu.bitcast(moved[:, 0:half], jnp.int32)
        hi = pltpu.bitcast(moved[:, half:2 * half], jnp.int32)
        stg_ref[r_lo:r_lo + STAGE_CHUNK, 0:half] = lax.shift_right_logical(lo, 16) | (hi & hi_mask)
        stg_ref[r_lo:r_lo + STAGE_CHUNK, half:half + LANES] = pltpu.bitcast(
            jnp.broadcast_to(gate, (STAGE_CHUNK, LANES)), jnp.int32)

    always = TOP_K * tm // STAGE_CHUNK
    for rc in range(STAGE_ROWS // STAGE_CHUNK):
        if rc < always:
            stage_chunk(rc * STAGE_CHUNK)
        else:
            pl.when(rc * STAGE_CHUNK < used_rows)(functools.partial(stage_chunk, rc * STAGE_CHUNK))

    def carry_tails(e, cr):
        first = pl.multiple_of(o_ref[base + e] * SUBLANES, SUBLANES)
        stg_ref[pl.ds(first, SUBLANES), :] = stg_ref[pl.ds(first, SUBLANES), :] | tails_ref[e]
        part = pl.multiple_of((o_ref[base + e] + f_ref[base + e]) * SUBLANES, SUBLANES)
        tails_ref[e] = jnp.where(rem_ref[base + e] > 0, stg_ref[pl.ds(part, SUBLANES), :], 0)
        return cr

    lax.fori_loop(0, N_EXPERTS, carry_tails, 0)

    def run_copies(e, cr):
        _group_copies(fl_ref[base + e], stg_ref, o_ref[base + e], xs_ref, hg_ref[base + e], sem, False)
        return cr

    lax.fori_loop(0, N_EXPERTS, run_copies, 0)

    @pl.when(i == pl.num_programs(0) - 1)
    def _():
        zero_ref[...] = jnp.zeros_like(zero_ref)
        tail_group = fs_ref[N_EXPERTS]
        n_tail = (n_slots // SUBLANES - tail_group) // (MOE_BLK // SUBLANES)

        def tail_copy(j, wait):
            off = pl.multiple_of(tail_group * SUBLANES + j * MOE_BLK, MOE_BLK)
            cp = pltpu.make_async_copy(zero_ref, xs_ref.at[pl.ds(off, MOE_BLK)], fill_sem)
            cp.wait() if wait else cp.start()

        for wait in (False, True):
            def pad_body(e, cr):
                _group_copies(flen_ref[e], zero_ref, 0, xs_ref, fs_ref[e], fill_sem, wait)
                return cr
            lax.fori_loop(0, N_EXPERTS, pad_body, 0)
            lax.fori_loop(0, n_tail, lambda j, cr: (tail_copy(j, wait), cr)[1], 0)

    @pl.when(i > 0)
    def _():
        _wait_groups(tot_ref[jnp.maximum(i - 1, 0)], stg2_ref.at[1 - slot], xs_ref, sem2.at[1 - slot])

    @pl.when(i == pl.num_programs(0) - 1)
    def _():
        _wait_groups(tot_ref[i], stg_ref, xs_ref, sem)


def _dispatch(tables, fill_groups, fill_len_groups, h, rt, stage_base, n_slots):
    t, d = h.shape
    tm = DISP_TM
    grid_spec = pltpu.PrefetchScalarGridSpec(
        num_scalar_prefetch=len(tables) + 2,
        grid=(t // tm,),
        in_specs=[pl.BlockSpec((tm, d), lambda i, *_: (i, 0)),
                  pl.BlockSpec((tm, LANES), lambda i, *_: (i, 0)),
                  pl.BlockSpec((1, LANES, 1), lambda i, *_: (i, 0, 0)),
                  pl.BlockSpec((tm, tm), lambda i, *_: (0, 0))],
        out_specs=pl.BlockSpec(memory_space=pl.ANY),
        scratch_shapes=[pltpu.VMEM((2, STAGE_ROWS, XS_W), jnp.int32),
                        pltpu.VMEM((N_EXPERTS, SUBLANES, XS_W), jnp.int32),
                        pltpu.VMEM((MOE_BLK, XS_W), jnp.int32),
                        pltpu.SemaphoreType.DMA((2,)), pltpu.SemaphoreType.DMA(())],
    )
    return pl.pallas_call(
        _dispatch_kernel,
        grid_spec=grid_spec,
        out_shape=jax.ShapeDtypeStruct((n_slots, XS_W), jnp.int32),
        compiler_params=_cparams(("arbitrary",)),
    )(*tables, fill_groups, fill_len_groups, h, rt, stage_base, _strict_upper(tm))


def _expert_kernel(be_ref, nused_ref, xs_ref, w1_ref, b1_ref, w2_ref, b2_ref, perm_ref, y_ref,
                   w1s_ref, w2s_ref):
    i = pl.program_id(0)
    prev = be_ref[jnp.maximum(i - 1, 0)]
    changed = (i == 0) | (be_ref[i] != prev)
    dff2 = w1_ref.shape[2]
    tile = 2 * LANES

    @pl.when(changed & (i < nused_ref[0]))
    def _():
        for j in range(dff2 // tile):
            wj = w1_ref[0, :, j * tile:(j + 1) * tile].astype(BF16)
            w1s_ref[:, j * tile:(j + 1) * tile] = _dot(wj, perm_ref[...]).astype(BF16)
        w2s_ref[...] = w2_ref[0].astype(BF16)

    @pl.when(i < nused_ref[0])
    def _():
        hi_mask = jnp.int32(-65536)
        words = xs_ref[:, 0:PACK_W]
        x = jnp.concatenate(
            [pltpu.bitcast(lax.shift_left(words, 16), F32).astype(BF16),
             pltpu.bitcast(words & hi_mask, F32).astype(BF16)], axis=1)
        gate = pltpu.bitcast(xs_ref[:, PACK_W:PACK_W + 1], F32)
        acts = []
        for j in range(dff2 // tile):
            hb = _dot(x, w1s_ref[:, j * tile:(j + 1) * tile]) + b1_ref[0, :, j * tile:(j + 1) * tile]
            gp = jnp.minimum(hb[:, 0:LANES], SWIGLU_LIMIT)
            up = jnp.clip(hb[:, LANES:tile], -SWIGLU_LIMIT, SWIGLU_LIMIT)
            acts.append((gp * _sigmoid(SWIGLU_ALPHA * gp) * (up + 1.0)).astype(BF16))
        act = jnp.concatenate(acts, axis=1)
        y = (_dot(act, w2s_ref[...]) + b2_ref[0]) * gate
        lo = pltpu.bitcast(y[:, 0:PACK_W].astype(BF16).astype(F32), jnp.int32)
        hi = pltpu.bitcast(y[:, PACK_W:2 * PACK_W].astype(BF16).astype(F32), jnp.int32)
        y_ref[...] = lax.shift_right_logical(lo, 16) | (hi & hi_mask)

    @pl.when(i >= nused_ref[0])
    def _():
        y_ref[...] = jnp.zeros_like(y_ref)


def _experts(block_exp, n_used, xs, w1, b1p, w2, b2, perm, n_blocks):
    blk = MOE_BLK
    d = w1.shape[1]
    dff2 = w1.shape[2]
    dff = w2.shape[1]

    def x_idx(i, be, nu):
        return (jnp.minimum(i, nu[0] - 1), 0)

    grid_spec = pltpu.PrefetchScalarGridSpec(
        num_scalar_prefetch=2,
        grid=(n_blocks,),
        in_specs=[pl.BlockSpec((blk, XS_W), x_idx),
                  pl.BlockSpec((1, d, dff2), lambda i, be, nu: (be[i], 0, 0)),
                  pl.BlockSpec((1, 1, dff2), lambda i, be, nu: (be[i], 0, 0)),
                  pl.BlockSpec((1, dff, d), lambda i, be, nu: (be[i], 0, 0)),
                  pl.BlockSpec((1, 1, d), lambda i, be, nu: (be[i], 0, 0)),
                  pl.BlockSpec(perm.shape, lambda i, be, nu: (0, 0))],
        out_specs=pl.BlockSpec((blk, PACK_W), lambda i, be, nu: (i, 0)),
        scratch_shapes=[pltpu.VMEM((d, dff2), BF16), pltpu.VMEM((dff, d), BF16)],
    )
    return pl.pallas_call(
        _expert_kernel,
        grid_spec=grid_spec,
        out_shape=jax.ShapeDtypeStruct((n_blocks * blk, PACK_W), jnp.int32),
        compiler_params=_cparams(("arbitrary",)),
    )(block_exp, n_used, xs, w1, b1p, w2, b2, perm)


def _combine_kernel(o_ref, g_ref, hg_ref, tot_ref, x1_ref, rt_ref, sb_ref, tri_ref, g2_ref, fnw_ref,
                    ys_ref, out_ref, stg_ref, sem):
    tm = x1_ref.shape[1]
    step = pl.program_id(0) * pl.num_programs(1) + pl.program_id(1)
    n_steps = pl.num_programs(0) * pl.num_programs(1)
    slot = step % 2
    hi_mask = jnp.int32(-65536)

    def fetch(tile, into):
        def body(e, cr):
            _group_copies(g_ref[tile * N_EXPERTS + e], ys_ref, hg_ref[tile * N_EXPERTS + e],
                          stg_ref.at[into], o_ref[tile * N_EXPERTS + e], sem.at[into], False)
            return cr
        lax.fori_loop(0, N_EXPERTS, body, 0)

    @pl.when(step == 0)
    def _():
        stg_ref[...] = jnp.zeros_like(stg_ref)
        fetch(0, 0)

    @pl.when(step + 1 < n_steps)
    def _():
        fetch(step + 1, 1 - slot)

    _wait_groups(tot_ref[step], ys_ref, stg_ref.at[slot], sem.at[slot])

    pos_rows = _stage_positions(rt_ref[0].T, sb_ref[0], tri_ref[...])
    expert_i = lax.broadcasted_iota(jnp.int32, (LANES, tm), 0)
    stacked = jnp.zeros((LANES, tm), F32)
    for k in range(TOP_K):
        stacked = jnp.where(expert_i == k, pos_rows[k], stacked)
    pos_cols = stacked.T
    col_f = lax.broadcasted_iota(jnp.int32, (tm, STAGE_ROWS), 1).astype(F32)
    sel = jnp.zeros((tm, STAGE_ROWS), F32)
    for k in range(TOP_K):
        sel = jnp.where(col_f == pos_cols[:, k:k + 1], 1.0, sel)
    sel = sel.astype(BF16)
    words = stg_ref[slot]
    lo = pltpu.bitcast(lax.shift_left(words, 16), F32).astype(BF16)
    hi = pltpu.bitcast(words & hi_mask, F32).astype(BF16)
    moe = jnp.concatenate([_dot(sel, lo), _dot(sel, hi)], axis=1)
    x2 = x1_ref[0] + g2_ref[0] * moe
    ms = jnp.mean(x2 * x2, axis=-1, keepdims=True)
    out_ref[0] = x2 * lax.rsqrt(ms + RMS_EPS) * fnw_ref[...]


def _combine(tables, x1, rt, stage_base, g2, fnw, ys):
    b, l, d = x1.shape
    tm = DISP_TM
    per_l = l // tm
    tok = lambda n: pl.BlockSpec((1, tm, n), lambda i, j, *_: (i, j, 0))
    grid_spec = pltpu.PrefetchScalarGridSpec(
        num_scalar_prefetch=len(tables),
        grid=(b, per_l),
        in_specs=[tok(d), tok(LANES),
                  pl.BlockSpec((1, LANES, 1), lambda i, j, *_: (i * per_l + j, 0, 0)),
                  pl.BlockSpec((tm, tm), lambda i, j, *_: (0, 0)),
                  pl.BlockSpec((1, 1, d), lambda i, j, *_: (i, 0, 0)),
                  pl.BlockSpec(fnw.shape, lambda i, j, *_: (0, 0)),
                  pl.BlockSpec(memory_space=pl.ANY)],
        out_specs=tok(d),
        scratch_shapes=[pltpu.VMEM((2, STAGE_ROWS, PACK_W), jnp.int32),
                        pltpu.SemaphoreType.DMA((2,))],
    )
    return pl.pallas_call(
        _combine_kernel,
        grid_spec=grid_spec,
        out_shape=jax.ShapeDtypeStruct((b, l, d), F32),
        compiler_params=_cparams(("arbitrary", "arbitrary")),
    )(*tables, x1, rt, stage_base, _strict_upper(tm), g2, fnw, ys)


def _pool_constants():
    i = np.arange(POOL_TM)[:, None]
    j = np.arange(POOL_TM)[None, :]
    same_row = (i // GRID_W) == (j // GRID_W)
    mats, cnts = [], []
    for w in POOL_WINDOWS:
        band = same_row & (j - i >= -(w // 2)) & (j - i < w - w // 2)
        mats.append(band)
        cnts.append(np.broadcast_to(band.sum(axis=1, keepdims=True), (POOL_TM, POOL_GW)))
    return (jnp.asarray(np.stack(mats), BF16), jnp.asarray(np.stack(cnts), F32))


def _head_expand():
    e = np.zeros((LANES, SSD_W), np.float32)
    for h in range(SSD_HEADS):
        e[h, h * SSD_HEADDIM:(h + 1) * SSD_HEADDIM] = 1.0
    return jnp.asarray(e, BF16)


def _deinterleave_perm():
    n = 2 * LANES
    p = np.zeros((n, n), np.float32)
    for k in range(LANES):
        p[2 * k, k] = 1.0
        p[2 * k + 1, LANES + k] = 1.0
    return jnp.asarray(p, BF16)


def _strict_upper(n):
    return jnp.asarray(np.triu(np.ones((n, n), np.float32), 1), BF16)


def _pad_lanes(a, n):
    return jnp.pad(a, [(0, 0)] * (a.ndim - 1) + [(0, n - a.shape[-1])])


def kernel(x, c, ctx, c_ctx, w_mod, b_mod, norm1_w, norm2_w, w_in, conv_w, conv_b, dt_bias, a_log,
           d_skip, ssd_norm_w, pool_w, pool_scale, w_out, router_w, router_b, w1, b1, w2, b2,
           final_norm_w):
    depth = w_mod.shape[0]
    assert depth == 1, "single-layer problem"
    b, l, d = x.shape
    lc = ctx.shape[1]
    xbcdt = CONV_CH + 2 * SSD_HEADS

    mod_rows = 2 * SUBLANES
    cc = jnp.zeros((mod_rows, d), F32).at[0:b].set(c).at[b].set(c_ctx)
    mod = _modulation(cc, w_mod[0], b_mod[0])
    sh1, sc1, g1, sh2, sc2, g2 = [m.reshape(b, 1, d) for m in jnp.split(mod[0:b], 6, axis=-1)]
    csh1, csc1 = [jnp.broadcast_to(m.reshape(1, 1, d), (b, 1, d))
                  for m in jnp.split(mod[b:b + 1], 6, axis=-1)[0:2]]

    wi = w_in[0]
    wx = wi[:, 0:CONV_CH].astype(BF16)
    wd = jnp.concatenate([_pad_lanes(wi[:, CONV_CH:CONV_CH + SSD_HEADS], LANES),
                          _pad_lanes(wi[:, CONV_CH + SSD_HEADS:xbcdt], LANES)], axis=1).astype(BF16)
    wz = wi[:, xbcdt:xbcdt + SSD_W].astype(BF16)
    wp = wi[:, xbcdt + SSD_W:].astype(BF16)
    n1 = norm1_w[0].reshape(1, d)
    conv_w8 = jnp.pad(conv_w[0], ((0, SUBLANES - SSD_CONV), (0, 0)))
    conv_b1 = conv_b[0].reshape(1, CONV_CH)
    dtb = _pad_lanes(dt_bias[0], LANES).reshape(2, 1, LANES)
    alog = _pad_lanes(a_log[0], LANES).reshape(2, 1, LANES)
    dsk = jnp.repeat(d_skip[0], SSD_HEADDIM).reshape(1, SSD_W)
    snw = ssd_norm_w[0].reshape(1, SSD_W)
    expand = _head_expand()

    xbc_c, dt_c = _inproj(ctx, n1, csh1, csc1, (wx, wd), (BF16, F32), min(PROJ_TM, lc))
    zero_state = jnp.zeros((b, 2, SSD_GROUPS, SSD_STATE, GROUP_W), F32)
    _, ctx_states = _ssd(xbc_c, dt_c, None, conv_w8, conv_b1, dtb, alog, dsk, snw, expand, zero_state)

    xbc, dt, z, u_pool = _inproj(x, n1, sh1, sc1, (wx, wd, wz, wp), (BF16, F32, BF16, BF16), PROJ_TM)
    y_ssd, _ = _ssd(xbc, dt, z, conv_w8, conv_b1, dtb, alog, dsk, snw, expand, ctx_states)

    pool_a, pool_cnt = _pool_constants()
    pw = pool_w[0].astype(BF16)
    zero_gw = jnp.zeros((POOL_GW, POOL_GW), BF16)
    pw_pairs = jnp.stack([jnp.block([[pw[2 * gp], zero_gw], [zero_gw, pw[2 * gp + 1]]])
                          for gp in range(len(POOL_WINDOWS) // 2)])
    x1, h2, rt, tcnt = _mix(
        y_ssd, u_pool, x, pool_a, pool_cnt, pw_pairs, pool_scale[0].reshape(1, POOL_W),
        w_out[0].astype(BF16), g1, norm2_w[0].reshape(1, d), sh2, sc2,
        _pad_lanes(router_w[0], LANES), _pad_lanes(router_b[0].reshape(1, N_EXPERTS), LANES))

    i32 = jnp.int32
    t = b * l
    n_tiles = t // DISP_TM
    tc = tcnt[:, 0:MIX_TM // DISP_TM, 0:N_EXPERTS].reshape(n_tiles, N_EXPERTS).astype(i32)
    counts = jnp.sum(tc, axis=0)
    run_start = jnp.cumsum(tc, axis=0) - tc
    padded = (counts + MOE_BLK - 1) // MOE_BLK * MOE_BLK
    pad_end = jnp.cumsum(padded)
    pad_start = pad_end - padded
    carried = run_start % SUBLANES
    span = carried + tc
    groups = (span + SUBLANES - 1) // SUBLANES
    full = span // SUBLANES
    stage_group = jnp.cumsum(groups, axis=1) - groups
    slot_group = (pad_start[None, :] + run_start - carried) // SUBLANES
    flush = full.at[n_tiles - 1].set(groups[n_tiles - 1])
    flat = lambda a: a.reshape(n_tiles * N_EXPERTS).astype(i32)
    stage_base = _pad_lanes((stage_group * SUBLANES + carried).astype(F32), LANES)
    stage_base = stage_base.reshape(n_tiles, LANES, 1)
    n_blocks = (t * TOP_K) // MOE_BLK + N_EXPERTS
    n_used = (pad_end[-1] // MOE_BLK).astype(i32).reshape(1)
    blk_start = jnp.minimum(jnp.arange(n_blocks, dtype=i32), n_used[0] - 1) * MOE_BLK
    block_exp = jnp.minimum(jnp.sum(blk_start[:, None] >= pad_end[None, :], axis=1),
                            N_EXPERTS - 1).astype(i32)
    written = (counts + SUBLANES - 1) // SUBLANES * SUBLANES
    fill_group = (jnp.concatenate([pad_start + written, pad_end[-1:]]) // SUBLANES).astype(i32)
    fill_groups = (jnp.concatenate([padded - written, jnp.zeros((1,), i32)]) // SUBLANES).astype(i32)

    xs = _dispatch((flat(stage_group), flat(full), flat(flush), flat(span % SUBLANES), flat(slot_group),
                    jnp.sum(flush, axis=1).astype(i32), jnp.sum(groups, axis=1).astype(i32)),
                   fill_group, fill_groups, h2.reshape(t, d), rt.reshape(t, LANES), stage_base,
                   n_blocks * MOE_BLK)
    dff2 = w1.shape[-1]
    b1p = jnp.concatenate(
        [b1[0].reshape(N_EXPERTS, dff2 // (2 * LANES), LANES, 2)[..., 0],
         b1[0].reshape(N_EXPERTS, dff2 // (2 * LANES), LANES, 2)[..., 1]], axis=-1
    ).reshape(N_EXPERTS, 1, dff2)
    ys = _experts(block_exp, n_used, xs, w1[0], b1p, w2[0], b2[0].reshape(N_EXPERTS, 1, d),
                  _deinterleave_perm(), n_blocks)
    return _combine((flat(stage_group), flat(groups), flat(slot_group),
                     jnp.sum(groups, axis=1).astype(i32)), x1, rt, stage_base, g2,
                    final_norm_w.reshape(1, d), ys)
```

```python
import functools

import numpy as np
import jax
import jax.numpy as jnp
from jax import lax
from jax.experimental import pallas as pl
from jax.experimental.pallas import tpu as pltpu

F32 = jnp.float32
BF16 = jnp.bfloat16

SSD_HEADDIM = 64
SSD_GROUPS = 4
SSD_HPG = 6
SSD_HEADS = SSD_GROUPS * SSD_HPG
SSD_STATE = 128
SSD_CONV = 5
SSD_CHUNK = 128
SSD_W = SSD_HEADS * SSD_HEADDIM
GROUP_W = SSD_HPG * SSD_HEADDIM
CONV_CH = SSD_W + 2 * SSD_GROUPS * SSD_STATE
POOL_WINDOWS = (2, 4, 8, 16)
POOL_GW = 128
POOL_W = POOL_GW * len(POOL_WINDOWS)
GRID_W = 64
N_EXPERTS = 32
TOP_K = 4
SWIGLU_ALPHA = 1.702
SWIGLU_LIMIT = 7.0
RMS_EPS = 1e-6
LOG2_E = 1.4426950408889634

LANES = 128
SUBLANES = 8
VMEM_LIMIT_BYTES = 56 * 1024 * 1024

MOD_TN = 1024
PROJ_TM = 512
MIX_TM = 512
POOL_TM = 256
MOE_BLK = 512
DISP_TM = 256
NEG_BIG = -1e30

GATE_LANE = 2 * TOP_K
RT_ROWS = 2 * SUBLANES
PACK_W = 512
XS_W = PACK_W + LANES
STAGE_CHUNK = 256
_STAGE_NEED = TOP_K * DISP_TM + N_EXPERTS * 2 * (SUBLANES - 1) + SUBLANES
STAGE_ROWS = -(-_STAGE_NEED // STAGE_CHUNK) * STAGE_CHUNK
RUN_BITS = (32, 16, 8, 4, 2, 1)
RUN_SMALL_BIT = 4
WAIT_BITS = (128, 64, 32, 16, 8, 4, 2, 1)


def _sigmoid(x):
    return 0.5 * jnp.tanh(0.5 * x) + 0.5


def _split2(a):
    hi = a.astype(BF16)
    mid = (a - hi.astype(F32)).astype(BF16)
    return hi, mid


def _split3(a):
    hi = a.astype(BF16)
    r = a - hi.astype(F32)
    mid = r.astype(BF16)
    lo = (r - mid.astype(F32)).astype(BF16)
    return hi, mid, lo


def _dot(a, b):
    return jnp.dot(a, b, preferred_element_type=F32)


def _dot_exact_rhs(a_f32, b_bf16, parts):
    pieces = _split3(a_f32) if parts == 3 else _split2(a_f32)
    out = _dot(pieces[0], b_bf16)
    for p in pieces[1:]:
        out = out + _dot(p, b_bf16)
    return out


def _dot_hi(a_f32, b_f32):
    a0, a1, a2 = _split3(a_f32)
    b0, b1, b2 = _split3(b_f32)
    out = _dot(a0, b0)
    out = out + _dot(a0, b1) + _dot(a1, b0)
    out = out + _dot(a1, b1) + _dot(a0, b2) + _dot(a2, b0)
    return out


def _cparams(sem):
    return pltpu.CompilerParams(dimension_semantics=sem, vmem_limit_bytes=VMEM_LIMIT_BYTES)


def _mod_kernel(c_ref, w_ref, b_ref, o_ref):
    c = c_ref[...]
    s = c * _sigmoid(c)
    o_ref[...] = _dot_hi(s, w_ref[...]) + b_ref[...]


def _modulation(cc, w_mod, b_mod):
    rows, d = cc.shape
    n = w_mod.shape[1]
    return pl.pallas_call(
        _mod_kernel,
        grid=(n // MOD_TN,),
        in_specs=[pl.BlockSpec((rows, d), lambda j: (0, 0)),
                  pl.BlockSpec((d, MOD_TN), lambda j: (0, j)),
                  pl.BlockSpec((1, MOD_TN), lambda j: (0, j))],
        out_specs=pl.BlockSpec((rows, MOD_TN), lambda j: (0, j)),
        out_shape=jax.ShapeDtypeStruct((rows, n), F32),
        compiler_params=_cparams(("arbitrary",)),
    )(cc, w_mod, b_mod.reshape(1, n))


def _inproj_kernel(n_groups, x0_ref, xn_ref, nw_ref, sh0_ref, sc0_ref, shn_ref, scn_ref, *refs):
    w_refs = refs[0:n_groups]
    out_refs = refs[n_groups:2 * n_groups]
    hb_even_ref, hb_odd_ref = refs[2 * n_groups:]
    step = pl.program_id(0) * pl.num_programs(1) + pl.program_id(1)

    def normed(x, sh, sc):
        ms = jnp.mean(x * x, axis=-1, keepdims=True)
        h = x * lax.rsqrt(ms + RMS_EPS) * nw_ref[...]
        return (h * (1.0 + sc) + sh).astype(BF16)

    @pl.when(step == 0)
    def _():
        hb_even_ref[...] = normed(x0_ref[0], sh0_ref[0], sc0_ref[0])

    def project(cur_ref, nxt_ref):
        hb = cur_ref[...]
        for w_ref, out_ref in zip(w_refs, out_refs):
            out_ref[0] = _dot(hb, w_ref[...]).astype(out_ref.dtype)
        nxt_ref[...] = normed(xn_ref[0], shn_ref[0], scn_ref[0])

    pl.when(step % 2 == 0)(functools.partial(project, hb_even_ref, hb_odd_ref))
    pl.when(step % 2 == 1)(functools.partial(project, hb_odd_ref, hb_even_ref))


def _inproj(x, norm_w, shift, scale, weights, out_dtypes, tm):
    b, l, d = x.shape
    per_l = l // tm
    last = b * per_l - 1

    def nxt(i, j):
        s = jnp.minimum(i * per_l + j + 1, last)
        return s // per_l, s % per_l

    full = lambda a: pl.BlockSpec(a.shape, lambda i, j: (0, 0))
    tok = lambda n: pl.BlockSpec((1, tm, n), lambda i, j: (i, j, 0))
    first_b = pl.BlockSpec((1, 1, d), lambda i, j: (0, 0, 0))
    next_b = pl.BlockSpec((1, 1, d), lambda i, j: (nxt(i, j)[0], 0, 0))
    return pl.pallas_call(
        functools.partial(_inproj_kernel, len(weights)),
        grid=(b, per_l),
        in_specs=[pl.BlockSpec((1, tm, d), lambda i, j: (0, 0, 0)),
                  pl.BlockSpec((1, tm, d), lambda i, j: (*nxt(i, j), 0)),
                  full(norm_w), first_b, first_b, next_b, next_b] + [full(w) for w in weights],
        out_specs=[tok(w.shape[1]) for w in weights],
        out_shape=[jax.ShapeDtypeStruct((b, l, w.shape[1]), dt)
                   for w, dt in zip(weights, out_dtypes)],
        scratch_shapes=[pltpu.VMEM((tm, d), BF16), pltpu.VMEM((tm, d), BF16)],
        compiler_params=_cparams(("arbitrary", "arbitrary")),
    )(x, x, norm_w, shift, scale, shift, scale, *weights)


def _ssd_kernel(nc, emit_y, *refs):
    if emit_y:
        (xbc_ref, prev_ref, next_ref, dt_ref, z_ref, cw_ref, cb_ref, dtb_ref, alog_ref, dsk_ref,
         nw_ref, exp_ref, init_ref, y_ref, fin_ref, act_ref, ybuf_ref, yf_ref, st_ref) = refs
        y_part = (z_ref, dsk_ref, nw_ref, y_ref, ybuf_ref, yf_ref)
    else:
        (xbc_ref, prev_ref, next_ref, dt_ref, cw_ref, cb_ref, dtb_ref, alog_ref,
         exp_ref, init_ref, fin_ref, act_ref, st_ref) = refs
        y_part = None
    ph = pl.program_id(1)
    c = pl.program_id(2)
    ci = c + ph * (nc - 1 - 2 * c)

    @pl.when(c == 0)
    def _():
        st_ref[...] = init_ref[0, ph]

    @pl.when(ph == 0)
    def _():
        _conv_silu(nc, ci, xbc_ref, prev_ref, next_ref, cw_ref, cb_ref, act_ref)

    _ssd_scan(nc, ph, c, ci, dt_ref, dtb_ref, alog_ref, exp_ref, act_ref, st_ref, fin_ref, y_part)


def _conv_silu(nc, ci, xbc_ref, prev_ref, next_ref, cw_ref, cb_ref, act_ref):
    ch = SSD_CHUNK
    has_prev = ci > 0
    has_next = ci < nc - 1
    row8 = lax.broadcasted_iota(jnp.int32, (SUBLANES, LANES), 0)
    half = SSD_CONV // 2
    for j in range(CONV_CH // LANES):
        cols = slice(j * LANES, (j + 1) * LANES)
        n_t = ch // SUBLANES
        tiles = [jnp.where(has_prev, prev_ref[0, SUBLANES:2 * SUBLANES, cols].astype(F32), 0.0)]
        for i in range(n_t // 2):
            both = xbc_ref[0, 2 * i * SUBLANES:2 * (i + 1) * SUBLANES, cols].astype(F32)
            tiles += [both[0:SUBLANES], both[SUBLANES:2 * SUBLANES]]
        tiles.append(jnp.where(has_next, next_ref[0, 0:SUBLANES, cols].astype(F32), 0.0))
        acc = [cb_ref[:, cols] + cw_ref[half:half + 1, cols] * tiles[i + 1] for i in range(n_t)]
        for s in range(1, half + 1):
            rot = [pltpu.roll(tl, s, axis=0) for tl in tiles[0:n_t + 1]]
            wk = cw_ref[half - s:half - s + 1, cols]
            for i in range(n_t):
                acc[i] = acc[i] + wk * jnp.where(row8 < s, rot[i], rot[i + 1])
            rot = [pltpu.roll(tl, SUBLANES - s, axis=0) for tl in tiles[1:n_t + 2]]
            wk = cw_ref[half + s:half + s + 1, cols]
            for i in range(n_t):
                acc[i] = acc[i] + wk * jnp.where(row8 >= SUBLANES - s, rot[i + 1], rot[i])
        silu = [0.5 * v * jnp.tanh(0.5 * v) + 0.5 * v for v in acc]
        act_ref[ci, :, cols] = jnp.concatenate(silu, axis=0).astype(BF16)


def _ssd_scan(nc, ph, c, ci, dt_ref, dtb_ref, alog_ref, exp_ref, act_ref, st_ref, fin_ref, y_part):
    ch = SSD_CHUNK
    emit_y = y_part is not None
    dtr = dt_ref[0] + dtb_ref[0]
    small = jnp.exp(-jnp.abs(dtr))
    one_plus = 1.0 + small
    log1p_small = jnp.where(one_plus == 1.0, small, jnp.log(one_plus) * (small / (one_plus - 1.0)))
    dtv = jnp.maximum(dtr, 0.0) + log1p_small
    a = dtv * (-jnp.exp(alog_ref[0]) * LOG2_E)
    row = lax.broadcasted_iota(jnp.int32, (ch, ch), 0)
    col = lax.broadcasted_iota(jnp.int32, (ch, ch), 1)
    tmask = (row - col) * (1 - 2 * ph) >= 0
    tri = jnp.where(tmask, 1.0, 0.0).astype(BF16)
    cs = _dot_exact_rhs_left(tri, a)
    tot = jnp.where(ph == 0, cs[ch - 1:ch, :], cs[0:1, :])
    cs_t = cs.T
    e_cs = jnp.exp2(cs)
    e_dec = jnp.exp2(tot - cs)
    e_tot = jnp.exp2(tot)
    expand = exp_ref[...]
    wdec_x = _dot((dtv * e_dec).astype(BF16), expand).astype(BF16)
    etot_x = _dot_exact_rhs(jnp.broadcast_to(e_tot, (SUBLANES, LANES)), expand, 2)[0:1]
    if emit_y:
        z_ref, dsk_ref, nw_ref, y_ref, ybuf_ref, yf_ref = y_part
        ecs_x = _dot(e_cs.astype(BF16), expand)
        src_t = cs_t - jnp.log2(dtv).T

    lane = lax.broadcasted_iota(jnp.int32, (ch, LANES), 1)
    for g in range(SSD_GROUPS):
        gs = slice(g * GROUP_W, (g + 1) * GROUP_W)
        b_bf = act_ref[ci, :, SSD_W + g * SSD_STATE:SSD_W + (g + 1) * SSD_STATE]
        x_bf = act_ref[ci, :, gs]
        s_prev = st_ref[g]
        if emit_y:
            c_bf = act_ref[ci, :, SSD_W + (SSD_GROUPS + g) * SSD_STATE:
                           SSD_W + (SSD_GROUPS + g + 1) * SSD_STATE]
            cb = lax.dot_general(c_bf, b_bf, (((1,), (1,)), ((), ())), preferred_element_type=F32)
            y_off = _dot(c_bf, s_prev.astype(BF16)) * ecs_x[:, gs]
        x_dec = x_bf * wdec_x[:, gs]
        st_ref[g] = s_prev * etot_x[:, gs] + lax.dot_general(
            b_bf, x_dec, (((0,), (0,)), ((), ())), preferred_element_type=F32)
        if not emit_y:
            continue
        for q in range(SSD_HPG // 2):
            lmats = []
            for h in (g * SSD_HPG + 2 * q, g * SSD_HPG + 2 * q + 1):
                diff = cs[:, h:h + 1] - src_t[h:h + 1, :]
                dec = jnp.exp2(jnp.where(tmask, diff, NEG_BIG))
                lmats.append((dec * cb).astype(BF16))
            xp = x_bf[:, q * LANES:(q + 1) * LANES]
            zero = jnp.zeros_like(xp)
            rhs = jnp.concatenate([jnp.where(lane < SSD_HEADDIM, xp, zero),
                                   jnp.where(lane >= SSD_HEADDIM, xp, zero)], axis=0)
            y_diag = _dot(jnp.concatenate(lmats, axis=1), rhs)
            ps = slice(g * GROUP_W + q * LANES, g * GROUP_W + (q + 1) * LANES)
            ybuf_ref[:, ps] = y_diag + y_off[:, q * LANES:(q + 1) * LANES]

    if emit_y:
        @pl.when(ph == 0)
        def _():
            yf_ref[ci] = ybuf_ref[...].astype(BF16)

        @pl.when(ph == 1)
        def _():
            yt = (yf_ref[ci].astype(F32) + ybuf_ref[...]
                  + act_ref[ci, :, 0:SSD_W].astype(F32) * dsk_ref[...])
            zz = z_ref[0].astype(F32)
            gt = yt * (zz * _sigmoid(zz))
            ms = jnp.mean(gt * gt, axis=-1, keepdims=True)
            y_ref[0] = (gt * lax.rsqrt(ms + RMS_EPS) * nw_ref[...]).astype(y_ref.dtype)

    @pl.when(c == nc - 1)
    def _():
        fin_ref[0, ph] = st_ref[...]


def _dot_exact_rhs_left(sel_bf16, a_f32):
    hi, mid, lo = _split3(a_f32)
    return _dot(sel_bf16, hi) + _dot(sel_bf16, mid) + _dot(sel_bf16, lo)


def _ssd(xbc, dt, z, conv_w8, conv_b, dt_bias, a_log, d_skip_x, norm_w, expand, init):
    emit_y = z is not None
    b, l, _ = xbc.shape
    ch = SSD_CHUNK
    nc = l // ch
    halo = 2 * SUBLANES
    per_ch = ch // halo

    def cidx(ph, c):
        return c + ph * (nc - 1 - 2 * c)

    def xidx(ph, c):
        return jnp.where(ph == 0, c, nc - 1)

    def out_idx(ph, c):
        return jnp.where(ph == 0, nc - 1, nc - 1 - c)

    full2 = lambda a: pl.BlockSpec(a.shape, lambda i, ph, c: (0, 0))
    st_spec = pl.BlockSpec((1, 2, SSD_GROUPS, SSD_STATE, GROUP_W), lambda i, ph, c: (i, 0, 0, 0, 0))
    per_phase = pl.BlockSpec((1, 1, LANES), lambda i, ph, c: (ph, 0, 0))
    y_spec = pl.BlockSpec((1, ch, SSD_W), lambda i, ph, c: (i, out_idx(ph, c), 0))
    operands = [(xbc, pl.BlockSpec((1, ch, CONV_CH), lambda i, ph, c: (i, xidx(ph, c), 0))),
                (xbc, pl.BlockSpec((1, halo, CONV_CH),
                                   lambda i, ph, c: (i, jnp.maximum(xidx(ph, c) * per_ch - 1, 0), 0))),
                (xbc, pl.BlockSpec((1, halo, CONV_CH),
                                   lambda i, ph, c: (i, jnp.minimum((xidx(ph, c) + 1) * per_ch,
                                                                    l // halo - 1), 0))),
                (dt, pl.BlockSpec((1, ch, LANES), lambda i, ph, c: (i, cidx(ph, c), ph)))]
    if emit_y:
        operands.append((z, y_spec))
    operands += [(conv_w8, full2(conv_w8)), (conv_b, full2(conv_b)),
                 (dt_bias, per_phase), (a_log, per_phase)]
    if emit_y:
        operands += [(d_skip_x, full2(d_skip_x)), (norm_w, full2(norm_w))]
    operands += [(expand, full2(expand)), (init, st_spec)]
    st_shape = jax.ShapeDtypeStruct((b, 2, SSD_GROUPS, SSD_STATE, GROUP_W), F32)
    scratch = [pltpu.VMEM((nc, ch, CONV_CH), BF16)]
    if emit_y:
        scratch += [pltpu.VMEM((ch, SSD_W), F32),
                    pltpu.VMEM((nc, ch, SSD_W), BF16)]
    scratch.append(pltpu.VMEM((SSD_GROUPS, SSD_STATE, GROUP_W), F32))
    outs = pl.pallas_call(
        functools.partial(_ssd_kernel, nc, emit_y),
        grid=(b, 2, nc),
        in_specs=[spec for _, spec in operands],
        out_specs=[y_spec, st_spec] if emit_y else [st_spec],
        out_shape=[jax.ShapeDtypeStruct((b, l, SSD_W), BF16), st_shape] if emit_y else [st_shape],
        scratch_shapes=scratch,
        compiler_params=_cparams(("arbitrary", "arbitrary", "arbitrary")),
    )(*[a for a, _ in operands])
    return outs if emit_y else (None, outs[0])


def _mix_kernel(y_ref, u_ref, x_ref, pa_ref, pcnt_ref, pw_ref, psc_ref, wo_ref, g1_ref,
                nw_ref, sh_ref, sc_ref, rw_ref, rb_ref,
                x1_ref, h_ref, rt_ref, tcnt_ref):
    tm = x_ref.shape[1]

    pooled = []
    for g in range(len(POOL_WINDOWS)):
        parts = []
        for r in range(tm // POOL_TM):
            u = u_ref[0, r * POOL_TM:(r + 1) * POOL_TM, g * POOL_GW:(g + 1) * POOL_GW]
            wsum = _dot(pa_ref[g], u)
            parts.append((wsum / pcnt_ref[g] - u.astype(F32)).astype(BF16))
        pooled.append(jnp.concatenate(parts, axis=0))
    mapped = []
    for gp in range(len(POOL_WINDOWS) // 2):
        pair = jnp.concatenate(pooled[2 * gp:2 * gp + 2], axis=1)
        mapped.append((_dot(pair, pw_ref[gp])
                       * psc_ref[:, 2 * gp * POOL_GW:(2 * gp + 2) * POOL_GW]).astype(BF16))
    y_pool = jnp.concatenate(mapped, axis=1)

    mix = _dot(y_ref[0], wo_ref[0:SSD_W, :]) + _dot(y_pool, wo_ref[SSD_W:SSD_W + POOL_W, :])
    x1 = x_ref[0] + g1_ref[0] * mix
    x1_ref[0] = x1

    ms = jnp.mean(x1 * x1, axis=-1, keepdims=True)
    h = x1 * lax.rsqrt(ms + RMS_EPS) * nw_ref[...]
    h = h * (1.0 + sc_ref[0]) + sh_ref[0]
    h_ref[0] = h

    h0, h1 = _split2(h)
    rw2 = jnp.concatenate(_split2(rw_ref[...]), axis=1)
    t0 = _dot(h0, rw2)
    t1 = _dot(h1, rw2)
    logits = (t0[:, 0:LANES] + t0[:, LANES:2 * LANES] + t1[:, 0:LANES] + t1[:, LANES:2 * LANES]
              + rb_ref[...])
    work = logits.T[0:N_EXPERTS, :]
    expert_f = lax.broadcasted_iota(jnp.int32, (N_EXPERTS, tm), 0).astype(F32)
    vals, idxs = [], []
    for _ in range(TOP_K):
        m = jnp.max(work, axis=0, keepdims=True)
        first_idx = jnp.min(jnp.where(work == m, expert_f, float(N_EXPERTS)), axis=0, keepdims=True)
        vals.append(m)
        idxs.append(first_idx)
        work = jnp.where(expert_f == first_idx, 2.0 * NEG_BIG, work)
    exps = [jnp.exp(v - vals[0]) for v in vals]
    denom = exps[0] + exps[1] + exps[2] + exps[3]

    rec_row = lax.broadcasted_iota(jnp.int32, (RT_ROWS, tm), 0)
    rec = jnp.zeros((RT_ROWS, tm), F32)
    onehot = jnp.zeros((N_EXPERTS, tm), F32)
    for k in range(TOP_K):
        rec = jnp.where(rec_row == k, idxs[k], rec)
        rec = jnp.where(rec_row == GATE_LANE + k, exps[k] / denom, rec)
        onehot = onehot + jnp.where(expert_f == idxs[k], 1.0, 0.0)
    rt_ref[0] = rec
    for r in range(tm // DISP_TM):
        tcnt_ref[0, r] = jnp.sum(onehot[:, r * DISP_TM:(r + 1) * DISP_TM], axis=1, keepdims=True)


def _mix(y, u, x, pool_a, pool_cnt, pool_w, pool_scale, w_out, g1, norm_w, shift, scale,
         router_w, router_b):
    b, l, d = x.shape
    tm = MIX_TM
    per_l = l // tm
    tok = lambda n: pl.BlockSpec((1, tm, n), lambda i, j: (i, j, 0))
    per_b = pl.BlockSpec((1, 1, d), lambda i, j: (i, 0, 0))
    full = lambda a: pl.BlockSpec(a.shape, lambda i, j: (0,) * a.ndim)
    return pl.pallas_call(
        _mix_kernel,
        grid=(b, per_l),
        in_specs=[tok(SSD_W), tok(POOL_W), tok(d), full(pool_a), full(pool_cnt), full(pool_w),
                  full(pool_scale), full(w_out), per_b, full(norm_w), per_b, per_b,
                  full(router_w), full(router_b)],
        out_specs=[tok(d), tok(d),
                   pl.BlockSpec((1, RT_ROWS, tm), lambda i, j: (i * per_l + j, 0, 0)),
                   pl.BlockSpec((1, tm // DISP_TM, N_EXPERTS, 1), lambda i, j: (i * per_l + j, 0, 0, 0))],
        out_shape=[jax.ShapeDtypeStruct((b, l, d), F32),
                   jax.ShapeDtypeStruct((b, l, d), F32),
                   jax.ShapeDtypeStruct((b * per_l, RT_ROWS, tm), F32),
                   jax.ShapeDtypeStruct((b * per_l, tm // DISP_TM, N_EXPERTS, 1), F32)],
        compiler_params=_cparams(("arbitrary", "arbitrary")),
    )(y, u, x, pool_a, pool_cnt, pool_w, pool_scale, w_out, g1, norm_w, shift, scale,
      router_w, router_b)


def _group_copies(n_groups, src_ref, src_group, dst_ref, dst_group, sem, wait):
    def pieces(bits):
        for bit in bits:
            @pl.when((n_groups & bit) != 0)
            def _():
                done = n_groups & ~(2 * bit - 1)
                src = src_ref.at[pl.ds(pl.multiple_of((src_group + done) * SUBLANES, SUBLANES),
                                       bit * SUBLANES)]
                dst = dst_ref.at[pl.ds(pl.multiple_of((dst_group + done) * SUBLANES, SUBLANES),
                                       bit * SUBLANES)]
                cp = pltpu.make_async_copy(src, dst, sem)
                cp.wait() if wait else cp.start()

    split = RUN_BITS.index(RUN_SMALL_BIT)
    pl.when(n_groups >= 2 * RUN_SMALL_BIT)(lambda: pieces(RUN_BITS[:split]))
    pieces(RUN_BITS[split:])


def _wait_groups(n_groups, src_ref, dst_ref, sem):
    for bit in WAIT_BITS:
        @pl.when((n_groups & bit) != 0)
        def _():
            rows = bit * SUBLANES
            pltpu.make_async_copy(src_ref.at[pl.ds(0, rows)], dst_ref.at[pl.ds(0, rows)], sem).wait()


def _stage_positions(rt_t, base_col, upper):
    tm = rt_t.shape[1]
    expert_f = lax.broadcasted_iota(jnp.int32, (LANES, tm), 0).astype(F32)
    hots = [expert_f == rt_t[k:k + 1, :] for k in range(TOP_K)]
    onehot = jnp.zeros((LANES, tm), F32)
    for hot in hots:
        onehot = onehot + jnp.where(hot, 1.0, 0.0)
    pos = _dot(onehot.astype(BF16), upper) + base_col
    return [jnp.sum(jnp.where(hot, pos, 0.0), axis=0, keepdims=True) for hot in hots]


def _dispatch_kernel(o_ref, f_ref, fl_ref, rem_ref, hg_ref, tot_ref, used_ref, fs_ref, flen_ref,
                     h_ref, rt_ref, sb_ref, tri_ref, xs_ref,
                     stg2_ref, tails_ref, zero_ref, sem2, fill_sem):
    i = pl.program_id(0)
    tm = h_ref.shape[0]
    n_slots = xs_ref.shape[0]
    base = i * N_EXPERTS
    hi_mask = jnp.int32(-65536)
    slot = i % 2
    stg_ref = stg2_ref.at[slot]
    sem = sem2.at[slot]

    @pl.when(i == 0)
    def _():
        tails_ref[...] = jnp.zeros_like(tails_ref)
        stg2_ref[...] = jnp.zeros_like(stg2_ref)

    rt_t = rt_ref[0]
    pos = _stage_positions(rt_t, sb_ref[0], tri_ref[...])
    gates = [rt_t[GATE_LANE + k:GATE_LANE + k + 1, :] for k in range(TOP_K)]

    hb = h_ref[...].astype(BF16)
    half = PACK_W
    used_rows = used_ref[i] * SUBLANES

    def stage_chunk(r_lo):
        row_f = (lax.broadcasted_iota(jnp.int32, (STAGE_CHUNK, tm), 0) + r_lo).astype(F32)
        sel = jnp.zeros((STAGE_CHUNK, tm), F32)
        gsel = jnp.zeros((STAGE_CHUNK, tm), F32)
        for k in range(TOP_K):
            hit = row_f == pos[k]
            sel = jnp.where(hit, 1.0, sel)
            gsel = jnp.where(hit, gates[k], gsel)
        gate = jnp.sum(gsel, axis=-1, keepdims=True)
        moved = _dot(sel.astype(BF16), hb)
        lo = pltpu.bitcast(moved[:, 0:half], jnp.int32)
        hi = pltpu.bitcast(moved[:, half:2 * half], jnp.int32)
        stg_ref[r_lo:r_lo + STAGE_CHUNK, 0:half] = lax.shift_right_logical(lo, 16) | (hi & hi_mask)
        stg_ref[r_lo:r_lo + STAGE_CHUNK, half:half + LANES] = pltpu.bitcast(
            jnp.broadcast_to(gate, (STAGE_CHUNK, LANES)), jnp.int32)

    always = TOP_K * tm // STAGE_CHUNK
    for rc in range(STAGE_ROWS // STAGE_CHUNK):
        if rc < always:
            stage_chunk(rc * STAGE_CHUNK)
        else:
            pl.when(rc * STAGE_CHUNK < used_rows)(functools.partial(stage_chunk, rc * STAGE_CHUNK))

    def finish_run(e, cr):
        stage_group = o_ref[base + e]
        first = pl.multiple_of(stage_group * SUBLANES, SUBLANES)
        stg_ref[pl.ds(first, SUBLANES), :] = stg_ref[pl.ds(first, SUBLANES), :] | tails_ref[e]
        part = pl.multiple_of((stage_group + f_ref[base + e]) * SUBLANES, SUBLANES)
        tails_ref[e] = jnp.where(rem_ref[base + e] > 0, stg_ref[pl.ds(part, SUBLANES), :], 0)
        _group_copies(fl_ref[base + e], stg_ref, stage_group, xs_ref, hg_ref[base + e], sem, False)
        return cr

    lax.fori_loop(0, N_EXPERTS, finish_run, 0)

    @pl.when(i == pl.num_programs(0) - 1)
    def _():
        zero_ref[...] = jnp.zeros_like(zero_ref)
        tail_group = fs_ref[N_EXPERTS]
        n_tail = (n_slots // SUBLANES - tail_group) // (MOE_BLK // SUBLANES)

        def tail_copy(j, wait):
            off = pl.multiple_of(tail_group * SUBLANES + j * MOE_BLK, MOE_BLK)
            cp = pltpu.make_async_copy(zero_ref, xs_ref.at[pl.ds(off, MOE_BLK)], fill_sem)
            cp.wait() if wait else cp.start()

        for wait in (False, True):
            def pad_body(e, cr):
                _group_copies(flen_ref[e], zero_ref, 0, xs_ref, fs_ref[e], fill_sem, wait)
                return cr
            lax.fori_loop(0, N_EXPERTS, pad_body, 0)
            lax.fori_loop(0, n_tail, lambda j, cr: (tail_copy(j, wait), cr)[1], 0)

    @pl.when(i > 0)
    def _():
        _wait_groups(tot_ref[jnp.maximum(i - 1, 0)], stg2_ref.at[1 - slot], xs_ref, sem2.at[1 - slot])

    @pl.when(i == pl.num_programs(0) - 1)
    def _():
        _wait_groups(tot_ref[i], stg_ref, xs_ref, sem)


def _dispatch(tables, fill_groups, fill_len_groups, h, rt, stage_base, n_slots):
    t, d = h.shape
    tm = DISP_TM
    rt_per = rt.shape[2] // tm
    grid_spec = pltpu.PrefetchScalarGridSpec(
        num_scalar_prefetch=len(tables) + 2,
        grid=(t // tm,),
        in_specs=[pl.BlockSpec((tm, d), lambda i, *_: (i, 0)),
                  pl.BlockSpec((1, RT_ROWS, tm), lambda i, *_: (i // rt_per, 0, i % rt_per)),
                  pl.BlockSpec((1, LANES, 1), lambda i, *_: (i, 0, 0)),
                  pl.BlockSpec((tm, tm), lambda i, *_: (0, 0))],
        out_specs=pl.BlockSpec(memory_space=pl.ANY),
        scratch_shapes=[pltpu.VMEM((2, STAGE_ROWS, XS_W), jnp.int32),
                        pltpu.VMEM((N_EXPERTS, SUBLANES, XS_W), jnp.int32),
                        pltpu.VMEM((MOE_BLK, XS_W), jnp.int32),
                        pltpu.SemaphoreType.DMA((2,)), pltpu.SemaphoreType.DMA(())],
    )
    return pl.pallas_call(
        _dispatch_kernel,
        grid_spec=grid_spec,
        out_shape=jax.ShapeDtypeStruct((n_slots, XS_W), jnp.int32),
        compiler_params=_cparams(("arbitrary",)),
    )(*tables, fill_groups, fill_len_groups, h, rt, stage_base, _strict_upper(tm))


def _expert_kernel(be_ref, nused_ref, xs_ref, w1_ref, b1_ref, w2_ref, b2_ref, perm_ref, y_ref,
                   w1s_ref, w2s_ref):
    i = pl.program_id(0)
    prev = be_ref[jnp.maximum(i - 1, 0)]
    changed = (i == 0) | (be_ref[i] != prev)
    dff2 = w1_ref.shape[2]
    tile = 2 * LANES

    @pl.when(changed & (i < nused_ref[0]))
    def _():
        for j in range(dff2 // tile):
            wj = w1_ref[0, :, j * tile:(j + 1) * tile].astype(BF16)
            w1s_ref[:, j * tile:(j + 1) * tile] = _dot(wj, perm_ref[...]).astype(BF16)
        w2s_ref[...] = w2_ref[0].astype(BF16)

    @pl.when(i < nused_ref[0])
    def _():
        hi_mask = jnp.int32(-65536)
        words = xs_ref[:, 0:PACK_W]
        x = jnp.concatenate(
            [pltpu.bitcast(lax.shift_left(words, 16), F32).astype(BF16),
             pltpu.bitcast(words & hi_mask, F32).astype(BF16)], axis=1)
        gate = pltpu.bitcast(xs_ref[:, PACK_W:PACK_W + 1], F32)
        acts = []
        for j in range(dff2 // tile):
            hb = _dot(x, w1s_ref[:, j * tile:(j + 1) * tile]) + b1_ref[0, :, j * tile:(j + 1) * tile]
            gp = jnp.minimum(hb[:, 0:LANES], SWIGLU_LIMIT)
            up = jnp.clip(hb[:, LANES:tile], -SWIGLU_LIMIT, SWIGLU_LIMIT)
            acts.append((gp * _sigmoid(SWIGLU_ALPHA * gp) * (up + 1.0)).astype(BF16))
        act = jnp.concatenate(acts, axis=1)
        y = (_dot(act, w2s_ref[...]) + b2_ref[0]) * gate
        lo = pltpu.bitcast(y[:, 0:PACK_W].astype(BF16).astype(F32), jnp.int32)
        hi = pltpu.bitcast(y[:, PACK_W:2 * PACK_W].astype(BF16).astype(F32), jnp.int32)
        y_ref[...] = lax.shift_right_logical(lo, 16) | (hi & hi_mask)

    @pl.when(i >= nused_ref[0])
    def _():
        y_ref[...] = jnp.zeros_like(y_ref)


def _experts(block_exp, n_used, xs, w1, b1p, w2, b2, perm, n_blocks):
    blk = MOE_BLK
    d = w1.shape[1]
    dff2 = w1.shape[2]
    dff = w2.shape[1]

    def x_idx(i, be, nu):
        return (jnp.minimum(i, nu[0] - 1), 0)

    grid_spec = pltpu.PrefetchScalarGridSpec(
        num_scalar_prefetch=2,
        grid=(n_blocks,),
        in_specs=[pl.BlockSpec((blk, XS_W), x_idx),
                  pl.BlockSpec((1, d, dff2), lambda i, be, nu: (be[i], 0, 0)),
                  pl.BlockSpec((1, 1, dff2), lambda i, be, nu: (be[i], 0, 0)),
                  pl.BlockSpec((1, dff, d), lambda i, be, nu: (be[i], 0, 0)),
                  pl.BlockSpec((1, 1, d), lambda i, be, nu: (be[i], 0, 0)),
                  pl.BlockSpec(perm.shape, lambda i, be, nu: (0, 0))],
        out_specs=pl.BlockSpec((blk, PACK_W), lambda i, be, nu: (i, 0)),
        scratch_shapes=[pltpu.VMEM((d, dff2), BF16), pltpu.VMEM((dff, d), BF16)],
    )
    return pl.pallas_call(
        _expert_kernel,
        grid_spec=grid_spec,
        out_shape=jax.ShapeDtypeStruct((n_blocks * blk, PACK_W), jnp.int32),
        compiler_params=_cparams(("arbitrary",)),
    )(block_exp, n_used, xs, w1, b1p, w2, b2, perm)


def _combine_kernel(o_ref, g_ref, hg_ref, tot_ref, x1_ref, rt_ref, sb_ref, tri_ref, g2_ref, fnw_ref,
                    ys_ref, out_ref, stg_ref, sem):
    tm = x1_ref.shape[1]
    step = pl.program_id(0) * pl.num_programs(1) + pl.program_id(1)
    n_steps = pl.num_programs(0) * pl.num_programs(1)
    slot = step % 2
    hi_mask = jnp.int32(-65536)

    def fetch(tile, into):
        def body(e, cr):
            _group_copies(g_ref[tile * N_EXPERTS + e], ys_ref, hg_ref[tile * N_EXPERTS + e],
                          stg_ref.at[into], o_ref[tile * N_EXPERTS + e], sem.at[into], False)
            return cr
        lax.fori_loop(0, N_EXPERTS, body, 0)

    @pl.when(step == 0)
    def _():
        stg_ref[...] = jnp.zeros_like(stg_ref)
        fetch(0, 0)

    @pl.when(step + 1 < n_steps)
    def _():
        fetch(step + 1, 1 - slot)

    _wait_groups(tot_ref[step], ys_ref, stg_ref.at[slot], sem.at[slot])

    pos_rows = _stage_positions(rt_ref[0], sb_ref[0], tri_ref[...])
    expert_i = lax.broadcasted_iota(jnp.int32, (LANES, tm), 0)
    stacked = jnp.zeros((LANES, tm), F32)
    for k in range(TOP_K):
        stacked = jnp.where(expert_i == k, pos_rows[k], stacked)
    pos_cols = stacked.T
    col_f = lax.broadcasted_iota(jnp.int32, (tm, STAGE_ROWS), 1).astype(F32)
    sel = jnp.zeros((tm, STAGE_ROWS), F32)
    for k in range(TOP_K):
        sel = jnp.where(col_f == pos_cols[:, k:k + 1], 1.0, sel)
    sel = sel.astype(BF16)
    words = stg_ref[slot]
    lo = pltpu.bitcast(lax.shift_left(words, 16), F32).astype(BF16)
    hi = pltpu.bitcast(words & hi_mask, F32).astype(BF16)
    moe = jnp.concatenate([_dot(sel, lo), _dot(sel, hi)], axis=1)
    x2 = x1_ref[0] + g2_ref[0] * moe
    ms = jnp.mean(x2 * x2, axis=-1, keepdims=True)
    out_ref[0] = x2 * lax.rsqrt(ms + RMS_EPS) * fnw_ref[...]


def _combine(tables, x1, rt, stage_base, g2, fnw, ys):
    b, l, d = x1.shape
    tm = DISP_TM
    per_l = l // tm
    rt_per = rt.shape[2] // tm
    tok = lambda n: pl.BlockSpec((1, tm, n), lambda i, j, *_: (i, j, 0))
    grid_spec = pltpu.PrefetchScalarGridSpec(
        num_scalar_prefetch=len(tables),
        grid=(b, per_l),
        in_specs=[tok(d),
                  pl.BlockSpec((1, RT_ROWS, tm),
                               lambda i, j, *_: ((i * per_l + j) // rt_per, 0, (i * per_l + j) % rt_per)),
                  pl.BlockSpec((1, LANES, 1), lambda i, j, *_: (i * per_l + j, 0, 0)),
                  pl.BlockSpec((tm, tm), lambda i, j, *_: (0, 0)),
                  pl.BlockSpec((1, 1, d), lambda i, j, *_: (i, 0, 0)),
                  pl.BlockSpec(fnw.shape, lambda i, j, *_: (0, 0)),
                  pl.BlockSpec(memory_space=pl.ANY)],
        out_specs=tok(d),
        scratch_shapes=[pltpu.VMEM((2, STAGE_ROWS, PACK_W), jnp.int32),
                        pltpu.SemaphoreType.DMA((2,))],
    )
    return pl.pallas_call(
        _combine_kernel,
        grid_spec=grid_spec,
        out_shape=jax.ShapeDtypeStruct((b, l, d), F32),
        compiler_params=_cparams(("arbitrary", "arbitrary")),
    )(*tables, x1, rt, stage_base, _strict_upper(tm), g2, fnw, ys)


def _pool_constants():
    i = np.arange(POOL_TM)[:, None]
    j = np.arange(POOL_TM)[None, :]
    same_row = (i // GRID_W) == (j // GRID_W)
    mats, cnts = [], []
    for w in POOL_WINDOWS:
        band = same_row & (j - i >= -(w // 2)) & (j - i < w - w // 2)
        mats.append(band)
        cnts.append(np.broadcast_to(band.sum(axis=1, keepdims=True), (POOL_TM, POOL_GW)))
    return (jnp.asarray(np.stack(mats), BF16), jnp.asarray(np.stack(cnts), F32))


def _head_expand():
    e = np.zeros((LANES, SSD_W), np.float32)
    for h in range(SSD_HEADS):
        e[h, h * SSD_HEADDIM:(h + 1) * SSD_HEADDIM] = 1.0
    return jnp.asarray(e, BF16)


def _deinterleave_perm():
    n = 2 * LANES
    p = np.zeros((n, n), np.float32)
    for k in range(LANES):
        p[2 * k, k] = 1.0
        p[2 * k + 1, LANES + k] = 1.0
    return jnp.asarray(p, BF16)


def _strict_upper(n):
    return jnp.asarray(np.triu(np.ones((n, n), np.float32), 1), BF16)


def _pad_lanes(a, n):
    return jnp.pad(a, [(0, 0)] * (a.ndim - 1) + [(0, n - a.shape[-1])])


def kernel(x, c, ctx, c_ctx, w_mod, b_mod, norm1_w, norm2_w, w_in, conv_w, conv_b, dt_bias, a_log,
           d_skip, ssd_norm_w, pool_w, pool_scale, w_out, router_w, router_b, w1, b1, w2, b2,
           final_norm_w):
    depth = w_mod.shape[0]
    assert depth == 1, "single-layer problem"
    b, l, d = x.shape
    lc = ctx.shape[1]
    xbcdt = CONV_CH + 2 * SSD_HEADS

    mod_rows = 2 * SUBLANES
    cc = jnp.zeros((mod_rows, d), F32).at[0:b].set(c).at[b].set(c_ctx)
    mod = _modulation(cc, w_mod[0], b_mod[0])
    sh1, sc1, g1, sh2, sc2, g2 = [m.reshape(b, 1, d) for m in jnp.split(mod[0:b], 6, axis=-1)]
    csh1, csc1 = [jnp.broadcast_to(m.reshape(1, 1, d), (b, 1, d))
                  for m in jnp.split(mod[b:b + 1], 6, axis=-1)[0:2]]

    wi = w_in[0]
    wx = wi[:, 0:CONV_CH].astype(BF16)
    wd = jnp.concatenate([_pad_lanes(wi[:, CONV_CH:CONV_CH + SSD_HEADS], LANES),
                          _pad_lanes(wi[:, CONV_CH + SSD_HEADS:xbcdt], LANES)], axis=1).astype(BF16)
    wz = wi[:, xbcdt:xbcdt + SSD_W].astype(BF16)
    wp = wi[:, xbcdt + SSD_W:].astype(BF16)
    n1 = norm1_w[0].reshape(1, d)
    conv_w8 = jnp.pad(conv_w[0], ((0, SUBLANES - SSD_CONV), (0, 0)))
    conv_b1 = conv_b[0].reshape(1, CONV_CH)
    dtb = _pad_lanes(dt_bias[0], LANES).reshape(2, 1, LANES)
    alog = _pad_lanes(a_log[0], LANES).reshape(2, 1, LANES)
    dsk = jnp.repeat(d_skip[0], SSD_HEADDIM).reshape(1, SSD_W)
    snw = ssd_norm_w[0].reshape(1, SSD_W)
    expand = _head_expand()

    xbc_c, dt_c = _inproj(ctx, n1, csh1, csc1, (wx, wd), (BF16, F32), min(PROJ_TM, lc))
    zero_state = jnp.zeros((b, 2, SSD_GROUPS, SSD_STATE, GROUP_W), F32)
    _, ctx_states = _ssd(xbc_c, dt_c, None, conv_w8, conv_b1, dtb, alog, dsk, snw, expand, zero_state)

    xbc, dt, z, u_pool = _inproj(x, n1, sh1, sc1, (wx, wd, wz, wp), (BF16, F32, BF16, BF16), PROJ_TM)
    y_ssd, _ = _ssd(xbc, dt, z, conv_w8, conv_b1, dtb, alog, dsk, snw, expand, ctx_states)

    pool_a, pool_cnt = _pool_constants()
    pw = pool_w[0].astype(BF16)
    zero_gw = jnp.zeros((POOL_GW, POOL_GW), BF16)
    pw_pairs = jnp.stack([jnp.block([[pw[2 * gp], zero_gw], [zero_gw, pw[2 * gp + 1]]])
                          for gp in range(len(POOL_WINDOWS) // 2)])
    x1, h2, rt, tcnt = _mix(
        y_ssd, u_pool, x, pool_a, pool_cnt, pw_pairs, pool_scale[0].reshape(1, POOL_W),
        w_out[0].astype(BF16), g1, norm2_w[0].reshape(1, d), sh2, sc2,
        _pad_lanes(router_w[0], LANES), _pad_lanes(router_b[0].reshape(1, N_EXPERTS), LANES))

    i32 = jnp.int32
    t = b * l
    n_tiles = t // DISP_TM
    tc = tcnt.reshape(n_tiles, N_EXPERTS).astype(i32)
    counts = jnp.sum(tc, axis=0)
    run_start = jnp.cumsum(tc, axis=0) - tc
    padded = (counts + MOE_BLK - 1) // MOE_BLK * MOE_BLK
    pad_end = jnp.cumsum(padded)
    pad_start = pad_end - padded
    carried = run_start % SUBLANES
    span = carried + tc
    groups = (span + SUBLANES - 1) // SUBLANES
    full = span // SUBLANES
    stage_group = jnp.cumsum(groups, axis=1) - groups
    slot_group = (pad_start[None, :] + run_start - carried) // SUBLANES
    flush = full.at[n_tiles - 1].set(groups[n_tiles - 1])
    flat = lambda a: a.reshape(n_tiles * N_EXPERTS).astype(i32)
    stage_base = _pad_lanes((stage_group * SUBLANES + carried).astype(F32), LANES)
    stage_base = stage_base.reshape(n_tiles, LANES, 1)
    n_blocks = (t * TOP_K) // MOE_BLK + N_EXPERTS
    n_used = (pad_end[-1] // MOE_BLK).astype(i32).reshape(1)
    blk_start = jnp.minimum(jnp.arange(n_blocks, dtype=i32), n_used[0] - 1) * MOE_BLK
    block_exp = jnp.minimum(jnp.sum(blk_start[:, None] >= pad_end[None, :], axis=1),
                            N_EXPERTS - 1).astype(i32)
    written = (counts + SUBLANES - 1) // SUBLANES * SUBLANES
    fill_group = (jnp.concatenate([pad_start + written, pad_end[-1:]]) // SUBLANES).astype(i32)
    fill_groups = (jnp.concatenate([padded - written, jnp.zeros((1,), i32)]) // SUBLANES).astype(i32)

    xs = _dispatch((flat(stage_group), flat(full), flat(flush), flat(span % SUBLANES), flat(slot_group),
                    jnp.sum(flush, axis=1).astype(i32), jnp.sum(groups, axis=1).astype(i32)),
                   fill_group, fill_groups, h2.reshape(t, d), rt, stage_base,
                   n_blocks * MOE_BLK)
    dff2 = w1.shape[-1]
    b1p = jnp.concatenate(
        [b1[0].reshape(N_EXPERTS, dff2 // (2 * LANES), LANES, 2)[..., 0],
         b1[0].reshape(N_EXPERTS, dff2 // (2 * LANES), LANES, 2)[..., 1]], axis=-1
    ).reshape(N_EXPERTS, 1, dff2)
    ys = _experts(block_exp, n_used, xs, w1[0], b1p, w2[0], b2[0].reshape(N_EXPERTS, 1, d),
                  _deinterleave_perm(), n_blocks)
    return _combine((flat(stage_group), flat(groups), flat(slot_group),
                     jnp.sum(groups, axis=1).astype(i32)), x1, rt, stage_base, g2,
                    final_norm_w.reshape(1, d), ys)
```

```python
import functools

import numpy as np
import jax
import jax.numpy as jnp
from jax import lax
from jax.experimental import pallas as pl
from jax.experimental.pallas import tpu as pltpu

F32 = jnp.float32
BF16 = jnp.bfloat16

SSD_HEADDIM = 64
SSD_GROUPS = 4
SSD_HPG = 6
SSD_HEADS = SSD_GROUPS * SSD_HPG
SSD_STATE = 128
SSD_CONV = 5
SSD_CHUNK = 128
SSD_W = SSD_HEADS * SSD_HEADDIM
GROUP_W = SSD_HPG * SSD_HEADDIM
CONV_CH = SSD_W + 2 * SSD_GROUPS * SSD_STATE
POOL_WINDOWS = (2, 4, 8, 16)
POOL_GW = 128
POOL_W = POOL_GW * len(POOL_WINDOWS)
GRID_W = 64
N_EXPERTS = 32
TOP_K = 4
SWIGLU_ALPHA = 1.702
SWIGLU_LIMIT = 7.0
RMS_EPS = 1e-6
LOG2_E = 1.4426950408889634

LANES = 128
SUBLANES = 8
VMEM_LIMIT_BYTES = 56 * 1024 * 1024

MOD_TN = 1024
PROJ_TM = 512
MIX_TM = 512
POOL_TM = 256
MOE_BLK = 512
DISP_TM = 256
NEG_BIG = -1e30

GATE_LANE = 2 * TOP_K
RT_ROWS = 2 * SUBLANES
PACK_W = 512
XS_W = PACK_W + LANES
STAGE_CHUNK = 256
_STAGE_NEED = TOP_K * DISP_TM + N_EXPERTS * 2 * (SUBLANES - 1) + SUBLANES
STAGE_ROWS = -(-_STAGE_NEED // STAGE_CHUNK) * STAGE_CHUNK
RUN_BITS = (32, 16, 8, 4, 2, 1)
RUN_SMALL_BIT = 4
WAIT_BITS = (128, 64, 32, 16, 8, 4, 2, 1)


def _sigmoid(x):
    return 0.5 * jnp.tanh(0.5 * x) + 0.5


def _split2(a):
    hi = a.astype(BF16)
    mid = (a - hi.astype(F32)).astype(BF16)
    return hi, mid


def _split3(a):
    hi = a.astype(BF16)
    r = a - hi.astype(F32)
    mid = r.astype(BF16)
    lo = (r - mid.astype(F32)).astype(BF16)
    return hi, mid, lo


def _dot(a, b):
    return jnp.dot(a, b, preferred_element_type=F32)


def _dot_exact_rhs(a_f32, b_bf16, parts):
    pieces = _split3(a_f32) if parts == 3 else _split2(a_f32)
    out = _dot(pieces[0], b_bf16)
    for p in pieces[1:]:
        out = out + _dot(p, b_bf16)
    return out


def _dot_hi(a_f32, b_f32):
    a0, a1, a2 = _split3(a_f32)
    b0, b1, b2 = _split3(b_f32)
    out = _dot(a0, b0)
    out = out + _dot(a0, b1) + _dot(a1, b0)
    out = out + _dot(a1, b1) + _dot(a0, b2) + _dot(a2, b0)
    return out


def _cparams(sem):
    return pltpu.CompilerParams(dimension_semantics=sem, vmem_limit_bytes=VMEM_LIMIT_BYTES)


def _mod_kernel(c_ref, w_ref, b_ref, o_ref):
    c = c_ref[...]
    s = c * _sigmoid(c)
    o_ref[...] = _dot_hi(s, w_ref[...]) + b_ref[...]


def _modulation(cc, w_mod, b_mod):
    rows, d = cc.shape
    n = w_mod.shape[1]
    return pl.pallas_call(
        _mod_kernel,
        grid=(n // MOD_TN,),
        in_specs=[pl.BlockSpec((rows, d), lambda j: (0, 0)),
                  pl.BlockSpec((d, MOD_TN), lambda j: (0, j)),
                  pl.BlockSpec((1, MOD_TN), lambda j: (0, j))],
        out_specs=pl.BlockSpec((rows, MOD_TN), lambda j: (0, j)),
        out_shape=jax.ShapeDtypeStruct((rows, n), F32),
        compiler_params=_cparams(("arbitrary",)),
    )(cc, w_mod, b_mod.reshape(1, n))


def _inproj_kernel(n_groups, x0_ref, xn_ref, nw_ref, sh0_ref, sc0_ref, shn_ref, scn_ref, *refs):
    w_refs = refs[0:n_groups]
    out_refs = refs[n_groups:2 * n_groups]
    hb_even_ref, hb_odd_ref = refs[2 * n_groups:]
    step = pl.program_id(0) * pl.num_programs(1) + pl.program_id(1)

    def normed(x, sh, sc):
        ms = jnp.mean(x * x, axis=-1, keepdims=True)
        h = x * lax.rsqrt(ms + RMS_EPS) * nw_ref[...]
        return (h * (1.0 + sc) + sh).astype(BF16)

    @pl.when(step == 0)
    def _():
        hb_even_ref[...] = normed(x0_ref[0], sh0_ref[0], sc0_ref[0])

    def project(cur_ref, nxt_ref):
        hb = cur_ref[...]
        for w_ref, out_ref in zip(w_refs, out_refs):
            out_ref[0] = _dot(hb, w_ref[...]).astype(out_ref.dtype)
        nxt_ref[...] = normed(xn_ref[0], shn_ref[0], scn_ref[0])

    pl.when(step % 2 == 0)(functools.partial(project, hb_even_ref, hb_odd_ref))
    pl.when(step % 2 == 1)(functools.partial(project, hb_odd_ref, hb_even_ref))


def _inproj(x, norm_w, shift, scale, weights, out_dtypes, tm):
    b, l, d = x.shape
    per_l = l // tm
    last = b * per_l - 1

    def nxt(i, j):
        s = jnp.minimum(i * per_l + j + 1, last)
        return s // per_l, s % per_l

    full = lambda a: pl.BlockSpec(a.shape, lambda i, j: (0, 0))
    tok = lambda n: pl.BlockSpec((1, tm, n), lambda i, j: (i, j, 0))
    first_b = pl.BlockSpec((1, 1, d), lambda i, j: (0, 0, 0))
    next_b = pl.BlockSpec((1, 1, d), lambda i, j: (nxt(i, j)[0], 0, 0))
    return pl.pallas_call(
        functools.partial(_inproj_kernel, len(weights)),
        grid=(b, per_l),
        in_specs=[pl.BlockSpec((1, tm, d), lambda i, j: (0, 0, 0)),
                  pl.BlockSpec((1, tm, d), lambda i, j: (*nxt(i, j), 0)),
                  full(norm_w), first_b, first_b, next_b, next_b] + [full(w) for w in weights],
        out_specs=[tok(w.shape[1]) for w in weights],
        out_shape=[jax.ShapeDtypeStruct((b, l, w.shape[1]), dt)
                   for w, dt in zip(weights, out_dtypes)],
        scratch_shapes=[pltpu.VMEM((tm, d), BF16), pltpu.VMEM((tm, d), BF16)],
        compiler_params=_cparams(("arbitrary", "arbitrary")),
    )(x, x, norm_w, shift, scale, shift, scale, *weights)


def _ssd_kernel(nc, emit_y, *refs):
    if emit_y:
        (xbc_ref, prev_ref, next_ref, dt_ref, z_ref, cw_ref, cb_ref, dtb_ref, alog_ref, dsk_ref,
         nw_ref, exp_ref, init_ref, y_ref, fin_ref, act_ref, ybuf_ref, yf_ref, st_ref) = refs
        y_part = (z_ref, dsk_ref, nw_ref, y_ref, ybuf_ref, yf_ref)
    else:
        (xbc_ref, prev_ref, next_ref, dt_ref, cw_ref, cb_ref, dtb_ref, alog_ref,
         exp_ref, init_ref, fin_ref, act_ref, st_ref) = refs
        y_part = None
    ph = pl.program_id(1)
    c = pl.program_id(2)
    ci = c + ph * (nc - 1 - 2 * c)

    @pl.when(c == 0)
    def _():
        st_ref[...] = init_ref[0, ph]

    @pl.when(ph == 0)
    def _():
        _conv_silu(nc, ci, xbc_ref, prev_ref, next_ref, cw_ref, cb_ref, act_ref)

    _ssd_scan(nc, ph, c, ci, dt_ref, dtb_ref, alog_ref, exp_ref, act_ref, st_ref, fin_ref, y_part)


def _conv_silu(nc, ci, xbc_ref, prev_ref, next_ref, cw_ref, cb_ref, act_ref):
    ch = SSD_CHUNK
    has_prev = ci > 0
    has_next = ci < nc - 1
    row8 = lax.broadcasted_iota(jnp.int32, (SUBLANES, LANES), 0)
    half = SSD_CONV // 2
    for j in range(CONV_CH // LANES):
        cols = slice(j * LANES, (j + 1) * LANES)
        n_t = ch // SUBLANES
        tiles = [jnp.where(has_prev, prev_ref[0, SUBLANES:2 * SUBLANES, cols].astype(F32), 0.0)]
        for i in range(n_t // 2):
            both = xbc_ref[0, 2 * i * SUBLANES:2 * (i + 1) * SUBLANES, cols].astype(F32)
            tiles += [both[0:SUBLANES], both[SUBLANES:2 * SUBLANES]]
        tiles.append(jnp.where(has_next, next_ref[0, 0:SUBLANES, cols].astype(F32), 0.0))
        acc = [cb_ref[:, cols] + cw_ref[half:half + 1, cols] * tiles[i + 1] for i in range(n_t)]
        for s in range(1, half + 1):
            rot = [pltpu.roll(tl, s, axis=0) for tl in tiles[0:n_t + 1]]
            wk = cw_ref[half - s:half - s + 1, cols]
            for i in range(n_t):
                acc[i] = acc[i] + wk * jnp.where(row8 < s, rot[i], rot[i + 1])
            rot = [pltpu.roll(tl, SUBLANES - s, axis=0) for tl in tiles[1:n_t + 2]]
            wk = cw_ref[half + s:half + s + 1, cols]
            for i in range(n_t):
                acc[i] = acc[i] + wk * jnp.where(row8 >= SUBLANES - s, rot[i + 1], rot[i])
        silu = [0.5 * v * jnp.tanh(0.5 * v) + 0.5 * v for v in acc]
        act_ref[ci, :, cols] = jnp.concatenate(silu, axis=0).astype(BF16)


def _ssd_scan(nc, ph, c, ci, dt_ref, dtb_ref, alog_ref, exp_ref, act_ref, st_ref, fin_ref, y_part):
    ch = SSD_CHUNK
    emit_y = y_part is not None
    dtr = dt_ref[0] + dtb_ref[0]
    small = jnp.exp(-jnp.abs(dtr))
    one_plus = 1.0 + small
    log1p_small = jnp.where(one_plus == 1.0, small, jnp.log(one_plus) * (small / (one_plus - 1.0)))
    dtv = jnp.maximum(dtr, 0.0) + log1p_small
    a = dtv * (-jnp.exp(alog_ref[0]) * LOG2_E)
    row = lax.broadcasted_iota(jnp.int32, (ch, ch), 0)
    col = lax.broadcasted_iota(jnp.int32, (ch, ch), 1)
    tmask = (row - col) * (1 - 2 * ph) >= 0
    tri = jnp.where(tmask, 1.0, 0.0).astype(BF16)
    cs = _dot_exact_rhs_left(tri, a)
    tot = jnp.where(ph == 0, cs[ch - 1:ch, :], cs[0:1, :])
    cs_t = cs.T
    e_cs = jnp.exp2(cs)
    e_dec = jnp.exp2(tot - cs)
    e_tot = jnp.exp2(tot)
    expand = exp_ref[...]
    wdec_x = _dot((dtv * e_dec).astype(BF16), expand).astype(BF16)
    etot_x = _dot_exact_rhs(jnp.broadcast_to(e_tot, (SUBLANES, LANES)), expand, 2)[0:1]
    if emit_y:
        z_ref, dsk_ref, nw_ref, y_ref, ybuf_ref, yf_ref = y_part
        ecs_x = _dot(e_cs.astype(BF16), expand)
        src_t = cs_t - jnp.log2(dtv).T

    lane = lax.broadcasted_iota(jnp.int32, (ch, LANES), 1)
    for g in range(SSD_GROUPS):
        gs = slice(g * GROUP_W, (g + 1) * GROUP_W)
        b_bf = act_ref[ci, :, SSD_W + g * SSD_STATE:SSD_W + (g + 1) * SSD_STATE]
        x_bf = act_ref[ci, :, gs]
        s_prev = st_ref[g]
        if emit_y:
            c_bf = act_ref[ci, :, SSD_W + (SSD_GROUPS + g) * SSD_STATE:
                           SSD_W + (SSD_GROUPS + g + 1) * SSD_STATE]
            cb = lax.dot_general(c_bf, b_bf, (((1,), (1,)), ((), ())), preferred_element_type=F32)
            y_off = _dot(c_bf, s_prev.astype(BF16)) * ecs_x[:, gs]
        x_dec = x_bf * wdec_x[:, gs]
        st_ref[g] = s_prev * etot_x[:, gs] + lax.dot_general(
            b_bf, x_dec, (((0,), (0,)), ((), ())), preferred_element_type=F32)
        if not emit_y:
            continue
        for q in range(SSD_HPG // 2):
            lmats = []
            for h in (g * SSD_HPG + 2 * q, g * SSD_HPG + 2 * q + 1):
                diff = cs[:, h:h + 1] - src_t[h:h + 1, :]
                dec = jnp.exp2(jnp.where(tmask, diff, NEG_BIG))
                lmats.append((dec * cb).astype(BF16))
            xp = x_bf[:, q * LANES:(q + 1) * LANES]
            zero = jnp.zeros_like(xp)
            rhs = jnp.concatenate([jnp.where(lane < SSD_HEADDIM, xp, zero),
                                   jnp.where(lane >= SSD_HEADDIM, xp, zero)], axis=0)
            y_diag = _dot(jnp.concatenate(lmats, axis=1), rhs)
            ps = slice(g * GROUP_W + q * LANES, g * GROUP_W + (q + 1) * LANES)
            ybuf_ref[:, ps] = y_diag + y_off[:, q * LANES:(q + 1) * LANES]

    if emit_y:
        @pl.when(ph == 0)
        def _():
            yf_ref[ci] = ybuf_ref[...].astype(BF16)

        @pl.when(ph == 1)
        def _():
            yt = (yf_ref[ci].astype(F32) + ybuf_ref[...]
                  + act_ref[ci, :, 0:SSD_W].astype(F32) * dsk_ref[...])
            hz = 0.5 * z_ref[0].astype(F32)
            gt = yt * (hz * jnp.tanh(hz) + hz)
            ms = jnp.mean(gt * gt, axis=-1, keepdims=True)
            y_ref[0] = (gt * lax.rsqrt(ms + RMS_EPS) * nw_ref[...]).astype(y_ref.dtype)

    @pl.when(c == nc - 1)
    def _():
        fin_ref[0, ph] = st_ref[...]


def _dot_exact_rhs_left(sel_bf16, a_f32):
    hi, mid, lo = _split3(a_f32)
    return _dot(sel_bf16, hi) + _dot(sel_bf16, mid) + _dot(sel_bf16, lo)


def _ssd(xbc, dt, z, conv_w8, conv_b, dt_bias, a_log, d_skip_x, norm_w, expand, init):
    emit_y = z is not None
    b, l, _ = xbc.shape
    ch = SSD_CHUNK
    nc = l // ch
    halo = 2 * SUBLANES
    per_ch = ch // halo

    def cidx(ph, c):
        return c + ph * (nc - 1 - 2 * c)

    def xidx(ph, c):
        return jnp.where(ph == 0, c, nc - 1)

    def out_idx(ph, c):
        return jnp.where(ph == 0, nc - 1, nc - 1 - c)

    full2 = lambda a: pl.BlockSpec(a.shape, lambda i, ph, c: (0, 0))
    st_spec = pl.BlockSpec((1, 2, SSD_GROUPS, SSD_STATE, GROUP_W), lambda i, ph, c: (i, 0, 0, 0, 0))
    per_phase = pl.BlockSpec((1, 1, LANES), lambda i, ph, c: (ph, 0, 0))
    y_spec = pl.BlockSpec((1, ch, SSD_W), lambda i, ph, c: (i, out_idx(ph, c), 0))
    operands = [(xbc, pl.BlockSpec((1, ch, CONV_CH), lambda i, ph, c: (i, xidx(ph, c), 0))),
                (xbc, pl.BlockSpec((1, halo, CONV_CH),
                                   lambda i, ph, c: (i, jnp.maximum(xidx(ph, c) * per_ch - 1, 0), 0))),
                (xbc, pl.BlockSpec((1, halo, CONV_CH),
                                   lambda i, ph, c: (i, jnp.minimum((xidx(ph, c) + 1) * per_ch,
                                                                    l // halo - 1), 0))),
                (dt, pl.BlockSpec((1, ch, LANES), lambda i, ph, c: (i, cidx(ph, c), ph)))]
    if emit_y:
        operands.append((z, y_spec))
    operands += [(conv_w8, full2(conv_w8)), (conv_b, full2(conv_b)),
                 (dt_bias, per_phase), (a_log, per_phase)]
    if emit_y:
        operands += [(d_skip_x, full2(d_skip_x)), (norm_w, full2(norm_w))]
    operands += [(expand, full2(expand)), (init, st_spec)]
    st_shape = jax.ShapeDtypeStruct((b, 2, SSD_GROUPS, SSD_STATE, GROUP_W), F32)
    scratch = [pltpu.VMEM((nc, ch, CONV_CH), BF16)]
    if emit_y:
        scratch += [pltpu.VMEM((ch, SSD_W), F32),
                    pltpu.VMEM((nc, ch, SSD_W), BF16)]
    scratch.append(pltpu.VMEM((SSD_GROUPS, SSD_STATE, GROUP_W), F32))
    outs = pl.pallas_call(
        functools.partial(_ssd_kernel, nc, emit_y),
        grid=(b, 2, nc),
        in_specs=[spec for _, spec in operands],
        out_specs=[y_spec, st_spec] if emit_y else [st_spec],
        out_shape=[jax.ShapeDtypeStruct((b, l, SSD_W), BF16), st_shape] if emit_y else [st_shape],
        scratch_shapes=scratch,
        compiler_params=_cparams(("arbitrary", "arbitrary", "arbitrary")),
    )(*[a for a, _ in operands])
    return outs if emit_y else (None, outs[0])


def _mix_kernel(y_ref, u_ref, x_ref, pa_ref, pcnt_ref, pw_ref, psc_ref, wo_ref, g1_ref,
                nw_ref, sh_ref, sc_ref, rw_ref, rb_ref,
                x1_ref, h_ref, rt_ref, tcnt_ref):
    tm = x_ref.shape[1]

    pooled = []
    for g in range(len(POOL_WINDOWS)):
        parts = []
        for r in range(tm // POOL_TM):
            u = u_ref[0, r * POOL_TM:(r + 1) * POOL_TM, g * POOL_GW:(g + 1) * POOL_GW]
            wsum = _dot(pa_ref[g], u)
            parts.append((wsum / pcnt_ref[g] - u.astype(F32)).astype(BF16))
        pooled.append(jnp.concatenate(parts, axis=0))
    mapped = []
    for gp in range(len(POOL_WINDOWS) // 2):
        pair = jnp.concatenate(pooled[2 * gp:2 * gp + 2], axis=1)
        mapped.append((_dot(pair, pw_ref[gp])
                       * psc_ref[:, 2 * gp * POOL_GW:(2 * gp + 2) * POOL_GW]).astype(BF16))
    y_pool = jnp.concatenate(mapped, axis=1)

    mix = _dot(y_ref[0], wo_ref[0:SSD_W, :]) + _dot(y_pool, wo_ref[SSD_W:SSD_W + POOL_W, :])
    x1 = x_ref[0] + g1_ref[0] * mix
    x1_ref[0] = x1

    ms = jnp.mean(x1 * x1, axis=-1, keepdims=True)
    h = x1 * lax.rsqrt(ms + RMS_EPS) * nw_ref[...]
    h = h * (1.0 + sc_ref[0]) + sh_ref[0]
    h_ref[0] = h

    h0, h1 = _split2(h)
    rw2 = jnp.concatenate(_split2(rw_ref[...]), axis=1)
    t0 = _dot(h0, rw2)
    t1 = _dot(h1, rw2)
    logits = (t0[:, 0:LANES] + t0[:, LANES:2 * LANES] + t1[:, 0:LANES] + t1[:, LANES:2 * LANES]
              + rb_ref[...])
    work = logits.T[0:N_EXPERTS, :]
    expert_f = lax.broadcasted_iota(jnp.int32, (N_EXPERTS, tm), 0).astype(F32)
    vals, idxs = [], []
    for _ in range(TOP_K):
        m = jnp.max(work, axis=0, keepdims=True)
        first_idx = jnp.min(jnp.where(work == m, expert_f, float(N_EXPERTS)), axis=0, keepdims=True)
        vals.append(m)
        idxs.append(first_idx)
        work = jnp.where(expert_f == first_idx, 2.0 * NEG_BIG, work)
    exps = [jnp.exp(v - vals[0]) for v in vals]
    denom = exps[0] + exps[1] + exps[2] + exps[3]

    rec_row = lax.broadcasted_iota(jnp.int32, (RT_ROWS, tm), 0)
    rec = jnp.zeros((RT_ROWS, tm), F32)
    onehot = jnp.zeros((N_EXPERTS, tm), F32)
    for k in range(TOP_K):
        rec = jnp.where(rec_row == k, idxs[k], rec)
        rec = jnp.where(rec_row == GATE_LANE + k, exps[k] / denom, rec)
        onehot = onehot + jnp.where(expert_f == idxs[k], 1.0, 0.0)
    rt_ref[0] = rec
    for r in range(tm // DISP_TM):
        tcnt_ref[0, r] = jnp.sum(onehot[:, r * DISP_TM:(r + 1) * DISP_TM], axis=1, keepdims=True)


def _mix(y, u, x, pool_a, pool_cnt, pool_w, pool_scale, w_out, g1, norm_w, shift, scale,
         router_w, router_b):
    b, l, d = x.shape
    tm = MIX_TM
    per_l = l // tm
    tok = lambda n: pl.BlockSpec((1, tm, n), lambda i, j: (i, j, 0))
    per_b = pl.BlockSpec((1, 1, d), lambda i, j: (i, 0, 0))
    full = lambda a: pl.BlockSpec(a.shape, lambda i, j: (0,) * a.ndim)
    return pl.pallas_call(
        _mix_kernel,
        grid=(b, per_l),
        in_specs=[tok(SSD_W), tok(POOL_W), tok(d), full(pool_a), full(pool_cnt), full(pool_w),
                  full(pool_scale), full(w_out), per_b, full(norm_w), per_b, per_b,
                  full(router_w), full(router_b)],
        out_specs=[tok(d), tok(d),
                   pl.BlockSpec((1, RT_ROWS, tm), lambda i, j: (i * per_l + j, 0, 0)),
                   pl.BlockSpec((1, tm // DISP_TM, N_EXPERTS, 1), lambda i, j: (i * per_l + j, 0, 0, 0))],
        out_shape=[jax.ShapeDtypeStruct((b, l, d), F32),
                   jax.ShapeDtypeStruct((b, l, d), F32),
                   jax.ShapeDtypeStruct((b * per_l, RT_ROWS, tm), F32),
                   jax.ShapeDtypeStruct((b * per_l, tm // DISP_TM, N_EXPERTS, 1), F32)],
        compiler_params=_cparams(("arbitrary", "arbitrary")),
    )(y, u, x, pool_a, pool_cnt, pool_w, pool_scale, w_out, g1, norm_w, shift, scale,
      router_w, router_b)


def _group_copies(n_groups, src_ref, src_group, dst_ref, dst_group, sem, wait):
    def pieces(bits):
        for bit in bits:
            @pl.when((n_groups & bit) != 0)
            def _():
                done = n_groups & ~(2 * bit - 1)
                src = src_ref.at[pl.ds(pl.multiple_of((src_group + done) * SUBLANES, SUBLANES),
                                       bit * SUBLANES)]
                dst = dst_ref.at[pl.ds(pl.multiple_of((dst_group + done) * SUBLANES, SUBLANES),
                                       bit * SUBLANES)]
                cp = pltpu.make_async_copy(src, dst, sem)
                cp.wait() if wait else cp.start()

    split = RUN_BITS.index(RUN_SMALL_BIT)
    pl.when(n_groups >= 2 * RUN_SMALL_BIT)(lambda: pieces(RUN_BITS[:split]))
    pieces(RUN_BITS[split:])


def _wait_groups(n_groups, src_ref, dst_ref, sem):
    for bit in WAIT_BITS:
        @pl.when((n_groups & bit) != 0)
        def _():
            rows = bit * SUBLANES
            pltpu.make_async_copy(src_ref.at[pl.ds(0, rows)], dst_ref.at[pl.ds(0, rows)], sem).wait()


def _stage_positions(rt_t, base_col, upper):
    tm = rt_t.shape[1]
    expert_f = lax.broadcasted_iota(jnp.int32, (LANES, tm), 0).astype(F32)
    hots = [expert_f == rt_t[k:k + 1, :] for k in range(TOP_K)]
    onehot = jnp.zeros((LANES, tm), F32)
    for hot in hots:
        onehot = onehot + jnp.where(hot, 1.0, 0.0)
    pos = _dot(onehot.astype(BF16), upper) + base_col
    return [jnp.sum(jnp.where(hot, pos, 0.0), axis=0, keepdims=True) for hot in hots]


def _dispatch_kernel(o_ref, f_ref, fl_ref, rem_ref, hg_ref, tot_ref, used_ref, fs_ref, flen_ref,
                     h_ref, rt_ref, sb_ref, tri_ref, xs_ref,
                     stg2_ref, tails_ref, zero_ref, sem2, fill_sem):
    i = pl.program_id(0)
    tm = h_ref.shape[0]
    n_slots = xs_ref.shape[0]
    base = i * N_EXPERTS
    hi_mask = jnp.int32(-65536)
    slot = i % 2
    stg_ref = stg2_ref.at[slot]
    sem = sem2.at[slot]

    @pl.when(i == 0)
    def _():
        tails_ref[...] = jnp.zeros_like(tails_ref)
        stg2_ref[...] = jnp.zeros_like(stg2_ref)

    rt_t = rt_ref[0]
    pos = _stage_positions(rt_t, sb_ref[0], tri_ref[...])
    gates = [rt_t[GATE_LANE + k:GATE_LANE + k + 1, :] for k in range(TOP_K)]

    hb = h_ref[...].astype(BF16)
    half = PACK_W
    used_rows = used_ref[i] * SUBLANES

    def stage_chunk(r_lo):
        row_f = (lax.broadcasted_iota(jnp.int32, (STAGE_CHUNK, tm), 0) + r_lo).astype(F32)
        sel = jnp.zeros((STAGE_CHUNK, tm), F32)
        gsel = jnp.zeros((STAGE_CHUNK, tm), F32)
        for k in range(TOP_K):
            hit = row_f == pos[k]
            sel = jnp.where(hit, 1.0, sel)
            gsel = jnp.where(hit, gates[k], gsel)
        gate = jnp.sum(gsel, axis=-1, keepdims=True)
        moved = _dot(sel.astype(BF16), hb)
        lo = pltpu.bitcast(moved[:, 0:half], jnp.int32)
        hi = pltpu.bitcast(moved[:, half:2 * half], jnp.int32)
        stg_ref[r_lo:r_lo + STAGE_CHUNK, 0:half] = lax.shift_right_logical(lo, 16) | (hi & hi_mask)
        stg_ref[r_lo:r_lo + STAGE_CHUNK, half:half + LANES] = pltpu.bitcast(
            jnp.broadcast_to(gate, (STAGE_CHUNK, LANES)), jnp.int32)

    always = TOP_K * tm // STAGE_CHUNK
    for rc in range(STAGE_ROWS // STAGE_CHUNK):
        if rc < always:
            stage_chunk(rc * STAGE_CHUNK)
        else:
            pl.when(rc * STAGE_CHUNK < used_rows)(functools.partial(stage_chunk, rc * STAGE_CHUNK))

    def finish_run(e, cr):
        stage_group = o_ref[base + e]
        first = pl.multiple_of(stage_group * SUBLANES, SUBLANES)
        stg_ref[pl.ds(first, SUBLANES), :] = stg_ref[pl.ds(first, SUBLANES), :] | tails_ref[e]
        part = pl.multiple_of((stage_group + f_ref[base + e]) * SUBLANES, SUBLANES)
        tails_ref[e] = jnp.where(rem_ref[base + e] > 0, stg_ref[pl.ds(part, SUBLANES), :], 0)
        _group_copies(fl_ref[base + e], stg_ref, stage_group, xs_ref, hg_ref[base + e], sem, False)
        return cr

    lax.fori_loop(0, N_EXPERTS, finish_run, 0)

    @pl.when(i == pl.num_programs(0) - 1)
    def _():
        zero_ref[...] = jnp.zeros_like(zero_ref)
        tail_group = fs_ref[N_EXPERTS]
        n_tail = (n_slots // SUBLANES - tail_group) // (MOE_BLK // SUBLANES)

        def tail_copy(j, wait):
            off = pl.multiple_of(tail_group * SUBLANES + j * MOE_BLK, MOE_BLK)
            cp = pltpu.make_async_copy(zero_ref, xs_ref.at[pl.ds(off, MOE_BLK)], fill_sem)
            cp.wait() if wait else cp.start()

        for wait in (False, True):
            def pad_body(e, cr):
                _group_copies(flen_ref[e], zero_ref, 0, xs_ref, fs_ref[e], fill_sem, wait)
                return cr
            lax.fori_loop(0, N_EXPERTS, pad_body, 0)
            lax.fori_loop(0, n_tail, lambda j, cr: (tail_copy(j, wait), cr)[1], 0)

    @pl.when(i > 0)
    def _():
        _wait_groups(tot_ref[jnp.maximum(i - 1, 0)], stg2_ref.at[1 - slot], xs_ref, sem2.at[1 - slot])

    @pl.when(i == pl.num_programs(0) - 1)
    def _():
        _wait_groups(tot_ref[i], stg_ref, xs_ref, sem)


def _dispatch(tables, fill_groups, fill_len_groups, h, rt, stage_base, n_slots):
    t, d = h.shape
    tm = DISP_TM
    rt_per = rt.shape[2] // tm
    grid_spec = pltpu.PrefetchScalarGridSpec(
        num_scalar_prefetch=len(tables) + 2,
        grid=(t // tm,),
        in_specs=[pl.BlockSpec((tm, d), lambda i, *_: (i, 0)),
                  pl.BlockSpec((1, RT_ROWS, tm), lambda i, *_: (i // rt_per, 0, i % rt_per)),
                  pl.BlockSpec((1, LANES, 1), lambda i, *_: (i, 0, 0)),
                  pl.BlockSpec((tm, tm), lambda i, *_: (0, 0))],
        out_specs=pl.BlockSpec(memory_space=pl.ANY),
        scratch_shapes=[pltpu.VMEM((2, STAGE_ROWS, XS_W), jnp.int32),
                        pltpu.VMEM((N_EXPERTS, SUBLANES, XS_W), jnp.int32),
                        pltpu.VMEM((MOE_BLK, XS_W), jnp.int32),
                        pltpu.SemaphoreType.DMA((2,)), pltpu.SemaphoreType.DMA(())],
    )
    return pl.pallas_call(
        _dispatch_kernel,
        grid_spec=grid_spec,
        out_shape=jax.ShapeDtypeStruct((n_slots, XS_W), jnp.int32),
        compiler_params=_cparams(("arbitrary",)),
    )(*tables, fill_groups, fill_len_groups, h, rt, stage_base, _strict_upper(tm))


def _expert_kernel(be_ref, nused_ref, xs_ref, w1_ref, b1_ref, w2_ref, b2_ref, perm_ref, y_ref,
                   w1s_ref, w2s_ref):
    i = pl.program_id(0)
    prev = be_ref[jnp.maximum(i - 1, 0)]
    changed = (i == 0) | (be_ref[i] != prev)
    dff2 = w1_ref.shape[2]
    tile = 2 * LANES

    @pl.when(changed & (i < nused_ref[0]))
    def _():
        for j in range(dff2 // tile):
            wj = w1_ref[0, :, j * tile:(j + 1) * tile].astype(BF16)
            w1s_ref[:, j * tile:(j + 1) * tile] = _dot(wj, perm_ref[...]).astype(BF16)
        w2s_ref[...] = w2_ref[0].astype(BF16)

    @pl.when(i < nused_ref[0])
    def _():
        hi_mask = jnp.int32(-65536)
        words = xs_ref[:, 0:PACK_W]
        x = jnp.concatenate(
            [pltpu.bitcast(lax.shift_left(words, 16), F32).astype(BF16),
             pltpu.bitcast(words & hi_mask, F32).astype(BF16)], axis=1)
        gate = pltpu.bitcast(xs_ref[:, PACK_W:PACK_W + 1], F32)
        acts = []
        for j in range(dff2 // tile):
            hb = _dot(x, w1s_ref[:, j * tile:(j + 1) * tile]) + b1_ref[0, :, j * tile:(j + 1) * tile]
            gp = jnp.minimum(hb[:, 0:LANES], SWIGLU_LIMIT)
            up = jnp.clip(hb[:, LANES:tile], -SWIGLU_LIMIT, SWIGLU_LIMIT)
            acts.append((gp * _sigmoid(SWIGLU_ALPHA * gp) * (up + 1.0)).astype(BF16))
        act = jnp.concatenate(acts, axis=1)
        y = (_dot(act, w2s_ref[...]) + b2_ref[0]) * gate
        lo = pltpu.bitcast(y[:, 0:PACK_W].astype(BF16).astype(F32), jnp.int32)
        hi = pltpu.bitcast(y[:, PACK_W:2 * PACK_W].astype(BF16).astype(F32), jnp.int32)
        y_ref[...] = lax.shift_right_logical(lo, 16) | (hi & hi_mask)

    @pl.when(i >= nused_ref[0])
    def _():
        y_ref[...] = jnp.zeros_like(y_ref)


def _experts(block_exp, n_used, xs, w1, b1p, w2, b2, perm, n_blocks):
    blk = MOE_BLK
    d = w1.shape[1]
    dff2 = w1.shape[2]
    dff = w2.shape[1]

    def x_idx(i, be, nu):
        return (jnp.minimum(i, nu[0] - 1), 0)

    grid_spec = pltpu.PrefetchScalarGridSpec(
        num_scalar_prefetch=2,
        grid=(n_blocks,),
        in_specs=[pl.BlockSpec((blk, XS_W), x_idx),
                  pl.BlockSpec((1, d, dff2), lambda i, be, nu: (be[i], 0, 0)),
                  pl.BlockSpec((1, 1, dff2), lambda i, be, nu: (be[i], 0, 0)),
                  pl.BlockSpec((1, dff, d), lambda i, be, nu: (be[i], 0, 0)),
                  pl.BlockSpec((1, 1, d), lambda i, be, nu: (be[i], 0, 0)),
                  pl.BlockSpec(perm.shape, lambda i, be, nu: (0, 0))],
        out_specs=pl.BlockSpec((blk, PACK_W), lambda i, be, nu: (i, 0)),
        scratch_shapes=[pltpu.VMEM((d, dff2), BF16), pltpu.VMEM((dff, d), BF16)],
    )
    return pl.pallas_call(
        _expert_kernel,
        grid_spec=grid_spec,
        out_shape=jax.ShapeDtypeStruct((n_blocks * blk, PACK_W), jnp.int32),
        compiler_params=_cparams(("arbitrary",)),
    )(block_exp, n_used, xs, w1, b1p, w2, b2, perm)


def _combine_kernel(o_ref, g_ref, hg_ref, tot_ref, x1_ref, rt_ref, sb_ref, tri_ref, g2_ref, fnw_ref,
                    ys_ref, out_ref, stg_ref, sem):
    tm = x1_ref.shape[1]
    step = pl.program_id(0) * pl.num_programs(1) + pl.program_id(1)
    n_steps = pl.num_programs(0) * pl.num_programs(1)
    slot = step % 2
    hi_mask = jnp.int32(-65536)

    def fetch(tile, into):
        def body(e, cr):
            _group_copies(g_ref[tile * N_EXPERTS + e], ys_ref, hg_ref[tile * N_EXPERTS + e],
                          stg_ref.at[into], o_ref[tile * N_EXPERTS + e], sem.at[into], False)
            return cr
        lax.fori_loop(0, N_EXPERTS, body, 0)

    @pl.when(step == 0)
    def _():
        stg_ref[...] = jnp.zeros_like(stg_ref)
        fetch(0, 0)

    @pl.when(step + 1 < n_steps)
    def _():
        fetch(step + 1, 1 - slot)

    _wait_groups(tot_ref[step], ys_ref, stg_ref.at[slot], sem.at[slot])

    pos_rows = _stage_positions(rt_ref[0], sb_ref[0], tri_ref[...])
    expert_i = lax.broadcasted_iota(jnp.int32, (LANES, tm), 0)
    stacked = jnp.zeros((LANES, tm), F32)
    for k in range(TOP_K):
        stacked = jnp.where(expert_i == k, pos_rows[k], stacked)
    pos_cols = stacked.T

    def unpermute(n_rows):
        col_f = lax.broadcasted_iota(jnp.int32, (tm, n_rows), 1).astype(F32)
        sel = jnp.zeros((tm, n_rows), F32)
        for k in range(TOP_K):
            sel = jnp.where(col_f == pos_cols[:, k:k + 1], 1.0, sel)
        sel = sel.astype(BF16)
        words = stg_ref[slot, 0:n_rows, :]
        lo = pltpu.bitcast(lax.shift_left(words, 16), F32).astype(BF16)
        hi = pltpu.bitcast(words & hi_mask, F32).astype(BF16)
        moe = jnp.concatenate([_dot(sel, lo), _dot(sel, hi)], axis=1)
        x2 = x1_ref[0] + g2_ref[0] * moe
        ms = jnp.mean(x2 * x2, axis=-1, keepdims=True)
        out_ref[0] = x2 * lax.rsqrt(ms + RMS_EPS) * fnw_ref[...]

    used_rows = tot_ref[step] * SUBLANES
    short = STAGE_ROWS - STAGE_CHUNK
    pl.when(used_rows <= short)(functools.partial(unpermute, short))
    pl.when(used_rows > short)(functools.partial(unpermute, STAGE_ROWS))


def _combine(tables, x1, rt, stage_base, g2, fnw, ys):
    b, l, d = x1.shape
    tm = DISP_TM
    per_l = l // tm
    rt_per = rt.shape[2] // tm
    tok = lambda n: pl.BlockSpec((1, tm, n), lambda i, j, *_: (i, j, 0))
    grid_spec = pltpu.PrefetchScalarGridSpec(
        num_scalar_prefetch=len(tables),
        grid=(b, per_l),
        in_specs=[tok(d),
                  pl.BlockSpec((1, RT_ROWS, tm),
                               lambda i, j, *_: ((i * per_l + j) // rt_per, 0, (i * per_l + j) % rt_per)),
                  pl.BlockSpec((1, LANES, 1), lambda i, j, *_: (i * per_l + j, 0, 0)),
                  pl.BlockSpec((tm, tm), lambda i, j, *_: (0, 0)),
                  pl.BlockSpec((1, 1, d), lambda i, j, *_: (i, 0, 0)),
                  pl.BlockSpec(fnw.shape, lambda i, j, *_: (0, 0)),
                  pl.BlockSpec(memory_space=pl.ANY)],
        out_specs=tok(d),
        scratch_shapes=[pltpu.VMEM((2, STAGE_ROWS, PACK_W), jnp.int32),
                        pltpu.SemaphoreType.DMA((2,))],
    )
    return pl.pallas_call(
        _combine_kernel,
        grid_spec=grid_spec,
        out_shape=jax.ShapeDtypeStruct((b, l, d), F32),
        compiler_params=_cparams(("arbitrary", "arbitrary")),
    )(*tables, x1, rt, stage_base, _strict_upper(tm), g2, fnw, ys)


def _pool_constants():
    i = np.arange(POOL_TM)[:, None]
    j = np.arange(POOL_TM)[None, :]
    same_row = (i // GRID_W) == (j // GRID_W)
    mats, cnts = [], []
    for w in POOL_WINDOWS:
        band = same_row & (j - i >= -(w // 2)) & (j - i < w - w // 2)
        mats.append(band)
        cnts.append(np.broadcast_to(band.sum(axis=1, keepdims=True), (POOL_TM, POOL_GW)))
    return (jnp.asarray(np.stack(mats), BF16), jnp.asarray(np.stack(cnts), F32))


def _head_expand():
    e = np.zeros((LANES, SSD_W), np.float32)
    for h in range(SSD_HEADS):
        e[h, h * SSD_HEADDIM:(h + 1) * SSD_HEADDIM] = 1.0
    return jnp.asarray(e, BF16)


def _deinterleave_perm():
    n = 2 * LANES
    p = np.zeros((n, n), np.float32)
    for k in range(LANES):
        p[2 * k, k] = 1.0
        p[2 * k + 1, LANES + k] = 1.0
    return jnp.asarray(p, BF16)


def _strict_upper(n):
    return jnp.asarray(np.triu(np.ones((n, n), np.float32), 1), BF16)


def _pad_lanes(a, n):
    return jnp.pad(a, [(0, 0)] * (a.ndim - 1) + [(0, n - a.shape[-1])])


def kernel(x, c, ctx, c_ctx, w_mod, b_mod, norm1_w, norm2_w, w_in, conv_w, conv_b, dt_bias, a_log,
           d_skip, ssd_norm_w, pool_w, pool_scale, w_out, router_w, router_b, w1, b1, w2, b2,
           final_norm_w):
    depth = w_mod.shape[0]
    assert depth == 1, "single-layer problem"
    b, l, d = x.shape
    lc = ctx.shape[1]
    xbcdt = CONV_CH + 2 * SSD_HEADS

    mod_rows = 2 * SUBLANES
    cc = jnp.zeros((mod_rows, d), F32).at[0:b].set(c).at[b].set(c_ctx)
    mod = _modulation(cc, w_mod[0], b_mod[0])
    sh1, sc1, g1, sh2, sc2, g2 = [m.reshape(b, 1, d) for m in jnp.split(mod[0:b], 6, axis=-1)]
    csh1, csc1 = [jnp.broadcast_to(m.reshape(1, 1, d), (b, 1, d))
                  for m in jnp.split(mod[b:b + 1], 6, axis=-1)[0:2]]

    wi = w_in[0]
    wx = wi[:, 0:CONV_CH].astype(BF16)
    wd = jnp.concatenate([_pad_lanes(wi[:, CONV_CH:CONV_CH + SSD_HEADS], LANES),
                          _pad_lanes(wi[:, CONV_CH + SSD_HEADS:xbcdt], LANES)], axis=1).astype(BF16)
    wz = wi[:, xbcdt:xbcdt + SSD_W].astype(BF16)
    wp = wi[:, xbcdt + SSD_W:].astype(BF16)
    n1 = norm1_w[0].reshape(1, d)
    conv_w8 = jnp.pad(conv_w[0], ((0, SUBLANES - SSD_CONV), (0, 0)))
    conv_b1 = conv_b[0].reshape(1, CONV_CH)
    dtb = _pad_lanes(dt_bias[0], LANES).reshape(2, 1, LANES)
    alog = _pad_lanes(a_log[0], LANES).reshape(2, 1, LANES)
    dsk = jnp.repeat(d_skip[0], SSD_HEADDIM).reshape(1, SSD_W)
    snw = ssd_norm_w[0].reshape(1, SSD_W)
    expand = _head_expand()

    xbc_c, dt_c = _inproj(ctx, n1, csh1, csc1, (wx, wd), (BF16, F32), min(PROJ_TM, lc))
    zero_state = jnp.zeros((b, 2, SSD_GROUPS, SSD_STATE, GROUP_W), F32)
    _, ctx_states = _ssd(xbc_c, dt_c, None, conv_w8, conv_b1, dtb, alog, dsk, snw, expand, zero_state)

    xbc, dt, z, u_pool = _inproj(x, n1, sh1, sc1, (wx, wd, wz, wp), (BF16, F32, BF16, BF16), PROJ_TM)
    y_ssd, _ = _ssd(xbc, dt, z, conv_w8, conv_b1, dtb, alog, dsk, snw, expand, ctx_states)

    pool_a, pool_cnt = _pool_constants()
    pw = pool_w[0].astype(BF16)
    zero_gw = jnp.zeros((POOL_GW, POOL_GW), BF16)
    pw_pairs = jnp.stack([jnp.block([[pw[2 * gp], zero_gw], [zero_gw, pw[2 * gp + 1]]])
                          for gp in range(len(POOL_WINDOWS) // 2)])
    x1, h2, rt, tcnt = _mix(
        y_ssd, u_pool, x, pool_a, pool_cnt, pw_pairs, pool_scale[0].reshape(1, POOL_W),
        w_out[0].astype(BF16), g1, norm2_w[0].reshape(1, d), sh2, sc2,
        _pad_lanes(router_w[0], LANES), _pad_lanes(router_b[0].reshape(1, N_EXPERTS), LANES))

    i32 = jnp.int32
    t = b * l
    n_tiles = t // DISP_TM
    tc = tcnt.reshape(n_tiles, N_EXPERTS).astype(i32)
    counts = jnp.sum(tc, axis=0)
    run_start = jnp.cumsum(tc, axis=0) - tc
    padded = (counts + MOE_BLK - 1) // MOE_BLK * MOE_BLK
    pad_end = jnp.cumsum(padded)
    pad_start = pad_end - padded
    carried = run_start % SUBLANES
    span = carried + tc
    groups = (span + SUBLANES - 1) // SUBLANES
    full = span // SUBLANES
    stage_group = jnp.cumsum(groups, axis=1) - groups
    slot_group = (pad_start[None, :] + run_start - carried) // SUBLANES
    flush = full.at[n_tiles - 1].set(groups[n_tiles - 1])
    flat = lambda a: a.reshape(n_tiles * N_EXPERTS).astype(i32)
    stage_base = _pad_lanes((stage_group * SUBLANES + carried).astype(F32), LANES)
    stage_base = stage_base.reshape(n_tiles, LANES, 1)
    n_blocks = (t * TOP_K) // MOE_BLK + N_EXPERTS
    n_used = (pad_end[-1] // MOE_BLK).astype(i32).reshape(1)
    blk_start = jnp.minimum(jnp.arange(n_blocks, dtype=i32), n_used[0] - 1) * MOE_BLK
    block_exp = jnp.minimum(jnp.sum(blk_start[:, None] >= pad_end[None, :], axis=1),
                            N_EXPERTS - 1).astype(i32)
    written = (counts + SUBLANES - 1) // SUBLANES * SUBLANES
    fill_group = (jnp.concatenate([pad_start + written, pad_end[-1:]]) // SUBLANES).astype(i32)
    fill_groups = (jnp.concatenate([padded - written, jnp.zeros((1,), i32)]) // SUBLANES).astype(i32)

    xs = _dispatch((flat(stage_group), flat(full), flat(flush), flat(span % SUBLANES), flat(slot_group),
                    jnp.sum(flush, axis=1).astype(i32), jnp.sum(groups, axis=1).astype(i32)),
                   fill_group, fill_groups, h2.reshape(t, d), rt, stage_base,
                   n_blocks * MOE_BLK)
    dff2 = w1.shape[-1]
    b1p = jnp.concatenate(
        [b1[0].reshape(N_EXPERTS, dff2 // (2 * LANES), LANES, 2)[..., 0],
         b1[0].reshape(N_EXPERTS, dff2 // (2 * LANES), LANES, 2)[..., 1]], axis=-1
    ).reshape(N_EXPERTS, 1, dff2)
    ys = _experts(block_exp, n_used, xs, w1[0], b1p, w2[0], b2[0].reshape(N_EXPERTS, 1, d),
                  _deinterleave_perm(), n_blocks)
    return _combine((flat(stage_group), flat(groups), flat(slot_group),
                     jnp.sum(groups, axis=1).astype(i32)), x1, rt, stage_base, g2,
                    final_norm_w.reshape(1, d), ys)
```

```python
import functools

import numpy as np
import jax
import jax.numpy as jnp
from jax import lax
from jax.experimental import pallas as pl
from jax.experimental.pallas import tpu as pltpu

F32 = jnp.float32
BF16 = jnp.bfloat16

SSD_HEADDIM = 64
SSD_GROUPS = 4
SSD_HPG = 6
SSD_HEADS = SSD_GROUPS * SSD_HPG
SSD_STATE = 128
SSD_CONV = 5
SSD_CHUNK = 128
SSD_W = SSD_HEADS * SSD_HEADDIM
GROUP_W = SSD_HPG * SSD_HEADDIM
CONV_CH = SSD_W + 2 * SSD_GROUPS * SSD_STATE
POOL_WINDOWS = (2, 4, 8, 16)
POOL_GW = 128
POOL_W = POOL_GW * len(POOL_WINDOWS)
GRID_W = 64
N_EXPERTS = 32
TOP_K = 4
SWIGLU_ALPHA = 1.702
SWIGLU_LIMIT = 7.0
RMS_EPS = 1e-6
LOG2_E = 1.4426950408889634

LANES = 128
SUBLANES = 8
VMEM_LIMIT_BYTES = 56 * 1024 * 1024

MOD_TN = 1024
PROJ_TM = 512
MIX_TM = 512
POOL_TM = 256
MOE_BLK = 512
DISP_TM = 256
NEG_BIG = -1e30

GATE_LANE = 2 * TOP_K
RT_ROWS = 2 * SUBLANES
PACK_W = 512
XS_W = PACK_W
STAGE_CHUNK = 256
_STAGE_NEED = TOP_K * DISP_TM + N_EXPERTS * 2 * (SUBLANES - 1) + SUBLANES
STAGE_ROWS = -(-_STAGE_NEED // STAGE_CHUNK) * STAGE_CHUNK
RUN_BITS = (32, 16, 8, 4, 2, 1)
RUN_SMALL_BIT = 4
WAIT_BITS = (128, 64, 32, 16, 8, 4, 2, 1)


def _sigmoid(x):
    return 0.5 * jnp.tanh(0.5 * x) + 0.5


def _split2(a):
    hi = a.astype(BF16)
    mid = (a - hi.astype(F32)).astype(BF16)
    return hi, mid


def _split3(a):
    hi = a.astype(BF16)
    r = a - hi.astype(F32)
    mid = r.astype(BF16)
    lo = (r - mid.astype(F32)).astype(BF16)
    return hi, mid, lo


def _dot(a, b):
    return jnp.dot(a, b, preferred_element_type=F32)


def _dot_exact_rhs(a_f32, b_bf16, parts):
    pieces = _split3(a_f32) if parts == 3 else _split2(a_f32)
    out = _dot(pieces[0], b_bf16)
    for p in pieces[1:]:
        out = out + _dot(p, b_bf16)
    return out


def _dot_hi(a_f32, b_f32):
    a0, a1, a2 = _split3(a_f32)
    b0, b1, b2 = _split3(b_f32)
    out = _dot(a0, b0)
    out = out + _dot(a0, b1) + _dot(a1, b0)
    out = out + _dot(a1, b1) + _dot(a0, b2) + _dot(a2, b0)
    return out


def _cparams(sem):
    return pltpu.CompilerParams(dimension_semantics=sem, vmem_limit_bytes=VMEM_LIMIT_BYTES)


def _mod_kernel(c_ref, w_ref, b_ref, o_ref):
    c = c_ref[...]
    s = c * _sigmoid(c)
    o_ref[...] = _dot_hi(s, w_ref[...]) + b_ref[...]


def _modulation(cc, w_mod, b_mod):
    rows, d = cc.shape
    n = w_mod.shape[1]
    return pl.pallas_call(
        _mod_kernel,
        grid=(n // MOD_TN,),
        in_specs=[pl.BlockSpec((rows, d), lambda j: (0, 0)),
                  pl.BlockSpec((d, MOD_TN), lambda j: (0, j)),
                  pl.BlockSpec((1, MOD_TN), lambda j: (0, j))],
        out_specs=pl.BlockSpec((rows, MOD_TN), lambda j: (0, j)),
        out_shape=jax.ShapeDtypeStruct((rows, n), F32),
        compiler_params=_cparams(("arbitrary",)),
    )(cc, w_mod, b_mod.reshape(1, n))


def _inproj_kernel(n_groups, x0_ref, xn_ref, nw_ref, sh0_ref, sc0_ref, shn_ref, scn_ref, *refs):
    w_refs = refs[0:n_groups]
    out_refs = refs[n_groups:2 * n_groups]
    hb_even_ref, hb_odd_ref = refs[2 * n_groups:]
    step = pl.program_id(0) * pl.num_programs(1) + pl.program_id(1)

    def normed(x, sh, sc):
        ms = jnp.mean(x * x, axis=-1, keepdims=True)
        h = x * lax.rsqrt(ms + RMS_EPS) * nw_ref[...]
        return (h * (1.0 + sc) + sh).astype(BF16)

    @pl.when(step == 0)
    def _():
        hb_even_ref[...] = normed(x0_ref[0], sh0_ref[0], sc0_ref[0])

    def project(cur_ref, nxt_ref):
        hb = cur_ref[...]
        for w_ref, out_ref in zip(w_refs, out_refs):
            out_ref[0] = _dot(hb, w_ref[...]).astype(out_ref.dtype)
        nxt_ref[...] = normed(xn_ref[0], shn_ref[0], scn_ref[0])

    pl.when(step % 2 == 0)(functools.partial(project, hb_even_ref, hb_odd_ref))
    pl.when(step % 2 == 1)(functools.partial(project, hb_odd_ref, hb_even_ref))


def _inproj(x, norm_w, shift, scale, weights, out_dtypes, tm):
    b, l, d = x.shape
    per_l = l // tm
    last = b * per_l - 1

    def nxt(i, j):
        s = jnp.minimum(i * per_l + j + 1, last)
        return s // per_l, s % per_l

    full = lambda a: pl.BlockSpec(a.shape, lambda i, j: (0, 0))
    tok = lambda n: pl.BlockSpec((1, tm, n), lambda i, j: (i, j, 0))
    first_b = pl.BlockSpec((1, 1, d), lambda i, j: (0, 0, 0))
    next_b = pl.BlockSpec((1, 1, d), lambda i, j: (nxt(i, j)[0], 0, 0))
    return pl.pallas_call(
        functools.partial(_inproj_kernel, len(weights)),
        grid=(b, per_l),
        in_specs=[pl.BlockSpec((1, tm, d), lambda i, j: (0, 0, 0)),
                  pl.BlockSpec((1, tm, d), lambda i, j: (*nxt(i, j), 0)),
                  full(norm_w), first_b, first_b, next_b, next_b] + [full(w) for w in weights],
        out_specs=[tok(w.shape[1]) for w in weights],
        out_shape=[jax.ShapeDtypeStruct((b, l, w.shape[1]), dt)
                   for w, dt in zip(weights, out_dtypes)],
        scratch_shapes=[pltpu.VMEM((tm, d), BF16), pltpu.VMEM((tm, d), BF16)],
        compiler_params=_cparams(("arbitrary", "arbitrary")),
    )(x, x, norm_w, shift, scale, shift, scale, *weights)


def _ssd_kernel(nc, emit_y, *refs):
    if emit_y:
        (xbc_ref, prev_ref, next_ref, dt_ref, z_ref, cw_ref, cb_ref, dtb_ref, alog_ref, dsk_ref,
         nw_ref, exp_ref, init_ref, y_ref, fin_ref, act_ref, ybuf_ref, yf_ref, st_ref) = refs
        y_part = (z_ref, dsk_ref, nw_ref, y_ref, ybuf_ref, yf_ref)
    else:
        (xbc_ref, prev_ref, next_ref, dt_ref, cw_ref, cb_ref, dtb_ref, alog_ref,
         exp_ref, init_ref, fin_ref, act_ref, st_ref) = refs
        y_part = None
    ph = pl.program_id(1)
    c = pl.program_id(2)
    ci = c + ph * (nc - 1 - 2 * c)

    @pl.when(c == 0)
    def _():
        st_ref[...] = init_ref[0, ph]

    @pl.when(ph == 0)
    def _():
        _conv_silu(nc, ci, xbc_ref, prev_ref, next_ref, cw_ref, cb_ref, act_ref)

    _ssd_scan(nc, ph, c, ci, dt_ref, dtb_ref, alog_ref, exp_ref, act_ref, st_ref, fin_ref, y_part)


def _conv_silu(nc, ci, xbc_ref, prev_ref, next_ref, cw_ref, cb_ref, act_ref):
    ch = SSD_CHUNK
    has_prev = ci > 0
    has_next = ci < nc - 1
    row8 = lax.broadcasted_iota(jnp.int32, (SUBLANES, LANES), 0)
    half = SSD_CONV // 2
    for j in range(CONV_CH // LANES):
        cols = slice(j * LANES, (j + 1) * LANES)
        n_t = ch // SUBLANES
        tiles = [jnp.where(has_prev, prev_ref[0, SUBLANES:2 * SUBLANES, cols].astype(F32), 0.0)]
        for i in range(n_t // 2):
            both = xbc_ref[0, 2 * i * SUBLANES:2 * (i + 1) * SUBLANES, cols].astype(F32)
            tiles += [both[0:SUBLANES], both[SUBLANES:2 * SUBLANES]]
        tiles.append(jnp.where(has_next, next_ref[0, 0:SUBLANES, cols].astype(F32), 0.0))
        acc = [cb_ref[:, cols] + cw_ref[half:half + 1, cols] * tiles[i + 1] for i in range(n_t)]
        for s in range(1, half + 1):
            rot = [pltpu.roll(tl, s, axis=0) for tl in tiles[0:n_t + 1]]
            wk = cw_ref[half - s:half - s + 1, cols]
            for i in range(n_t):
                acc[i] = acc[i] + wk * jnp.where(row8 < s, rot[i], rot[i + 1])
            rot = [pltpu.roll(tl, SUBLANES - s, axis=0) for tl in tiles[1:n_t + 2]]
            wk = cw_ref[half + s:half + s + 1, cols]
            for i in range(n_t):
                acc[i] = acc[i] + wk * jnp.where(row8 >= SUBLANES - s, rot[i + 1], rot[i])
        silu = [0.5 * v * jnp.tanh(0.5 * v) + 0.5 * v for v in acc]
        act_ref[ci, :, cols] = jnp.concatenate(silu, axis=0).astype(BF16)


def _ssd_scan(nc, ph, c, ci, dt_ref, dtb_ref, alog_ref, exp_ref, act_ref, st_ref, fin_ref, y_part):
    ch = SSD_CHUNK
    emit_y = y_part is not None
    dtr = dt_ref[0] + dtb_ref[0]
    small = jnp.exp(-jnp.abs(dtr))
    one_plus = 1.0 + small
    log1p_small = jnp.where(one_plus == 1.0, small, jnp.log(one_plus) * (small / (one_plus - 1.0)))
    dtv = jnp.maximum(dtr, 0.0) + log1p_small
    a = dtv * (-jnp.exp(alog_ref[0]) * LOG2_E)
    row = lax.broadcasted_iota(jnp.int32, (ch, ch), 0)
    col = lax.broadcasted_iota(jnp.int32, (ch, ch), 1)
    tmask = (row - col) * (1 - 2 * ph) >= 0
    tri = jnp.where(tmask, 1.0, 0.0).astype(BF16)
    cs = _dot_exact_rhs_left(tri, a)
    tot = jnp.where(ph == 0, cs[ch - 1:ch, :], cs[0:1, :])
    cs_t = cs.T
    e_cs = jnp.exp2(cs)
    e_dec = jnp.exp2(tot - cs)
    e_tot = jnp.exp2(tot)
    expand = exp_ref[...]
    wdec_x = _dot((dtv * e_dec).astype(BF16), expand).astype(BF16)
    etot_x = _dot_exact_rhs(jnp.broadcast_to(e_tot, (SUBLANES, LANES)), expand, 2)[0:1]
    if emit_y:
        z_ref, dsk_ref, nw_ref, y_ref, ybuf_ref, yf_ref = y_part
        ecs_x = _dot(e_cs.astype(BF16), expand)
        src_t = cs_t - jnp.log2(dtv).T

    lane = lax.broadcasted_iota(jnp.int32, (ch, LANES), 1)
    for g in range(SSD_GROUPS):
        gs = slice(g * GROUP_W, (g + 1) * GROUP_W)
        b_bf = act_ref[ci, :, SSD_W + g * SSD_STATE:SSD_W + (g + 1) * SSD_STATE]
        x_bf = act_ref[ci, :, gs]
        s_prev = st_ref[g]
        if emit_y:
            c_bf = act_ref[ci, :, SSD_W + (SSD_GROUPS + g) * SSD_STATE:
                           SSD_W + (SSD_GROUPS + g + 1) * SSD_STATE]
            cb = lax.dot_general(c_bf, b_bf, (((1,), (1,)), ((), ())), preferred_element_type=F32)
            y_off = _dot(c_bf, s_prev.astype(BF16)) * ecs_x[:, gs]
        x_dec = x_bf * wdec_x[:, gs]
        st_ref[g] = s_prev * etot_x[:, gs] + lax.dot_general(
            b_bf, x_dec, (((0,), (0,)), ((), ())), preferred_element_type=F32)
        if not emit_y:
            continue
        for q in range(SSD_HPG // 2):
            lmats = []
            for h in (g * SSD_HPG + 2 * q, g * SSD_HPG + 2 * q + 1):
                diff = cs[:, h:h + 1] - src_t[h:h + 1, :]
                dec = jnp.exp2(jnp.where(tmask, diff, NEG_BIG))
                lmats.append((dec * cb).astype(BF16))
            xp = x_bf[:, q * LANES:(q + 1) * LANES]
            zero = jnp.zeros_like(xp)
            rhs = jnp.concatenate([jnp.where(lane < SSD_HEADDIM, xp, zero),
                                   jnp.where(lane >= SSD_HEADDIM, xp, zero)], axis=0)
            y_diag = _dot(jnp.concatenate(lmats, axis=1), rhs)
            ps = slice(g * GROUP_W + q * LANES, g * GROUP_W + (q + 1) * LANES)
            ybuf_ref[:, ps] = y_diag + y_off[:, q * LANES:(q + 1) * LANES]

    if emit_y:
        @pl.when(ph == 0)
        def _():
            yf_ref[ci] = ybuf_ref[...].astype(BF16)

        @pl.when(ph == 1)
        def _():
            yt = (yf_ref[ci].astype(F32) + ybuf_ref[...]
                  + act_ref[ci, :, 0:SSD_W].astype(F32) * dsk_ref[...])
            hz = 0.5 * z_ref[0].astype(F32)
            gt = yt * (hz * jnp.tanh(hz) + hz)
            ms = jnp.mean(gt * gt, axis=-1, keepdims=True)
            y_ref[0] = (gt * lax.rsqrt(ms + RMS_EPS) * nw_ref[...]).astype(y_ref.dtype)

    @pl.when(c == nc - 1)
    def _():
        fin_ref[0, ph] = st_ref[...]


def _dot_exact_rhs_left(sel_bf16, a_f32):
    hi, mid, lo = _split3(a_f32)
    return _dot(sel_bf16, hi) + _dot(sel_bf16, mid) + _dot(sel_bf16, lo)


def _ssd(xbc, dt, z, conv_w8, conv_b, dt_bias, a_log, d_skip_x, norm_w, expand, init):
    emit_y = z is not None
    b, l, _ = xbc.shape
    ch = SSD_CHUNK
    nc = l // ch
    halo = 2 * SUBLANES
    per_ch = ch // halo

    def cidx(ph, c):
        return c + ph * (nc - 1 - 2 * c)

    def xidx(ph, c):
        return jnp.where(ph == 0, c, nc - 1)

    def out_idx(ph, c):
        return jnp.where(ph == 0, nc - 1, nc - 1 - c)

    full2 = lambda a: pl.BlockSpec(a.shape, lambda i, ph, c: (0, 0))
    st_spec = pl.BlockSpec((1, 2, SSD_GROUPS, SSD_STATE, GROUP_W), lambda i, ph, c: (i, 0, 0, 0, 0))
    per_phase = pl.BlockSpec((1, 1, LANES), lambda i, ph, c: (ph, 0, 0))
    y_spec = pl.BlockSpec((1, ch, SSD_W), lambda i, ph, c: (i, out_idx(ph, c), 0))
    operands = [(xbc, pl.BlockSpec((1, ch, CONV_CH), lambda i, ph, c: (i, xidx(ph, c), 0))),
                (xbc, pl.BlockSpec((1, halo, CONV_CH),
                                   lambda i, ph, c: (i, jnp.maximum(xidx(ph, c) * per_ch - 1, 0), 0))),
                (xbc, pl.BlockSpec((1, halo, CONV_CH),
                                   lambda i, ph, c: (i, jnp.minimum((xidx(ph, c) + 1) * per_ch,
                                                                    l // halo - 1), 0))),
                (dt, pl.BlockSpec((1, ch, LANES), lambda i, ph, c: (i, cidx(ph, c), ph)))]
    if emit_y:
        operands.append((z, y_spec))
    operands += [(conv_w8, full2(conv_w8)), (conv_b, full2(conv_b)),
                 (dt_bias, per_phase), (a_log, per_phase)]
    if emit_y:
        operands += [(d_skip_x, full2(d_skip_x)), (norm_w, full2(norm_w))]
    operands += [(expand, full2(expand)), (init, st_spec)]
    st_shape = jax.ShapeDtypeStruct((b, 2, SSD_GROUPS, SSD_STATE, GROUP_W), F32)
    scratch = [pltpu.VMEM((nc, ch, CONV_CH), BF16)]
    if emit_y:
        scratch += [pltpu.VMEM((ch, SSD_W), F32),
                    pltpu.VMEM((nc, ch, SSD_W), BF16)]
    scratch.append(pltpu.VMEM((SSD_GROUPS, SSD_STATE, GROUP_W), F32))
    outs = pl.pallas_call(
        functools.partial(_ssd_kernel, nc, emit_y),
        grid=(b, 2, nc),
        in_specs=[spec for _, spec in operands],
        out_specs=[y_spec, st_spec] if emit_y else [st_spec],
        out_shape=[jax.ShapeDtypeStruct((b, l, SSD_W), BF16), st_shape] if emit_y else [st_shape],
        scratch_shapes=scratch,
        compiler_params=_cparams(("arbitrary", "arbitrary", "arbitrary")),
    )(*[a for a, _ in operands])
    return outs if emit_y else (None, outs[0])


def _mix_kernel(y_ref, u_ref, x_ref, pa_ref, pcnt_ref, pw_ref, psc_ref, wo_ref, g1_ref,
                nw_ref, sh_ref, sc_ref, rw_ref, rb_ref,
                x1_ref, h_ref, rt_ref, tcnt_ref):
    tm = x_ref.shape[1]

    pooled = []
    for g in range(len(POOL_WINDOWS)):
        parts = []
        for r in range(tm // POOL_TM):
            u = u_ref[0, r * POOL_TM:(r + 1) * POOL_TM, g * POOL_GW:(g + 1) * POOL_GW]
            wsum = _dot(pa_ref[g], u)
            parts.append((wsum / pcnt_ref[g] - u.astype(F32)).astype(BF16))
        pooled.append(jnp.concatenate(parts, axis=0))
    mapped = []
    for gp in range(len(POOL_WINDOWS) // 2):
        pair = jnp.concatenate(pooled[2 * gp:2 * gp + 2], axis=1)
        mapped.append((_dot(pair, pw_ref[gp])
                       * psc_ref[:, 2 * gp * POOL_GW:(2 * gp + 2) * POOL_GW]).astype(BF16))
    y_pool = jnp.concatenate(mapped, axis=1)

    mix = _dot(y_ref[0], wo_ref[0:SSD_W, :]) + _dot(y_pool, wo_ref[SSD_W:SSD_W + POOL_W, :])
    x1 = x_ref[0] + g1_ref[0] * mix
    x1_ref[0] = x1

    ms = jnp.mean(x1 * x1, axis=-1, keepdims=True)
    h = x1 * lax.rsqrt(ms + RMS_EPS) * nw_ref[...]
    h = h * (1.0 + sc_ref[0]) + sh_ref[0]
    h_ref[0] = h

    h0, h1 = _split2(h)
    rw2 = jnp.concatenate(_split2(rw_ref[...]), axis=1)
    t0 = _dot(h0, rw2)
    t1 = _dot(h1, rw2)
    logits = (t0[:, 0:LANES] + t0[:, LANES:2 * LANES] + t1[:, 0:LANES] + t1[:, LANES:2 * LANES]
              + rb_ref[...])
    work = logits.T[0:N_EXPERTS, :]
    expert_f = lax.broadcasted_iota(jnp.int32, (N_EXPERTS, tm), 0).astype(F32)
    vals, idxs = [], []
    for _ in range(TOP_K):
        m = jnp.max(work, axis=0, keepdims=True)
        first_idx = jnp.min(jnp.where(work == m, expert_f, float(N_EXPERTS)), axis=0, keepdims=True)
        vals.append(m)
        idxs.append(first_idx)
        work = jnp.where(expert_f == first_idx, 2.0 * NEG_BIG, work)
    exps = [jnp.exp(v - vals[0]) for v in vals]
    denom = exps[0] + exps[1] + exps[2] + exps[3]

    rec_row = lax.broadcasted_iota(jnp.int32, (RT_ROWS, tm), 0)
    rec = jnp.zeros((RT_ROWS, tm), F32)
    onehot = jnp.zeros((N_EXPERTS, tm), F32)
    for k in range(TOP_K):
        rec = jnp.where(rec_row == k, idxs[k], rec)
        rec = jnp.where(rec_row == GATE_LANE + k, exps[k] / denom, rec)
        onehot = onehot + jnp.where(expert_f == idxs[k], 1.0, 0.0)
    rt_ref[0] = rec
    for r in range(tm // DISP_TM):
        tcnt_ref[0, r] = jnp.sum(onehot[:, r * DISP_TM:(r + 1) * DISP_TM], axis=1, keepdims=True)


def _mix(y, u, x, pool_a, pool_cnt, pool_w, pool_scale, w_out, g1, norm_w, shift, scale,
         router_w, router_b):
    b, l, d = x.shape
    tm = MIX_TM
    per_l = l // tm
    tok = lambda n: pl.BlockSpec((1, tm, n), lambda i, j: (i, j, 0))
    per_b = pl.BlockSpec((1, 1, d), lambda i, j: (i, 0, 0))
    full = lambda a: pl.BlockSpec(a.shape, lambda i, j: (0,) * a.ndim)
    return pl.pallas_call(
        _mix_kernel,
        grid=(b, per_l),
        in_specs=[tok(SSD_W), tok(POOL_W), tok(d), full(pool_a), full(pool_cnt), full(pool_w),
                  full(pool_scale), full(w_out), per_b, full(norm_w), per_b, per_b,
                  full(router_w), full(router_b)],
        out_specs=[tok(d), tok(d),
                   pl.BlockSpec((1, RT_ROWS, tm), lambda i, j: (i * per_l + j, 0, 0)),
                   pl.BlockSpec((1, tm // DISP_TM, N_EXPERTS, 1), lambda i, j: (i * per_l + j, 0, 0, 0))],
        out_shape=[jax.ShapeDtypeStruct((b, l, d), F32),
                   jax.ShapeDtypeStruct((b, l, d), F32),
                   jax.ShapeDtypeStruct((b * per_l, RT_ROWS, tm), F32),
                   jax.ShapeDtypeStruct((b * per_l, tm // DISP_TM, N_EXPERTS, 1), F32)],
        compiler_params=_cparams(("arbitrary", "arbitrary")),
    )(y, u, x, pool_a, pool_cnt, pool_w, pool_scale, w_out, g1, norm_w, shift, scale,
      router_w, router_b)


def _group_copies(n_groups, src_ref, src_group, dst_ref, dst_group, sem, wait):
    def pieces(bits):
        for bit in bits:
            @pl.when((n_groups & bit) != 0)
            def _():
                done = n_groups & ~(2 * bit - 1)
                src = src_ref.at[pl.ds(pl.multiple_of((src_group + done) * SUBLANES, SUBLANES),
                                       bit * SUBLANES)]
                dst = dst_ref.at[pl.ds(pl.multiple_of((dst_group + done) * SUBLANES, SUBLANES),
                                       bit * SUBLANES)]
                cp = pltpu.make_async_copy(src, dst, sem)
                cp.wait() if wait else cp.start()

    split = RUN_BITS.index(RUN_SMALL_BIT)
    pl.when(n_groups >= 2 * RUN_SMALL_BIT)(lambda: pieces(RUN_BITS[:split]))
    pieces(RUN_BITS[split:])


def _wait_groups(n_groups, src_ref, dst_ref, sem):
    for bit in WAIT_BITS:
        @pl.when((n_groups & bit) != 0)
        def _():
            rows = bit * SUBLANES
            pltpu.make_async_copy(src_ref.at[pl.ds(0, rows)], dst_ref.at[pl.ds(0, rows)], sem).wait()


def _stage_positions(rt_t, base_col, upper):
    tm = rt_t.shape[1]
    expert_f = lax.broadcasted_iota(jnp.int32, (LANES, tm), 0).astype(F32)
    hots = [expert_f == rt_t[k:k + 1, :] for k in range(TOP_K)]
    onehot = jnp.zeros((LANES, tm), F32)
    for hot in hots:
        onehot = onehot + jnp.where(hot, 1.0, 0.0)
    pos = _dot(onehot.astype(BF16), upper) + base_col
    return [jnp.sum(jnp.where(hot, pos, 0.0), axis=0, keepdims=True) for hot in hots]


def _dispatch_kernel(o_ref, f_ref, fl_ref, rem_ref, hg_ref, tot_ref, used_ref, fs_ref, flen_ref,
                     h_ref, rt_ref, sb_ref, tri_ref, xs_ref,
                     stg2_ref, tails_ref, zero_ref, sem2, fill_sem):
    i = pl.program_id(0)
    tm = h_ref.shape[0]
    n_slots = xs_ref.shape[0]
    base = i * N_EXPERTS
    hi_mask = jnp.int32(-65536)
    slot = i % 2
    stg_ref = stg2_ref.at[slot]
    sem = sem2.at[slot]

    @pl.when(i == 0)
    def _():
        tails_ref[...] = jnp.zeros_like(tails_ref)
        stg2_ref[...] = jnp.zeros_like(stg2_ref)

    pos = _stage_positions(rt_ref[0], sb_ref[0], tri_ref[...])

    hb = h_ref[...].astype(BF16)
    half = PACK_W
    used_rows = used_ref[i] * SUBLANES

    def stage_chunk(r_lo):
        row_f = (lax.broadcasted_iota(jnp.int32, (STAGE_CHUNK, tm), 0) + r_lo).astype(F32)
        sel = jnp.zeros((STAGE_CHUNK, tm), F32)
        for k in range(TOP_K):
            sel = jnp.where(row_f == pos[k], 1.0, sel)
        moved = _dot(sel.astype(BF16), hb)
        lo = pltpu.bitcast(moved[:, 0:half], jnp.int32)
        hi = pltpu.bitcast(moved[:, half:2 * half], jnp.int32)
        stg_ref[r_lo:r_lo + STAGE_CHUNK, :] = lax.shift_right_logical(lo, 16) | (hi & hi_mask)

    always = TOP_K * tm // STAGE_CHUNK
    for rc in range(STAGE_ROWS // STAGE_CHUNK):
        if rc < always:
            stage_chunk(rc * STAGE_CHUNK)
        else:
            pl.when(rc * STAGE_CHUNK < used_rows)(functools.partial(stage_chunk, rc * STAGE_CHUNK))

    def finish_run(e, cr):
        stage_group = o_ref[base + e]
        first = pl.multiple_of(stage_group * SUBLANES, SUBLANES)
        stg_ref[pl.ds(first, SUBLANES), :] = stg_ref[pl.ds(first, SUBLANES), :] | tails_ref[e]
        part = pl.multiple_of((stage_group + f_ref[base + e]) * SUBLANES, SUBLANES)
        tails_ref[e] = jnp.where(rem_ref[base + e] > 0, stg_ref[pl.ds(part, SUBLANES), :], 0)
        _group_copies(fl_ref[base + e], stg_ref, stage_group, xs_ref, hg_ref[base + e], sem, False)
        return cr

    lax.fori_loop(0, N_EXPERTS, finish_run, 0)

    @pl.when(i == pl.num_programs(0) - 1)
    def _():
        zero_ref[...] = jnp.zeros_like(zero_ref)
        tail_group = fs_ref[N_EXPERTS]
        n_tail = (n_slots // SUBLANES - tail_group) // (MOE_BLK // SUBLANES)

        def tail_copy(j, wait):
            off = pl.multiple_of(tail_group * SUBLANES + j * MOE_BLK, MOE_BLK)
            cp = pltpu.make_async_copy(zero_ref, xs_ref.at[pl.ds(off, MOE_BLK)], fill_sem)
            cp.wait() if wait else cp.start()

        for wait in (False, True):
            def pad_body(e, cr):
                _group_copies(flen_ref[e], zero_ref, 0, xs_ref, fs_ref[e], fill_sem, wait)
                return cr
            lax.fori_loop(0, N_EXPERTS, pad_body, 0)
            lax.fori_loop(0, n_tail, lambda j, cr: (tail_copy(j, wait), cr)[1], 0)

    @pl.when(i > 0)
    def _():
        _wait_groups(tot_ref[jnp.maximum(i - 1, 0)], stg2_ref.at[1 - slot], xs_ref, sem2.at[1 - slot])

    @pl.when(i == pl.num_programs(0) - 1)
    def _():
        _wait_groups(tot_ref[i], stg_ref, xs_ref, sem)


def _dispatch(tables, fill_groups, fill_len_groups, h, rt, stage_base, n_slots):
    t, d = h.shape
    tm = DISP_TM
    rt_per = rt.shape[2] // tm
    grid_spec = pltpu.PrefetchScalarGridSpec(
        num_scalar_prefetch=len(tables) + 2,
        grid=(t // tm,),
        in_specs=[pl.BlockSpec((tm, d), lambda i, *_: (i, 0)),
                  pl.BlockSpec((1, RT_ROWS, tm), lambda i, *_: (i // rt_per, 0, i % rt_per)),
                  pl.BlockSpec((1, LANES, 1), lambda i, *_: (i, 0, 0)),
                  pl.BlockSpec((tm, tm), lambda i, *_: (0, 0))],
        out_specs=pl.BlockSpec(memory_space=pl.ANY),
        scratch_shapes=[pltpu.VMEM((2, STAGE_ROWS, XS_W), jnp.int32),
                        pltpu.VMEM((N_EXPERTS, SUBLANES, XS_W), jnp.int32),
                        pltpu.VMEM((MOE_BLK, XS_W), jnp.int32),
                        pltpu.SemaphoreType.DMA((2,)), pltpu.SemaphoreType.DMA(())],
    )
    return pl.pallas_call(
        _dispatch_kernel,
        grid_spec=grid_spec,
        out_shape=jax.ShapeDtypeStruct((n_slots, XS_W), jnp.int32),
        compiler_params=_cparams(("arbitrary",)),
    )(*tables, fill_groups, fill_len_groups, h, rt, stage_base, _strict_upper(tm))


def _expert_kernel(be_ref, nused_ref, xs_ref, w1_ref, b1_ref, w2_ref, b2_ref, perm_ref, y_ref,
                   w1s_ref, w2s_ref):
    i = pl.program_id(0)
    prev = be_ref[jnp.maximum(i - 1, 0)]
    changed = (i == 0) | (be_ref[i] != prev)
    dff2 = w1_ref.shape[2]
    tile = 2 * LANES

    @pl.when(changed & (i < nused_ref[0]))
    def _():
        for j in range(dff2 // tile):
            wj = w1_ref[0, :, j * tile:(j + 1) * tile].astype(BF16)
            w1s_ref[:, j * tile:(j + 1) * tile] = _dot(wj, perm_ref[...]).astype(BF16)
        w2s_ref[...] = w2_ref[0].astype(BF16)

    @pl.when(i < nused_ref[0])
    def _():
        hi_mask = jnp.int32(-65536)
        words = xs_ref[...]
        x = jnp.concatenate(
            [pltpu.bitcast(lax.shift_left(words, 16), F32).astype(BF16),
             pltpu.bitcast(words & hi_mask, F32).astype(BF16)], axis=1)
        acts = []
        for j in range(dff2 // tile):
            hb = _dot(x, w1s_ref[:, j * tile:(j + 1) * tile]) + b1_ref[0, :, j * tile:(j + 1) * tile]
            gp = jnp.minimum(hb[:, 0:LANES], SWIGLU_LIMIT)
            up = jnp.clip(hb[:, LANES:tile], -SWIGLU_LIMIT, SWIGLU_LIMIT)
            acts.append((gp * _sigmoid(SWIGLU_ALPHA * gp) * (up + 1.0)).astype(BF16))
        act = jnp.concatenate(acts, axis=1)
        y = _dot(act, w2s_ref[...]) + b2_ref[0]
        lo = pltpu.bitcast(y[:, 0:PACK_W].astype(BF16).astype(F32), jnp.int32)
        hi = pltpu.bitcast(y[:, PACK_W:2 * PACK_W].astype(BF16).astype(F32), jnp.int32)
        y_ref[...] = lax.shift_right_logical(lo, 16) | (hi & hi_mask)

    @pl.when(i >= nused_ref[0])
    def _():
        y_ref[...] = jnp.zeros_like(y_ref)


def _experts(block_exp, n_used, xs, w1, b1p, w2, b2, perm, n_blocks):
    blk = MOE_BLK
    d = w1.shape[1]
    dff2 = w1.shape[2]
    dff = w2.shape[1]

    def x_idx(i, be, nu):
        return (jnp.minimum(i, nu[0] - 1), 0)

    grid_spec = pltpu.PrefetchScalarGridSpec(
        num_scalar_prefetch=2,
        grid=(n_blocks,),
        in_specs=[pl.BlockSpec((blk, XS_W), x_idx),
                  pl.BlockSpec((1, d, dff2), lambda i, be, nu: (be[i], 0, 0)),
                  pl.BlockSpec((1, 1, dff2), lambda i, be, nu: (be[i], 0, 0)),
                  pl.BlockSpec((1, dff, d), lambda i, be, nu: (be[i], 0, 0)),
                  pl.BlockSpec((1, 1, d), lambda i, be, nu: (be[i], 0, 0)),
                  pl.BlockSpec(perm.shape, lambda i, be, nu: (0, 0))],
        out_specs=pl.BlockSpec((blk, PACK_W), lambda i, be, nu: (i, 0)),
        scratch_shapes=[pltpu.VMEM((d, dff2), BF16), pltpu.VMEM((dff, d), BF16)],
    )
    return pl.pallas_call(
        _expert_kernel,
        grid_spec=grid_spec,
        out_shape=jax.ShapeDtypeStruct((n_blocks * blk, PACK_W), jnp.int32),
        compiler_params=_cparams(("arbitrary",)),
    )(block_exp, n_used, xs, w1, b1p, w2, b2, perm)


def _combine_kernel(o_ref, g_ref, hg_ref, tot_ref, x1_ref, rt_ref, sb_ref, tri_ref, g2_ref, fnw_ref,
                    ys_ref, out_ref, stg_ref, sem):
    tm = x1_ref.shape[1]
    step = pl.program_id(0) * pl.num_programs(1) + pl.program_id(1)
    n_steps = pl.num_programs(0) * pl.num_programs(1)
    slot = step % 2
    hi_mask = jnp.int32(-65536)

    def fetch(tile, into):
        def body(e, cr):
            _group_copies(g_ref[tile * N_EXPERTS + e], ys_ref, hg_ref[tile * N_EXPERTS + e],
                          stg_ref.at[into], o_ref[tile * N_EXPERTS + e], sem.at[into], False)
            return cr
        lax.fori_loop(0, N_EXPERTS, body, 0)

    @pl.when(step == 0)
    def _():
        stg_ref[...] = jnp.zeros_like(stg_ref)
        fetch(0, 0)

    @pl.when(step + 1 < n_steps)
    def _():
        fetch(step + 1, 1 - slot)

    _wait_groups(tot_ref[step], ys_ref, stg_ref.at[slot], sem.at[slot])

    rt_t = rt_ref[0]
    pos_rows = _stage_positions(rt_t, sb_ref[0], tri_ref[...])
    expert_i = lax.broadcasted_iota(jnp.int32, (LANES, tm), 0)
    stacked = jnp.zeros((LANES, tm), F32)
    for k in range(TOP_K):
        stacked = jnp.where(expert_i == k, pos_rows[k], stacked)
        stacked = jnp.where(expert_i == TOP_K + k, rt_t[GATE_LANE + k:GATE_LANE + k + 1, :], stacked)
    cols = stacked.T

    def unpermute(n_rows):
        col_f = lax.broadcasted_iota(jnp.int32, (tm, n_rows), 1).astype(F32)
        sel = jnp.zeros((tm, n_rows), F32)
        for k in range(TOP_K):
            sel = jnp.where(col_f == cols[:, k:k + 1], cols[:, TOP_K + k:TOP_K + k + 1], sel)
        sel = sel.astype(BF16)
        words = stg_ref[slot, 0:n_rows, :]
        lo = pltpu.bitcast(lax.shift_left(words, 16), F32).astype(BF16)
        hi = pltpu.bitcast(words & hi_mask, F32).astype(BF16)
        moe = jnp.concatenate([_dot(sel, lo), _dot(sel, hi)], axis=1)
        x2 = x1_ref[0] + g2_ref[0] * moe
        ms = jnp.mean(x2 * x2, axis=-1, keepdims=True)
        out_ref[0] = x2 * lax.rsqrt(ms + RMS_EPS) * fnw_ref[...]

    used_rows = tot_ref[step] * SUBLANES
    short = STAGE_ROWS - STAGE_CHUNK
    pl.when(used_rows <= short)(functools.partial(unpermute, short))
    pl.when(used_rows > short)(functools.partial(unpermute, STAGE_ROWS))


def _combine(tables, x1, rt, stage_base, g2, fnw, ys):
    b, l, d = x1.shape
    tm = DISP_TM
    per_l = l // tm
    rt_per = rt.shape[2] // tm
    tok = lambda n: pl.BlockSpec((1, tm, n), lambda i, j, *_: (i, j, 0))
    grid_spec = pltpu.PrefetchScalarGridSpec(
        num_scalar_prefetch=len(tables),
        grid=(b, per_l),
        in_specs=[tok(d),
                  pl.BlockSpec((1, RT_ROWS, tm),
                               lambda i, j, *_: ((i * per_l + j) // rt_per, 0, (i * per_l + j) % rt_per)),
                  pl.BlockSpec((1, LANES, 1), lambda i, j, *_: (i * per_l + j, 0, 0)),
                  pl.BlockSpec((tm, tm), lambda i, j, *_: (0, 0)),
                  pl.BlockSpec((1, 1, d), lambda i, j, *_: (i, 0, 0)),
                  pl.BlockSpec(fnw.shape, lambda i, j, *_: (0, 0)),
                  pl.BlockSpec(memory_space=pl.ANY)],
        out_specs=tok(d),
        scratch_shapes=[pltpu.VMEM((2, STAGE_ROWS, PACK_W), jnp.int32),
                        pltpu.SemaphoreType.DMA((2,))],
    )
    return pl.pallas_call(
        _combine_kernel,
        grid_spec=grid_spec,
        out_shape=jax.ShapeDtypeStruct((b, l, d), F32),
        compiler_params=_cparams(("arbitrary", "arbitrary")),
    )(*tables, x1, rt, stage_base, _strict_upper(tm), g2, fnw, ys)


def _pool_constants():
    i = np.arange(POOL_TM)[:, None]
    j = np.arange(POOL_TM)[None, :]
    same_row = (i // GRID_W) == (j // GRID_W)
    mats, cnts = [], []
    for w in POOL_WINDOWS:
        band = same_row & (j - i >= -(w // 2)) & (j - i < w - w // 2)
        mats.append(band)
        cnts.append(np.broadcast_to(band.sum(axis=1, keepdims=True), (POOL_TM, POOL_GW)))
    return (jnp.asarray(np.stack(mats), BF16), jnp.asarray(np.stack(cnts), F32))


def _head_expand():
    e = np.zeros((LANES, SSD_W), np.float32)
    for h in range(SSD_HEADS):
        e[h, h * SSD_HEADDIM:(h + 1) * SSD_HEADDIM] = 1.0
    return jnp.asarray(e, BF16)


def _deinterleave_perm():
    n = 2 * LANES
    p = np.zeros((n, n), np.float32)
    for k in range(LANES):
        p[2 * k, k] = 1.0
        p[2 * k + 1, LANES + k] = 1.0
    return jnp.asarray(p, BF16)


def _strict_upper(n):
    return jnp.asarray(np.triu(np.ones((n, n), np.float32), 1), BF16)


def _pad_lanes(a, n):
    return jnp.pad(a, [(0, 0)] * (a.ndim - 1) + [(0, n - a.shape[-1])])


def kernel(x, c, ctx, c_ctx, w_mod, b_mod, norm1_w, norm2_w, w_in, conv_w, conv_b, dt_bias, a_log,
           d_skip, ssd_norm_w, pool_w, pool_scale, w_out, router_w, router_b, w1, b1, w2, b2,
           final_norm_w):
    depth = w_mod.shape[0]
    assert depth == 1, "single-layer problem"
    b, l, d = x.shape
    lc = ctx.shape[1]
    xbcdt = CONV_CH + 2 * SSD_HEADS

    mod_rows = 2 * SUBLANES
    cc = jnp.zeros((mod_rows, d), F32).at[0:b].set(c).at[b].set(c_ctx)
    mod = _modulation(cc, w_mod[0], b_mod[0])
    sh1, sc1, g1, sh2, sc2, g2 = [m.reshape(b, 1, d) for m in jnp.split(mod[0:b], 6, axis=-1)]
    csh1, csc1 = [jnp.broadcast_to(m.reshape(1, 1, d), (b, 1, d))
                  for m in jnp.split(mod[b:b + 1], 6, axis=-1)[0:2]]

    wi = w_in[0]
    wx = wi[:, 0:CONV_CH].astype(BF16)
    wd = jnp.concatenate([_pad_lanes(wi[:, CONV_CH:CONV_CH + SSD_HEADS], LANES),
                          _pad_lanes(wi[:, CONV_CH + SSD_HEADS:xbcdt], LANES)], axis=1).astype(BF16)
    wz = wi[:, xbcdt:xbcdt + SSD_W].astype(BF16)
    wp = wi[:, xbcdt + SSD_W:].astype(BF16)
    n1 = norm1_w[0].reshape(1, d)
    conv_w8 = jnp.pad(conv_w[0], ((0, SUBLANES - SSD_CONV), (0, 0)))
    conv_b1 = conv_b[0].reshape(1, CONV_CH)
    dtb = _pad_lanes(dt_bias[0], LANES).reshape(2, 1, LANES)
    alog = _pad_lanes(a_log[0], LANES).reshape(2, 1, LANES)
    dsk = jnp.repeat(d_skip[0], SSD_HEADDIM).reshape(1, SSD_W)
    snw = ssd_norm_w[0].reshape(1, SSD_W)
    expand = _head_expand()

    xbc_c, dt_c = _inproj(ctx, n1, csh1, csc1, (wx, wd), (BF16, F32), min(PROJ_TM, lc))
    zero_state = jnp.zeros((b, 2, SSD_GROUPS, SSD_STATE, GROUP_W), F32)
    _, ctx_states = _ssd(xbc_c, dt_c, None, conv_w8, conv_b1, dtb, alog, dsk, snw, expand, zero_state)

    xbc, dt, z, u_pool = _inproj(x, n1, sh1, sc1, (wx, wd, wz, wp), (BF16, F32, BF16, BF16), PROJ_TM)
    y_ssd, _ = _ssd(xbc, dt, z, conv_w8, conv_b1, dtb, alog, dsk, snw, expand, ctx_states)

    pool_a, pool_cnt = _pool_constants()
    pw = pool_w[0].astype(BF16)
    zero_gw = jnp.zeros((POOL_GW, POOL_GW), BF16)
    pw_pairs = jnp.stack([jnp.block([[pw[2 * gp], zero_gw], [zero_gw, pw[2 * gp + 1]]])
                          for gp in range(len(POOL_WINDOWS) // 2)])
    x1, h2, rt, tcnt = _mix(
        y_ssd, u_pool, x, pool_a, pool_cnt, pw_pairs, pool_scale[0].reshape(1, POOL_W),
        w_out[0].astype(BF16), g1, norm2_w[0].reshape(1, d), sh2, sc2,
        _pad_lanes(router_w[0], LANES), _pad_lanes(router_b[0].reshape(1, N_EXPERTS), LANES))

    i32 = jnp.int32
    t = b * l
    n_tiles = t // DISP_TM
    tc = tcnt.reshape(n_tiles, N_EXPERTS).astype(i32)
    counts = jnp.sum(tc, axis=0)
    run_start = jnp.cumsum(tc, axis=0) - tc
    padded = (counts + MOE_BLK - 1) // MOE_BLK * MOE_BLK
    pad_end = jnp.cumsum(padded)
    pad_start = pad_end - padded
    carried = run_start % SUBLANES
    span = carried + tc
    groups = (span + SUBLANES - 1) // SUBLANES
    full = span // SUBLANES
    stage_group = jnp.cumsum(groups, axis=1) - groups
    slot_group = (pad_start[None, :] + run_start - carried) // SUBLANES
    flush = full.at[n_tiles - 1].set(groups[n_tiles - 1])
    flat = lambda a: a.reshape(n_tiles * N_EXPERTS).astype(i32)
    stage_base = _pad_lanes((stage_group * SUBLANES + carried).astype(F32), LANES)
    stage_base = stage_base.reshape(n_tiles, LANES, 1)
    n_blocks = (t * TOP_K) // MOE_BLK + N_EXPERTS
    n_used = (pad_end[-1] // MOE_BLK).astype(i32).reshape(1)
    blk_start = jnp.minimum(jnp.arange(n_blocks, dtype=i32), n_used[0] - 1) * MOE_BLK
    block_exp = jnp.minimum(jnp.sum(blk_start[:, None] >= pad_end[None, :], axis=1),
                            N_EXPERTS - 1).astype(i32)
    written = (counts + SUBLANES - 1) // SUBLANES * SUBLANES
    fill_group = (jnp.concatenate([pad_start + written, pad_end[-1:]]) // SUBLANES).astype(i32)
    fill_groups = (jnp.concatenate([padded - written, jnp.zeros((1,), i32)]) // SUBLANES).astype(i32)

    xs = _dispatch((flat(stage_group), flat(full), flat(flush), flat(span % SUBLANES), flat(slot_group),
                    jnp.sum(flush, axis=1).astype(i32), jnp.sum(groups, axis=1).astype(i32)),
                   fill_group, fill_groups, h2.reshape(t, d), rt, stage_base,
                   n_blocks * MOE_BLK)
    dff2 = w1.shape[-1]
    b1p = jnp.concatenate(
        [b1[0].reshape(N_EXPERTS, dff2 // (2 * LANES), LANES, 2)[..., 0],
         b1[0].reshape(N_EXPERTS, dff2 // (2 * LANES), LANES, 2)[..., 1]], axis=-1
    ).reshape(N_EXPERTS, 1, dff2)
    ys = _experts(block_exp, n_used, xs, w1[0], b1p, w2[0], b2[0].reshape(N_EXPERTS, 1, d),
                  _deinterleave_perm(), n_blocks)
    return _combine((flat(stage_group), flat(groups), flat(slot_group),
                     jnp.sum(groups, axis=1).astype(i32)), x1, rt, stage_base, g2,
                    final_norm_w.reshape(1, d), ys)
```

```python
import functools

import numpy as np
import jax
import jax.numpy as jnp
from jax import lax
from jax.experimental import pallas as pl
from jax.experimental.pallas import tpu as pltpu

F32 = jnp.float32
BF16 = jnp.bfloat16

SSD_HEADDIM = 64
SSD_GROUPS = 4
SSD_HPG = 6
SSD_HEADS = SSD_GROUPS * SSD_HPG
SSD_STATE = 128
SSD_CONV = 5
SSD_CHUNK = 128
SSD_W = SSD_HEADS * SSD_HEADDIM
GROUP_W = SSD_HPG * SSD_HEADDIM
CONV_CH = SSD_W + 2 * SSD_GROUPS * SSD_STATE
POOL_WINDOWS = (2, 4, 8, 16)
POOL_GW = 128
POOL_W = POOL_GW * len(POOL_WINDOWS)
GRID_W = 64
N_EXPERTS = 32
TOP_K = 4
SWIGLU_ALPHA = 1.702
SWIGLU_LIMIT = 7.0
RMS_EPS = 1e-6
LOG2_E = 1.4426950408889634

LANES = 128
SUBLANES = 8
VMEM_LIMIT_BYTES = 56 * 1024 * 1024

MOD_TN = 1024
PROJ_TM = 512
MIX_TM = 512
POOL_TM = 256
MOE_BLK = 512
DISP_TM = 256
NEG_BIG = -1e30

GATE_LANE = 2 * TOP_K
RT_ROWS = 2 * SUBLANES
PACK_W = 512
XS_W = PACK_W
STAGE_CHUNK = 256
_STAGE_NEED = TOP_K * DISP_TM + N_EXPERTS * 2 * (SUBLANES - 1) + SUBLANES
STAGE_ROWS = -(-_STAGE_NEED // STAGE_CHUNK) * STAGE_CHUNK
RUN_BITS = (32, 16, 8, 4, 2, 1)
RUN_SMALL_BIT = 4
WAIT_BITS = (128, 64, 32, 16, 8, 4, 2, 1)


def _sigmoid(x):
    return 0.5 * jnp.tanh(0.5 * x) + 0.5


def _split2(a):
    hi = a.astype(BF16)
    mid = (a - hi.astype(F32)).astype(BF16)
    return hi, mid


def _split3(a):
    hi = a.astype(BF16)
    r = a - hi.astype(F32)
    mid = r.astype(BF16)
    lo = (r - mid.astype(F32)).astype(BF16)
    return hi, mid, lo


def _dot(a, b):
    return jnp.dot(a, b, preferred_element_type=F32)


def _dot_exact_rhs(a_f32, b_bf16, parts):
    pieces = _split3(a_f32) if parts == 3 else _split2(a_f32)
    out = _dot(pieces[0], b_bf16)
    for p in pieces[1:]:
        out = out + _dot(p, b_bf16)
    return out


def _dot_hi(a_f32, b_f32):
    a0, a1, a2 = _split3(a_f32)
    b0, b1, b2 = _split3(b_f32)
    out = _dot(a0, b0)
    out = out + _dot(a0, b1) + _dot(a1, b0)
    out = out + _dot(a1, b1) + _dot(a0, b2) + _dot(a2, b0)
    return out


def _cparams(sem):
    return pltpu.CompilerParams(dimension_semantics=sem, vmem_limit_bytes=VMEM_LIMIT_BYTES)


def _mod_kernel(c_ref, w_ref, b_ref, o_ref):
    c = c_ref[...]
    s = c * _sigmoid(c)
    o_ref[...] = _dot_hi(s, w_ref[...]) + b_ref[...]


def _modulation(cc, w_mod, b_mod):
    rows, d = cc.shape
    n = w_mod.shape[1]
    return pl.pallas_call(
        _mod_kernel,
        grid=(n // MOD_TN,),
        in_specs=[pl.BlockSpec((rows, d), lambda j: (0, 0)),
                  pl.BlockSpec((d, MOD_TN), lambda j: (0, j)),
                  pl.BlockSpec((1, MOD_TN), lambda j: (0, j))],
        out_specs=pl.BlockSpec((rows, MOD_TN), lambda j: (0, j)),
        out_shape=jax.ShapeDtypeStruct((rows, n), F32),
        compiler_params=_cparams(("arbitrary",)),
    )(cc, w_mod, b_mod.reshape(1, n))


def _inproj_kernel(n_groups, x0_ref, xn_ref, nw_ref, sh0_ref, sc0_ref, shn_ref, scn_ref, *refs):
    w_refs = refs[0:n_groups]
    out_refs = refs[n_groups:2 * n_groups]
    hb_even_ref, hb_odd_ref = refs[2 * n_groups:]
    step = pl.program_id(0) * pl.num_programs(1) + pl.program_id(1)

    def normed(x, sh, sc):
        ms = jnp.mean(x * x, axis=-1, keepdims=True)
        h = x * lax.rsqrt(ms + RMS_EPS) * nw_ref[...]
        return (h * (1.0 + sc) + sh).astype(BF16)

    @pl.when(step == 0)
    def _():
        hb_even_ref[...] = normed(x0_ref[0], sh0_ref[0], sc0_ref[0])

    def project(cur_ref, nxt_ref):
        hb = cur_ref[...]
        for w_ref, out_ref in zip(w_refs, out_refs):
            out_ref[0] = _dot(hb, w_ref[...]).astype(out_ref.dtype)
        nxt_ref[...] = normed(xn_ref[0], shn_ref[0], scn_ref[0])

    pl.when(step % 2 == 0)(functools.partial(project, hb_even_ref, hb_odd_ref))
    pl.when(step % 2 == 1)(functools.partial(project, hb_odd_ref, hb_even_ref))


def _inproj(x, norm_w, shift, scale, weights, out_dtypes, tm):
    b, l, d = x.shape
    per_l = l // tm
    last = b * per_l - 1

    def nxt(i, j):
        s = jnp.minimum(i * per_l + j + 1, last)
        return s // per_l, s % per_l

    full = lambda a: pl.BlockSpec(a.shape, lambda i, j: (0, 0))
    tok = lambda n: pl.BlockSpec((1, tm, n), lambda i, j: (i, j, 0))
    first_b = pl.BlockSpec((1, 1, d), lambda i, j: (0, 0, 0))
    next_b = pl.BlockSpec((1, 1, d), lambda i, j: (nxt(i, j)[0], 0, 0))
    return pl.pallas_call(
        functools.partial(_inproj_kernel, len(weights)),
        grid=(b, per_l),
        in_specs=[pl.BlockSpec((1, tm, d), lambda i, j: (0, 0, 0)),
                  pl.BlockSpec((1, tm, d), lambda i, j: (*nxt(i, j), 0)),
                  full(norm_w), first_b, first_b, next_b, next_b] + [full(w) for w in weights],
        out_specs=[tok(w.shape[1]) for w in weights],
        out_shape=[jax.ShapeDtypeStruct((b, l, w.shape[1]), dt)
                   for w, dt in zip(weights, out_dtypes)],
        scratch_shapes=[pltpu.VMEM((tm, d), BF16), pltpu.VMEM((tm, d), BF16)],
        compiler_params=_cparams(("arbitrary", "arbitrary")),
    )(x, x, norm_w, shift, scale, shift, scale, *weights)


def _ssd_kernel(nc, emit_y, *refs):
    n_fac = 5 if emit_y else 3
    fac_even, fac_odd = refs[-2 * n_fac:-n_fac], refs[-n_fac:]
    if emit_y:
        (xbc_ref, prev_ref, next_ref, dt0_ref, dtn_ref, z_ref, cw_ref, cb_ref, dtb0_ref, alog0_ref,
         dtbn_ref, alogn_ref, dsk_ref, nw_ref, exp_ref, init_ref, y_ref, fin_ref,
         act_ref, ybuf_ref, yf_ref, st_ref) = refs[:-2 * n_fac]
        y_part = (z_ref, dsk_ref, nw_ref, y_ref, ybuf_ref, yf_ref)
    else:
        (xbc_ref, prev_ref, next_ref, dt0_ref, dtn_ref, cw_ref, cb_ref, dtb0_ref, alog0_ref,
         dtbn_ref, alogn_ref, exp_ref, init_ref, fin_ref, act_ref, st_ref) = refs[:-2 * n_fac]
        y_part = None
    ph = pl.program_id(1)
    c = pl.program_id(2)
    ci = c + ph * (nc - 1 - 2 * c)
    step = (pl.program_id(0) * 2 + ph) * nc + c
    next_step = jnp.minimum(step + 1, pl.num_programs(0) * 2 * nc - 1)
    next_ph = (next_step // nc) % 2

    @pl.when(c == 0)
    def _():
        st_ref[...] = init_ref[0, ph]

    @pl.when(ph == 0)
    def _():
        _conv_silu(nc, ci, xbc_ref, prev_ref, next_ref, cw_ref, cb_ref, act_ref)

    def store_factors(fac_refs, values):
        for ref, val in zip(fac_refs, values):
            ref[...] = val

    @pl.when(step == 0)
    def _():
        store_factors(fac_even, _chunk_factors(dt0_ref[0], dtb0_ref[0], alog0_ref[0], 0,
                                               exp_ref[...], emit_y))

    def run(cur, nxt):
        store_factors(nxt, _chunk_factors(dtn_ref[0], dtbn_ref[0], alogn_ref[0], next_ph,
                                          exp_ref[...], emit_y))
        _ssd_scan(nc, ph, c, ci, cur, act_ref, st_ref, fin_ref, y_part)

    pl.when(step % 2 == 0)(functools.partial(run, fac_even, fac_odd))
    pl.when(step % 2 == 1)(functools.partial(run, fac_odd, fac_even))


def _conv_silu(nc, ci, xbc_ref, prev_ref, next_ref, cw_ref, cb_ref, act_ref):
    ch = SSD_CHUNK
    has_prev = ci > 0
    has_next = ci < nc - 1
    row8 = lax.broadcasted_iota(jnp.int32, (SUBLANES, LANES), 0)
    half = SSD_CONV // 2
    for j in range(CONV_CH // LANES):
        cols = slice(j * LANES, (j + 1) * LANES)
        n_t = ch // SUBLANES
        tiles = [jnp.where(has_prev, prev_ref[0, SUBLANES:2 * SUBLANES, cols].astype(F32), 0.0)]
        for i in range(n_t // 2):
            both = xbc_ref[0, 2 * i * SUBLANES:2 * (i + 1) * SUBLANES, cols].astype(F32)
            tiles += [both[0:SUBLANES], both[SUBLANES:2 * SUBLANES]]
        tiles.append(jnp.where(has_next, next_ref[0, 0:SUBLANES, cols].astype(F32), 0.0))
        acc = [cb_ref[:, cols] + cw_ref[half:half + 1, cols] * tiles[i + 1] for i in range(n_t)]
        for s in range(1, half + 1):
            rot = [pltpu.roll(tl, s, axis=0) for tl in tiles[0:n_t + 1]]
            wk = cw_ref[half - s:half - s + 1, cols]
            for i in range(n_t):
                acc[i] = acc[i] + wk * jnp.where(row8 < s, rot[i], rot[i + 1])
            rot = [pltpu.roll(tl, SUBLANES - s, axis=0) for tl in tiles[1:n_t + 2]]
            wk = cw_ref[half + s:half + s + 1, cols]
            for i in range(n_t):
                acc[i] = acc[i] + wk * jnp.where(row8 >= SUBLANES - s, rot[i + 1], rot[i])
        silu = [0.5 * v * jnp.tanh(0.5 * v) + 0.5 * v for v in acc]
        act_ref[ci, :, cols] = jnp.concatenate(silu, axis=0).astype(BF16)


def _scan_mask(backward):
    row = lax.broadcasted_iota(jnp.int32, (SSD_CHUNK, SSD_CHUNK), 0)
    col = lax.broadcasted_iota(jnp.int32, (SSD_CHUNK, SSD_CHUNK), 1)
    return (row - col) * (1 - 2 * backward) >= 0


def _chunk_factors(dt_blk, dtb, alog, backward, expand, emit_y):
    ch = SSD_CHUNK
    dtr = dt_blk + dtb
    small = jnp.exp(-jnp.abs(dtr))
    one_plus = 1.0 + small
    log1p_small = jnp.where(one_plus == 1.0, small, jnp.log(one_plus) * (small / (one_plus - 1.0)))
    dtv = jnp.maximum(dtr, 0.0) + log1p_small
    a = dtv * (-jnp.exp(alog) * LOG2_E)
    tri = jnp.where(_scan_mask(backward), 1.0, 0.0).astype(BF16)
    cs = _dot_exact_rhs_left(tri, a)
    tot = jnp.where(backward == 0, cs[ch - 1:ch, :], cs[0:1, :])
    e_dec = jnp.exp2(tot - cs)
    e_tot = jnp.exp2(tot)
    wdec_x = _dot((dtv * e_dec).astype(BF16), expand).astype(BF16)
    etot_x = _dot_exact_rhs(jnp.broadcast_to(e_tot, (SUBLANES, LANES)), expand, 2)
    if not emit_y:
        return cs, wdec_x, etot_x
    ecs_x = _dot(jnp.exp2(cs).astype(BF16), expand)
    src_t = (cs - jnp.log2(dtv)).T
    return cs, wdec_x, etot_x, ecs_x, src_t


def _ssd_scan(nc, ph, c, ci, fac_refs, act_ref, st_ref, fin_ref, y_part):
    ch = SSD_CHUNK
    emit_y = y_part is not None
    wdec_x = fac_refs[1][...]
    etot_x = fac_refs[2][0:1, :]
    if emit_y:
        z_ref, dsk_ref, nw_ref, y_ref, ybuf_ref, yf_ref = y_part
        cs = fac_refs[0][...]
        ecs_x = fac_refs[3][...]
        src_t = fac_refs[4][...]
        tmask = _scan_mask(ph)

    lane = lax.broadcasted_iota(jnp.int32, (ch, LANES), 1)
    for g in range(SSD_GROUPS):
        gs = slice(g * GROUP_W, (g + 1) * GROUP_W)
        b_bf = act_ref[ci, :, SSD_W + g * SSD_STATE:SSD_W + (g + 1) * SSD_STATE]
        x_bf = act_ref[ci, :, gs]
        s_prev = st_ref[g]
        if emit_y:
            c_bf = act_ref[ci, :, SSD_W + (SSD_GROUPS + g) * SSD_STATE:
                           SSD_W + (SSD_GROUPS + g + 1) * SSD_STATE]
            cb = lax.dot_general(c_bf, b_bf, (((1,), (1,)), ((), ())), preferred_element_type=F32)
            y_off = _dot(c_bf, s_prev.astype(BF16)) * ecs_x[:, gs]
        x_dec = x_bf * wdec_x[:, gs]
        st_ref[g] = s_prev * etot_x[:, gs] + lax.dot_general(
            b_bf, x_dec, (((0,), (0,)), ((), ())), preferred_element_type=F32)
        if not emit_y:
            continue
        for q in range(SSD_HPG // 2):
            lmats = []
            for h in (g * SSD_HPG + 2 * q, g * SSD_HPG + 2 * q + 1):
                diff = cs[:, h:h + 1] - src_t[h:h + 1, :]
                dec = jnp.exp2(jnp.where(tmask, diff, NEG_BIG))
                lmats.append((dec * cb).astype(BF16))
            xp = x_bf[:, q * LANES:(q + 1) * LANES]
            zero = jnp.zeros_like(xp)
            rhs = jnp.concatenate([jnp.where(lane < SSD_HEADDIM, xp, zero),
                                   jnp.where(lane >= SSD_HEADDIM, xp, zero)], axis=0)
            y_diag = _dot(jnp.concatenate(lmats, axis=1), rhs)
            ps = slice(g * GROUP_W + q * LANES, g * GROUP_W + (q + 1) * LANES)
            ybuf_ref[:, ps] = y_diag + y_off[:, q * LANES:(q + 1) * LANES]

    if emit_y:
        @pl.when(ph == 0)
        def _():
            yf_ref[ci] = ybuf_ref[...].astype(BF16)

        @pl.when(ph == 1)
        def _():
            yt = (yf_ref[ci].astype(F32) + ybuf_ref[...]
                  + act_ref[ci, :, 0:SSD_W].astype(F32) * dsk_ref[...])
            hz = 0.5 * z_ref[0].astype(F32)
            gt = yt * (hz * jnp.tanh(hz) + hz)
            ms = jnp.mean(gt * gt, axis=-1, keepdims=True)
            y_ref[0] = (gt * lax.rsqrt(ms + RMS_EPS) * nw_ref[...]).astype(y_ref.dtype)

    @pl.when(c == nc - 1)
    def _():
        fin_ref[0, ph] = st_ref[...]


def _dot_exact_rhs_left(sel_bf16, a_f32):
    hi, mid, lo = _split3(a_f32)
    return _dot(sel_bf16, hi) + _dot(sel_bf16, mid) + _dot(sel_bf16, lo)


def _ssd(xbc, dt, z, conv_w8, conv_b, dt_bias, a_log, d_skip_x, norm_w, expand, init):
    emit_y = z is not None
    b, l, _ = xbc.shape
    ch = SSD_CHUNK
    nc = l // ch
    halo = 2 * SUBLANES
    per_ch = ch // halo

    def cidx(ph, c):
        return c + ph * (nc - 1 - 2 * c)

    def xidx(ph, c):
        return jnp.where(ph == 0, c, nc - 1)

    def out_idx(ph, c):
        return jnp.where(ph == 0, nc - 1, nc - 1 - c)

    def nxt(i, ph, c):
        s = jnp.minimum((i * 2 + ph) * nc + c + 1, b * 2 * nc - 1)
        ph_n = (s // nc) % 2
        return s // (2 * nc), cidx(ph_n, s % nc), ph_n

    full2 = lambda a: pl.BlockSpec(a.shape, lambda i, ph, c: (0, 0))
    st_spec = pl.BlockSpec((1, 2, SSD_GROUPS, SSD_STATE, GROUP_W), lambda i, ph, c: (i, 0, 0, 0, 0))
    first_phase = pl.BlockSpec((1, 1, LANES), lambda i, ph, c: (0, 0, 0))
    next_phase = pl.BlockSpec((1, 1, LANES), lambda i, ph, c: (nxt(i, ph, c)[2], 0, 0))
    y_spec = pl.BlockSpec((1, ch, SSD_W), lambda i, ph, c: (i, out_idx(ph, c), 0))
    operands = [(xbc, pl.BlockSpec((1, ch, CONV_CH), lambda i, ph, c: (i, xidx(ph, c), 0))),
                (xbc, pl.BlockSpec((1, halo, CONV_CH),
                                   lambda i, ph, c: (i, jnp.maximum(xidx(ph, c) * per_ch - 1, 0), 0))),
                (xbc, pl.BlockSpec((1, halo, CONV_CH),
                                   lambda i, ph, c: (i, jnp.minimum((xidx(ph, c) + 1) * per_ch,
                                                                    l // halo - 1), 0))),
                (dt, pl.BlockSpec((1, ch, LANES), lambda i, ph, c: (0, 0, 0))),
                (dt, pl.BlockSpec((1, ch, LANES), lambda i, ph, c: nxt(i, ph, c)))]
    if emit_y:
        operands.append((z, y_spec))
    operands += [(conv_w8, full2(conv_w8)), (conv_b, full2(conv_b)),
                 (dt_bias, first_phase), (a_log, first_phase),
                 (dt_bias, next_phase), (a_log, next_phase)]
    if emit_y:
        operands += [(d_skip_x, full2(d_skip_x)), (norm_w, full2(norm_w))]
    operands += [(expand, full2(expand)), (init, st_spec)]
    st_shape = jax.ShapeDtypeStruct((b, 2, SSD_GROUPS, SSD_STATE, GROUP_W), F32)
    scratch = [pltpu.VMEM((nc, ch, CONV_CH), BF16)]
    if emit_y:
        scratch += [pltpu.VMEM((ch, SSD_W), F32),
                    pltpu.VMEM((nc, ch, SSD_W), BF16)]
    scratch.append(pltpu.VMEM((SSD_GROUPS, SSD_STATE, GROUP_W), F32))
    factors = [pltpu.VMEM((ch, LANES), F32), pltpu.VMEM((ch, SSD_W), BF16),
               pltpu.VMEM((SUBLANES, SSD_W), F32)]
    if emit_y:
        factors += [pltpu.VMEM((ch, SSD_W), F32), pltpu.VMEM((LANES, ch), F32)]
    scratch += factors + factors
    outs = pl.pallas_call(
        functools.partial(_ssd_kernel, nc, emit_y),
        grid=(b, 2, nc),
        in_specs=[spec for _, spec in operands],
        out_specs=[y_spec, st_spec] if emit_y else [st_spec],
        out_shape=[jax.ShapeDtypeStruct((b, l, SSD_W), BF16), st_shape] if emit_y else [st_shape],
        scratch_shapes=scratch,
        compiler_params=_cparams(("arbitrary", "arbitrary", "arbitrary")),
    )(*[a for a, _ in operands])
    return outs if emit_y else (None, outs[0])


def _mix_kernel(y_ref, u_ref, x_ref, pa_ref, pcnt_ref, pw_ref, psc_ref, wo_ref, g1_ref,
                nw_ref, sh_ref, sc_ref, rw_ref, rb_ref,
                x1_ref, h_ref, rt_ref, tcnt_ref):
    tm = x_ref.shape[1]

    pooled = []
    for g in range(len(POOL_WINDOWS)):
        parts = []
        for r in range(tm // POOL_TM):
            u = u_ref[0, r * POOL_TM:(r + 1) * POOL_TM, g * POOL_GW:(g + 1) * POOL_GW]
            wsum = _dot(pa_ref[g], u)
            parts.append((wsum / pcnt_ref[g] - u.astype(F32)).astype(BF16))
        pooled.append(jnp.concatenate(parts, axis=0))
    mapped = []
    for gp in range(len(POOL_WINDOWS) // 2):
        pair = jnp.concatenate(pooled[2 * gp:2 * gp + 2], axis=1)
        mapped.append((_dot(pair, pw_ref[gp])
                       * psc_ref[:, 2 * gp * POOL_GW:(2 * gp + 2) * POOL_GW]).astype(BF16))
    y_pool = jnp.concatenate(mapped, axis=1)

    mix = _dot(y_ref[0], wo_ref[0:SSD_W, :]) + _dot(y_pool, wo_ref[SSD_W:SSD_W + POOL_W, :])
    x1 = x_ref[0] + g1_ref[0] * mix
    x1_ref[0] = x1

    ms = jnp.mean(x1 * x1, axis=-1, keepdims=True)
    h = x1 * lax.rsqrt(ms + RMS_EPS) * nw_ref[...]
    h = h * (1.0 + sc_ref[0]) + sh_ref[0]
    h_ref[0] = h

    h0, h1 = _split2(h)
    rw2 = jnp.concatenate(_split2(rw_ref[...]), axis=1)
    t0 = _dot(h0, rw2)
    t1 = _dot(h1, rw2)
    logits = (t0[:, 0:LANES] + t0[:, LANES:2 * LANES] + t1[:, 0:LANES] + t1[:, LANES:2 * LANES]
              + rb_ref[...])
    work = logits.T[0:N_EXPERTS, :]
    expert_f = lax.broadcasted_iota(jnp.int32, (N_EXPERTS, tm), 0).astype(F32)
    vals, idxs = [], []
    for _ in range(TOP_K):
        m = jnp.max(work, axis=0, keepdims=True)
        first_idx = jnp.min(jnp.where(work == m, expert_f, float(N_EXPERTS)), axis=0, keepdims=True)
        vals.append(m)
        idxs.append(first_idx)
        work = jnp.where(expert_f == first_idx, 2.0 * NEG_BIG, work)
    exps = [jnp.exp(v - vals[0]) for v in vals]
    denom = exps[0] + exps[1] + exps[2] + exps[3]

    rec_row = lax.broadcasted_iota(jnp.int32, (RT_ROWS, tm), 0)
    rec = jnp.zeros((RT_ROWS, tm), F32)
    onehot = jnp.zeros((N_EXPERTS, tm), F32)
    for k in range(TOP_K):
        rec = jnp.where(rec_row == k, idxs[k], rec)
        rec = jnp.where(rec_row == GATE_LANE + k, exps[k] / denom, rec)
        onehot = onehot + jnp.where(expert_f == idxs[k], 1.0, 0.0)
    rt_ref[0] = rec
    for r in range(tm // DISP_TM):
        tcnt_ref[0, r] = jnp.sum(onehot[:, r * DISP_TM:(r + 1) * DISP_TM], axis=1, keepdims=True)


def _mix(y, u, x, pool_a, pool_cnt, pool_w, pool_scale, w_out, g1, norm_w, shift, scale,
         router_w, router_b):
    b, l, d = x.shape
    tm = MIX_TM
    per_l = l // tm
    tok = lambda n: pl.BlockSpec((1, tm, n), lambda i, j: (i, j, 0))
    per_b = pl.BlockSpec((1, 1, d), lambda i, j: (i, 0, 0))
    full = lambda a: pl.BlockSpec(a.shape, lambda i, j: (0,) * a.ndim)
    return pl.pallas_call(
        _mix_kernel,
        grid=(b, per_l),
        in_specs=[tok(SSD_W), tok(POOL_W), tok(d), full(pool_a), full(pool_cnt), full(pool_w),
                  full(pool_scale), full(w_out), per_b, full(norm_w), per_b, per_b,
                  full(router_w), full(router_b)],
        out_specs=[tok(d), tok(d),
                   pl.BlockSpec((1, RT_ROWS, tm), lambda i, j: (i * per_l + j, 0, 0)),
                   pl.BlockSpec((1, tm // DISP_TM, N_EXPERTS, 1), lambda i, j: (i * per_l + j, 0, 0, 0))],
        out_shape=[jax.ShapeDtypeStruct((b, l, d), F32),
                   jax.ShapeDtypeStruct((b, l, d), F32),
                   jax.ShapeDtypeStruct((b * per_l, RT_ROWS, tm), F32),
                   jax.ShapeDtypeStruct((b * per_l, tm // DISP_TM, N_EXPERTS, 1), F32)],
        compiler_params=_cparams(("arbitrary", "arbitrary")),
    )(y, u, x, pool_a, pool_cnt, pool_w, pool_scale, w_out, g1, norm_w, shift, scale,
      router_w, router_b)


def _group_copies(n_groups, src_ref, src_group, dst_ref, dst_group, sem, wait):
    def pieces(bits):
        for bit in bits:
            @pl.when((n_groups & bit) != 0)
            def _():
                done = n_groups & ~(2 * bit - 1)
                src = src_ref.at[pl.ds(pl.multiple_of((src_group + done) * SUBLANES, SUBLANES),
                                       bit * SUBLANES)]
                dst = dst_ref.at[pl.ds(pl.multiple_of((dst_group + done) * SUBLANES, SUBLANES),
                                       bit * SUBLANES)]
                cp = pltpu.make_async_copy(src, dst, sem)
                cp.wait() if wait else cp.start()

    split = RUN_BITS.index(RUN_SMALL_BIT)
    pl.when(n_groups >= 2 * RUN_SMALL_BIT)(lambda: pieces(RUN_BITS[:split]))
    pieces(RUN_BITS[split:])


def _wait_groups(n_groups, src_ref, dst_ref, sem):
    for bit in WAIT_BITS:
        @pl.when((n_groups & bit) != 0)
        def _():
            rows = bit * SUBLANES
            pltpu.make_async_copy(src_ref.at[pl.ds(0, rows)], dst_ref.at[pl.ds(0, rows)], sem).wait()


def _stage_positions(rt_t, base_col, upper):
    tm = rt_t.shape[1]
    expert_f = lax.broadcasted_iota(jnp.int32, (LANES, tm), 0).astype(F32)
    hots = [expert_f == rt_t[k:k + 1, :] for k in range(TOP_K)]
    onehot = jnp.zeros((LANES, tm), F32)
    for hot in hots:
        onehot = onehot + jnp.where(hot, 1.0, 0.0)
    pos = _dot(onehot.astype(BF16), upper) + base_col
    return [jnp.sum(jnp.where(hot, pos, 0.0), axis=0, keepdims=True) for hot in hots]


def _dispatch_kernel(o_ref, f_ref, fl_ref, rem_ref, hg_ref, tot_ref, used_ref, fs_ref, flen_ref,
                     h_ref, rt_ref, sb_ref, tri_ref, xs_ref,
                     stg2_ref, tails_ref, zero_ref, sem2, fill_sem):
    i = pl.program_id(0)
    tm = h_ref.shape[0]
    n_slots = xs_ref.shape[0]
    base = i * N_EXPERTS
    hi_mask = jnp.int32(-65536)
    slot = i % 2
    stg_ref = stg2_ref.at[slot]
    sem = sem2.at[slot]

    @pl.when(i == 0)
    def _():
        tails_ref[...] = jnp.zeros_like(tails_ref)
        stg2_ref[...] = jnp.zeros_like(stg2_ref)

    pos = _stage_positions(rt_ref[0], sb_ref[0], tri_ref[...])

    hb = h_ref[...].astype(BF16)
    half = PACK_W
    used_rows = used_ref[i] * SUBLANES

    def stage_chunk(r_lo):
        row_f = (lax.broadcasted_iota(jnp.int32, (STAGE_CHUNK, tm), 0) + r_lo).astype(F32)
        sel = jnp.zeros((STAGE_CHUNK, tm), F32)
        for k in range(TOP_K):
            sel = jnp.where(row_f == pos[k], 1.0, sel)
        moved = _dot(sel.astype(BF16), hb)
        lo = pltpu.bitcast(moved[:, 0:half], jnp.int32)
        hi = pltpu.bitcast(moved[:, half:2 * half], jnp.int32)
        stg_ref[r_lo:r_lo + STAGE_CHUNK, :] = lax.shift_right_logical(lo, 16) | (hi & hi_mask)

    always = TOP_K * tm // STAGE_CHUNK
    for rc in range(STAGE_ROWS // STAGE_CHUNK):
        if rc < always:
            stage_chunk(rc * STAGE_CHUNK)
        else:
            pl.when(rc * STAGE_CHUNK < used_rows)(functools.partial(stage_chunk, rc * STAGE_CHUNK))

    def finish_run(e, cr):
        stage_group = o_ref[base + e]
        first = pl.multiple_of(stage_group * SUBLANES, SUBLANES)
        stg_ref[pl.ds(first, SUBLANES), :] = stg_ref[pl.ds(first, SUBLANES), :] | tails_ref[e]
        part = pl.multiple_of((stage_group + f_ref[base + e]) * SUBLANES, SUBLANES)
        tails_ref[e] = jnp.where(rem_ref[base + e] > 0, stg_ref[pl.ds(part, SUBLANES), :], 0)
        _group_copies(fl_ref[base + e], stg_ref, stage_group, xs_ref, hg_ref[base + e], sem, False)
        return cr

    lax.fori_loop(0, N_EXPERTS, finish_run, 0)

    @pl.when(i == pl.num_programs(0) - 1)
    def _():
        zero_ref[...] = jnp.zeros_like(zero_ref)
        tail_group = fs_ref[N_EXPERTS]
        n_tail = (n_slots // SUBLANES - tail_group) // (MOE_BLK // SUBLANES)

        def tail_copy(j, wait):
            off = pl.multiple_of(tail_group * SUBLANES + j * MOE_BLK, MOE_BLK)
            cp = pltpu.make_async_copy(zero_ref, xs_ref.at[pl.ds(off, MOE_BLK)], fill_sem)
            cp.wait() if wait else cp.start()

        for wait in (False, True):
            def pad_body(e, cr):
                _group_copies(flen_ref[e], zero_ref, 0, xs_ref, fs_ref[e], fill_sem, wait)
                return cr
            lax.fori_loop(0, N_EXPERTS, pad_body, 0)
            lax.fori_loop(0, n_tail, lambda j, cr: (tail_copy(j, wait), cr)[1], 0)

    @pl.when(i > 0)
    def _():
        _wait_groups(tot_ref[jnp.maximum(i - 1, 0)], stg2_ref.at[1 - slot], xs_ref, sem2.at[1 - slot])

    @pl.when(i == pl.num_programs(0) - 1)
    def _():
        _wait_groups(tot_ref[i], stg_ref, xs_ref, sem)


def _dispatch(tables, fill_groups, fill_len_groups, h, rt, stage_base, n_slots):
    t, d = h.shape
    tm = DISP_TM
    rt_per = rt.shape[2] // tm
    grid_spec = pltpu.PrefetchScalarGridSpec(
        num_scalar_prefetch=len(tables) + 2,
        grid=(t // tm,),
        in_specs=[pl.BlockSpec((tm, d), lambda i, *_: (i, 0)),
                  pl.BlockSpec((1, RT_ROWS, tm), lambda i, *_: (i // rt_per, 0, i % rt_per)),
                  pl.BlockSpec((1, LANES, 1), lambda i, *_: (i, 0, 0)),
                  pl.BlockSpec((tm, tm), lambda i, *_: (0, 0))],
        out_specs=pl.BlockSpec(memory_space=pl.ANY),
        scratch_shapes=[pltpu.VMEM((2, STAGE_ROWS, XS_W), jnp.int32),
                        pltpu.VMEM((N_EXPERTS, SUBLANES, XS_W), jnp.int32),
                        pltpu.VMEM((MOE_BLK, XS_W), jnp.int32),
                        pltpu.SemaphoreType.DMA((2,)), pltpu.SemaphoreType.DMA(())],
    )
    return pl.pallas_call(
        _dispatch_kernel,
        grid_spec=grid_spec,
        out_shape=jax.ShapeDtypeStruct((n_slots, XS_W), jnp.int32),
        compiler_params=_cparams(("arbitrary",)),
    )(*tables, fill_groups, fill_len_groups, h, rt, stage_base, _strict_upper(tm))


def _expert_kernel(be_ref, nused_ref, xs_ref, w1_ref, b1_ref, w2_ref, b2_ref, perm_ref, y_ref,
                   w1s_ref, w2s_ref):
    i = pl.program_id(0)
    prev = be_ref[jnp.maximum(i - 1, 0)]
    changed = (i == 0) | (be_ref[i] != prev)
    dff2 = w1_ref.shape[2]
    tile = 2 * LANES

    @pl.when(changed & (i < nused_ref[0]))
    def _():
        for j in range(dff2 // tile):
            wj = w1_ref[0, :, j * tile:(j + 1) * tile].astype(BF16)
            w1s_ref[:, j * tile:(j + 1) * tile] = _dot(wj, perm_ref[...]).astype(BF16)
        w2s_ref[...] = w2_ref[0].astype(BF16)

    @pl.when(i < nused_ref[0])
    def _():
        hi_mask = jnp.int32(-65536)
        words = xs_ref[...]
        x = jnp.concatenate(
            [pltpu.bitcast(lax.shift_left(words, 16), F32).astype(BF16),
             pltpu.bitcast(words & hi_mask, F32).astype(BF16)], axis=1)
        acts = []
        for j in range(dff2 // tile):
            hb = _dot(x, w1s_ref[:, j * tile:(j + 1) * tile]) + b1_ref[0, :, j * tile:(j + 1) * tile]
            gp = jnp.minimum(hb[:, 0:LANES], SWIGLU_LIMIT)
            up = jnp.clip(hb[:, LANES:tile], -SWIGLU_LIMIT, SWIGLU_LIMIT)
            acts.append((gp * _sigmoid(SWIGLU_ALPHA * gp) * (up + 1.0)).astype(BF16))
        act = jnp.concatenate(acts, axis=1)
        y = _dot(act, w2s_ref[...]) + b2_ref[0]
        lo = pltpu.bitcast(y[:, 0:PACK_W].astype(BF16).astype(F32), jnp.int32)
        hi = pltpu.bitcast(y[:, PACK_W:2 * PACK_W].astype(BF16).astype(F32), jnp.int32)
        y_ref[...] = lax.shift_right_logical(lo, 16) | (hi & hi_mask)

    @pl.when(i >= nused_ref[0])
    def _():
        y_ref[...] = jnp.zeros_like(y_ref)


def _experts(block_exp, n_used, xs, w1, b1p, w2, b2, perm, n_blocks):
    blk = MOE_BLK
    d = w1.shape[1]
    dff2 = w1.shape[2]
    dff = w2.shape[1]

    def x_idx(i, be, nu):
        return (jnp.minimum(i, nu[0] - 1), 0)

    grid_spec = pltpu.PrefetchScalarGridSpec(
        num_scalar_prefetch=2,
        grid=(n_blocks,),
        in_specs=[pl.BlockSpec((blk, XS_W), x_idx),
                  pl.BlockSpec((1, d, dff2), lambda i, be, nu: (be[i], 0, 0)),
                  pl.BlockSpec((1, 1, dff2), lambda i, be, nu: (be[i], 0, 0)),
                  pl.BlockSpec((1, dff, d), lambda i, be, nu: (be[i], 0, 0)),
                  pl.BlockSpec((1, 1, d), lambda i, be, nu: (be[i], 0, 0)),
                  pl.BlockSpec(perm.shape, lambda i, be, nu: (0, 0))],
        out_specs=pl.BlockSpec((blk, PACK_W), lambda i, be, nu: (i, 0)),
        scratch_shapes=[pltpu.VMEM((d, dff2), BF16), pltpu.VMEM((dff, d), BF16)],
    )
    return pl.pallas_call(
        _expert_kernel,
        grid_spec=grid_spec,
        out_shape=jax.ShapeDtypeStruct((n_blocks * blk, PACK_W), jnp.int32),
        compiler_params=_cparams(("arbitrary",)),
    )(block_exp, n_used, xs, w1, b1p, w2, b2, perm)


def _combine_kernel(o_ref, g_ref, hg_ref, tot_ref, x1_ref, rt_ref, sb_ref, tri_ref, g2_ref, fnw_ref,
                    ys_ref, out_ref, stg_ref, sem):
    tm = x1_ref.shape[1]
    step = pl.program_id(0) * pl.num_programs(1) + pl.program_id(1)
    n_steps = pl.num_programs(0) * pl.num_programs(1)
    slot = step % 2
    hi_mask = jnp.int32(-65536)

    def fetch(tile, into):
        def body(e, cr):
            _group_copies(g_ref[tile * N_EXPERTS + e], ys_ref, hg_ref[tile * N_EXPERTS + e],
                          stg_ref.at[into], o_ref[tile * N_EXPERTS + e], sem.at[into], False)
            return cr
        lax.fori_loop(0, N_EXPERTS, body, 0)

    @pl.when(step == 0)
    def _():
        stg_ref[...] = jnp.zeros_like(stg_ref)
        fetch(0, 0)

    @pl.when(step + 1 < n_steps)
    def _():
        fetch(step + 1, 1 - slot)

    _wait_groups(tot_ref[step], ys_ref, stg_ref.at[slot], sem.at[slot])

    rt_t = rt_ref[0]
    pos_rows = _stage_positions(rt_t, sb_ref[0], tri_ref[...])
    expert_i = lax.broadcasted_iota(jnp.int32, (LANES, tm), 0)
    stacked = jnp.zeros((LANES, tm), F32)
    for k in range(TOP_K):
        stacked = jnp.where(expert_i == k, pos_rows[k], stacked)
        stacked = jnp.where(expert_i == TOP_K + k, rt_t[GATE_LANE + k:GATE_LANE + k + 1, :], stacked)
    cols = stacked.T

    def unpermute(n_rows):
        col_f = lax.broadcasted_iota(jnp.int32, (tm, n_rows), 1).astype(F32)
        sel = jnp.zeros((tm, n_rows), F32)
        for k in range(TOP_K):
            sel = jnp.where(col_f == cols[:, k:k + 1], cols[:, TOP_K + k:TOP_K + k + 1], sel)
        sel = sel.astype(BF16)
        words = stg_ref[slot, 0:n_rows, :]
        lo = pltpu.bitcast(lax.shift_left(words, 16), F32).astype(BF16)
        hi = pltpu.bitcast(words & hi_mask, F32).astype(BF16)
        moe = jnp.concatenate([_dot(sel, lo), _dot(sel, hi)], axis=1)
        x2 = x1_ref[0] + g2_ref[0] * moe
        ms = jnp.mean(x2 * x2, axis=-1, keepdims=True)
        out_ref[0] = x2 * lax.rsqrt(ms + RMS_EPS) * fnw_ref[...]

    used_rows = tot_ref[step] * SUBLANES
    short = STAGE_ROWS - STAGE_CHUNK
    pl.when(used_rows <= short)(functools.partial(unpermute, short))
    pl.when(used_rows > short)(functools.partial(unpermute, STAGE_ROWS))


def _combine(tables, x1, rt, stage_base, g2, fnw, ys):
    b, l, d = x1.shape
    tm = DISP_TM
    per_l = l // tm
    rt_per = rt.shape[2] // tm
    tok = lambda n: pl.BlockSpec((1, tm, n), lambda i, j, *_: (i, j, 0))
    grid_spec = pltpu.PrefetchScalarGridSpec(
        num_scalar_prefetch=len(tables),
        grid=(b, per_l),
        in_specs=[tok(d),
                  pl.BlockSpec((1, RT_ROWS, tm),
                               lambda i, j, *_: ((i * per_l + j) // rt_per, 0, (i * per_l + j) % rt_per)),
                  pl.BlockSpec((1, LANES, 1), lambda i, j, *_: (i * per_l + j, 0, 0)),
                  pl.BlockSpec((tm, tm), lambda i, j, *_: (0, 0)),
                  pl.BlockSpec((1, 1, d), lambda i, j, *_: (i, 0, 0)),
                  pl.BlockSpec(fnw.shape, lambda i, j, *_: (0, 0)),
                  pl.BlockSpec(memory_space=pl.ANY)],
        out_specs=tok(d),
        scratch_shapes=[pltpu.VMEM((2, STAGE_ROWS, PACK_W), jnp.int32),
                        pltpu.SemaphoreType.DMA((2,))],
    )
    return pl.pallas_call(
        _combine_kernel,
        grid_spec=grid_spec,
        out_shape=jax.ShapeDtypeStruct((b, l, d), F32),
        compiler_params=_cparams(("arbitrary", "arbitrary")),
    )(*tables, x1, rt, stage_base, _strict_upper(tm), g2, fnw, ys)


def _pool_constants():
    i = np.arange(POOL_TM)[:, None]
    j = np.arange(POOL_TM)[None, :]
    same_row = (i // GRID_W) == (j // GRID_W)
    mats, cnts = [], []
    for w in POOL_WINDOWS:
        band = same_row & (j - i >= -(w // 2)) & (j - i < w - w // 2)
        mats.append(band)
        cnts.append(np.broadcast_to(band.sum(axis=1, keepdims=True), (POOL_TM, POOL_GW)))
    return (jnp.asarray(np.stack(mats), BF16), jnp.asarray(np.stack(cnts), F32))


def _head_expand():
    e = np.zeros((LANES, SSD_W), np.float32)
    for h in range(SSD_HEADS):
        e[h, h * SSD_HEADDIM:(h + 1) * SSD_HEADDIM] = 1.0
    return jnp.asarray(e, BF16)


def _deinterleave_perm():
    n = 2 * LANES
    p = np.zeros((n, n), np.float32)
    for k in range(LANES):
        p[2 * k, k] = 1.0
        p[2 * k + 1, LANES + k] = 1.0
    return jnp.asarray(p, BF16)


def _strict_upper(n):
    return jnp.asarray(np.triu(np.ones((n, n), np.float32), 1), BF16)


def _pad_lanes(a, n):
    return jnp.pad(a, [(0, 0)] * (a.ndim - 1) + [(0, n - a.shape[-1])])


def kernel(x, c, ctx, c_ctx, w_mod, b_mod, norm1_w, norm2_w, w_in, conv_w, conv_b, dt_bias, a_log,
           d_skip, ssd_norm_w, pool_w, pool_scale, w_out, router_w, router_b, w1, b1, w2, b2,
           final_norm_w):
    depth = w_mod.shape[0]
    assert depth == 1, "single-layer problem"
    b, l, d = x.shape
    lc = ctx.shape[1]
    xbcdt = CONV_CH + 2 * SSD_HEADS

    mod_rows = 2 * SUBLANES
    cc = jnp.zeros((mod_rows, d), F32).at[0:b].set(c).at[b].set(c_ctx)
    mod = _modulation(cc, w_mod[0], b_mod[0])
    sh1, sc1, g1, sh2, sc2, g2 = [m.reshape(b, 1, d) for m in jnp.split(mod[0:b], 6, axis=-1)]
    csh1, csc1 = [jnp.broadcast_to(m.reshape(1, 1, d), (b, 1, d))
                  for m in jnp.split(mod[b:b + 1], 6, axis=-1)[0:2]]

    wi = w_in[0]
    wx = wi[:, 0:CONV_CH].astype(BF16)
    wd = jnp.concatenate([_pad_lanes(wi[:, CONV_CH:CONV_CH + SSD_HEADS], LANES),
                          _pad_lanes(wi[:, CONV_CH + SSD_HEADS:xbcdt], LANES)], axis=1).astype(BF16)
    wz = wi[:, xbcdt:xbcdt + SSD_W].astype(BF16)
    wp = wi[:, xbcdt + SSD_W:].astype(BF16)
    n1 = norm1_w[0].reshape(1, d)
    conv_w8 = jnp.pad(conv_w[0], ((0, SUBLANES - SSD_CONV), (0, 0)))
    conv_b1 = conv_b[0].reshape(1, CONV_CH)
    dtb = _pad_lanes(dt_bias[0], LANES).reshape(2, 1, LANES)
    alog = _pad_lanes(a_log[0], LANES).reshape(2, 1, LANES)
    dsk = jnp.repeat(d_skip[0], SSD_HEADDIM).reshape(1, SSD_W)
    snw = ssd_norm_w[0].reshape(1, SSD_W)
    expand = _head_expand()

    xbc_c, dt_c = _inproj(ctx, n1, csh1, csc1, (wx, wd), (BF16, F32), min(PROJ_TM, lc))
    zero_state = jnp.zeros((b, 2, SSD_GROUPS, SSD_STATE, GROUP_W), F32)
    _, ctx_states = _ssd(xbc_c, dt_c, None, conv_w8, conv_b1, dtb, alog, dsk, snw, expand, zero_state)

    xbc, dt, z, u_pool = _inproj(x, n1, sh1, sc1, (wx, wd, wz, wp), (BF16, F32, BF16, BF16), PROJ_TM)
    y_ssd, _ = _ssd(xbc, dt, z, conv_w8, conv_b1, dtb, alog, dsk, snw, expand, ctx_states)

    pool_a, pool_cnt = _pool_constants()
    pw = pool_w[0].astype(BF16)
    zero_gw = jnp.zeros((POOL_GW, POOL_GW), BF16)
    pw_pairs = jnp.stack([jnp.block([[pw[2 * gp], zero_gw], [zero_gw, pw[2 * gp + 1]]])
                          for gp in range(len(POOL_WINDOWS) // 2)])
    x1, h2, rt, tcnt = _mix(
        y_ssd, u_pool, x, pool_a, pool_cnt, pw_pairs, pool_scale[0].reshape(1, POOL_W),
        w_out[0].astype(BF16), g1, norm2_w[0].reshape(1, d), sh2, sc2,
        _pad_lanes(router_w[0], LANES), _pad_lanes(router_b[0].reshape(1, N_EXPERTS), LANES))

    i32 = jnp.int32
    t = b * l
    n_tiles = t // DISP_TM
    tc = tcnt.reshape(n_tiles, N_EXPERTS).astype(i32)
    counts = jnp.sum(tc, axis=0)
    run_start = jnp.cumsum(tc, axis=0) - tc
    padded = (counts + MOE_BLK - 1) // MOE_BLK * MOE_BLK
    pad_end = jnp.cumsum(padded)
    pad_start = pad_end - padded
    carried = run_start % SUBLANES
    span = carried + tc
    groups = (span + SUBLANES - 1) // SUBLANES
    full = span // SUBLANES
    stage_group = jnp.cumsum(groups, axis=1) - groups
    slot_group = (pad_start[None, :] + run_start - carried) // SUBLANES
    flush = full.at[n_tiles - 1].set(groups[n_tiles - 1])
    flat = lambda a: a.reshape(n_tiles * N_EXPERTS).astype(i32)
    stage_base = _pad_lanes((stage_group * SUBLANES + carried).astype(F32), LANES)
    stage_base = stage_base.reshape(n_tiles, LANES, 1)
    n_blocks = (t * TOP_K) // MOE_BLK + N_EXPERTS
    n_used = (pad_end[-1] // MOE_BLK).astype(i32).reshape(1)
    blk_start = jnp.minimum(jnp.arange(n_blocks, dtype=i32), n_used[0] - 1) * MOE_BLK
    block_exp = jnp.minimum(jnp.sum(blk_start[:, None] >= pad_end[None, :], axis=1),
                            N_EXPERTS - 1).astype(i32)
    written = (counts + SUBLANES - 1) // SUBLANES * SUBLANES
    fill_group = (jnp.concatenate([pad_start + written, pad_end[-1:]]) // SUBLANES).astype(i32)
    fill_groups = (jnp.concatenate([padded - written, jnp.zeros((1,), i32)]) // SUBLANES).astype(i32)

    xs = _dispatch((flat(stage_group), flat(full), flat(flush), flat(span % SUBLANES), flat(slot_group),
                    jnp.sum(flush, axis=1).astype(i32), jnp.sum(groups, axis=1).astype(i32)),
                   fill_group, fill_groups, h2.reshape(t, d), rt, stage_base,
                   n_blocks * MOE_BLK)
    dff2 = w1.shape[-1]
    b1p = jnp.concatenate(
        [b1[0].reshape(N_EXPERTS, dff2 // (2 * LANES), LANES, 2)[..., 0],
         b1[0].reshape(N_EXPERTS, dff2 // (2 * LANES), LANES, 2)[..., 1]], axis=-1
    ).reshape(N_EXPERTS, 1, dff2)
    ys = _experts(block_exp, n_used, xs, w1[0], b1p, w2[0], b2[0].reshape(N_EXPERTS, 1, d),
                  _deinterleave_perm(), n_blocks)
    return _combine((flat(stage_group), flat(groups), flat(slot_group),
                     jnp.sum(groups, axis=1).astype(i32)), x1, rt, stage_base, g2,
                    final_norm_w.reshape(1, d), ys)
```

```python
import functools

import numpy as np
import jax
import jax.numpy as jnp
from jax import lax
from jax.experimental import pallas as pl
from jax.experimental.pallas import tpu as pltpu

F32 = jnp.float32
BF16 = jnp.bfloat16

SSD_HEADDIM = 64
SSD_GROUPS = 4
SSD_HPG = 6
SSD_HEADS = SSD_GROUPS * SSD_HPG
SSD_STATE = 128
SSD_CONV = 5
SSD_CHUNK = 128
SSD_W = SSD_HEADS * SSD_HEADDIM
GROUP_W = SSD_HPG * SSD_HEADDIM
CONV_CH = SSD_W + 2 * SSD_GROUPS * SSD_STATE
POOL_WINDOWS = (2, 4, 8, 16)
POOL_GW = 128
POOL_W = POOL_GW * len(POOL_WINDOWS)
GRID_W = 64
N_EXPERTS = 32
TOP_K = 4
SWIGLU_ALPHA = 1.702
SWIGLU_LIMIT = 7.0
RMS_EPS = 1e-6
LOG2_E = 1.4426950408889634

LANES = 128
SUBLANES = 8
VMEM_LIMIT_BYTES = 56 * 1024 * 1024

MOD_TN = 1024
PROJ_TM = 512
MIX_TM = 512
POOL_TM = 256
MOE_BLK = 512
DISP_TM = 256
NEG_BIG = -1e30

GATE_LANE = 2 * TOP_K
RT_ROWS = 2 * SUBLANES
PACK_W = 512
XS_W = PACK_W
STAGE_CHUNK = 256
_STAGE_NEED = TOP_K * DISP_TM + N_EXPERTS * 2 * (SUBLANES - 1) + SUBLANES
STAGE_ROWS = -(-_STAGE_NEED // STAGE_CHUNK) * STAGE_CHUNK
RUN_BITS = (32, 16, 8, 4, 2, 1)
RUN_SMALL_BIT = 4
WAIT_BITS = (128, 64, 32, 16, 8, 4, 2, 1)


def _sigmoid(x):
    return 0.5 * jnp.tanh(0.5 * x) + 0.5


def _split2(a):
    hi = a.astype(BF16)
    mid = (a - hi.astype(F32)).astype(BF16)
    return hi, mid


def _split3(a):
    hi = a.astype(BF16)
    r = a - hi.astype(F32)
    mid = r.astype(BF16)
    lo = (r - mid.astype(F32)).astype(BF16)
    return hi, mid, lo


def _dot(a, b):
    return jnp.dot(a, b, preferred_element_type=F32)


def _dot_exact_rhs(a_f32, b_bf16, parts):
    pieces = _split3(a_f32) if parts == 3 else _split2(a_f32)
    out = _dot(pieces[0], b_bf16)
    for p in pieces[1:]:
        out = out + _dot(p, b_bf16)
    return out


def _dot_hi(a_f32, b_f32):
    a0, a1, a2 = _split3(a_f32)
    b0, b1, b2 = _split3(b_f32)
    out = _dot(a0, b0)
    out = out + _dot(a0, b1) + _dot(a1, b0)
    out = out + _dot(a1, b1) + _dot(a0, b2) + _dot(a2, b0)
    return out


def _cparams(sem):
    return pltpu.CompilerParams(dimension_semantics=sem, vmem_limit_bytes=VMEM_LIMIT_BYTES)


def _mod_kernel(c_ref, w_ref, b_ref, o_ref):
    c = c_ref[...]
    s = c * _sigmoid(c)
    o_ref[...] = _dot_hi(s, w_ref[...]) + b_ref[...]


def _modulation(cc, w_mod, b_mod):
    rows, d = cc.shape
    n = w_mod.shape[1]
    return pl.pallas_call(
        _mod_kernel,
        grid=(n // MOD_TN,),
        in_specs=[pl.BlockSpec((rows, d), lambda j: (0, 0)),
                  pl.BlockSpec((d, MOD_TN), lambda j: (0, j)),
                  pl.BlockSpec((1, MOD_TN), lambda j: (0, j))],
        out_specs=pl.BlockSpec((rows, MOD_TN), lambda j: (0, j)),
        out_shape=jax.ShapeDtypeStruct((rows, n), F32),
        compiler_params=_cparams(("arbitrary",)),
    )(cc, w_mod, b_mod.reshape(1, n))


def _inproj_kernel(n_groups, x0_ref, xn_ref, nw_ref, sh0_ref, sc0_ref, shn_ref, scn_ref, *refs):
    w_refs = refs[0:n_groups]
    out_refs = refs[n_groups:2 * n_groups]
    hb_even_ref, hb_odd_ref = refs[2 * n_groups:]
    step = pl.program_id(0) * pl.num_programs(1) + pl.program_id(1)

    def normed(x, sh, sc):
        ms = jnp.mean(x * x, axis=-1, keepdims=True)
        h = x * lax.rsqrt(ms + RMS_EPS) * nw_ref[...]
        return (h * (1.0 + sc) + sh).astype(BF16)

    @pl.when(step == 0)
    def _():
        hb_even_ref[...] = normed(x0_ref[0], sh0_ref[0], sc0_ref[0])

    def project(cur_ref, nxt_ref):
        hb = cur_ref[...]
        for w_ref, out_ref in zip(w_refs, out_refs):
            out_ref[0] = _dot(hb, w_ref[...]).astype(out_ref.dtype)
        nxt_ref[...] = normed(xn_ref[0], shn_ref[0], scn_ref[0])

    pl.when(step % 2 == 0)(functools.partial(project, hb_even_ref, hb_odd_ref))
    pl.when(step % 2 == 1)(functools.partial(project, hb_odd_ref, hb_even_ref))


def _inproj(x, norm_w, shift, scale, weights, out_dtypes, tm):
    b, l, d = x.shape
    per_l = l // tm
    last = b * per_l - 1

    def nxt(i, j):
        s = jnp.minimum(i * per_l + j + 1, last)
        return s // per_l, s % per_l

    full = lambda a: pl.BlockSpec(a.shape, lambda i, j: (0, 0))
    tok = lambda n: pl.BlockSpec((1, tm, n), lambda i, j: (i, j, 0))
    first_b = pl.BlockSpec((1, 1, d), lambda i, j: (0, 0, 0))
    next_b = pl.BlockSpec((1, 1, d), lambda i, j: (nxt(i, j)[0], 0, 0))
    return pl.pallas_call(
        functools.partial(_inproj_kernel, len(weights)),
        grid=(b, per_l),
        in_specs=[pl.BlockSpec((1, tm, d), lambda i, j: (0, 0, 0)),
                  pl.BlockSpec((1, tm, d), lambda i, j: (*nxt(i, j), 0)),
                  full(norm_w), first_b, first_b, next_b, next_b] + [full(w) for w in weights],
        out_specs=[tok(w.shape[1]) for w in weights],
        out_shape=[jax.ShapeDtypeStruct((b, l, w.shape[1]), dt)
                   for w, dt in zip(weights, out_dtypes)],
        scratch_shapes=[pltpu.VMEM((tm, d), BF16), pltpu.VMEM((tm, d), BF16)],
        compiler_params=_cparams(("arbitrary", "arbitrary")),
    )(x, x, norm_w, shift, scale, shift, scale, *weights)


def _ssd_kernel(nc, emit_y, *refs):
    if emit_y:
        (xbc_ref, prev_ref, next_ref, dt_ref, z_ref, cw_ref, cb_ref, dtb_ref, alog_ref, dsk_ref,
         nw_ref, exp_ref, init_ref, y_ref, fin_ref, act_ref, ybuf_ref, yf_ref, st_ref) = refs
        y_part = (z_ref, dsk_ref, nw_ref, y_ref, ybuf_ref, yf_ref)
    else:
        (xbc_ref, prev_ref, next_ref, dt_ref, cw_ref, cb_ref, dtb_ref, alog_ref,
         exp_ref, init_ref, fin_ref, act_ref, st_ref) = refs
        y_part = None
    ph = pl.program_id(1)
    c = pl.program_id(2)
    ci = c + ph * (nc - 1 - 2 * c)

    @pl.when(c == 0)
    def _():
        st_ref[...] = init_ref[0, ph]

    @pl.when(ph == 0)
    def _():
        _conv_silu(nc, ci, xbc_ref, prev_ref, next_ref, cw_ref, cb_ref, act_ref)

    _ssd_scan(nc, ph, c, ci, dt_ref, dtb_ref, alog_ref, exp_ref, act_ref, st_ref, fin_ref, y_part)


def _conv_silu(nc, ci, xbc_ref, prev_ref, next_ref, cw_ref, cb_ref, act_ref):
    ch = SSD_CHUNK
    has_prev = ci > 0
    has_next = ci < nc - 1
    row8 = lax.broadcasted_iota(jnp.int32, (SUBLANES, LANES), 0)
    half = SSD_CONV // 2
    for j in range(CONV_CH // LANES):
        cols = slice(j * LANES, (j + 1) * LANES)
        n_t = ch // SUBLANES
        tiles = [jnp.where(has_prev, prev_ref[0, SUBLANES:2 * SUBLANES, cols].astype(F32), 0.0)]
        for i in range(n_t // 2):
            both = xbc_ref[0, 2 * i * SUBLANES:2 * (i + 1) * SUBLANES, cols].astype(F32)
            tiles += [both[0:SUBLANES], both[SUBLANES:2 * SUBLANES]]
        tiles.append(jnp.where(has_next, next_ref[0, 0:SUBLANES, cols].astype(F32), 0.0))
        acc = [cb_ref[:, cols] + cw_ref[half:half + 1, cols] * tiles[i + 1] for i in range(n_t)]
        for s in range(1, half + 1):
            rot = [pltpu.roll(tl, s, axis=0) for tl in tiles[0:n_t + 1]]
            wk = cw_ref[half - s:half - s + 1, cols]
            for i in range(n_t):
                acc[i] = acc[i] + wk * jnp.where(row8 < s, rot[i], rot[i + 1])
            rot = [pltpu.roll(tl, SUBLANES - s, axis=0) for tl in tiles[1:n_t + 2]]
            wk = cw_ref[half + s:half + s + 1, cols]
            for i in range(n_t):
                acc[i] = acc[i] + wk * jnp.where(row8 >= SUBLANES - s, rot[i + 1], rot[i])
        silu = [0.5 * v * jnp.tanh(0.5 * v) + 0.5 * v for v in acc]
        act_ref[ci, :, cols] = jnp.concatenate(silu, axis=0).astype(BF16)


def _ssd_scan(nc, ph, c, ci, dt_ref, dtb_ref, alog_ref, exp_ref, act_ref, st_ref, fin_ref, y_part):
    ch = SSD_CHUNK
    emit_y = y_part is not None
    dtr = dt_ref[0] + dtb_ref[0]
    small = jnp.exp(-jnp.abs(dtr))
    one_plus = 1.0 + small
    log1p_small = jnp.where(one_plus == 1.0, small, jnp.log(one_plus) * (small / (one_plus - 1.0)))
    dtv = jnp.maximum(dtr, 0.0) + log1p_small
    a = dtv * (-jnp.exp(alog_ref[0]) * LOG2_E)
    row = lax.broadcasted_iota(jnp.int32, (ch, ch), 0)
    col = lax.broadcasted_iota(jnp.int32, (ch, ch), 1)
    tmask = (row - col) * (1 - 2 * ph) >= 0
    tri = jnp.where(tmask, 1.0, 0.0).astype(BF16)
    cs = _dot_exact_rhs_left(tri, a)
    tot = jnp.where(ph == 0, cs[ch - 1:ch, :], cs[0:1, :])
    cs_t = cs.T
    e_cs = jnp.exp2(cs)
    e_dec = jnp.exp2(tot - cs)
    e_tot = jnp.exp2(tot)
    expand = exp_ref[...]
    wdec_x = _dot((dtv * e_dec).astype(BF16), expand).astype(BF16)
    etot_x = _dot_exact_rhs(jnp.broadcast_to(e_tot, (SUBLANES, LANES)), expand, 2)[0:1]
    if emit_y:
        z_ref, dsk_ref, nw_ref, y_ref, ybuf_ref, yf_ref = y_part
        ecs_x = _dot(e_cs.astype(BF16), expand)
        src_t = cs_t - jnp.log2(dtv).T

    lane = lax.broadcasted_iota(jnp.int32, (ch, LANES), 1)
    for g in range(SSD_GROUPS):
        gs = slice(g * GROUP_W, (g + 1) * GROUP_W)
        b_bf = act_ref[ci, :, SSD_W + g * SSD_STATE:SSD_W + (g + 1) * SSD_STATE]
        x_bf = act_ref[ci, :, gs]
        s_prev = st_ref[g]
        if emit_y:
            c_bf = act_ref[ci, :, SSD_W + (SSD_GROUPS + g) * SSD_STATE:
                           SSD_W + (SSD_GROUPS + g + 1) * SSD_STATE]
            cb = lax.dot_general(c_bf, b_bf, (((1,), (1,)), ((), ())), preferred_element_type=F32)
            y_off = _dot(c_bf, s_prev.astype(BF16)) * ecs_x[:, gs]
        x_dec = x_bf * wdec_x[:, gs]
        st_ref[g] = s_prev * etot_x[:, gs] + lax.dot_general(
            b_bf, x_dec, (((0,), (0,)), ((), ())), preferred_element_type=F32)
        if not emit_y:
            continue
        for q in range(SSD_HPG // 2):
            lmats = []
            for h in (g * SSD_HPG + 2 * q, g * SSD_HPG + 2 * q + 1):
                diff = cs[:, h:h + 1] - src_t[h:h + 1, :]
                dec = jnp.exp2(jnp.where(tmask, diff, NEG_BIG))
                lmats.append((dec * cb).astype(BF16))
            xp = x_bf[:, q * LANES:(q + 1) * LANES]
            zero = jnp.zeros_like(xp)
            rhs = jnp.concatenate([jnp.where(lane < SSD_HEADDIM, xp, zero),
                                   jnp.where(lane >= SSD_HEADDIM, xp, zero)], axis=0)
            y_diag = _dot(jnp.concatenate(lmats, axis=1), rhs)
            ps = slice(g * GROUP_W + q * LANES, g * GROUP_W + (q + 1) * LANES)
            ybuf_ref[:, ps] = y_diag + y_off[:, q * LANES:(q + 1) * LANES]

    if emit_y:
        @pl.when(ph == 0)
        def _():
            yf_ref[ci] = ybuf_ref[...].astype(BF16)

        @pl.when(ph == 1)
        def _():
            yt = (yf_ref[ci].astype(F32) + ybuf_ref[...]
                  + act_ref[ci, :, 0:SSD_W].astype(F32) * dsk_ref[...])
            hz = 0.5 * z_ref[0].astype(F32)
            gt = yt * (hz * jnp.tanh(hz) + hz)
            ms = jnp.mean(gt * gt, axis=-1, keepdims=True)
            y_ref[0] = (gt * lax.rsqrt(ms + RMS_EPS) * nw_ref[...]).astype(y_ref.dtype)

    @pl.when(c == nc - 1)
    def _():
        fin_ref[0, ph] = st_ref[...]


def _dot_exact_rhs_left(sel_bf16, a_f32):
    hi, mid, lo = _split3(a_f32)
    return _dot(sel_bf16, hi) + _dot(sel_bf16, mid) + _dot(sel_bf16, lo)


def _ssd(xbc, dt, z, conv_w8, conv_b, dt_bias, a_log, d_skip_x, norm_w, expand, init):
    emit_y = z is not None
    b, l, _ = xbc.shape
    ch = SSD_CHUNK
    nc = l // ch
    halo = 2 * SUBLANES
    per_ch = ch // halo

    def cidx(ph, c):
        return c + ph * (nc - 1 - 2 * c)

    def xidx(ph, c):
        return jnp.where(ph == 0, c, nc - 1)

    def out_idx(ph, c):
        return jnp.where(ph == 0, nc - 1, nc - 1 - c)

    full2 = lambda a: pl.BlockSpec(a.shape, lambda i, ph, c: (0, 0))
    st_spec = pl.BlockSpec((1, 2, SSD_GROUPS, SSD_STATE, GROUP_W), lambda i, ph, c: (i, 0, 0, 0, 0))
    per_phase = pl.BlockSpec((1, 1, LANES), lambda i, ph, c: (ph, 0, 0))
    y_spec = pl.BlockSpec((1, ch, SSD_W), lambda i, ph, c: (i, out_idx(ph, c), 0))
    operands = [(xbc, pl.BlockSpec((1, ch, CONV_CH), lambda i, ph, c: (i, xidx(ph, c), 0))),
                (xbc, pl.BlockSpec((1, halo, CONV_CH),
                                   lambda i, ph, c: (i, jnp.maximum(xidx(ph, c) * per_ch - 1, 0), 0))),
                (xbc, pl.BlockSpec((1, halo, CONV_CH),
                                   lambda i, ph, c: (i, jnp.minimum((xidx(ph, c) + 1) * per_ch,
                                                                    l // halo - 1), 0))),
                (dt, pl.BlockSpec((1, ch, LANES), lambda i, ph, c: (i, cidx(ph, c), ph)))]
    if emit_y:
        operands.append((z, y_spec))
    operands += [(conv_w8, full2(conv_w8)), (conv_b, full2(conv_b)),
                 (dt_bias, per_phase), (a_log, per_phase)]
    if emit_y:
        operands += [(d_skip_x, full2(d_skip_x)), (norm_w, full2(norm_w))]
    operands += [(expand, full2(expand)), (init, st_spec)]
    st_shape = jax.ShapeDtypeStruct((b, 2, SSD_GROUPS, SSD_STATE, GROUP_W), F32)
    scratch = [pltpu.VMEM((nc, ch, CONV_CH), BF16)]
    if emit_y:
        scratch += [pltpu.VMEM((ch, SSD_W), F32),
                    pltpu.VMEM((nc, ch, SSD_W), BF16)]
    scratch.append(pltpu.VMEM((SSD_GROUPS, SSD_STATE, GROUP_W), F32))
    outs = pl.pallas_call(
        functools.partial(_ssd_kernel, nc, emit_y),
        grid=(b, 2, nc),
        in_specs=[spec for _, spec in operands],
        out_specs=[y_spec, st_spec] if emit_y else [st_spec],
        out_shape=[jax.ShapeDtypeStruct((b, l, SSD_W), BF16), st_shape] if emit_y else [st_shape],
        scratch_shapes=scratch,
        compiler_params=_cparams(("arbitrary", "arbitrary", "arbitrary")),
    )(*[a for a, _ in operands])
    return outs if emit_y else (None, outs[0])


def _mix_kernel(y_ref, u_ref, x_ref, pa_ref, pcnt_ref, pw_ref, psc_ref, wo_ref, g1_ref,
                nw_ref, sh_ref, sc_ref, rw_ref, rb_ref,
                x1_ref, h_ref, rt_ref, tcnt_ref):
    tm = x_ref.shape[1]

    pooled = []
    for g in range(len(POOL_WINDOWS)):
        parts = []
        for r in range(tm // POOL_TM):
            u = u_ref[0, r * POOL_TM:(r + 1) * POOL_TM, g * POOL_GW:(g + 1) * POOL_GW]
            wsum = _dot(pa_ref[g], u)
            parts.append((wsum / pcnt_ref[g] - u.astype(F32)).astype(BF16))
        pooled.append(jnp.concatenate(parts, axis=0))
    mapped = []
    for gp in range(len(POOL_WINDOWS) // 2):
        pair = jnp.concatenate(pooled[2 * gp:2 * gp + 2], axis=1)
        mapped.append((_dot(pair, pw_ref[gp])
                       * psc_ref[:, 2 * gp * POOL_GW:(2 * gp + 2) * POOL_GW]).astype(BF16))
    y_pool = jnp.concatenate(mapped, axis=1)

    mix = _dot(y_ref[0], wo_ref[0:SSD_W, :]) + _dot(y_pool, wo_ref[SSD_W:SSD_W + POOL_W, :])
    x1 = x_ref[0] + g1_ref[0] * mix
    x1_ref[0] = x1

    ms = jnp.mean(x1 * x1, axis=-1, keepdims=True)
    h = x1 * lax.rsqrt(ms + RMS_EPS) * nw_ref[...]
    h = h * (1.0 + sc_ref[0]) + sh_ref[0]
    h_ref[0] = h

    h0, h1 = _split2(h)
    rw2 = jnp.concatenate(_split2(rw_ref[...]), axis=1)
    t0 = _dot(h0, rw2)
    t1 = _dot(h1, rw2)
    logits = (t0[:, 0:LANES] + t0[:, LANES:2 * LANES] + t1[:, 0:LANES] + t1[:, LANES:2 * LANES]
              + rb_ref[...])
    work = logits.T[0:N_EXPERTS, :]
    expert_f = lax.broadcasted_iota(jnp.int32, (N_EXPERTS, tm), 0).astype(F32)
    vals, idxs = [], []
    for _ in range(TOP_K):
        m = jnp.max(work, axis=0, keepdims=True)
        first_idx = jnp.min(jnp.where(work == m, expert_f, float(N_EXPERTS)), axis=0, keepdims=True)
        vals.append(m)
        idxs.append(first_idx)
        work = jnp.where(expert_f == first_idx, 2.0 * NEG_BIG, work)
    exps = [jnp.exp(v - vals[0]) for v in vals]
    denom = exps[0] + exps[1] + exps[2] + exps[3]

    rec_row = lax.broadcasted_iota(jnp.int32, (RT_ROWS, tm), 0)
    rec = jnp.zeros((RT_ROWS, tm), F32)
    onehot = jnp.zeros((N_EXPERTS, tm), F32)
    for k in range(TOP_K):
        rec = jnp.where(rec_row == k, idxs[k], rec)
        rec = jnp.where(rec_row == GATE_LANE + k, exps[k] / denom, rec)
        onehot = onehot + jnp.where(expert_f == idxs[k], 1.0, 0.0)
    rt_ref[0] = rec
    for r in range(tm // DISP_TM):
        tcnt_ref[0, r] = jnp.sum(onehot[:, r * DISP_TM:(r + 1) * DISP_TM], axis=1, keepdims=True)


def _mix(y, u, x, pool_a, pool_cnt, pool_w, pool_scale, w_out, g1, norm_w, shift, scale,
         router_w, router_b):
    b, l, d = x.shape
    tm = MIX_TM
    per_l = l // tm
    tok = lambda n: pl.BlockSpec((1, tm, n), lambda i, j: (i, j, 0))
    per_b = pl.BlockSpec((1, 1, d), lambda i, j: (i, 0, 0))
    full = lambda a: pl.BlockSpec(a.shape, lambda i, j: (0,) * a.ndim)
    return pl.pallas_call(
        _mix_kernel,
        grid=(b, per_l),
        in_specs=[tok(SSD_W), tok(POOL_W), tok(d), full(pool_a), full(pool_cnt), full(pool_w),
                  full(pool_scale), full(w_out), per_b, full(norm_w), per_b, per_b,
                  full(router_w), full(router_b)],
        out_specs=[tok(d), tok(d),
                   pl.BlockSpec((1, RT_ROWS, tm), lambda i, j: (i * per_l + j, 0, 0)),
                   pl.BlockSpec((1, tm // DISP_TM, N_EXPERTS, 1), lambda i, j: (i * per_l + j, 0, 0, 0))],
        out_shape=[jax.ShapeDtypeStruct((b, l, d), F32),
                   jax.ShapeDtypeStruct((b, l, d), F32),
                   jax.ShapeDtypeStruct((b * per_l, RT_ROWS, tm), F32),
                   jax.ShapeDtypeStruct((b * per_l, tm // DISP_TM, N_EXPERTS, 1), F32)],
        compiler_params=_cparams(("arbitrary", "arbitrary")),
    )(y, u, x, pool_a, pool_cnt, pool_w, pool_scale, w_out, g1, norm_w, shift, scale,
      router_w, router_b)


def _group_copies(n_groups, src_ref, src_group, dst_ref, dst_group, sem, wait):
    def pieces(bits):
        for bit in bits:
            @pl.when((n_groups & bit) != 0)
            def _():
                done = n_groups & ~(2 * bit - 1)
                src = src_ref.at[pl.ds(pl.multiple_of((src_group + done) * SUBLANES, SUBLANES),
                                       bit * SUBLANES)]
                dst = dst_ref.at[pl.ds(pl.multiple_of((dst_group + done) * SUBLANES, SUBLANES),
                                       bit * SUBLANES)]
                cp = pltpu.make_async_copy(src, dst, sem)
                cp.wait() if wait else cp.start()

    split = RUN_BITS.index(RUN_SMALL_BIT)
    pl.when(n_groups >= 2 * RUN_SMALL_BIT)(lambda: pieces(RUN_BITS[:split]))
    pieces(RUN_BITS[split:])


def _wait_groups(n_groups, src_ref, dst_ref, sem):
    for bit in WAIT_BITS:
        @pl.when((n_groups & bit) != 0)
        def _():
            rows = bit * SUBLANES
            pltpu.make_async_copy(src_ref.at[pl.ds(0, rows)], dst_ref.at[pl.ds(0, rows)], sem).wait()


def _stage_positions(rt_t, base_col, upper):
    tm = rt_t.shape[1]
    expert_f = lax.broadcasted_iota(jnp.int32, (LANES, tm), 0).astype(F32)
    hots = [expert_f == rt_t[k:k + 1, :] for k in range(TOP_K)]
    onehot = jnp.zeros((LANES, tm), F32)
    for hot in hots:
        onehot = onehot + jnp.where(hot, 1.0, 0.0)
    pos = _dot(onehot.astype(BF16), upper) + base_col
    return [jnp.sum(jnp.where(hot, pos, 0.0), axis=0, keepdims=True) for hot in hots]


def _dispatch_kernel(o_ref, f_ref, fl_ref, rem_ref, hg_ref, tot_ref, used_ref, fs_ref, flen_ref,
                     h_ref, rt_ref, sb_ref, tri_ref, xs_ref,
                     stg2_ref, tails_ref, zero_ref, sem2, fill_sem):
    i = pl.program_id(0)
    tm = h_ref.shape[0]
    n_slots = xs_ref.shape[0]
    base = i * N_EXPERTS
    hi_mask = jnp.int32(-65536)
    slot = i % 2
    stg_ref = stg2_ref.at[slot]
    sem = sem2.at[slot]

    @pl.when(i == 0)
    def _():
        tails_ref[...] = jnp.zeros_like(tails_ref)
        stg2_ref[...] = jnp.zeros_like(stg2_ref)

    pos = _stage_positions(rt_ref[0], sb_ref[0], tri_ref[...])

    hb = h_ref[...].astype(BF16)
    half = PACK_W
    used_rows = used_ref[i] * SUBLANES

    def stage_chunk(r_lo):
        row_f = (lax.broadcasted_iota(jnp.int32, (STAGE_CHUNK, tm), 0) + r_lo).astype(F32)
        sel = jnp.zeros((STAGE_CHUNK, tm), F32)
        for k in range(TOP_K):
            sel = jnp.where(row_f == pos[k], 1.0, sel)
        moved = _dot(sel.astype(BF16), hb)
        lo = pltpu.bitcast(moved[:, 0:half], jnp.int32)
        hi = pltpu.bitcast(moved[:, half:2 * half], jnp.int32)
        stg_ref[r_lo:r_lo + STAGE_CHUNK, :] = lax.shift_right_logical(lo, 16) | (hi & hi_mask)

    always = TOP_K * tm // STAGE_CHUNK
    for rc in range(STAGE_ROWS // STAGE_CHUNK):
        if rc < always:
            stage_chunk(rc * STAGE_CHUNK)
        else:
            pl.when(rc * STAGE_CHUNK < used_rows)(functools.partial(stage_chunk, rc * STAGE_CHUNK))

    def finish_run(e, cr):
        stage_group = o_ref[base + e]
        first = pl.multiple_of(stage_group * SUBLANES, SUBLANES)
        stg_ref[pl.ds(first, SUBLANES), :] = stg_ref[pl.ds(first, SUBLANES), :] | tails_ref[e]
        part = pl.multiple_of((stage_group + f_ref[base + e]) * SUBLANES, SUBLANES)
        tails_ref[e] = jnp.where(rem_ref[base + e] > 0, stg_ref[pl.ds(part, SUBLANES), :], 0)
        _group_copies(fl_ref[base + e], stg_ref, stage_group, xs_ref, hg_ref[base + e], sem, False)
        return cr

    lax.fori_loop(0, N_EXPERTS, finish_run, 0)

    @pl.when(i == pl.num_programs(0) - 1)
    def _():
        zero_ref[...] = jnp.zeros_like(zero_ref)
        tail_group = fs_ref[N_EXPERTS]
        n_tail = (n_slots // SUBLANES - tail_group) // (MOE_BLK // SUBLANES)

        def tail_copy(j, wait):
            off = pl.multiple_of(tail_group * SUBLANES + j * MOE_BLK, MOE_BLK)
            cp = pltpu.make_async_copy(zero_ref, xs_ref.at[pl.ds(off, MOE_BLK)], fill_sem)
            cp.wait() if wait else cp.start()

        for wait in (False, True):
            def pad_body(e, cr):
                _group_copies(flen_ref[e], zero_ref, 0, xs_ref, fs_ref[e], fill_sem, wait)
                return cr
            lax.fori_loop(0, N_EXPERTS, pad_body, 0)
            lax.fori_loop(0, n_tail, lambda j, cr: (tail_copy(j, wait), cr)[1], 0)

    @pl.when(i > 0)
    def _():
        _wait_groups(tot_ref[jnp.maximum(i - 1, 0)], stg2_ref.at[1 - slot], xs_ref, sem2.at[1 - slot])

    @pl.when(i == pl.num_programs(0) - 1)
    def _():
        _wait_groups(tot_ref[i], stg_ref, xs_ref, sem)


def _dispatch(tables, fill_groups, fill_len_groups, h, rt, stage_base, n_slots):
    t, d = h.shape
    tm = DISP_TM
    rt_per = rt.shape[2] // tm
    grid_spec = pltpu.PrefetchScalarGridSpec(
        num_scalar_prefetch=len(tables) + 2,
        grid=(t // tm,),
        in_specs=[pl.BlockSpec((tm, d), lambda i, *_: (i, 0)),
                  pl.BlockSpec((1, RT_ROWS, tm), lambda i, *_: (i // rt_per, 0, i % rt_per)),
                  pl.BlockSpec((1, LANES, 1), lambda i, *_: (i, 0, 0)),
                  pl.BlockSpec((tm, tm), lambda i, *_: (0, 0))],
        out_specs=pl.BlockSpec(memory_space=pl.ANY),
        scratch_shapes=[pltpu.VMEM((2, STAGE_ROWS, XS_W), jnp.int32),
                        pltpu.VMEM((N_EXPERTS, SUBLANES, XS_W), jnp.int32),
                        pltpu.VMEM((MOE_BLK, XS_W), jnp.int32),
                        pltpu.SemaphoreType.DMA((2,)), pltpu.SemaphoreType.DMA(())],
    )
    return pl.pallas_call(
        _dispatch_kernel,
        grid_spec=grid_spec,
        out_shape=jax.ShapeDtypeStruct((n_slots, XS_W), jnp.int32),
        compiler_params=_cparams(("arbitrary",)),
    )(*tables, fill_groups, fill_len_groups, h, rt, stage_base, _strict_upper(tm))


def _expert_kernel(be_ref, nused_ref, xs_ref, w1_ref, b1_ref, w2_ref, b2_ref, perm_ref, y_ref,
                   w1s_ref, w2s_ref):
    i = pl.program_id(0)
    prev = be_ref[jnp.maximum(i - 1, 0)]
    changed = (i == 0) | (be_ref[i] != prev)
    dff2 = w1_ref.shape[2]
    tile = 2 * LANES

    @pl.when(changed & (i < nused_ref[0]))
    def _():
        for j in range(dff2 // tile):
            wj = w1_ref[0, :, j * tile:(j + 1) * tile].astype(BF16)
            w1s_ref[:, j * tile:(j + 1) * tile] = _dot(wj, perm_ref[...]).astype(BF16)
        w2s_ref[...] = w2_ref[0].astype(BF16)

    @pl.when(i < nused_ref[0])
    def _():
        hi_mask = jnp.int32(-65536)
        words = xs_ref[...]
        x = jnp.concatenate(
            [pltpu.bitcast(lax.shift_left(words, 16), F32).astype(BF16),
             pltpu.bitcast(words & hi_mask, F32).astype(BF16)], axis=1)
        acts = []
        for j in range(dff2 // tile):
            hb = _dot(x, w1s_ref[:, j * tile:(j + 1) * tile]) + b1_ref[0, :, j * tile:(j + 1) * tile]
            gp = jnp.minimum(hb[:, 0:LANES], SWIGLU_LIMIT)
            up = jnp.clip(hb[:, LANES:tile], -SWIGLU_LIMIT, SWIGLU_LIMIT)
            acts.append((gp * _sigmoid(SWIGLU_ALPHA * gp) * (up + 1.0)).astype(BF16))
        act = jnp.concatenate(acts, axis=1)
        y = _dot(act, w2s_ref[...]) + b2_ref[0]
        lo = pltpu.bitcast(y[:, 0:PACK_W].astype(BF16).astype(F32), jnp.int32)
        hi = pltpu.bitcast(y[:, PACK_W:2 * PACK_W].astype(BF16).astype(F32), jnp.int32)
        y_ref[...] = lax.shift_right_logical(lo, 16) | (hi & hi_mask)

    @pl.when(i >= nused_ref[0])
    def _():
        y_ref[...] = jnp.zeros_like(y_ref)


def _experts(block_exp, n_used, xs, w1, b1p, w2, b2, perm, n_blocks):
    blk = MOE_BLK
    d = w1.shape[1]
    dff2 = w1.shape[2]
    dff = w2.shape[1]

    def x_idx(i, be, nu):
        return (jnp.minimum(i, nu[0] - 1), 0)

    grid_spec = pltpu.PrefetchScalarGridSpec(
        num_scalar_prefetch=2,
        grid=(n_blocks,),
        in_specs=[pl.BlockSpec((blk, XS_W), x_idx),
                  pl.BlockSpec((1, d, dff2), lambda i, be, nu: (be[i], 0, 0)),
                  pl.BlockSpec((1, 1, dff2), lambda i, be, nu: (be[i], 0, 0)),
                  pl.BlockSpec((1, dff, d), lambda i, be, nu: (be[i], 0, 0)),
                  pl.BlockSpec((1, 1, d), lambda i, be, nu: (be[i], 0, 0)),
                  pl.BlockSpec(perm.shape, lambda i, be, nu: (0, 0))],
        out_specs=pl.BlockSpec((blk, PACK_W), lambda i, be, nu: (i, 0)),
        scratch_shapes=[pltpu.VMEM((d, dff2), BF16), pltpu.VMEM((dff, d), BF16)],
    )
    return pl.pallas_call(
        _expert_kernel,
        grid_spec=grid_spec,
        out_shape=jax.ShapeDtypeStruct((n_blocks * blk, PACK_W), jnp.int32),
        compiler_params=_cparams(("arbitrary",)),
    )(block_exp, n_used, xs, w1, b1p, w2, b2, perm)


def _combine_kernel(o_ref, g_ref, hg_ref, tot_ref, x1_ref, rt_ref, sb_ref, tri_ref, g2_ref, fnw_ref,
                    ys_ref, out_ref, stg_ref, sem):
    tm = x1_ref.shape[1]
    step = pl.program_id(0) * pl.num_programs(1) + pl.program_id(1)
    n_steps = pl.num_programs(0) * pl.num_programs(1)
    slot = step % 2
    hi_mask = jnp.int32(-65536)

    def fetch(tile, into):
        def body(e, cr):
            _group_copies(g_ref[tile * N_EXPERTS + e], ys_ref, hg_ref[tile * N_EXPERTS + e],
                          stg_ref.at[into], o_ref[tile * N_EXPERTS + e], sem.at[into], False)
            return cr
        lax.fori_loop(0, N_EXPERTS, body, 0)

    @pl.when(step == 0)
    def _():
        stg_ref[...] = jnp.zeros_like(stg_ref)
        fetch(0, 0)

    @pl.when(step + 1 < n_steps)
    def _():
        fetch(step + 1, 1 - slot)

    _wait_groups(tot_ref[step], ys_ref, stg_ref.at[slot], sem.at[slot])

    rt_t = rt_ref[0]
    pos_rows = _stage_positions(rt_t, sb_ref[0], tri_ref[...])
    expert_i = lax.broadcasted_iota(jnp.int32, (LANES, tm), 0)
    stacked = jnp.zeros((LANES, tm), F32)
    for k in range(TOP_K):
        stacked = jnp.where(expert_i == k, pos_rows[k], stacked)
        stacked = jnp.where(expert_i == TOP_K + k, rt_t[GATE_LANE + k:GATE_LANE + k + 1, :], stacked)
    cols = stacked.T

    def unpermute(n_rows):
        col_f = lax.broadcasted_iota(jnp.int32, (tm, n_rows), 1).astype(F32)
        sel = jnp.zeros((tm, n_rows), F32)
        for k in range(TOP_K):
            sel = jnp.where(col_f == cols[:, k:k + 1], cols[:, TOP_K + k:TOP_K + k + 1], sel)
        sel = sel.astype(BF16)
        words = stg_ref[slot, 0:n_rows, :]
        lo = pltpu.bitcast(lax.shift_left(words, 16), F32).astype(BF16)
        hi = pltpu.bitcast(words & hi_mask, F32).astype(BF16)
        moe = jnp.concatenate([_dot(sel, lo), _dot(sel, hi)], axis=1)
        x2 = x1_ref[0] + g2_ref[0] * moe
        ms = jnp.mean(x2 * x2, axis=-1, keepdims=True)
        out_ref[0] = x2 * lax.rsqrt(ms + RMS_EPS) * fnw_ref[...]

    used_rows = tot_ref[step] * SUBLANES
    short = STAGE_ROWS - STAGE_CHUNK
    pl.when(used_rows <= short)(functools.partial(unpermute, short))
    pl.when(used_rows > short)(functools.partial(unpermute, STAGE_ROWS))


def _combine(tables, x1, rt, stage_base, g2, fnw, ys):
    b, l, d = x1.shape
    tm = DISP_TM
    per_l = l // tm
    rt_per = rt.shape[2] // tm
    tok = lambda n: pl.BlockSpec((1, tm, n), lambda i, j, *_: (i, j, 0))
    grid_spec = pltpu.PrefetchScalarGridSpec(
        num_scalar_prefetch=len(tables),
        grid=(b, per_l),
        in_specs=[tok(d),
                  pl.BlockSpec((1, RT_ROWS, tm),
                               lambda i, j, *_: ((i * per_l + j) // rt_per, 0, (i * per_l + j) % rt_per)),
                  pl.BlockSpec((1, LANES, 1), lambda i, j, *_: (i * per_l + j, 0, 0)),
                  pl.BlockSpec((tm, tm), lambda i, j, *_: (0, 0)),
                  pl.BlockSpec((1, 1, d), lambda i, j, *_: (i, 0, 0)),
                  pl.BlockSpec(fnw.shape, lambda i, j, *_: (0, 0)),
                  pl.BlockSpec(memory_space=pl.ANY)],
        out_specs=tok(d),
        scratch_shapes=[pltpu.VMEM((2, STAGE_ROWS, PACK_W), jnp.int32),
                        pltpu.SemaphoreType.DMA((2,))],
    )
    return pl.pallas_call(
        _combine_kernel,
        grid_spec=grid_spec,
        out_shape=jax.ShapeDtypeStruct((b, l, d), F32),
        compiler_params=_cparams(("arbitrary", "arbitrary")),
    )(*tables, x1, rt, stage_base, _strict_upper(tm), g2, fnw, ys)


def _pool_constants():
    i = np.arange(POOL_TM)[:, None]
    j = np.arange(POOL_TM)[None, :]
    same_row = (i // GRID_W) == (j // GRID_W)
    mats, cnts = [], []
    for w in POOL_WINDOWS:
        band = same_row & (j - i >= -(w // 2)) & (j - i < w - w // 2)
        mats.append(band)
        cnts.append(np.broadcast_to(band.sum(axis=1, keepdims=True), (POOL_TM, POOL_GW)))
    return (jnp.asarray(np.stack(mats), BF16), jnp.asarray(np.stack(cnts), F32))


def _head_expand():
    e = np.zeros((LANES, SSD_W), np.float32)
    for h in range(SSD_HEADS):
        e[h, h * SSD_HEADDIM:(h + 1) * SSD_HEADDIM] = 1.0
    return jnp.asarray(e, BF16)


def _deinterleave_perm():
    n = 2 * LANES
    p = np.zeros((n, n), np.float32)
    for k in range(LANES):
        p[2 * k, k] = 1.0
        p[2 * k + 1, LANES + k] = 1.0
    return jnp.asarray(p, BF16)


def _strict_upper(n):
    return jnp.asarray(np.triu(np.ones((n, n), np.float32), 1), BF16)


def _pad_lanes(a, n):
    return jnp.pad(a, [(0, 0)] * (a.ndim - 1) + [(0, n - a.shape[-1])])


def kernel(x, c, ctx, c_ctx, w_mod, b_mod, norm1_w, norm2_w, w_in, conv_w, conv_b, dt_bias, a_log,
           d_skip, ssd_norm_w, pool_w, pool_scale, w_out, router_w, router_b, w1, b1, w2, b2,
           final_norm_w):
    depth = w_mod.shape[0]
    assert depth == 1, "single-layer problem"
    b, l, d = x.shape
    lc = ctx.shape[1]
    xbcdt = CONV_CH + 2 * SSD_HEADS

    mod_rows = 2 * SUBLANES
    cc = jnp.zeros((mod_rows, d), F32).at[0:b].set(c).at[b].set(c_ctx)
    mod = _modulation(cc, w_mod[0], b_mod[0])
    sh1, sc1, g1, sh2, sc2, g2 = [m.reshape(b, 1, d) for m in jnp.split(mod[0:b], 6, axis=-1)]
    csh1, csc1 = [jnp.broadcast_to(m.reshape(1, 1, d), (b, 1, d))
                  for m in jnp.split(mod[b:b + 1], 6, axis=-1)[0:2]]

    wi = w_in[0]
    wx = wi[:, 0:CONV_CH].astype(BF16)
    wd = jnp.concatenate([_pad_lanes(wi[:, CONV_CH:CONV_CH + SSD_HEADS], LANES),
                          _pad_lanes(wi[:, CONV_CH + SSD_HEADS:xbcdt], LANES)], axis=1).astype(BF16)
    wz = wi[:, xbcdt:xbcdt + SSD_W].astype(BF16)
    wp = wi[:, xbcdt + SSD_W:].astype(BF16)
    n1 = norm1_w[0].reshape(1, d)
    conv_w8 = jnp.pad(conv_w[0], ((0, SUBLANES - SSD_CONV), (0, 0)))
    conv_b1 = conv_b[0].reshape(1, CONV_CH)
    dtb = _pad_lanes(dt_bias[0], LANES).reshape(2, 1, LANES)
    alog = _pad_lanes(a_log[0], LANES).reshape(2, 1, LANES)
    dsk = jnp.repeat(d_skip[0], SSD_HEADDIM).reshape(1, SSD_W)
    snw = ssd_norm_w[0].reshape(1, SSD_W)
    expand = _head_expand()

    xbc_c, dt_c = _inproj(ctx, n1, csh1, csc1, (wx, wd), (BF16, F32), min(PROJ_TM, lc))
    zero_state = jnp.zeros((b, 2, SSD_GROUPS, SSD_STATE, GROUP_W), F32)
    _, ctx_states = _ssd(xbc_c, dt_c, None, conv_w8, conv_b1, dtb, alog, dsk, snw, expand, zero_state)

    xbc, dt, z, u_pool = _inproj(x, n1, sh1, sc1, (wx, wd, wz, wp), (BF16, F32, BF16, BF16), PROJ_TM)
    y_ssd, _ = _ssd(xbc, dt, z, conv_w8, conv_b1, dtb, alog, dsk, snw, expand, ctx_states)

    pool_a, pool_cnt = _pool_constants()
    pw = pool_w[0].astype(BF16)
    zero_gw = jnp.zeros((POOL_GW, POOL_GW), BF16)
    pw_pairs = jnp.stack([jnp.block([[pw[2 * gp], zero_gw], [zero_gw, pw[2 * gp + 1]]])
                          for gp in range(len(POOL_WINDOWS) // 2)])
    x1, h2, rt, tcnt = _mix(
        y_ssd, u_pool, x, pool_a, pool_cnt, pw_pairs, pool_scale[0].reshape(1, POOL_W),
        w_out[0].astype(BF16), g1, norm2_w[0].reshape(1, d), sh2, sc2,
        _pad_lanes(router_w[0], LANES), _pad_lanes(router_b[0].reshape(1, N_EXPERTS), LANES))

    i32 = jnp.int32
    t = b * l
    n_tiles = t // DISP_TM
    tc = tcnt.reshape(n_tiles, N_EXPERTS).astype(i32)
    counts = jnp.sum(tc, axis=0)
    run_start = jnp.cumsum(tc, axis=0) - tc
    padded = (counts + MOE_BLK - 1) // MOE_BLK * MOE_BLK
    pad_end = jnp.cumsum(padded)
    pad_start = pad_end - padded
    carried = run_start % SUBLANES
    span = carried + tc
    groups = (span + SUBLANES - 1) // SUBLANES
    full = span // SUBLANES
    stage_group = jnp.cumsum(groups, axis=1) - groups
    slot_group = (pad_start[None, :] + run_start - carried) // SUBLANES
    flush = full.at[n_tiles - 1].set(groups[n_tiles - 1])
    flat = lambda a: a.reshape(n_tiles * N_EXPERTS).astype(i32)
    stage_base = _pad_lanes((stage_group * SUBLANES + carried).astype(F32), LANES)
    stage_base = stage_base.reshape(n_tiles, LANES, 1)
    n_blocks = (t * TOP_K) // MOE_BLK + N_EXPERTS
    n_used = (pad_end[-1] // MOE_BLK).astype(i32).reshape(1)
    blk_start = jnp.minimum(jnp.arange(n_blocks, dtype=i32), n_used[0] - 1) * MOE_BLK
    block_exp = jnp.minimum(jnp.sum(blk_start[:, None] >= pad_end[None, :], axis=1),
                            N_EXPERTS - 1).astype(i32)
    written = (counts + SUBLANES - 1) // SUBLANES * SUBLANES
    fill_group = (jnp.concatenate([pad_start + written, pad_end[-1:]]) // SUBLANES).astype(i32)
    fill_groups = (jnp.concatenate([padded - written, jnp.zeros((1,), i32)]) // SUBLANES).astype(i32)

    xs = _dispatch((flat(stage_group), flat(full), flat(flush), flat(span % SUBLANES), flat(slot_group),
                    jnp.sum(flush, axis=1).astype(i32), jnp.sum(groups, axis=1).astype(i32)),
                   fill_group, fill_groups, h2.reshape(t, d), rt, stage_base,
                   n_blocks * MOE_BLK)
    dff2 = w1.shape[-1]
    b1p = jnp.concatenate(
        [b1[0].reshape(N_EXPERTS, dff2 // (2 * LANES), LANES, 2)[..., 0],
         b1[0].reshape(N_EXPERTS, dff2 // (2 * LANES), LANES, 2)[..., 1]], axis=-1
    ).reshape(N_EXPERTS, 1, dff2)
    ys = _experts(block_exp, n_used, xs, w1[0], b1p, w2[0], b2[0].reshape(N_EXPERTS, 1, d),
                  _deinterleave_perm(), n_blocks)
    return _combine((flat(stage_group), flat(groups), flat(slot_group),
                     jnp.sum(groups, axis=1).astype(i32)), x1, rt, stage_base, g2,
                    final_norm_w.reshape(1, d), ys)
```

```python
import functools

import numpy as np
import jax
import jax.numpy as jnp
from jax import lax
from jax.experimental import pallas as pl
from jax.experimental.pallas import tpu as pltpu

F32 = jnp.float32
BF16 = jnp.bfloat16

SSD_HEADDIM = 64
SSD_GROUPS = 4
SSD_HPG = 6
SSD_HEADS = SSD_GROUPS * SSD_HPG
SSD_STATE = 128
SSD_CONV = 5
SSD_CHUNK = 128
SSD_W = SSD_HEADS * SSD_HEADDIM
GROUP_W = SSD_HPG * SSD_HEADDIM
CONV_CH = SSD_W + 2 * SSD_GROUPS * SSD_STATE
POOL_WINDOWS = (2, 4, 8, 16)
POOL_GW = 128
POOL_W = POOL_GW * len(POOL_WINDOWS)
GRID_W = 64
N_EXPERTS = 32
TOP_K = 4
SWIGLU_ALPHA = 1.702
SWIGLU_LIMIT = 7.0
RMS_EPS = 1e-6
LOG2_E = 1.4426950408889634

LANES = 128
SUBLANES = 8
VMEM_LIMIT_BYTES = 56 * 1024 * 1024

MOD_TN = 1024
PROJ_TM = 512
MIX_TM = 512
POOL_TM = 256
MOE_BLK = 512
DISP_TM = 256
NEG_BIG = -1e30

GATE_LANE = 2 * TOP_K
RT_ROWS = 2 * SUBLANES
PACK_W = 512
XS_W = PACK_W
STAGE_CHUNK = 256
_STAGE_NEED = TOP_K * DISP_TM + N_EXPERTS * 2 * (SUBLANES - 1) + SUBLANES
STAGE_ROWS = -(-_STAGE_NEED // STAGE_CHUNK) * STAGE_CHUNK
RUN_BITS = (32, 16, 8, 4, 2, 1)
RUN_SMALL_BIT = 4
WAIT_BITS = (128, 64, 32, 16, 8, 4, 2, 1)


def _sigmoid(x):
    return 0.5 * jnp.tanh(0.5 * x) + 0.5


def _split2(a):
    hi = a.astype(BF16)
    mid = (a - hi.astype(F32)).astype(BF16)
    return hi, mid


def _split3(a):
    hi = a.astype(BF16)
    r = a - hi.astype(F32)
    mid = r.astype(BF16)
    lo = (r - mid.astype(F32)).astype(BF16)
    return hi, mid, lo


def _dot(a, b):
    return jnp.dot(a, b, preferred_element_type=F32)


def _dot_exact_rhs(a_f32, b_bf16):
    hi, mid = _split2(a_f32)
    return _dot(hi, b_bf16) + _dot(mid, b_bf16)


def _dot_hi(a_f32, b_f32):
    a0, a1, a2 = _split3(a_f32)
    b0, b1, b2 = _split3(b_f32)
    out = _dot(a0, b0)
    out = out + _dot(a0, b1) + _dot(a1, b0)
    out = out + _dot(a1, b1) + _dot(a0, b2) + _dot(a2, b0)
    return out


def _cparams(sem):
    return pltpu.CompilerParams(dimension_semantics=sem, vmem_limit_bytes=VMEM_LIMIT_BYTES)


def _mod_kernel(c_ref, w_ref, b_ref, o_ref):
    c = c_ref[...]
    s = c * _sigmoid(c)
    o_ref[...] = _dot_hi(s, w_ref[...]) + b_ref[...]


def _modulation(cc, w_mod, b_mod):
    rows, d = cc.shape
    n = w_mod.shape[1]
    return pl.pallas_call(
        _mod_kernel,
        grid=(n // MOD_TN,),
        in_specs=[pl.BlockSpec((rows, d), lambda j: (0, 0)),
                  pl.BlockSpec((d, MOD_TN), lambda j: (0, j)),
                  pl.BlockSpec((1, MOD_TN), lambda j: (0, j))],
        out_specs=pl.BlockSpec((rows, MOD_TN), lambda j: (0, j)),
        out_shape=jax.ShapeDtypeStruct((rows, n), F32),
        compiler_params=_cparams(("arbitrary",)),
    )(cc, w_mod, b_mod.reshape(1, n))


def _inproj_kernel(n_groups, x_ref, nw_ref, sh_ref, sc_ref, *refs):
    w_refs = refs[0:n_groups]
    out_refs = refs[n_groups:2 * n_groups]
    x = x_ref[0]
    ms = jnp.mean(x * x, axis=-1, keepdims=True)
    h = x * lax.rsqrt(ms + RMS_EPS) * nw_ref[...]
    hb = (h * (1.0 + sc_ref[0]) + sh_ref[0]).astype(BF16)
    for w_ref, out_ref in zip(w_refs, out_refs):
        out_ref[0] = _dot(hb, w_ref[...]).astype(out_ref.dtype)


def _inproj(x, norm_w, shift, scale, weights, out_dtypes, tm):
    b, l, d = x.shape
    full = lambda a: pl.BlockSpec(a.shape, lambda i, j: (0, 0))
    tok = lambda n: pl.BlockSpec((1, tm, n), lambda i, j: (i, j, 0))
    per_b = pl.BlockSpec((1, 1, d), lambda i, j: (i, 0, 0))
    return pl.pallas_call(
        functools.partial(_inproj_kernel, len(weights)),
        grid=(b, l // tm),
        in_specs=[tok(d), full(norm_w), per_b, per_b] + [full(w) for w in weights],
        out_specs=[tok(w.shape[1]) for w in weights],
        out_shape=[jax.ShapeDtypeStruct((b, l, w.shape[1]), dt)
                   for w, dt in zip(weights, out_dtypes)],
        compiler_params=_cparams(("arbitrary", "arbitrary")),
    )(x, norm_w, shift, scale, *weights)


def _ssd_kernel(nc, emit_y, *refs):
    if emit_y:
        (xbc_ref, prev_ref, next_ref, dt_ref, z_ref, cw_ref, cb_ref, dtb_ref, alog_ref, dsk_ref,
         nw_ref, exp_ref, init_ref, y_ref, fin_ref, act_ref, ybuf_ref, yf_ref, st_ref) = refs
        y_part = (z_ref, dsk_ref, nw_ref, y_ref, ybuf_ref, yf_ref)
    else:
        (xbc_ref, prev_ref, next_ref, dt_ref, cw_ref, cb_ref, dtb_ref, alog_ref,
         exp_ref, init_ref, fin_ref, act_ref, st_ref) = refs
        y_part = None
    ph = pl.program_id(1)
    c = pl.program_id(2)
    ci = c + ph * (nc - 1 - 2 * c)

    @pl.when(c == 0)
    def _():
        st_ref[...] = init_ref[0, ph]

    @pl.when(ph == 0)
    def _():
        _conv_silu(nc, ci, xbc_ref, prev_ref, next_ref, cw_ref, cb_ref, act_ref)

    _ssd_scan(nc, ph, c, ci, dt_ref, dtb_ref, alog_ref, exp_ref, act_ref, st_ref, fin_ref, y_part)


def _conv_silu(nc, ci, xbc_ref, prev_ref, next_ref, cw_ref, cb_ref, act_ref):
    ch = SSD_CHUNK
    has_prev = ci > 0
    has_next = ci < nc - 1
    row8 = lax.broadcasted_iota(jnp.int32, (SUBLANES, LANES), 0)
    half = SSD_CONV // 2
    for j in range(CONV_CH // LANES):
        cols = slice(j * LANES, (j + 1) * LANES)
        n_t = ch // SUBLANES
        tiles = [jnp.where(has_prev, prev_ref[0, SUBLANES:2 * SUBLANES, cols].astype(F32), 0.0)]
        for i in range(n_t // 2):
            both = xbc_ref[0, 2 * i * SUBLANES:2 * (i + 1) * SUBLANES, cols].astype(F32)
            tiles += [both[0:SUBLANES], both[SUBLANES:2 * SUBLANES]]
        tiles.append(jnp.where(has_next, next_ref[0, 0:SUBLANES, cols].astype(F32), 0.0))
        acc = [cb_ref[:, cols] + cw_ref[half:half + 1, cols] * tiles[i + 1] for i in range(n_t)]
        for s in range(1, half + 1):
            rot = [pltpu.roll(tl, s, axis=0) for tl in tiles[0:n_t + 1]]
            wk = cw_ref[half - s:half - s + 1, cols]
            for i in range(n_t):
                acc[i] = acc[i] + wk * jnp.where(row8 < s, rot[i], rot[i + 1])
            rot = [pltpu.roll(tl, SUBLANES - s, axis=0) for tl in tiles[1:n_t + 2]]
            wk = cw_ref[half + s:half + s + 1, cols]
            for i in range(n_t):
                acc[i] = acc[i] + wk * jnp.where(row8 >= SUBLANES - s, rot[i + 1], rot[i])
        silu = [0.5 * v * jnp.tanh(0.5 * v) + 0.5 * v for v in acc]
        act_ref[ci, :, cols] = jnp.concatenate(silu, axis=0).astype(BF16)


def _ssd_scan(nc, ph, c, ci, dt_ref, dtb_ref, alog_ref, exp_ref, act_ref, st_ref, fin_ref, y_part):
    ch = SSD_CHUNK
    emit_y = y_part is not None
    dtr = dt_ref[0] + dtb_ref[0]
    small = jnp.exp(-jnp.abs(dtr))
    one_plus = 1.0 + small
    log1p_small = jnp.where(one_plus == 1.0, small, jnp.log(one_plus) * (small / (one_plus - 1.0)))
    dtv = jnp.maximum(dtr, 0.0) + log1p_small
    a = dtv * (-jnp.exp(alog_ref[0]) * LOG2_E)
    row = lax.broadcasted_iota(jnp.int32, (ch, ch), 0)
    col = lax.broadcasted_iota(jnp.int32, (ch, ch), 1)
    tmask = (row - col) * (1 - 2 * ph) >= 0
    tri = jnp.where(tmask, 1.0, 0.0).astype(BF16)
    cs = _dot_exact_rhs_left(tri, a)
    tot = jnp.where(ph == 0, cs[ch - 1:ch, :], cs[0:1, :])
    cs_t = cs.T
    e_cs = jnp.exp2(cs)
    e_dec = jnp.exp2(tot - cs)
    e_tot = jnp.exp2(tot)
    expand = exp_ref[...]
    wdec_x = _dot((dtv * e_dec).astype(BF16), expand).astype(BF16)
    etot_x = _dot_exact_rhs(jnp.broadcast_to(e_tot, (SUBLANES, LANES)), expand)[0:1]
    if emit_y:
        z_ref, dsk_ref, nw_ref, y_ref, ybuf_ref, yf_ref = y_part
        ecs_x = _dot(e_cs.astype(BF16), expand)
        src_t = cs_t - jnp.log2(dtv).T

    lane = lax.broadcasted_iota(jnp.int32, (ch, LANES), 1)
    for g in range(SSD_GROUPS):
        gs = slice(g * GROUP_W, (g + 1) * GROUP_W)
        b_bf = act_ref[ci, :, SSD_W + g * SSD_STATE:SSD_W + (g + 1) * SSD_STATE]
        x_bf = act_ref[ci, :, gs]
        s_prev = st_ref[g]
        if emit_y:
            c_bf = act_ref[ci, :, SSD_W + (SSD_GROUPS + g) * SSD_STATE:
                           SSD_W + (SSD_GROUPS + g + 1) * SSD_STATE]
            cb = lax.dot_general(c_bf, b_bf, (((1,), (1,)), ((), ())), preferred_element_type=F32)
            y_off = _dot(c_bf, s_prev.astype(BF16)) * ecs_x[:, gs]
        x_dec = x_bf * wdec_x[:, gs]
        st_ref[g] = s_prev * etot_x[:, gs] + lax.dot_general(
            b_bf, x_dec, (((0,), (0,)), ((), ())), preferred_element_type=F32)
        if not emit_y:
            continue
        for q in range(SSD_HPG // 2):
            lmats = []
            for h in (g * SSD_HPG + 2 * q, g * SSD_HPG + 2 * q + 1):
                diff = cs[:, h:h + 1] - src_t[h:h + 1, :]
                dec = jnp.exp2(jnp.where(tmask, diff, NEG_BIG))
                lmats.append((dec * cb).astype(BF16))
            xp = x_bf[:, q * LANES:(q + 1) * LANES]
            zero = jnp.zeros_like(xp)
            rhs = jnp.concatenate([jnp.where(lane < SSD_HEADDIM, xp, zero),
                                   jnp.where(lane >= SSD_HEADDIM, xp, zero)], axis=0)
            y_diag = _dot(jnp.concatenate(lmats, axis=1), rhs)
            ps = slice(g * GROUP_W + q * LANES, g * GROUP_W + (q + 1) * LANES)
            ybuf_ref[:, ps] = y_diag + y_off[:, q * LANES:(q + 1) * LANES]

    if emit_y:
        @pl.when(ph == 0)
        def _():
            yf_ref[ci] = ybuf_ref[...].astype(BF16)

        @pl.when(ph == 1)
        def _():
            yt = (yf_ref[ci].astype(F32) + ybuf_ref[...]
                  + act_ref[ci, :, 0:SSD_W].astype(F32) * dsk_ref[...])
            hz = 0.5 * z_ref[0].astype(F32)
            gt = yt * (hz * jnp.tanh(hz) + hz)
            ms = jnp.mean(gt * gt, axis=-1, keepdims=True)
            y_ref[0] = (gt * lax.rsqrt(ms + RMS_EPS) * nw_ref[...]).astype(y_ref.dtype)

    @pl.when(c == nc - 1)
    def _():
        fin_ref[0, ph] = st_ref[...]


def _dot_exact_rhs_left(sel_bf16, a_f32):
    hi, mid, lo = _split3(a_f32)
    return _dot(sel_bf16, hi) + _dot(sel_bf16, mid) + _dot(sel_bf16, lo)


def _ssd(xbc, dt, z, conv_w8, conv_b, dt_bias, a_log, d_skip_x, norm_w, expand, init):
    emit_y = z is not None
    b, l, _ = xbc.shape
    ch = SSD_CHUNK
    nc = l // ch
    halo = 2 * SUBLANES
    per_ch = ch // halo

    def cidx(ph, c):
        return c + ph * (nc - 1 - 2 * c)

    def xidx(ph, c):
        return jnp.where(ph == 0, c, nc - 1)

    def out_idx(ph, c):
        return jnp.where(ph == 0, nc - 1, nc - 1 - c)

    full2 = lambda a: pl.BlockSpec(a.shape, lambda i, ph, c: (0, 0))
    st_spec = pl.BlockSpec((1, 2, SSD_GROUPS, SSD_STATE, GROUP_W), lambda i, ph, c: (i, 0, 0, 0, 0))
    per_phase = pl.BlockSpec((1, 1, LANES), lambda i, ph, c: (ph, 0, 0))
    y_spec = pl.BlockSpec((1, ch, SSD_W), lambda i, ph, c: (i, out_idx(ph, c), 0))
    operands = [(xbc, pl.BlockSpec((1, ch, CONV_CH), lambda i, ph, c: (i, xidx(ph, c), 0))),
                (xbc, pl.BlockSpec((1, halo, CONV_CH),
                                   lambda i, ph, c: (i, jnp.maximum(xidx(ph, c) * per_ch - 1, 0), 0))),
                (xbc, pl.BlockSpec((1, halo, CONV_CH),
                                   lambda i, ph, c: (i, jnp.minimum((xidx(ph, c) + 1) * per_ch,
                                                                    l // halo - 1), 0))),
                (dt, pl.BlockSpec((1, ch, LANES), lambda i, ph, c: (i, cidx(ph, c), ph)))]
    if emit_y:
        operands.append((z, y_spec))
    operands += [(conv_w8, full2(conv_w8)), (conv_b, full2(conv_b)),
                 (dt_bias, per_phase), (a_log, per_phase)]
    if emit_y:
        operands += [(d_skip_x, full2(d_skip_x)), (norm_w, full2(norm_w))]
    operands += [(expand, full2(expand)), (init, st_spec)]
    st_shape = jax.ShapeDtypeStruct((b, 2, SSD_GROUPS, SSD_STATE, GROUP_W), F32)
    scratch = [pltpu.VMEM((nc, ch, CONV_CH), BF16)]
    if emit_y:
        scratch += [pltpu.VMEM((ch, SSD_W), F32),
                    pltpu.VMEM((nc, ch, SSD_W), BF16)]
    scratch.append(pltpu.VMEM((SSD_GROUPS, SSD_STATE, GROUP_W), F32))
    outs = pl.pallas_call(
        functools.partial(_ssd_kernel, nc, emit_y),
        grid=(b, 2, nc),
        in_specs=[spec for _, spec in operands],
        out_specs=[y_spec, st_spec] if emit_y else [st_spec],
        out_shape=[jax.ShapeDtypeStruct((b, l, SSD_W), BF16), st_shape] if emit_y else [st_shape],
        scratch_shapes=scratch,
        compiler_params=_cparams(("arbitrary", "arbitrary", "arbitrary")),
    )(*[a for a, _ in operands])
    return outs if emit_y else (None, outs[0])


def _mix_kernel(y_ref, u_ref, x_ref, pa_ref, pcnt_ref, pw_ref, psc_ref, wo_ref, g1_ref,
                nw_ref, sh_ref, sc_ref, rw_ref, rb_ref,
                x1_ref, h_ref, rt_ref, tcnt_ref):
    tm = x_ref.shape[1]

    pooled = []
    for g in range(len(POOL_WINDOWS)):
        parts = []
        for r in range(tm // POOL_TM):
            u = u_ref[0, r * POOL_TM:(r + 1) * POOL_TM, g * POOL_GW:(g + 1) * POOL_GW]
            wsum = _dot(pa_ref[g], u)
            parts.append((wsum / pcnt_ref[g] - u.astype(F32)).astype(BF16))
        pooled.append(jnp.concatenate(parts, axis=0))
    mapped = []
    for gp in range(len(POOL_WINDOWS) // 2):
        pair = jnp.concatenate(pooled[2 * gp:2 * gp + 2], axis=1)
        mapped.append((_dot(pair, pw_ref[gp])
                       * psc_ref[:, 2 * gp * POOL_GW:(2 * gp + 2) * POOL_GW]).astype(BF16))
    y_pool = jnp.concatenate(mapped, axis=1)

    mix = _dot(y_ref[0], wo_ref[0:SSD_W, :]) + _dot(y_pool, wo_ref[SSD_W:SSD_W + POOL_W, :])
    x1 = x_ref[0] + g1_ref[0] * mix
    x1_ref[0] = x1

    ms = jnp.mean(x1 * x1, axis=-1, keepdims=True)
    h = x1 * lax.rsqrt(ms + RMS_EPS) * nw_ref[...]
    h = h * (1.0 + sc_ref[0]) + sh_ref[0]
    h_ref[0] = h

    h0, h1 = _split2(h)
    rw2 = jnp.concatenate(_split2(rw_ref[...]), axis=1)
    t0 = _dot(h0, rw2)
    t1 = _dot(h1, rw2)
    logits = (t0[:, 0:LANES] + t0[:, LANES:2 * LANES] + t1[:, 0:LANES] + t1[:, LANES:2 * LANES]
              + rb_ref[...])
    work = logits.T[0:N_EXPERTS, :]
    expert_f = lax.broadcasted_iota(jnp.int32, (N_EXPERTS, tm), 0).astype(F32)
    vals, idxs = [], []
    for _ in range(TOP_K):
        m = jnp.max(work, axis=0, keepdims=True)
        first_idx = jnp.min(jnp.where(work == m, expert_f, float(N_EXPERTS)), axis=0, keepdims=True)
        vals.append(m)
        idxs.append(first_idx)
        work = jnp.where(expert_f == first_idx, 2.0 * NEG_BIG, work)
    exps = [jnp.exp(v - vals[0]) for v in vals]
    denom = exps[0] + exps[1] + exps[2] + exps[3]

    rec_row = lax.broadcasted_iota(jnp.int32, (RT_ROWS, tm), 0)
    rec = jnp.zeros((RT_ROWS, tm), F32)
    onehot = jnp.zeros((N_EXPERTS, tm), F32)
    for k in range(TOP_K):
        rec = jnp.where(rec_row == k, idxs[k], rec)
        rec = jnp.where(rec_row == GATE_LANE + k, exps[k] / denom, rec)
        onehot = onehot + jnp.where(expert_f == idxs[k], 1.0, 0.0)
    rt_ref[0] = rec
    for r in range(tm // DISP_TM):
        tcnt_ref[0, r] = jnp.sum(onehot[:, r * DISP_TM:(r + 1) * DISP_TM], axis=1, keepdims=True)


def _mix(y, u, x, pool_a, pool_cnt, pool_w, pool_scale, w_out, g1, norm_w, shift, scale,
         router_w, router_b):
    b, l, d = x.shape
    tm = MIX_TM
    per_l = l // tm
    tok = lambda n: pl.BlockSpec((1, tm, n), lambda i, j: (i, j, 0))
    per_b = pl.BlockSpec((1, 1, d), lambda i, j: (i, 0, 0))
    full = lambda a: pl.BlockSpec(a.shape, lambda i, j: (0,) * a.ndim)
    return pl.pallas_call(
        _mix_kernel,
        grid=(b, per_l),
        in_specs=[tok(SSD_W), tok(POOL_W), tok(d), full(pool_a), full(pool_cnt), full(pool_w),
                  full(pool_scale), full(w_out), per_b, full(norm_w), per_b, per_b,
                  full(router_w), full(router_b)],
        out_specs=[tok(d), tok(d),
                   pl.BlockSpec((1, RT_ROWS, tm), lambda i, j: (i * per_l + j, 0, 0)),
                   pl.BlockSpec((1, tm // DISP_TM, N_EXPERTS, 1), lambda i, j: (i * per_l + j, 0, 0, 0))],
        out_shape=[jax.ShapeDtypeStruct((b, l, d), F32),
                   jax.ShapeDtypeStruct((b, l, d), F32),
                   jax.ShapeDtypeStruct((b * per_l, RT_ROWS, tm), F32),
                   jax.ShapeDtypeStruct((b * per_l, tm // DISP_TM, N_EXPERTS, 1), F32)],
        compiler_params=_cparams(("arbitrary", "arbitrary")),
    )(y, u, x, pool_a, pool_cnt, pool_w, pool_scale, w_out, g1, norm_w, shift, scale,
      router_w, router_b)


def _group_copies(n_groups, src_ref, src_group, dst_ref, dst_group, sem, wait):
    def pieces(bits):
        for bit in bits:
            @pl.when((n_groups & bit) != 0)
            def _():
                done = n_groups & ~(2 * bit - 1)
                src = src_ref.at[pl.ds(pl.multiple_of((src_group + done) * SUBLANES, SUBLANES),
                                       bit * SUBLANES)]
                dst = dst_ref.at[pl.ds(pl.multiple_of((dst_group + done) * SUBLANES, SUBLANES),
                                       bit * SUBLANES)]
                cp = pltpu.make_async_copy(src, dst, sem)
                cp.wait() if wait else cp.start()

    split = RUN_BITS.index(RUN_SMALL_BIT)
    pl.when(n_groups >= 2 * RUN_SMALL_BIT)(lambda: pieces(RUN_BITS[:split]))
    pieces(RUN_BITS[split:])


def _wait_groups(n_groups, src_ref, dst_ref, sem):
    for bit in WAIT_BITS:
        @pl.when((n_groups & bit) != 0)
        def _():
            rows = bit * SUBLANES
            pltpu.make_async_copy(src_ref.at[pl.ds(0, rows)], dst_ref.at[pl.ds(0, rows)], sem).wait()


def _stage_positions(rt_t, base_col, upper):
    tm = rt_t.shape[1]
    expert_f = lax.broadcasted_iota(jnp.int32, (LANES, tm), 0).astype(F32)
    hots = [expert_f == rt_t[k:k + 1, :] for k in range(TOP_K)]
    onehot = jnp.zeros((LANES, tm), F32)
    for hot in hots:
        onehot = onehot + jnp.where(hot, 1.0, 0.0)
    pos = _dot(onehot.astype(BF16), upper) + base_col
    return [jnp.sum(jnp.where(hot, pos, 0.0), axis=0, keepdims=True) for hot in hots]


def _dispatch_kernel(o_ref, f_ref, fl_ref, rem_ref, hg_ref, tot_ref, used_ref, fs_ref, flen_ref,
                     h_ref, rt_ref, sb_ref, tri_ref, xs_ref,
                     stg2_ref, tails_ref, zero_ref, sem2, fill_sem):
    i = pl.program_id(0)
    tm = h_ref.shape[0]
    n_slots = xs_ref.shape[0]
    base = i * N_EXPERTS
    hi_mask = jnp.int32(-65536)
    slot = i % 2
    stg_ref = stg2_ref.at[slot]
    sem = sem2.at[slot]

    @pl.when(i == 0)
    def _():
        tails_ref[...] = jnp.zeros_like(tails_ref)
        stg2_ref[...] = jnp.zeros_like(stg2_ref)

    pos = _stage_positions(rt_ref[0], sb_ref[0], tri_ref[...])

    hb = h_ref[...].astype(BF16)
    half = PACK_W
    used_rows = used_ref[i] * SUBLANES

    def stage_chunk(r_lo):
        row_f = (lax.broadcasted_iota(jnp.int32, (STAGE_CHUNK, tm), 0) + r_lo).astype(F32)
        sel = jnp.zeros((STAGE_CHUNK, tm), F32)
        for k in range(TOP_K):
            sel = jnp.where(row_f == pos[k], 1.0, sel)
        moved = _dot(sel.astype(BF16), hb)
        lo = pltpu.bitcast(moved[:, 0:half], jnp.int32)
        hi = pltpu.bitcast(moved[:, half:2 * half], jnp.int32)
        stg_ref[r_lo:r_lo + STAGE_CHUNK, :] = lax.shift_right_logical(lo, 16) | (hi & hi_mask)

    always = TOP_K * tm // STAGE_CHUNK
    for rc in range(STAGE_ROWS // STAGE_CHUNK):
        if rc < always:
            stage_chunk(rc * STAGE_CHUNK)
        else:
            pl.when(rc * STAGE_CHUNK < used_rows)(functools.partial(stage_chunk, rc * STAGE_CHUNK))

    def finish_run(e, cr):
        stage_group = o_ref[base + e]
        first = pl.multiple_of(stage_group * SUBLANES, SUBLANES)
        stg_ref[pl.ds(first, SUBLANES), :] = stg_ref[pl.ds(first, SUBLANES), :] | tails_ref[e]
        part = pl.multiple_of((stage_group + f_ref[base + e]) * SUBLANES, SUBLANES)
        tails_ref[e] = jnp.where(rem_ref[base + e] > 0, stg_ref[pl.ds(part, SUBLANES), :], 0)
        _group_copies(fl_ref[base + e], stg_ref, stage_group, xs_ref, hg_ref[base + e], sem, False)
        return cr

    lax.fori_loop(0, N_EXPERTS, finish_run, 0)

    @pl.when(i == pl.num_programs(0) - 1)
    def _():
        zero_ref[...] = jnp.zeros_like(zero_ref)
        tail_group = fs_ref[N_EXPERTS]
        n_tail = (n_slots // SUBLANES - tail_group) // (MOE_BLK // SUBLANES)

        def tail_copy(j, wait):
            off = pl.multiple_of(tail_group * SUBLANES + j * MOE_BLK, MOE_BLK)
            cp = pltpu.make_async_copy(zero_ref, xs_ref.at[pl.ds(off, MOE_BLK)], fill_sem)
            cp.wait() if wait else cp.start()

        for wait in (False, True):
            def pad_body(e, cr):
                _group_copies(flen_ref[e], zero_ref, 0, xs_ref, fs_ref[e], fill_sem, wait)
                return cr
            lax.fori_loop(0, N_EXPERTS, pad_body, 0)
            lax.fori_loop(0, n_tail, lambda j, cr: (tail_copy(j, wait), cr)[1], 0)

    @pl.when(i > 0)
    def _():
        _wait_groups(tot_ref[jnp.maximum(i - 1, 0)], stg2_ref.at[1 - slot], xs_ref, sem2.at[1 - slot])

    @pl.when(i == pl.num_programs(0) - 1)
    def _():
        _wait_groups(tot_ref[i], stg_ref, xs_ref, sem)


def _dispatch(tables, fill_groups, fill_len_groups, h, rt, stage_base, n_slots):
    t, d = h.shape
    tm = DISP_TM
    rt_per = rt.shape[2] // tm
    grid_spec = pltpu.PrefetchScalarGridSpec(
        num_scalar_prefetch=len(tables) + 2,
        grid=(t // tm,),
        in_specs=[pl.BlockSpec((tm, d), lambda i, *_: (i, 0)),
                  pl.BlockSpec((1, RT_ROWS, tm), lambda i, *_: (i // rt_per, 0, i % rt_per)),
                  pl.BlockSpec((1, LANES, 1), lambda i, *_: (i, 0, 0)),
                  pl.BlockSpec((tm, tm), lambda i, *_: (0, 0))],
        out_specs=pl.BlockSpec(memory_space=pl.ANY),
        scratch_shapes=[pltpu.VMEM((2, STAGE_ROWS, XS_W), jnp.int32),
                        pltpu.VMEM((N_EXPERTS, SUBLANES, XS_W), jnp.int32),
                        pltpu.VMEM((MOE_BLK, XS_W), jnp.int32),
                        pltpu.SemaphoreType.DMA((2,)), pltpu.SemaphoreType.DMA(())],
    )
    return pl.pallas_call(
        _dispatch_kernel,
        grid_spec=grid_spec,
        out_shape=jax.ShapeDtypeStruct((n_slots, XS_W), jnp.int32),
        compiler_params=_cparams(("arbitrary",)),
    )(*tables, fill_groups, fill_len_groups, h, rt, stage_base, _strict_upper(tm))


def _expert_kernel(be_ref, nused_ref, xs_ref, w1_ref, b1_ref, w2_ref, b2_ref, perm_ref, y_ref,
                   w1s_ref, w2s_ref):
    i = pl.program_id(0)
    prev = be_ref[jnp.maximum(i - 1, 0)]
    changed = (i == 0) | (be_ref[i] != prev)
    dff2 = w1_ref.shape[2]
    tile = 2 * LANES

    @pl.when(changed & (i < nused_ref[0]))
    def _():
        for j in range(dff2 // tile):
            wj = w1_ref[0, :, j * tile:(j + 1) * tile].astype(BF16)
            w1s_ref[:, j * tile:(j + 1) * tile] = _dot(wj, perm_ref[...]).astype(BF16)
        w2s_ref[...] = w2_ref[0].astype(BF16)

    @pl.when(i < nused_ref[0])
    def _():
        hi_mask = jnp.int32(-65536)
        words = xs_ref[...]
        x = jnp.concatenate(
            [pltpu.bitcast(lax.shift_left(words, 16), F32).astype(BF16),
             pltpu.bitcast(words & hi_mask, F32).astype(BF16)], axis=1)
        acts = []
        for j in range(dff2 // tile):
            hb = _dot(x, w1s_ref[:, j * tile:(j + 1) * tile]) + b1_ref[0, :, j * tile:(j + 1) * tile]
            gp = jnp.minimum(hb[:, 0:LANES], SWIGLU_LIMIT)
            up = jnp.clip(hb[:, LANES:tile], -SWIGLU_LIMIT, SWIGLU_LIMIT)
            acts.append((gp * _sigmoid(SWIGLU_ALPHA * gp) * (up + 1.0)).astype(BF16))
        act = jnp.concatenate(acts, axis=1)
        y = _dot(act, w2s_ref[...]) + b2_ref[0]
        lo = pltpu.bitcast(y[:, 0:PACK_W].astype(BF16).astype(F32), jnp.int32)
        hi = pltpu.bitcast(y[:, PACK_W:2 * PACK_W].astype(BF16).astype(F32), jnp.int32)
        y_ref[...] = lax.shift_right_logical(lo, 16) | (hi & hi_mask)

    @pl.when(i >= nused_ref[0])
    def _():
        y_ref[...] = jnp.zeros_like(y_ref)


def _experts(block_exp, n_used, xs, w1, b1p, w2, b2, perm, n_blocks):
    blk = MOE_BLK
    d = w1.shape[1]
    dff2 = w1.shape[2]
    dff = w2.shape[1]

    def x_idx(i, be, nu):
        return (jnp.minimum(i, nu[0] - 1), 0)

    grid_spec = pltpu.PrefetchScalarGridSpec(
        num_scalar_prefetch=2,
        grid=(n_blocks,),
        in_specs=[pl.BlockSpec((blk, XS_W), x_idx),
                  pl.BlockSpec((1, d, dff2), lambda i, be, nu: (be[i], 0, 0)),
                  pl.BlockSpec((1, 1, dff2), lambda i, be, nu: (be[i], 0, 0)),
                  pl.BlockSpec((1, dff, d), lambda i, be, nu: (be[i], 0, 0)),
                  pl.BlockSpec((1, 1, d), lambda i, be, nu: (be[i], 0, 0)),
                  pl.BlockSpec(perm.shape, lambda i, be, nu: (0, 0))],
        out_specs=pl.BlockSpec((blk, PACK_W), lambda i, be, nu: (i, 0)),
        scratch_shapes=[pltpu.VMEM((d, dff2), BF16), pltpu.VMEM((dff, d), BF16)],
    )
    return pl.pallas_call(
        _expert_kernel,
        grid_spec=grid_spec,
        out_shape=jax.ShapeDtypeStruct((n_blocks * blk, PACK_W), jnp.int32),
        compiler_params=_cparams(("arbitrary",)),
    )(block_exp, n_used, xs, w1, b1p, w2, b2, perm)


def _combine_kernel(o_ref, g_ref, hg_ref, tot_ref, x1_ref, rt_ref, sb_ref, tri_ref, g2_ref, fnw_ref,
                    ys_ref, out_ref, stg_ref, sem):
    tm = x1_ref.shape[1]
    step = pl.program_id(0) * pl.num_programs(1) + pl.program_id(1)
    n_steps = pl.num_programs(0) * pl.num_programs(1)
    slot = step % 2
    hi_mask = jnp.int32(-65536)

    def fetch(tile, into):
        def body(e, cr):
            _group_copies(g_ref[tile * N_EXPERTS + e], ys_ref, hg_ref[tile * N_EXPERTS + e],
                          stg_ref.at[into], o_ref[tile * N_EXPERTS + e], sem.at[into], False)
            return cr
        lax.fori_loop(0, N_EXPERTS, body, 0)

    @pl.when(step == 0)
    def _():
        stg_ref[...] = jnp.zeros_like(stg_ref)
        fetch(0, 0)

    @pl.when(step + 1 < n_steps)
    def _():
        fetch(step + 1, 1 - slot)

    _wait_groups(tot_ref[step], ys_ref, stg_ref.at[slot], sem.at[slot])

    rt_t = rt_ref[0]
    pos_rows = _stage_positions(rt_t, sb_ref[0], tri_ref[...])
    expert_i = lax.broadcasted_iota(jnp.int32, (LANES, tm), 0)
    stacked = jnp.zeros((LANES, tm), F32)
    for k in range(TOP_K):
        stacked = jnp.where(expert_i == k, pos_rows[k], stacked)
        stacked = jnp.where(expert_i == TOP_K + k, rt_t[GATE_LANE + k:GATE_LANE + k + 1, :], stacked)
    cols = stacked.T

    def unpermute(n_rows):
        col_f = lax.broadcasted_iota(jnp.int32, (tm, n_rows), 1).astype(F32)
        sel = jnp.zeros((tm, n_rows), F32)
        for k in range(TOP_K):
            sel = jnp.where(col_f == cols[:, k:k + 1], cols[:, TOP_K + k:TOP_K + k + 1], sel)
        sel = sel.astype(BF16)
        words = stg_ref[slot, 0:n_rows, :]
        lo = pltpu.bitcast(lax.shift_left(words, 16), F32).astype(BF16)
        hi = pltpu.bitcast(words & hi_mask, F32).astype(BF16)
        moe = jnp.concatenate([_dot(sel, lo), _dot(sel, hi)], axis=1)
        x2 = x1_ref[0] + g2_ref[0] * moe
        ms = jnp.mean(x2 * x2, axis=-1, keepdims=True)
        out_ref[0] = x2 * lax.rsqrt(ms + RMS_EPS) * fnw_ref[...]

    used_rows = tot_ref[step] * SUBLANES
    short = STAGE_ROWS - STAGE_CHUNK
    pl.when(used_rows <= short)(functools.partial(unpermute, short))
    pl.when(used_rows > short)(functools.partial(unpermute, STAGE_ROWS))


def _combine(tables, x1, rt, stage_base, g2, fnw, ys):
    b, l, d = x1.shape
    tm = DISP_TM
    per_l = l // tm
    rt_per = rt.shape[2] // tm
    tok = lambda n: pl.BlockSpec((1, tm, n), lambda i, j, *_: (i, j, 0))
    grid_spec = pltpu.PrefetchScalarGridSpec(
        num_scalar_prefetch=len(tables),
        grid=(b, per_l),
        in_specs=[tok(d),
                  pl.BlockSpec((1, RT_ROWS, tm),
                               lambda i, j, *_: ((i * per_l + j) // rt_per, 0, (i * per_l + j) % rt_per)),
                  pl.BlockSpec((1, LANES, 1), lambda i, j, *_: (i * per_l + j, 0, 0)),
                  pl.BlockSpec((tm, tm), lambda i, j, *_: (0, 0)),
                  pl.BlockSpec((1, 1, d), lambda i, j, *_: (i, 0, 0)),
                  pl.BlockSpec(fnw.shape, lambda i, j, *_: (0, 0)),
                  pl.BlockSpec(memory_space=pl.ANY)],
        out_specs=tok(d),
        scratch_shapes=[pltpu.VMEM((2, STAGE_ROWS, PACK_W), jnp.int32),
                        pltpu.SemaphoreType.DMA((2,))],
    )
    return pl.pallas_call(
        _combine_kernel,
        grid_spec=grid_spec,
        out_shape=jax.ShapeDtypeStruct((b, l, d), F32),
        compiler_params=_cparams(("arbitrary", "arbitrary")),
    )(*tables, x1, rt, stage_base, _strict_upper(tm), g2, fnw, ys)


def _pool_constants():
    i = np.arange(POOL_TM)[:, None]
    j = np.arange(POOL_TM)[None, :]
    same_row = (i // GRID_W) == (j // GRID_W)
    mats, cnts = [], []
    for w in POOL_WINDOWS:
        band = same_row & (j - i >= -(w // 2)) & (j - i < w - w // 2)
        mats.append(band)
        cnts.append(np.broadcast_to(band.sum(axis=1, keepdims=True), (POOL_TM, POOL_GW)))
    return (jnp.asarray(np.stack(mats), BF16), jnp.asarray(np.stack(cnts), F32))


def _head_expand():
    e = np.zeros((LANES, SSD_W), np.float32)
    for h in range(SSD_HEADS):
        e[h, h * SSD_HEADDIM:(h + 1) * SSD_HEADDIM] = 1.0
    return jnp.asarray(e, BF16)


def _deinterleave_perm():
    n = 2 * LANES
    p = np.zeros((n, n), np.float32)
    for k in range(LANES):
        p[2 * k, k] = 1.0
        p[2 * k + 1, LANES + k] = 1.0
    return jnp.asarray(p, BF16)


def _strict_upper(n):
    return jnp.asarray(np.triu(np.ones((n, n), np.float32), 1), BF16)


def _pad_lanes(a, n):
    return jnp.pad(a, [(0, 0)] * (a.ndim - 1) + [(0, n - a.shape[-1])])


def kernel(x, c, ctx, c_ctx, w_mod, b_mod, norm1_w, norm2_w, w_in, conv_w, conv_b, dt_bias, a_log,
           d_skip, ssd_norm_w, pool_w, pool_scale, w_out, router_w, router_b, w1, b1, w2, b2,
           final_norm_w):
    depth = w_mod.shape[0]
    assert depth == 1, "single-layer problem"
    b, l, d = x.shape
    lc = ctx.shape[1]
    xbcdt = CONV_CH + 2 * SSD_HEADS

    mod_rows = 2 * SUBLANES
    cc = jnp.zeros((mod_rows, d), F32).at[0:b].set(c).at[b].set(c_ctx)
    mod = _modulation(cc, w_mod[0], b_mod[0])
    sh1, sc1, g1, sh2, sc2, g2 = [m.reshape(b, 1, d) for m in jnp.split(mod[0:b], 6, axis=-1)]
    csh1, csc1 = [jnp.broadcast_to(m.reshape(1, 1, d), (b, 1, d))
                  for m in jnp.split(mod[b:b + 1], 6, axis=-1)[0:2]]

    wi = w_in[0]
    wx = wi[:, 0:CONV_CH].astype(BF16)
    wd = jnp.concatenate([_pad_lanes(wi[:, CONV_CH:CONV_CH + SSD_HEADS], LANES),
                          _pad_lanes(wi[:, CONV_CH + SSD_HEADS:xbcdt], LANES)], axis=1).astype(BF16)
    wz = wi[:, xbcdt:xbcdt + SSD_W].astype(BF16)
    wp = wi[:, xbcdt + SSD_W:].astype(BF16)
    n1 = norm1_w[0].reshape(1, d)
    conv_w8 = jnp.pad(conv_w[0], ((0, SUBLANES - SSD_CONV), (0, 0)))
    conv_b1 = conv_b[0].reshape(1, CONV_CH)
    dtb = _pad_lanes(dt_bias[0], LANES).reshape(2, 1, LANES)
    alog = _pad_lanes(a_log[0], LANES).reshape(2, 1, LANES)
    dsk = jnp.repeat(d_skip[0], SSD_HEADDIM).reshape(1, SSD_W)
    snw = ssd_norm_w[0].reshape(1, SSD_W)
    expand = _head_expand()

    xbc_c, dt_c = _inproj(ctx, n1, csh1, csc1, (wx, wd), (BF16, F32), min(PROJ_TM, lc))
    zero_state = jnp.zeros((b, 2, SSD_GROUPS, SSD_STATE, GROUP_W), F32)
    _, ctx_states = _ssd(xbc_c, dt_c, None, conv_w8, conv_b1, dtb, alog, dsk, snw, expand, zero_state)

    xbc, dt, z, u_pool = _inproj(x, n1, sh1, sc1, (wx, wd, wz, wp), (BF16, F32, BF16, BF16), PROJ_TM)
    y_ssd, _ = _ssd(xbc, dt, z, conv_w8, conv_b1, dtb, alog, dsk, snw, expand, ctx_states)

    pool_a, pool_cnt = _pool_constants()
    pw = pool_w[0].astype(BF16)
    zero_gw = jnp.zeros((POOL_GW, POOL_GW), BF16)
    pw_pairs = jnp.stack([jnp.block([[pw[2 * gp], zero_gw], [zero_gw, pw[2 * gp + 1]]])
                          for gp in range(len(POOL_WINDOWS) // 2)])
    x1, h2, rt, tcnt = _mix(
        y_ssd, u_pool, x, pool_a, pool_cnt, pw_pairs, pool_scale[0].reshape(1, POOL_W),
        w_out[0].astype(BF16), g1, norm2_w[0].reshape(1, d), sh2, sc2,
        _pad_lanes(router_w[0], LANES), _pad_lanes(router_b[0].reshape(1, N_EXPERTS), LANES))

    i32 = jnp.int32
    t = b * l
    n_tiles = t // DISP_TM
    tc = tcnt.reshape(n_tiles, N_EXPERTS).astype(i32)
    counts = jnp.sum(tc, axis=0)
    run_start = jnp.cumsum(tc, axis=0) - tc
    padded = (counts + MOE_BLK - 1) // MOE_BLK * MOE_BLK
    pad_end = jnp.cumsum(padded)
    pad_start = pad_end - padded
    carried = run_start % SUBLANES
    span = carried + tc
    groups = (span + SUBLANES - 1) // SUBLANES
    full = span // SUBLANES
    stage_group = jnp.cumsum(groups, axis=1) - groups
    slot_group = (pad_start[None, :] + run_start - carried) // SUBLANES
    flush = full.at[n_tiles - 1].set(groups[n_tiles - 1])
    flat = lambda a: a.reshape(n_tiles * N_EXPERTS).astype(i32)
    stage_base = _pad_lanes((stage_group * SUBLANES + carried).astype(F32), LANES)
    stage_base = stage_base.reshape(n_tiles, LANES, 1)
    n_blocks = (t * TOP_K) // MOE_BLK + N_EXPERTS
    n_used = (pad_end[-1] // MOE_BLK).astype(i32).reshape(1)
    blk_start = jnp.minimum(jnp.arange(n_blocks, dtype=i32), n_used[0] - 1) * MOE_BLK
    block_exp = jnp.minimum(jnp.sum(blk_start[:, None] >= pad_end[None, :], axis=1),
                            N_EXPERTS - 1).astype(i32)
    written = (counts + SUBLANES - 1) // SUBLANES * SUBLANES
    fill_group = (jnp.concatenate([pad_start + written, pad_end[-1:]]) // SUBLANES).astype(i32)
    fill_groups = (jnp.concatenate([padded - written, jnp.zeros((1,), i32)]) // SUBLANES).astype(i32)

    xs = _dispatch((flat(stage_group), flat(full), flat(flush), flat(span % SUBLANES), flat(slot_group),
                    jnp.sum(flush, axis=1).astype(i32), jnp.sum(groups, axis=1).astype(i32)),
                   fill_group, fill_groups, h2.reshape(t, d), rt, stage_base,
                   n_blocks * MOE_BLK)
    dff2 = w1.shape[-1]
    b1p = jnp.concatenate(
        [b1[0].reshape(N_EXPERTS, dff2 // (2 * LANES), LANES, 2)[..., 0],
         b1[0].reshape(N_EXPERTS, dff2 // (2 * LANES), LANES, 2)[..., 1]], axis=-1
    ).reshape(N_EXPERTS, 1, dff2)
    ys = _experts(block_exp, n_used, xs, w1[0], b1p, w2[0], b2[0].reshape(N_EXPERTS, 1, d),
                  _deinterleave_perm(), n_blocks)
    return _combine((flat(stage_group), flat(groups), flat(slot_group),
                     jnp.sum(groups, axis=1).astype(i32)), x1, rt, stage_base, g2,
                    final_norm_w.reshape(1, d), ys)
```

```python
import functools

import numpy as np
import jax
import jax.numpy as jnp
from jax import lax
from jax.experimental import pallas as pl
from jax.experimental.pallas import tpu as pltpu

F32 = jnp.float32
BF16 = jnp.bfloat16

SSD_HEADDIM = 64
SSD_GROUPS = 4
SSD_HPG = 6
SSD_HEADS = SSD_GROUPS * SSD_HPG
SSD_STATE = 128
SSD_CONV = 5
SSD_CHUNK = 128
SSD_W = SSD_HEADS * SSD_HEADDIM
GROUP_W = SSD_HPG * SSD_HEADDIM
CONV_CH = SSD_W + 2 * SSD_GROUPS * SSD_STATE
POOL_WINDOWS = (2, 4, 8, 16)
POOL_GW = 128
POOL_W = POOL_GW * len(POOL_WINDOWS)
GRID_W = 64
N_EXPERTS = 32
TOP_K = 4
SWIGLU_ALPHA = 1.702
SWIGLU_LIMIT = 7.0
RMS_EPS = 1e-6
LOG2_E = 1.4426950408889634

LANES = 128
SUBLANES = 8
VMEM_LIMIT_BYTES = 56 * 1024 * 1024

MOD_TN = 1024
PROJ_TM = 1024
MIX_TM = 512
POOL_TM = 256
MOE_BLK = 512
DISP_TM = 256
NEG_BIG = -1e30

GATE_LANE = 2 * TOP_K
RT_ROWS = 2 * SUBLANES
PACK_W = 512
XS_W = PACK_W
STAGE_CHUNK = 256
_STAGE_NEED = TOP_K * DISP_TM + N_EXPERTS * 2 * (SUBLANES - 1) + SUBLANES
STAGE_ROWS = -(-_STAGE_NEED // STAGE_CHUNK) * STAGE_CHUNK
RUN_BITS = (32, 16, 8, 4, 2, 1)
RUN_SMALL_BIT = 4
WAIT_BITS = (128, 64, 32, 16, 8, 4, 2, 1)


def _sigmoid(x):
    return 0.5 * jnp.tanh(0.5 * x) + 0.5


def _split2(a):
    hi = a.astype(BF16)
    mid = (a - hi.astype(F32)).astype(BF16)
    return hi, mid


def _split3(a):
    hi = a.astype(BF16)
    r = a - hi.astype(F32)
    mid = r.astype(BF16)
    lo = (r - mid.astype(F32)).astype(BF16)
    return hi, mid, lo


def _dot(a, b):
    return jnp.dot(a, b, preferred_element_type=F32)


def _dot_exact_rhs(a_f32, b_bf16):
    hi, mid = _split2(a_f32)
    return _dot(hi, b_bf16) + _dot(mid, b_bf16)


def _dot_hi(a_f32, b_f32):
    a0, a1, a2 = _split3(a_f32)
    b0, b1, b2 = _split3(b_f32)
    out = _dot(a0, b0)
    out = out + _dot(a0, b1) + _dot(a1, b0)
    out = out + _dot(a1, b1) + _dot(a0, b2) + _dot(a2, b0)
    return out


def _cparams(sem):
    return pltpu.CompilerParams(dimension_semantics=sem, vmem_limit_bytes=VMEM_LIMIT_BYTES)


def _mod_kernel(c_ref, w_ref, b_ref, o_ref):
    c = c_ref[...]
    s = c * _sigmoid(c)
    o_ref[...] = _dot_hi(s, w_ref[...]) + b_ref[...]


def _modulation(cc, w_mod, b_mod):
    rows, d = cc.shape
    n = w_mod.shape[1]
    return pl.pallas_call(
        _mod_kernel,
        grid=(n // MOD_TN,),
        in_specs=[pl.BlockSpec((rows, d), lambda j: (0, 0)),
                  pl.BlockSpec((d, MOD_TN), lambda j: (0, j)),
                  pl.BlockSpec((1, MOD_TN), lambda j: (0, j))],
        out_specs=pl.BlockSpec((rows, MOD_TN), lambda j: (0, j)),
        out_shape=jax.ShapeDtypeStruct((rows, n), F32),
        compiler_params=_cparams(("arbitrary",)),
    )(cc, w_mod, b_mod.reshape(1, n))


def _inproj_kernel(n_groups, x_ref, nw_ref, sh_ref, sc_ref, *refs):
    w_refs = refs[0:n_groups]
    out_refs = refs[n_groups:2 * n_groups]
    x = x_ref[0]
    ms = jnp.mean(x * x, axis=-1, keepdims=True)
    h = x * lax.rsqrt(ms + RMS_EPS) * nw_ref[...]
    hb = (h * (1.0 + sc_ref[0]) + sh_ref[0]).astype(BF16)
    for w_ref, out_ref in zip(w_refs, out_refs):
        out_ref[0] = _dot(hb, w_ref[...]).astype(out_ref.dtype)


def _inproj(x, norm_w, shift, scale, weights, out_dtypes, tm):
    b, l, d = x.shape
    full = lambda a: pl.BlockSpec(a.shape, lambda i, j: (0, 0))
    tok = lambda n: pl.BlockSpec((1, tm, n), lambda i, j: (i, j, 0))
    per_b = pl.BlockSpec((1, 1, d), lambda i, j: (i, 0, 0))
    return pl.pallas_call(
        functools.partial(_inproj_kernel, len(weights)),
        grid=(b, l // tm),
        in_specs=[tok(d), full(norm_w), per_b, per_b] + [full(w) for w in weights],
        out_specs=[tok(w.shape[1]) for w in weights],
        out_shape=[jax.ShapeDtypeStruct((b, l, w.shape[1]), dt)
                   for w, dt in zip(weights, out_dtypes)],
        compiler_params=_cparams(("arbitrary", "arbitrary")),
    )(x, norm_w, shift, scale, *weights)


def _ssd_kernel(nc, emit_y, *refs):
    if emit_y:
        (xbc_ref, prev_ref, next_ref, dt_ref, z_ref, cw_ref, cb_ref, dtb_ref, alog_ref, dsk_ref,
         nw_ref, exp_ref, init_ref, y_ref, fin_ref, act_ref, ybuf_ref, yf_ref, st_ref) = refs
        y_part = (z_ref, dsk_ref, nw_ref, y_ref, ybuf_ref, yf_ref)
    else:
        (xbc_ref, prev_ref, next_ref, dt_ref, cw_ref, cb_ref, dtb_ref, alog_ref,
         exp_ref, init_ref, fin_ref, act_ref, st_ref) = refs
        y_part = None
    ph = pl.program_id(1)
    c = pl.program_id(2)
    ci = c + ph * (nc - 1 - 2 * c)

    @pl.when(c == 0)
    def _():
        st_ref[...] = init_ref[0, ph]

    @pl.when(ph == 0)
    def _():
        _conv_silu(nc, ci, xbc_ref, prev_ref, next_ref, cw_ref, cb_ref, act_ref)

    _ssd_scan(nc, ph, c, ci, dt_ref, dtb_ref, alog_ref, exp_ref, act_ref, st_ref, fin_ref, y_part)


def _conv_silu(nc, ci, xbc_ref, prev_ref, next_ref, cw_ref, cb_ref, act_ref):
    ch = SSD_CHUNK
    has_prev = ci > 0
    has_next = ci < nc - 1
    row8 = lax.broadcasted_iota(jnp.int32, (SUBLANES, LANES), 0)
    half = SSD_CONV // 2
    for j in range(CONV_CH // LANES):
        cols = slice(j * LANES, (j + 1) * LANES)
        n_t = ch // SUBLANES
        tiles = [jnp.where(has_prev, prev_ref[0, SUBLANES:2 * SUBLANES, cols].astype(F32), 0.0)]
        for i in range(n_t // 2):
            both = xbc_ref[0, 2 * i * SUBLANES:2 * (i + 1) * SUBLANES, cols].astype(F32)
            tiles += [both[0:SUBLANES], both[SUBLANES:2 * SUBLANES]]
        tiles.append(jnp.where(has_next, next_ref[0, 0:SUBLANES, cols].astype(F32), 0.0))
        acc = [cb_ref[:, cols] + cw_ref[half:half + 1, cols] * tiles[i + 1] for i in range(n_t)]
        for s in range(1, half + 1):
            rot = [pltpu.roll(tl, s, axis=0) for tl in tiles[0:n_t + 1]]
            wk = cw_ref[half - s:half - s + 1, cols]
            for i in range(n_t):
                acc[i] = acc[i] + wk * jnp.where(row8 < s, rot[i], rot[i + 1])
            rot = [pltpu.roll(tl, SUBLANES - s, axis=0) for tl in tiles[1:n_t + 2]]
            wk = cw_ref[half + s:half + s + 1, cols]
            for i in range(n_t):
                acc[i] = acc[i] + wk * jnp.where(row8 >= SUBLANES - s, rot[i + 1], rot[i])
        silu = [0.5 * v * jnp.tanh(0.5 * v) + 0.5 * v for v in acc]
        act_ref[ci, :, cols] = jnp.concatenate(silu, axis=0).astype(BF16)


def _ssd_scan(nc, ph, c, ci, dt_ref, dtb_ref, alog_ref, exp_ref, act_ref, st_ref, fin_ref, y_part):
    ch = SSD_CHUNK
    emit_y = y_part is not None
    dtr = dt_ref[0] + dtb_ref[0]
    small = jnp.exp(-jnp.abs(dtr))
    one_plus = 1.0 + small
    log1p_small = jnp.where(one_plus == 1.0, small, jnp.log(one_plus) * (small / (one_plus - 1.0)))
    dtv = jnp.maximum(dtr, 0.0) + log1p_small
    a = dtv * (-jnp.exp(alog_ref[0]) * LOG2_E)
    row = lax.broadcasted_iota(jnp.int32, (ch, ch), 0)
    col = lax.broadcasted_iota(jnp.int32, (ch, ch), 1)
    tmask = (row - col) * (1 - 2 * ph) >= 0
    tri = jnp.where(tmask, 1.0, 0.0).astype(BF16)
    cs = _dot_exact_rhs_left(tri, a)
    tot = jnp.where(ph == 0, cs[ch - 1:ch, :], cs[0:1, :])
    cs_t = cs.T
    e_cs = jnp.exp2(cs)
    e_dec = jnp.exp2(tot - cs)
    e_tot = jnp.exp2(tot)
    expand = exp_ref[...]
    wdec_x = _dot((dtv * e_dec).astype(BF16), expand).astype(BF16)
    etot_x = _dot_exact_rhs(jnp.broadcast_to(e_tot, (SUBLANES, LANES)), expand)[0:1]
    if emit_y:
        z_ref, dsk_ref, nw_ref, y_ref, ybuf_ref, yf_ref = y_part
        ecs_x = _dot(e_cs.astype(BF16), expand)
        src_t = cs_t - jnp.log2(dtv).T

    lane = lax.broadcasted_iota(jnp.int32, (ch, LANES), 1)
    for g in range(SSD_GROUPS):
        gs = slice(g * GROUP_W, (g + 1) * GROUP_W)
        b_bf = act_ref[ci, :, SSD_W + g * SSD_STATE:SSD_W + (g + 1) * SSD_STATE]
        x_bf = act_ref[ci, :, gs]
        s_prev = st_ref[g]
        if emit_y:
            c_bf = act_ref[ci, :, SSD_W + (SSD_GROUPS + g) * SSD_STATE:
                           SSD_W + (SSD_GROUPS + g + 1) * SSD_STATE]
            cb = lax.dot_general(c_bf, b_bf, (((1,), (1,)), ((), ())), preferred_element_type=F32)
            y_off = _dot(c_bf, s_prev.astype(BF16)) * ecs_x[:, gs]
        x_dec = x_bf * wdec_x[:, gs]
        st_ref[g] = s_prev * etot_x[:, gs] + lax.dot_general(
            b_bf, x_dec, (((0,), (0,)), ((), ())), preferred_element_type=F32)
        if not emit_y:
            continue
        for q in range(SSD_HPG // 2):
            lmats = []
            for h in (g * SSD_HPG + 2 * q, g * SSD_HPG + 2 * q + 1):
                diff = cs[:, h:h + 1] - src_t[h:h + 1, :]
                dec = jnp.exp2(jnp.where(tmask, diff, NEG_BIG))
                lmats.append((dec * cb).astype(BF16))
            xp = x_bf[:, q * LANES:(q + 1) * LANES]
            zero = jnp.zeros_like(xp)
            rhs = jnp.concatenate([jnp.where(lane < SSD_HEADDIM, xp, zero),
                                   jnp.where(lane >= SSD_HEADDIM, xp, zero)], axis=0)
            y_diag = _dot(jnp.concatenate(lmats, axis=1), rhs)
            ps = slice(g * GROUP_W + q * LANES, g * GROUP_W + (q + 1) * LANES)
            ybuf_ref[:, ps] = y_diag + y_off[:, q * LANES:(q + 1) * LANES]

    if emit_y:
        @pl.when(ph == 0)
        def _():
            yf_ref[ci] = ybuf_ref[...].astype(BF16)

        @pl.when(ph == 1)
        def _():
            yt = (yf_ref[ci].astype(F32) + ybuf_ref[...]
                  + act_ref[ci, :, 0:SSD_W].astype(F32) * dsk_ref[...])
            hz = 0.5 * z_ref[0].astype(F32)
            gt = yt * (hz * jnp.tanh(hz) + hz)
            ms = jnp.mean(gt * gt, axis=-1, keepdims=True)
            y_ref[0] = (gt * lax.rsqrt(ms + RMS_EPS) * nw_ref[...]).astype(y_ref.dtype)

    @pl.when(c == nc - 1)
    def _():
        fin_ref[0, ph] = st_ref[...]


def _dot_exact_rhs_left(sel_bf16, a_f32):
    hi, mid, lo = _split3(a_f32)
    return _dot(sel_bf16, hi) + _dot(sel_bf16, mid) + _dot(sel_bf16, lo)


def _ssd(xbc, dt, z, conv_w8, conv_b, dt_bias, a_log, d_skip_x, norm_w, expand, init):
    emit_y = z is not None
    b, l, _ = xbc.shape
    ch = SSD_CHUNK
    nc = l // ch
    halo = 2 * SUBLANES
    per_ch = ch // halo

    def cidx(ph, c):
        return c + ph * (nc - 1 - 2 * c)

    def xidx(ph, c):
        return jnp.where(ph == 0, c, nc - 1)

    def out_idx(ph, c):
        return jnp.where(ph == 0, nc - 1, nc - 1 - c)

    full2 = lambda a: pl.BlockSpec(a.shape, lambda i, ph, c: (0, 0))
    st_spec = pl.BlockSpec((1, 2, SSD_GROUPS, SSD_STATE, GROUP_W), lambda i, ph, c: (i, 0, 0, 0, 0))
    per_phase = pl.BlockSpec((1, 1, LANES), lambda i, ph, c: (ph, 0, 0))
    y_spec = pl.BlockSpec((1, ch, SSD_W), lambda i, ph, c: (i, out_idx(ph, c), 0))
    operands = [(xbc, pl.BlockSpec((1, ch, CONV_CH), lambda i, ph, c: (i, xidx(ph, c), 0))),
                (xbc, pl.BlockSpec((1, halo, CONV_CH),
                                   lambda i, ph, c: (i, jnp.maximum(xidx(ph, c) * per_ch - 1, 0), 0))),
                (xbc, pl.BlockSpec((1, halo, CONV_CH),
                                   lambda i, ph, c: (i, jnp.minimum((xidx(ph, c) + 1) * per_ch,
                                                                    l // halo - 1), 0))),
                (dt, pl.BlockSpec((1, ch, LANES), lambda i, ph, c: (i, cidx(ph, c), ph)))]
    if emit_y:
        operands.append((z, y_spec))
    operands += [(conv_w8, full2(conv_w8)), (conv_b, full2(conv_b)),
                 (dt_bias, per_phase), (a_log, per_phase)]
    if emit_y:
        operands += [(d_skip_x, full2(d_skip_x)), (norm_w, full2(norm_w))]
    operands += [(expand, full2(expand)), (init, st_spec)]
    st_shape = jax.ShapeDtypeStruct((b, 2, SSD_GROUPS, SSD_STATE, GROUP_W), F32)
    scratch = [pltpu.VMEM((nc, ch, CONV_CH), BF16)]
    if emit_y:
        scratch += [pltpu.VMEM((ch, SSD_W), F32),
                    pltpu.VMEM((nc, ch, SSD_W), BF16)]
    scratch.append(pltpu.VMEM((SSD_GROUPS, SSD_STATE, GROUP_W), F32))
    outs = pl.pallas_call(
        functools.partial(_ssd_kernel, nc, emit_y),
        grid=(b, 2, nc),
        in_specs=[spec for _, spec in operands],
        out_specs=[y_spec, st_spec] if emit_y else [st_spec],
        out_shape=[jax.ShapeDtypeStruct((b, l, SSD_W), BF16), st_shape] if emit_y else [st_shape],
        scratch_shapes=scratch,
        compiler_params=_cparams(("arbitrary", "arbitrary", "arbitrary")),
    )(*[a for a, _ in operands])
    return outs if emit_y else (None, outs[0])


def _mix_kernel(y_ref, u_ref, x_ref, pa_ref, pcnt_ref, pw_ref, psc_ref, wo_ref, g1_ref,
                nw_ref, sh_ref, sc_ref, rw_ref, rb_ref,
                x1_ref, h_ref, rt_ref, tcnt_ref):
    tm = x_ref.shape[1]

    pooled = []
    for g in range(len(POOL_WINDOWS)):
        parts = []
        for r in range(tm // POOL_TM):
            u = u_ref[0, r * POOL_TM:(r + 1) * POOL_TM, g * POOL_GW:(g + 1) * POOL_GW]
            wsum = _dot(pa_ref[g], u)
            parts.append((wsum / pcnt_ref[g] - u.astype(F32)).astype(BF16))
        pooled.append(jnp.concatenate(parts, axis=0))
    mapped = []
    for gp in range(len(POOL_WINDOWS) // 2):
        pair = jnp.concatenate(pooled[2 * gp:2 * gp + 2], axis=1)
        mapped.append((_dot(pair, pw_ref[gp])
                       * psc_ref[:, 2 * gp * POOL_GW:(2 * gp + 2) * POOL_GW]).astype(BF16))
    y_pool = jnp.concatenate(mapped, axis=1)

    mix = _dot(y_ref[0], wo_ref[0:SSD_W, :]) + _dot(y_pool, wo_ref[SSD_W:SSD_W + POOL_W, :])
    x1 = x_ref[0] + g1_ref[0] * mix
    x1_ref[0] = x1

    ms = jnp.mean(x1 * x1, axis=-1, keepdims=True)
    h = x1 * lax.rsqrt(ms + RMS_EPS) * nw_ref[...]
    h = h * (1.0 + sc_ref[0]) + sh_ref[0]
    h_ref[0] = h

    h0, h1 = _split2(h)
    rw2 = jnp.concatenate(_split2(rw_ref[...]), axis=1)
    t0 = _dot(h0, rw2)
    t1 = _dot(h1, rw2)
    logits = (t0[:, 0:LANES] + t0[:, LANES:2 * LANES] + t1[:, 0:LANES] + t1[:, LANES:2 * LANES]
              + rb_ref[...])
    work = logits.T[0:N_EXPERTS, :]
    expert_f = lax.broadcasted_iota(jnp.int32, (N_EXPERTS, tm), 0).astype(F32)
    vals, idxs = [], []
    for _ in range(TOP_K):
        m = jnp.max(work, axis=0, keepdims=True)
        first_idx = jnp.min(jnp.where(work == m, expert_f, float(N_EXPERTS)), axis=0, keepdims=True)
        vals.append(m)
        idxs.append(first_idx)
        work = jnp.where(expert_f == first_idx, 2.0 * NEG_BIG, work)
    exps = [jnp.exp(v - vals[0]) for v in vals]
    denom = exps[0] + exps[1] + exps[2] + exps[3]

    rec_row = lax.broadcasted_iota(jnp.int32, (RT_ROWS, tm), 0)
    rec = jnp.zeros((RT_ROWS, tm), F32)
    onehot = jnp.zeros((N_EXPERTS, tm), F32)
    for k in range(TOP_K):
        rec = jnp.where(rec_row == k, idxs[k], rec)
        rec = jnp.where(rec_row == GATE_LANE + k, exps[k] / denom, rec)
        onehot = onehot + jnp.where(expert_f == idxs[k], 1.0, 0.0)
    rt_ref[0] = rec
    for r in range(tm // DISP_TM):
        tcnt_ref[0, r] = jnp.sum(onehot[:, r * DISP_TM:(r + 1) * DISP_TM], axis=1, keepdims=True)


def _mix(y, u, x, pool_a, pool_cnt, pool_w, pool_scale, w_out, g1, norm_w, shift, scale,
         router_w, router_b):
    b, l, d = x.shape
    tm = MIX_TM
    per_l = l // tm
    tok = lambda n: pl.BlockSpec((1, tm, n), lambda i, j: (i, j, 0))
    per_b = pl.BlockSpec((1, 1, d), lambda i, j: (i, 0, 0))
    full = lambda a: pl.BlockSpec(a.shape, lambda i, j: (0,) * a.ndim)
    return pl.pallas_call(
        _mix_kernel,
        grid=(b, per_l),
        in_specs=[tok(SSD_W), tok(POOL_W), tok(d), full(pool_a), full(pool_cnt), full(pool_w),
                  full(pool_scale), full(w_out), per_b, full(norm_w), per_b, per_b,
                  full(router_w), full(router_b)],
        out_specs=[tok(d), tok(d),
                   pl.BlockSpec((1, RT_ROWS, tm), lambda i, j: (i * per_l + j, 0, 0)),
                   pl.BlockSpec((1, tm // DISP_TM, N_EXPERTS, 1), lambda i, j: (i * per_l + j, 0, 0, 0))],
        out_shape=[jax.ShapeDtypeStruct((b, l, d), F32),
                   jax.ShapeDtypeStruct((b, l, d), F32),
                   jax.ShapeDtypeStruct((b * per_l, RT_ROWS, tm), F32),
                   jax.ShapeDtypeStruct((b * per_l, tm // DISP_TM, N_EXPERTS, 1), F32)],
        compiler_params=_cparams(("arbitrary", "arbitrary")),
    )(y, u, x, pool_a, pool_cnt, pool_w, pool_scale, w_out, g1, norm_w, shift, scale,
      router_w, router_b)


def _group_copies(n_groups, src_ref, src_group, dst_ref, dst_group, sem, wait):
    def pieces(bits):
        for bit in bits:
            @pl.when((n_groups & bit) != 0)
            def _():
                done = n_groups & ~(2 * bit - 1)
                src = src_ref.at[pl.ds(pl.multiple_of((src_group + done) * SUBLANES, SUBLANES),
                                       bit * SUBLANES)]
                dst = dst_ref.at[pl.ds(pl.multiple_of((dst_group + done) * SUBLANES, SUBLANES),
                                       bit * SUBLANES)]
                cp = pltpu.make_async_copy(src, dst, sem)
                cp.wait() if wait else cp.start()

    split = RUN_BITS.index(RUN_SMALL_BIT)
    pl.when(n_groups >= 2 * RUN_SMALL_BIT)(lambda: pieces(RUN_BITS[:split]))
    pieces(RUN_BITS[split:])


def _wait_groups(n_groups, src_ref, dst_ref, sem):
    for bit in WAIT_BITS:
        @pl.when((n_groups & bit) != 0)
        def _():
            rows = bit * SUBLANES
            pltpu.make_async_copy(src_ref.at[pl.ds(0, rows)], dst_ref.at[pl.ds(0, rows)], sem).wait()


def _stage_positions(rt_t, base_col, upper):
    tm = rt_t.shape[1]
    expert_f = lax.broadcasted_iota(jnp.int32, (LANES, tm), 0).astype(F32)
    hots = [expert_f == rt_t[k:k + 1, :] for k in range(TOP_K)]
    onehot = jnp.zeros((LANES, tm), F32)
    for hot in hots:
        onehot = onehot + jnp.where(hot, 1.0, 0.0)
    pos = _dot(onehot.astype(BF16), upper) + base_col
    return [jnp.sum(jnp.where(hot, pos, 0.0), axis=0, keepdims=True) for hot in hots]


def _dispatch_kernel(o_ref, f_ref, fl_ref, rem_ref, hg_ref, tot_ref, used_ref, fs_ref, flen_ref,
                     h_ref, rt_ref, sb_ref, tri_ref, xs_ref,
                     stg2_ref, tails_ref, zero_ref, sem2, fill_sem):
    i = pl.program_id(0)
    tm = h_ref.shape[0]
    n_slots = xs_ref.shape[0]
    base = i * N_EXPERTS
    hi_mask = jnp.int32(-65536)
    slot = i % 2
    stg_ref = stg2_ref.at[slot]
    sem = sem2.at[slot]

    @pl.when(i == 0)
    def _():
        tails_ref[...] = jnp.zeros_like(tails_ref)
        stg2_ref[...] = jnp.zeros_like(stg2_ref)

    pos = _stage_positions(rt_ref[0], sb_ref[0], tri_ref[...])

    hb = h_ref[...].astype(BF16)
    half = PACK_W
    used_rows = used_ref[i] * SUBLANES

    def stage_chunk(r_lo):
        row_f = (lax.broadcasted_iota(jnp.int32, (STAGE_CHUNK, tm), 0) + r_lo).astype(F32)
        sel = jnp.zeros((STAGE_CHUNK, tm), F32)
        for k in range(TOP_K):
            sel = jnp.where(row_f == pos[k], 1.0, sel)
        moved = _dot(sel.astype(BF16), hb)
        lo = pltpu.bitcast(moved[:, 0:half], jnp.int32)
        hi = pltpu.bitcast(moved[:, half:2 * half], jnp.int32)
        stg_ref[r_lo:r_lo + STAGE_CHUNK, :] = lax.shift_right_logical(lo, 16) | (hi & hi_mask)

    always = TOP_K * tm // STAGE_CHUNK
    for rc in range(STAGE_ROWS // STAGE_CHUNK):
        if rc < always:
            stage_chunk(rc * STAGE_CHUNK)
        else:
            pl.when(rc * STAGE_CHUNK < used_rows)(functools.partial(stage_chunk, rc * STAGE_CHUNK))

    def finish_run(e, cr):
        stage_group = o_ref[base + e]
        first = pl.multiple_of(stage_group * SUBLANES, SUBLANES)
        stg_ref[pl.ds(first, SUBLANES), :] = stg_ref[pl.ds(first, SUBLANES), :] | tails_ref[e]
        part = pl.multiple_of((stage_group + f_ref[base + e]) * SUBLANES, SUBLANES)
        tails_ref[e] = jnp.where(rem_ref[base + e] > 0, stg_ref[pl.ds(part, SUBLANES), :], 0)
        _group_copies(fl_ref[base + e], stg_ref, stage_group, xs_ref, hg_ref[base + e], sem, False)
        return cr

    lax.fori_loop(0, N_EXPERTS, finish_run, 0)

    @pl.when(i == pl.num_programs(0) - 1)
    def _():
        zero_ref[...] = jnp.zeros_like(zero_ref)
        tail_group = fs_ref[N_EXPERTS]
        n_tail = (n_slots // SUBLANES - tail_group) // (MOE_BLK // SUBLANES)

        def tail_copy(j, wait):
            off = pl.multiple_of(tail_group * SUBLANES + j * MOE_BLK, MOE_BLK)
            cp = pltpu.make_async_copy(zero_ref, xs_ref.at[pl.ds(off, MOE_BLK)], fill_sem)
            cp.wait() if wait else cp.start()

        for wait in (False, True):
            def pad_body(e, cr):
                _group_copies(flen_ref[e], zero_ref, 0, xs_ref, fs_ref[e], fill_sem, wait)
                return cr
            lax.fori_loop(0, N_EXPERTS, pad_body, 0)
            lax.fori_loop(0, n_tail, lambda j, cr: (tail_copy(j, wait), cr)[1], 0)

    @pl.when(i > 0)
    def _():
        _wait_groups(tot_ref[jnp.maximum(i - 1, 0)], stg2_ref.at[1 - slot], xs_ref, sem2.at[1 - slot])

    @pl.when(i == pl.num_programs(0) - 1)
    def _():
        _wait_groups(tot_ref[i], stg_ref, xs_ref, sem)


def _dispatch(tables, fill_groups, fill_len_groups, h, rt, stage_base, n_slots):
    t, d = h.shape
    tm = DISP_TM
    rt_per = rt.shape[2] // tm
    grid_spec = pltpu.PrefetchScalarGridSpec(
        num_scalar_prefetch=len(tables) + 2,
        grid=(t // tm,),
        in_specs=[pl.BlockSpec((tm, d), lambda i, *_: (i, 0)),
                  pl.BlockSpec((1, RT_ROWS, tm), lambda i, *_: (i // rt_per, 0, i % rt_per)),
                  pl.BlockSpec((1, LANES, 1), lambda i, *_: (i, 0, 0)),
                  pl.BlockSpec((tm, tm), lambda i, *_: (0, 0))],
        out_specs=pl.BlockSpec(memory_space=pl.ANY),
        scratch_shapes=[pltpu.VMEM((2, STAGE_ROWS, XS_W), jnp.int32),
                        pltpu.VMEM((N_EXPERTS, SUBLANES, XS_W), jnp.int32),
                        pltpu.VMEM((MOE_BLK, XS_W), jnp.int32),
                        pltpu.SemaphoreType.DMA((2,)), pltpu.SemaphoreType.DMA(())],
    )
    return pl.pallas_call(
        _dispatch_kernel,
        grid_spec=grid_spec,
        out_shape=jax.ShapeDtypeStruct((n_slots, XS_W), jnp.int32),
        compiler_params=_cparams(("arbitrary",)),
    )(*tables, fill_groups, fill_len_groups, h, rt, stage_base, _strict_upper(tm))


def _expert_kernel(be_ref, nused_ref, xs_ref, w1_ref, b1_ref, w2_ref, b2_ref, perm_ref, y_ref,
                   w1s_ref, w2s_ref):
    i = pl.program_id(0)
    prev = be_ref[jnp.maximum(i - 1, 0)]
    changed = (i == 0) | (be_ref[i] != prev)
    dff2 = w1_ref.shape[2]
    tile = 2 * LANES

    @pl.when(changed & (i < nused_ref[0]))
    def _():
        for j in range(dff2 // tile):
            wj = w1_ref[0, :, j * tile:(j + 1) * tile].astype(BF16)
            w1s_ref[:, j * tile:(j + 1) * tile] = _dot(wj, perm_ref[...]).astype(BF16)
        w2s_ref[...] = w2_ref[0].astype(BF16)

    @pl.when(i < nused_ref[0])
    def _():
        hi_mask = jnp.int32(-65536)
        words = xs_ref[...]
        x = jnp.concatenate(
            [pltpu.bitcast(lax.shift_left(words, 16), F32).astype(BF16),
             pltpu.bitcast(words & hi_mask, F32).astype(BF16)], axis=1)
        acts = []
        for j in range(dff2 // tile):
            hb = _dot(x, w1s_ref[:, j * tile:(j + 1) * tile]) + b1_ref[0, :, j * tile:(j + 1) * tile]
            gp = jnp.minimum(hb[:, 0:LANES], SWIGLU_LIMIT)
            up = jnp.clip(hb[:, LANES:tile], -SWIGLU_LIMIT, SWIGLU_LIMIT)
            acts.append((gp * _sigmoid(SWIGLU_ALPHA * gp) * (up + 1.0)).astype(BF16))
        act = jnp.concatenate(acts, axis=1)
        y = _dot(act, w2s_ref[...]) + b2_ref[0]
        lo = pltpu.bitcast(y[:, 0:PACK_W].astype(BF16).astype(F32), jnp.int32)
        hi = pltpu.bitcast(y[:, PACK_W:2 * PACK_W].astype(BF16).astype(F32), jnp.int32)
        y_ref[...] = lax.shift_right_logical(lo, 16) | (hi & hi_mask)

    @pl.when(i >= nused_ref[0])
    def _():
        y_ref[...] = jnp.zeros_like(y_ref)


def _experts(block_exp, n_used, xs, w1, b1p, w2, b2, perm, n_blocks):
    blk = MOE_BLK
    d = w1.shape[1]
    dff2 = w1.shape[2]
    dff = w2.shape[1]

    def x_idx(i, be, nu):
        return (jnp.minimum(i, nu[0] - 1), 0)

    grid_spec = pltpu.PrefetchScalarGridSpec(
        num_scalar_prefetch=2,
        grid=(n_blocks,),
        in_specs=[pl.BlockSpec((blk, XS_W), x_idx),
                  pl.BlockSpec((1, d, dff2), lambda i, be, nu: (be[i], 0, 0)),
                  pl.BlockSpec((1, 1, dff2), lambda i, be, nu: (be[i], 0, 0)),
                  pl.BlockSpec((1, dff, d), lambda i, be, nu: (be[i], 0, 0)),
                  pl.BlockSpec((1, 1, d), lambda i, be, nu: (be[i], 0, 0)),
                  pl.BlockSpec(perm.shape, lambda i, be, nu: (0, 0))],
        out_specs=pl.BlockSpec((blk, PACK_W), lambda i, be, nu: (i, 0)),
        scratch_shapes=[pltpu.VMEM((d, dff2), BF16), pltpu.VMEM((dff, d), BF16)],
    )
    return pl.pallas_call(
        _expert_kernel,
        grid_spec=grid_spec,
        out_shape=jax.ShapeDtypeStruct((n_blocks * blk, PACK_W), jnp.int32),
        compiler_params=_cparams(("arbitrary",)),
    )(block_exp, n_used, xs, w1, b1p, w2, b2, perm)


def _combine_kernel(o_ref, g_ref, hg_ref, tot_ref, x1_ref, rt_ref, sb_ref, tri_ref, g2_ref, fnw_ref,
                    ys_ref, out_ref, stg_ref, sem):
    tm = x1_ref.shape[1]
    step = pl.program_id(0) * pl.num_programs(1) + pl.program_id(1)
    n_steps = pl.num_programs(0) * pl.num_programs(1)
    slot = step % 2
    hi_mask = jnp.int32(-65536)

    def fetch(tile, into):
        def body(e, cr):
            _group_copies(g_ref[tile * N_EXPERTS + e], ys_ref, hg_ref[tile * N_EXPERTS + e],
                          stg_ref.at[into], o_ref[tile * N_EXPERTS + e], sem.at[into], False)
            return cr
        lax.fori_loop(0, N_EXPERTS, body, 0)

    @pl.when(step == 0)
    def _():
        stg_ref[...] = jnp.zeros_like(stg_ref)
        fetch(0, 0)

    @pl.when(step + 1 < n_steps)
    def _():
        fetch(step + 1, 1 - slot)

    _wait_groups(tot_ref[step], ys_ref, stg_ref.at[slot], sem.at[slot])

    rt_t = rt_ref[0]
    pos_rows = _stage_positions(rt_t, sb_ref[0], tri_ref[...])
    expert_i = lax.broadcasted_iota(jnp.int32, (LANES, tm), 0)
    stacked = jnp.zeros((LANES, tm), F32)
    for k in range(TOP_K):
        stacked = jnp.where(expert_i == k, pos_rows[k], stacked)
        stacked = jnp.where(expert_i == TOP_K + k, rt_t[GATE_LANE + k:GATE_LANE + k + 1, :], stacked)
    cols = stacked.T

    def unpermute(n_rows):
        col_f = lax.broadcasted_iota(jnp.int32, (tm, n_rows), 1).astype(F32)
        sel = jnp.zeros((tm, n_rows), F32)
        for k in range(TOP_K):
            sel = jnp.where(col_f == cols[:, k:k + 1], cols[:, TOP_K + k:TOP_K + k + 1], sel)
        sel = sel.astype(BF16)
        words = stg_ref[slot, 0:n_rows, :]
        lo = pltpu.bitcast(lax.shift_left(words, 16), F32).astype(BF16)
        hi = pltpu.bitcast(words & hi_mask, F32).astype(BF16)
        moe = jnp.concatenate([_dot(sel, lo), _dot(sel, hi)], axis=1)
        x2 = x1_ref[0] + g2_ref[0] * moe
        ms = jnp.mean(x2 * x2, axis=-1, keepdims=True)
        out_ref[0] = x2 * lax.rsqrt(ms + RMS_EPS) * fnw_ref[...]

    used_rows = tot_ref[step] * SUBLANES
    short = STAGE_ROWS - STAGE_CHUNK
    pl.when(used_rows <= short)(functools.partial(unpermute, short))
    pl.when(used_rows > short)(functools.partial(unpermute, STAGE_ROWS))


def _combine(tables, x1, rt, stage_base, g2, fnw, ys):
    b, l, d = x1.shape
    tm = DISP_TM
    per_l = l // tm
    rt_per = rt.shape[2] // tm
    tok = lambda n: pl.BlockSpec((1, tm, n), lambda i, j, *_: (i, j, 0))
    grid_spec = pltpu.PrefetchScalarGridSpec(
        num_scalar_prefetch=len(tables),
        grid=(b, per_l),
        in_specs=[tok(d),
                  pl.BlockSpec((1, RT_ROWS, tm),
                               lambda i, j, *_: ((i * per_l + j) // rt_per, 0, (i * per_l + j) % rt_per)),
                  pl.BlockSpec((1, LANES, 1), lambda i, j, *_: (i * per_l + j, 0, 0)),
                  pl.BlockSpec((tm, tm), lambda i, j, *_: (0, 0)),
                  pl.BlockSpec((1, 1, d), lambda i, j, *_: (i, 0, 0)),
                  pl.BlockSpec(fnw.shape, lambda i, j, *_: (0, 0)),
                  pl.BlockSpec(memory_space=pl.ANY)],
        out_specs=tok(d),
        scratch_shapes=[pltpu.VMEM((2, STAGE_ROWS, PACK_W), jnp.int32),
                        pltpu.SemaphoreType.DMA((2,))],
    )
    return pl.pallas_call(
        _combine_kernel,
        grid_spec=grid_spec,
        out_shape=jax.ShapeDtypeStruct((b, l, d), F32),
        compiler_params=_cparams(("arbitrary", "arbitrary")),
    )(*tables, x1, rt, stage_base, _strict_upper(tm), g2, fnw, ys)


def _pool_constants():
    i = np.arange(POOL_TM)[:, None]
    j = np.arange(POOL_TM)[None, :]
    same_row = (i // GRID_W) == (j // GRID_W)
    mats, cnts = [], []
    for w in POOL_WINDOWS:
        band = same_row & (j - i >= -(w // 2)) & (j - i < w - w // 2)
        mats.append(band)
        cnts.append(np.broadcast_to(band.sum(axis=1, keepdims=True), (POOL_TM, POOL_GW)))
    return (jnp.asarray(np.stack(mats), BF16), jnp.asarray(np.stack(cnts), F32))


def _head_expand():
    e = np.zeros((LANES, SSD_W), np.float32)
    for h in range(SSD_HEADS):
        e[h, h * SSD_HEADDIM:(h + 1) * SSD_HEADDIM] = 1.0
    return jnp.asarray(e, BF16)


def _deinterleave_perm():
    n = 2 * LANES
    p = np.zeros((n, n), np.float32)
    for k in range(LANES):
        p[2 * k, k] = 1.0
        p[2 * k + 1, LANES + k] = 1.0
    return jnp.asarray(p, BF16)


def _strict_upper(n):
    return jnp.asarray(np.triu(np.ones((n, n), np.float32), 1), BF16)


def _pad_lanes(a, n):
    return jnp.pad(a, [(0, 0)] * (a.ndim - 1) + [(0, n - a.shape[-1])])


def kernel(x, c, ctx, c_ctx, w_mod, b_mod, norm1_w, norm2_w, w_in, conv_w, conv_b, dt_bias, a_log,
           d_skip, ssd_norm_w, pool_w, pool_scale, w_out, router_w, router_b, w1, b1, w2, b2,
           final_norm_w):
    depth = w_mod.shape[0]
    assert depth == 1, "single-layer problem"
    b, l, d = x.shape
    lc = ctx.shape[1]
    xbcdt = CONV_CH + 2 * SSD_HEADS

    mod_rows = 2 * SUBLANES
    cc = jnp.zeros((mod_rows, d), F32).at[0:b].set(c).at[b].set(c_ctx)
    mod = _modulation(cc, w_mod[0], b_mod[0])
    sh1, sc1, g1, sh2, sc2, g2 = [m.reshape(b, 1, d) for m in jnp.split(mod[0:b], 6, axis=-1)]
    csh1, csc1 = [jnp.broadcast_to(m.reshape(1, 1, d), (b, 1, d))
                  for m in jnp.split(mod[b:b + 1], 6, axis=-1)[0:2]]

    wi = w_in[0]
    wx = wi[:, 0:CONV_CH].astype(BF16)
    wd = jnp.concatenate([_pad_lanes(wi[:, CONV_CH:CONV_CH + SSD_HEADS], LANES),
                          _pad_lanes(wi[:, CONV_CH + SSD_HEADS:xbcdt], LANES)], axis=1).astype(BF16)
    wz = wi[:, xbcdt:xbcdt + SSD_W].astype(BF16)
    wp = wi[:, xbcdt + SSD_W:].astype(BF16)
    n1 = norm1_w[0].reshape(1, d)
    conv_w8 = jnp.pad(conv_w[0], ((0, SUBLANES - SSD_CONV), (0, 0)))
    conv_b1 = conv_b[0].reshape(1, CONV_CH)
    dtb = _pad_lanes(dt_bias[0], LANES).reshape(2, 1, LANES)
    alog = _pad_lanes(a_log[0], LANES).reshape(2, 1, LANES)
    dsk = jnp.repeat(d_skip[0], SSD_HEADDIM).reshape(1, SSD_W)
    snw = ssd_norm_w[0].reshape(1, SSD_W)
    expand = _head_expand()

    xbc_c, dt_c = _inproj(ctx, n1, csh1, csc1, (wx, wd), (BF16, F32), min(PROJ_TM, lc))
    zero_state = jnp.zeros((b, 2, SSD_GROUPS, SSD_STATE, GROUP_W), F32)
    _, ctx_states = _ssd(xbc_c, dt_c, None, conv_w8, conv_b1, dtb, alog, dsk, snw, expand, zero_state)

    xbc, dt, z, u_pool = _inproj(x, n1, sh1, sc1, (wx, wd, wz, wp), (BF16, F32, BF16, BF16), PROJ_TM)
    y_ssd, _ = _ssd(xbc, dt, z, conv_w8, conv_b1, dtb, alog, dsk, snw, expand, ctx_states)

    pool_a, pool_cnt = _pool_constants()
    pw = pool_w[0].astype(BF16)
    zero_gw = jnp.zeros((POOL_GW, POOL_GW), BF16)
    pw_pairs = jnp.stack([jnp.block([[pw[2 * gp], zero_gw], [zero_gw, pw[2 * gp + 1]]])
                          for gp in range(len(POOL_WINDOWS) // 2)])
    x1, h2, rt, tcnt = _mix(
        y_ssd, u_pool, x, pool_a, pool_cnt, pw_pairs, pool_scale[0].reshape(1, POOL_W),
        w_out[0].astype(BF16), g1, norm2_w[0].reshape(1, d), sh2, sc2,
        _pad_lanes(router_w[0], LANES), _pad_lanes(router_b[0].reshape(1, N_EXPERTS), LANES))

    i32 = jnp.int32
    t = b * l
    n_tiles = t // DISP_TM
    tc = tcnt.reshape(n_tiles, N_EXPERTS).astype(i32)
    counts = jnp.sum(tc, axis=0)
    run_start = jnp.cumsum(tc, axis=0) - tc
    padded = (counts + MOE_BLK - 1) // MOE_BLK * MOE_BLK
    pad_end = jnp.cumsum(padded)
    pad_start = pad_end - padded
    carried = run_start % SUBLANES
    span = carried + tc
    groups = (span + SUBLANES - 1) // SUBLANES
    full = span // SUBLANES
    stage_group = jnp.cumsum(groups, axis=1) - groups
    slot_group = (pad_start[None, :] + run_start - carried) // SUBLANES
    flush = full.at[n_tiles - 1].set(groups[n_tiles - 1])
    flat = lambda a: a.reshape(n_tiles * N_EXPERTS).astype(i32)
    stage_base = _pad_lanes((stage_group * SUBLANES + carried).astype(F32), LANES)
    stage_base = stage_base.reshape(n_tiles, LANES, 1)
    n_blocks = (t * TOP_K) // MOE_BLK + N_EXPERTS
    n_used = (pad_end[-1] // MOE_BLK).astype(i32).reshape(1)
    blk_start = jnp.minimum(jnp.arange(n_blocks, dtype=i32), n_used[0] - 1) * MOE_BLK
    block_exp = jnp.minimum(jnp.sum(blk_start[:, None] >= pad_end[None, :], axis=1),
                            N_EXPERTS - 1).astype(i32)
    written = (counts + SUBLANES - 1) // SUBLANES * SUBLANES
    fill_group = (jnp.concatenate([pad_start + written, pad_end[-1:]]) // SUBLANES).astype(i32)
    fill_groups = (jnp.concatenate([padded - written, jnp.zeros((1,), i32)]) // SUBLANES).astype(i32)

    xs = _dispatch((flat(stage_group), flat(full), flat(flush), flat(span % SUBLANES), flat(slot_group),
                    jnp.sum(flush, axis=1).astype(i32), jnp.sum(groups, axis=1).astype(i32)),
                   fill_group, fill_groups, h2.reshape(t, d), rt, stage_base,
                   n_blocks * MOE_BLK)
    dff2 = w1.shape[-1]
    b1p = jnp.concatenate(
        [b1[0].reshape(N_EXPERTS, dff2 // (2 * LANES), LANES, 2)[..., 0],
         b1[0].reshape(N_EXPERTS, dff2 // (2 * LANES), LANES, 2)[..., 1]], axis=-1
    ).reshape(N_EXPERTS, 1, dff2)
    ys = _experts(block_exp, n_used, xs, w1[0], b1p, w2[0], b2[0].reshape(N_EXPERTS, 1, d),
                  _deinterleave_perm(), n_blocks)
    return _combine((flat(stage_group), flat(groups), flat(slot_group),
                     jnp.sum(groups, axis=1).astype(i32)), x1, rt, stage_base, g2,
                    final_norm_w.reshape(1, d), ys)
```

```python
import functools

import numpy as np
import jax
import jax.numpy as jnp
from jax import lax
from jax.experimental import pallas as pl
from jax.experimental.pallas import tpu as pltpu

F32 = jnp.float32
BF16 = jnp.bfloat16

SSD_HEADDIM = 64
SSD_GROUPS = 4
SSD_HPG = 6
SSD_HEADS = SSD_GROUPS * SSD_HPG
SSD_STATE = 128
SSD_CONV = 5
SSD_CHUNK = 128
SSD_W = SSD_HEADS * SSD_HEADDIM
GROUP_W = SSD_HPG * SSD_HEADDIM
CONV_CH = SSD_W + 2 * SSD_GROUPS * SSD_STATE
POOL_WINDOWS = (2, 4, 8, 16)
POOL_GW = 128
POOL_W = POOL_GW * len(POOL_WINDOWS)
GRID_W = 64
N_EXPERTS = 32
TOP_K = 4
SWIGLU_ALPHA = 1.702
SWIGLU_LIMIT = 7.0
RMS_EPS = 1e-6
LOG2_E = 1.4426950408889634

LANES = 128
SUBLANES = 8
VMEM_LIMIT_BYTES = 56 * 1024 * 1024

MOD_TN = 1024
PROJ_TM = 1024
MIX_TM = 1024
POOL_TM = 256
MOE_BLK = 512
DISP_TM = 256
NEG_BIG = -1e30

GATE_LANE = 2 * TOP_K
RT_ROWS = 2 * SUBLANES
PACK_W = 512
XS_W = PACK_W
STAGE_CHUNK = 256
_STAGE_NEED = TOP_K * DISP_TM + N_EXPERTS * 2 * (SUBLANES - 1) + SUBLANES
STAGE_ROWS = -(-_STAGE_NEED // STAGE_CHUNK) * STAGE_CHUNK
RUN_BITS = (32, 16, 8, 4, 2, 1)
RUN_SMALL_BIT = 4
WAIT_BITS = (128, 64, 32, 16, 8, 4, 2, 1)


def _sigmoid(x):
    return 0.5 * jnp.tanh(0.5 * x) + 0.5


def _split2(a):
    hi = a.astype(BF16)
    mid = (a - hi.astype(F32)).astype(BF16)
    return hi, mid


def _split3(a):
    hi = a.astype(BF16)
    r = a - hi.astype(F32)
    mid = r.astype(BF16)
    lo = (r - mid.astype(F32)).astype(BF16)
    return hi, mid, lo


def _dot(a, b):
    return jnp.dot(a, b, preferred_element_type=F32)


def _dot_exact_rhs(a_f32, b_bf16):
    hi, mid = _split2(a_f32)
    return _dot(hi, b_bf16) + _dot(mid, b_bf16)


def _dot_hi(a_f32, b_f32):
    a0, a1, a2 = _split3(a_f32)
    b0, b1, b2 = _split3(b_f32)
    out = _dot(a0, b0)
    out = out + _dot(a0, b1) + _dot(a1, b0)
    out = out + _dot(a1, b1) + _dot(a0, b2) + _dot(a2, b0)
    return out


def _cparams(sem):
    return pltpu.CompilerParams(dimension_semantics=sem, vmem_limit_bytes=VMEM_LIMIT_BYTES)


def _mod_kernel(c_ref, w_ref, b_ref, o_ref):
    c = c_ref[...]
    s = c * _sigmoid(c)
    o_ref[...] = _dot_hi(s, w_ref[...]) + b_ref[...]


def _modulation(cc, w_mod, b_mod):
    rows, d = cc.shape
    n = w_mod.shape[1]
    return pl.pallas_call(
        _mod_kernel,
        grid=(n // MOD_TN,),
        in_specs=[pl.BlockSpec((rows, d), lambda j: (0, 0)),
                  pl.BlockSpec((d, MOD_TN), lambda j: (0, j)),
                  pl.BlockSpec((1, MOD_TN), lambda j: (0, j))],
        out_specs=pl.BlockSpec((rows, MOD_TN), lambda j: (0, j)),
        out_shape=jax.ShapeDtypeStruct((rows, n), F32),
        compiler_params=_cparams(("arbitrary",)),
    )(cc, w_mod, b_mod.reshape(1, n))


def _inproj_kernel(n_groups, x_ref, nw_ref, sh_ref, sc_ref, *refs):
    w_refs = refs[0:n_groups]
    out_refs = refs[n_groups:2 * n_groups]
    x = x_ref[0]
    ms = jnp.mean(x * x, axis=-1, keepdims=True)
    h = x * lax.rsqrt(ms + RMS_EPS) * nw_ref[...]
    hb = (h * (1.0 + sc_ref[0]) + sh_ref[0]).astype(BF16)
    for w_ref, out_ref in zip(w_refs, out_refs):
        out_ref[0] = _dot(hb, w_ref[...]).astype(out_ref.dtype)


def _inproj(x, norm_w, shift, scale, weights, out_dtypes, tm):
    b, l, d = x.shape
    full = lambda a: pl.BlockSpec(a.shape, lambda i, j: (0, 0))
    tok = lambda n: pl.BlockSpec((1, tm, n), lambda i, j: (i, j, 0))
    per_b = pl.BlockSpec((1, 1, d), lambda i, j: (i, 0, 0))
    return pl.pallas_call(
        functools.partial(_inproj_kernel, len(weights)),
        grid=(b, l // tm),
        in_specs=[tok(d), full(norm_w), per_b, per_b] + [full(w) for w in weights],
        out_specs=[tok(w.shape[1]) for w in weights],
        out_shape=[jax.ShapeDtypeStruct((b, l, w.shape[1]), dt)
                   for w, dt in zip(weights, out_dtypes)],
        compiler_params=_cparams(("arbitrary", "arbitrary")),
    )(x, norm_w, shift, scale, *weights)


def _ssd_kernel(nc, emit_y, *refs):
    if emit_y:
        (xbc_ref, prev_ref, next_ref, dt_ref, z_ref, cw_ref, cb_ref, dtb_ref, alog_ref, dsk_ref,
         nw_ref, exp_ref, init_ref, y_ref, fin_ref, act_ref, ybuf_ref, yf_ref, st_ref) = refs
        y_part = (z_ref, dsk_ref, nw_ref, y_ref, ybuf_ref, yf_ref)
    else:
        (xbc_ref, prev_ref, next_ref, dt_ref, cw_ref, cb_ref, dtb_ref, alog_ref,
         exp_ref, init_ref, fin_ref, act_ref, st_ref) = refs
        y_part = None
    ph = pl.program_id(1)
    c = pl.program_id(2)
    ci = c + ph * (nc - 1 - 2 * c)

    @pl.when(c == 0)
    def _():
        st_ref[...] = init_ref[0, ph]

    @pl.when(ph == 0)
    def _():
        _conv_silu(nc, ci, xbc_ref, prev_ref, next_ref, cw_ref, cb_ref, act_ref)

    _ssd_scan(nc, ph, c, ci, dt_ref, dtb_ref, alog_ref, exp_ref, act_ref, st_ref, fin_ref, y_part)


def _conv_silu(nc, ci, xbc_ref, prev_ref, next_ref, cw_ref, cb_ref, act_ref):
    ch = SSD_CHUNK
    has_prev = ci > 0
    has_next = ci < nc - 1
    row8 = lax.broadcasted_iota(jnp.int32, (SUBLANES, LANES), 0)
    half = SSD_CONV // 2
    for j in range(CONV_CH // LANES):
        cols = slice(j * LANES, (j + 1) * LANES)
        n_t = ch // SUBLANES
        tiles = [jnp.where(has_prev, prev_ref[0, SUBLANES:2 * SUBLANES, cols].astype(F32), 0.0)]
        for i in range(n_t // 2):
            both = xbc_ref[0, 2 * i * SUBLANES:2 * (i + 1) * SUBLANES, cols].astype(F32)
            tiles += [both[0:SUBLANES], both[SUBLANES:2 * SUBLANES]]
        tiles.append(jnp.where(has_next, next_ref[0, 0:SUBLANES, cols].astype(F32), 0.0))
        acc = [cb_ref[:, cols] + cw_ref[half:half + 1, cols] * tiles[i + 1] for i in range(n_t)]
        for s in range(1, half + 1):
            rot = [pltpu.roll(tl, s, axis=0) for tl in tiles[0:n_t + 1]]
            wk = cw_ref[half - s:half - s + 1, cols]
            for i in range(n_t):
                acc[i] = acc[i] + wk * jnp.where(row8 < s, rot[i], rot[i + 1])
            rot = [pltpu.roll(tl, SUBLANES - s, axis=0) for tl in tiles[1:n_t + 2]]
            wk = cw_ref[half + s:half + s + 1, cols]
            for i in range(n_t):
                acc[i] = acc[i] + wk * jnp.where(row8 >= SUBLANES - s, rot[i + 1], rot[i])
        silu = [0.5 * v * jnp.tanh(0.5 * v) + 0.5 * v for v in acc]
        act_ref[ci, :, cols] = jnp.concatenate(silu, axis=0).astype(BF16)


def _ssd_scan(nc, ph, c, ci, dt_ref, dtb_ref, alog_ref, exp_ref, act_ref, st_ref, fin_ref, y_part):
    ch = SSD_CHUNK
    emit_y = y_part is not None
    dtr = dt_ref[0] + dtb_ref[0]
    small = jnp.exp(-jnp.abs(dtr))
    one_plus = 1.0 + small
    log1p_small = jnp.where(one_plus == 1.0, small, jnp.log(one_plus) * (small / (one_plus - 1.0)))
    dtv = jnp.maximum(dtr, 0.0) + log1p_small
    a = dtv * (-jnp.exp(alog_ref[0]) * LOG2_E)
    row = lax.broadcasted_iota(jnp.int32, (ch, ch), 0)
    col = lax.broadcasted_iota(jnp.int32, (ch, ch), 1)
    tmask = (row - col) * (1 - 2 * ph) >= 0
    tri = jnp.where(tmask, 1.0, 0.0).astype(BF16)
    cs = _dot_exact_rhs_left(tri, a)
    tot = jnp.where(ph == 0, cs[ch - 1:ch, :], cs[0:1, :])
    cs_t = cs.T
    e_cs = jnp.exp2(cs)
    e_dec = jnp.exp2(tot - cs)
    e_tot = jnp.exp2(tot)
    expand = exp_ref[...]
    wdec_x = _dot((dtv * e_dec).astype(BF16), expand).astype(BF16)
    etot_x = _dot_exact_rhs(jnp.broadcast_to(e_tot, (SUBLANES, LANES)), expand)[0:1]
    if emit_y:
        z_ref, dsk_ref, nw_ref, y_ref, ybuf_ref, yf_ref = y_part
        ecs_x = _dot(e_cs.astype(BF16), expand)
        src_t = cs_t - jnp.log2(dtv).T

    lane = lax.broadcasted_iota(jnp.int32, (ch, LANES), 1)
    for g in range(SSD_GROUPS):
        gs = slice(g * GROUP_W, (g + 1) * GROUP_W)
        b_bf = act_ref[ci, :, SSD_W + g * SSD_STATE:SSD_W + (g + 1) * SSD_STATE]
        x_bf = act_ref[ci, :, gs]
        s_prev = st_ref[g]
        if emit_y:
            c_bf = act_ref[ci, :, SSD_W + (SSD_GROUPS + g) * SSD_STATE:
                           SSD_W + (SSD_GROUPS + g + 1) * SSD_STATE]
            cb = lax.dot_general(c_bf, b_bf, (((1,), (1,)), ((), ())), preferred_element_type=F32)
            y_off = _dot(c_bf, s_prev.astype(BF16)) * ecs_x[:, gs]
        x_dec = x_bf * wdec_x[:, gs]
        st_ref[g] = s_prev * etot_x[:, gs] + lax.dot_general(
            b_bf, x_dec, (((0,), (0,)), ((), ())), preferred_element_type=F32)
        if not emit_y:
            continue
        for q in range(SSD_HPG // 2):
            lmats = []
            for h in (g * SSD_HPG + 2 * q, g * SSD_HPG + 2 * q + 1):
                diff = cs[:, h:h + 1] - src_t[h:h + 1, :]
                dec = jnp.exp2(jnp.where(tmask, diff, NEG_BIG))
                lmats.append((dec * cb).astype(BF16))
            xp = x_bf[:, q * LANES:(q + 1) * LANES]
            zero = jnp.zeros_like(xp)
            rhs = jnp.concatenate([jnp.where(lane < SSD_HEADDIM, xp, zero),
                                   jnp.where(lane >= SSD_HEADDIM, xp, zero)], axis=0)
            y_diag = _dot(jnp.concatenate(lmats, axis=1), rhs)
            ps = slice(g * GROUP_W + q * LANES, g * GROUP_W + (q + 1) * LANES)
            ybuf_ref[:, ps] = y_diag + y_off[:, q * LANES:(q + 1) * LANES]

    if emit_y:
        @pl.when(ph == 0)
        def _():
            yf_ref[ci] = ybuf_ref[...].astype(BF16)

        @pl.when(ph == 1)
        def _():
            yt = (yf_ref[ci].astype(F32) + ybuf_ref[...]
                  + act_ref[ci, :, 0:SSD_W].astype(F32) * dsk_ref[...])
            hz = 0.5 * z_ref[0].astype(F32)
            gt = yt * (hz * jnp.tanh(hz) + hz)
            ms = jnp.mean(gt * gt, axis=-1, keepdims=True)
            y_ref[0] = (gt * lax.rsqrt(ms + RMS_EPS) * nw_ref[...]).astype(y_ref.dtype)

    @pl.when(c == nc - 1)
    def _():
        fin_ref[0, ph] = st_ref[...]


def _dot_exact_rhs_left(sel_bf16, a_f32):
    hi, mid, lo = _split3(a_f32)
    return _dot(sel_bf16, hi) + _dot(sel_bf16, mid) + _dot(sel_bf16, lo)


def _ssd(xbc, dt, z, conv_w8, conv_b, dt_bias, a_log, d_skip_x, norm_w, expand, init):
    emit_y = z is not None
    b, l, _ = xbc.shape
    ch = SSD_CHUNK
    nc = l // ch
    halo = 2 * SUBLANES
    per_ch = ch // halo

    def cidx(ph, c):
        return c + ph * (nc - 1 - 2 * c)

    def xidx(ph, c):
        return jnp.where(ph == 0, c, nc - 1)

    def out_idx(ph, c):
        return jnp.where(ph == 0, nc - 1, nc - 1 - c)

    full2 = lambda a: pl.BlockSpec(a.shape, lambda i, ph, c: (0, 0))
    st_spec = pl.BlockSpec((1, 2, SSD_GROUPS, SSD_STATE, GROUP_W), lambda i, ph, c: (i, 0, 0, 0, 0))
    per_phase = pl.BlockSpec((1, 1, LANES), lambda i, ph, c: (ph, 0, 0))
    y_spec = pl.BlockSpec((1, ch, SSD_W), lambda i, ph, c: (i, out_idx(ph, c), 0))
    operands = [(xbc, pl.BlockSpec((1, ch, CONV_CH), lambda i, ph, c: (i, xidx(ph, c), 0))),
                (xbc, pl.BlockSpec((1, halo, CONV_CH),
                                   lambda i, ph, c: (i, jnp.maximum(xidx(ph, c) * per_ch - 1, 0), 0))),
                (xbc, pl.BlockSpec((1, halo, CONV_CH),
                                   lambda i, ph, c: (i, jnp.minimum((xidx(ph, c) + 1) * per_ch,
                                                                    l // halo - 1), 0))),
                (dt, pl.BlockSpec((1, ch, LANES), lambda i, ph, c: (i, cidx(ph, c), ph)))]
    if emit_y:
        operands.append((z, y_spec))
    operands += [(conv_w8, full2(conv_w8)), (conv_b, full2(conv_b)),
                 (dt_bias, per_phase), (a_log, per_phase)]
    if emit_y:
        operands += [(d_skip_x, full2(d_skip_x)), (norm_w, full2(norm_w))]
    operands += [(expand, full2(expand)), (init, st_spec)]
    st_shape = jax.ShapeDtypeStruct((b, 2, SSD_GROUPS, SSD_STATE, GROUP_W), F32)
    scratch = [pltpu.VMEM((nc, ch, CONV_CH), BF16)]
    if emit_y:
        scratch += [pltpu.VMEM((ch, SSD_W), F32),
                    pltpu.VMEM((nc, ch, SSD_W), BF16)]
    scratch.append(pltpu.VMEM((SSD_GROUPS, SSD_STATE, GROUP_W), F32))
    outs = pl.pallas_call(
        functools.partial(_ssd_kernel, nc, emit_y),
        grid=(b, 2, nc),
        in_specs=[spec for _, spec in operands],
        out_specs=[y_spec, st_spec] if emit_y else [st_spec],
        out_shape=[jax.ShapeDtypeStruct((b, l, SSD_W), BF16), st_shape] if emit_y else [st_shape],
        scratch_shapes=scratch,
        compiler_params=_cparams(("arbitrary", "arbitrary", "arbitrary")),
    )(*[a for a, _ in operands])
    return outs if emit_y else (None, outs[0])


def _mix_kernel(y_ref, u_ref, x_ref, pa_ref, pcnt_ref, pw_ref, psc_ref, wo_ref, g1_ref,
                nw_ref, sh_ref, sc_ref, rw_ref, rb_ref,
                x1_ref, h_ref, rt_ref, tcnt_ref):
    tm = x_ref.shape[1]

    pooled = []
    for g in range(len(POOL_WINDOWS)):
        parts = []
        for r in range(tm // POOL_TM):
            u = u_ref[0, r * POOL_TM:(r + 1) * POOL_TM, g * POOL_GW:(g + 1) * POOL_GW]
            wsum = _dot(pa_ref[g], u)
            parts.append((wsum / pcnt_ref[g] - u.astype(F32)).astype(BF16))
        pooled.append(jnp.concatenate(parts, axis=0))
    mapped = []
    for gp in range(len(POOL_WINDOWS) // 2):
        pair = jnp.concatenate(pooled[2 * gp:2 * gp + 2], axis=1)
        mapped.append((_dot(pair, pw_ref[gp])
                       * psc_ref[:, 2 * gp * POOL_GW:(2 * gp + 2) * POOL_GW]).astype(BF16))
    y_pool = jnp.concatenate(mapped, axis=1)

    mix = _dot(y_ref[0], wo_ref[0:SSD_W, :]) + _dot(y_pool, wo_ref[SSD_W:SSD_W + POOL_W, :])
    x1 = x_ref[0] + g1_ref[0] * mix
    x1_ref[0] = x1

    ms = jnp.mean(x1 * x1, axis=-1, keepdims=True)
    h = x1 * lax.rsqrt(ms + RMS_EPS) * nw_ref[...]
    h = h * (1.0 + sc_ref[0]) + sh_ref[0]
    h_ref[0] = h

    h0, h1 = _split2(h)
    rw2 = jnp.concatenate(_split2(rw_ref[...]), axis=1)
    t0 = _dot(h0, rw2)
    t1 = _dot(h1, rw2)
    logits = (t0[:, 0:LANES] + t0[:, LANES:2 * LANES] + t1[:, 0:LANES] + t1[:, LANES:2 * LANES]
              + rb_ref[...])
    work = logits.T[0:N_EXPERTS, :]
    expert_f = lax.broadcasted_iota(jnp.int32, (N_EXPERTS, tm), 0).astype(F32)
    vals, idxs = [], []
    for _ in range(TOP_K):
        m = jnp.max(work, axis=0, keepdims=True)
        first_idx = jnp.min(jnp.where(work == m, expert_f, float(N_EXPERTS)), axis=0, keepdims=True)
        vals.append(m)
        idxs.append(first_idx)
        work = jnp.where(expert_f == first_idx, 2.0 * NEG_BIG, work)
    exps = [jnp.exp(v - vals[0]) for v in vals]
    denom = exps[0] + exps[1] + exps[2] + exps[3]

    rec_row = lax.broadcasted_iota(jnp.int32, (RT_ROWS, tm), 0)
    rec = jnp.zeros((RT_ROWS, tm), F32)
    onehot = jnp.zeros((N_EXPERTS, tm), F32)
    for k in range(TOP_K):
        rec = jnp.where(rec_row == k, idxs[k], rec)
        rec = jnp.where(rec_row == GATE_LANE + k, exps[k] / denom, rec)
        onehot = onehot + jnp.where(expert_f == idxs[k], 1.0, 0.0)
    rt_ref[0] = rec
    for r in range(tm // DISP_TM):
        tcnt_ref[0, r] = jnp.sum(onehot[:, r * DISP_TM:(r + 1) * DISP_TM], axis=1, keepdims=True)


def _mix(y, u, x, pool_a, pool_cnt, pool_w, pool_scale, w_out, g1, norm_w, shift, scale,
         router_w, router_b):
    b, l, d = x.shape
    tm = MIX_TM
    per_l = l // tm
    tok = lambda n: pl.BlockSpec((1, tm, n), lambda i, j: (i, j, 0))
    per_b = pl.BlockSpec((1, 1, d), lambda i, j: (i, 0, 0))
    full = lambda a: pl.BlockSpec(a.shape, lambda i, j: (0,) * a.ndim)
    return pl.pallas_call(
        _mix_kernel,
        grid=(b, per_l),
        in_specs=[tok(SSD_W), tok(POOL_W), tok(d), full(pool_a), full(pool_cnt), full(pool_w),
                  full(pool_scale), full(w_out), per_b, full(norm_w), per_b, per_b,
                  full(router_w), full(router_b)],
        out_specs=[tok(d), tok(d),
                   pl.BlockSpec((1, RT_ROWS, tm), lambda i, j: (i * per_l + j, 0, 0)),
                   pl.BlockSpec((1, tm // DISP_TM, N_EXPERTS, 1), lambda i, j: (i * per_l + j, 0, 0, 0))],
        out_shape=[jax.ShapeDtypeStruct((b, l, d), F32),
                   jax.ShapeDtypeStruct((b, l, d), F32),
                   jax.ShapeDtypeStruct((b * per_l, RT_ROWS, tm), F32),
                   jax.ShapeDtypeStruct((b * per_l, tm // DISP_TM, N_EXPERTS, 1), F32)],
        compiler_params=_cparams(("arbitrary", "arbitrary")),
    )(y, u, x, pool_a, pool_cnt, pool_w, pool_scale, w_out, g1, norm_w, shift, scale,
      router_w, router_b)


def _group_copies(n_groups, src_ref, src_group, dst_ref, dst_group, sem, wait):
    def pieces(bits):
        for bit in bits:
            @pl.when((n_groups & bit) != 0)
            def _():
                done = n_groups & ~(2 * bit - 1)
                src = src_ref.at[pl.ds(pl.multiple_of((src_group + done) * SUBLANES, SUBLANES),
                                       bit * SUBLANES)]
                dst = dst_ref.at[pl.ds(pl.multiple_of((dst_group + done) * SUBLANES, SUBLANES),
                                       bit * SUBLANES)]
                cp = pltpu.make_async_copy(src, dst, sem)
                cp.wait() if wait else cp.start()

    split = RUN_BITS.index(RUN_SMALL_BIT)
    pl.when(n_groups >= 2 * RUN_SMALL_BIT)(lambda: pieces(RUN_BITS[:split]))
    pieces(RUN_BITS[split:])


def _wait_groups(n_groups, src_ref, dst_ref, sem):
    for bit in WAIT_BITS:
        @pl.when((n_groups & bit) != 0)
        def _():
            rows = bit * SUBLANES
            pltpu.make_async_copy(src_ref.at[pl.ds(0, rows)], dst_ref.at[pl.ds(0, rows)], sem).wait()


def _stage_positions(rt_t, base_col, upper):
    tm = rt_t.shape[1]
    expert_f = lax.broadcasted_iota(jnp.int32, (LANES, tm), 0).astype(F32)
    hots = [expert_f == rt_t[k:k + 1, :] for k in range(TOP_K)]
    onehot = jnp.zeros((LANES, tm), F32)
    for hot in hots:
        onehot = onehot + jnp.where(hot, 1.0, 0.0)
    pos = _dot(onehot.astype(BF16), upper) + base_col
    return [jnp.sum(jnp.where(hot, pos, 0.0), axis=0, keepdims=True) for hot in hots]


def _dispatch_kernel(o_ref, f_ref, fl_ref, rem_ref, hg_ref, tot_ref, used_ref, fs_ref, flen_ref,
                     h_ref, rt_ref, sb_ref, tri_ref, xs_ref,
                     stg2_ref, tails_ref, zero_ref, sem2, fill_sem):
    i = pl.program_id(0)
    tm = h_ref.shape[0]
    n_slots = xs_ref.shape[0]
    base = i * N_EXPERTS
    hi_mask = jnp.int32(-65536)
    slot = i % 2
    stg_ref = stg2_ref.at[slot]
    sem = sem2.at[slot]

    @pl.when(i == 0)
    def _():
        tails_ref[...] = jnp.zeros_like(tails_ref)
        stg2_ref[...] = jnp.zeros_like(stg2_ref)

    pos = _stage_positions(rt_ref[0], sb_ref[0], tri_ref[...])

    hb = h_ref[...].astype(BF16)
    half = PACK_W
    used_rows = used_ref[i] * SUBLANES

    def stage_chunk(r_lo):
        row_f = (lax.broadcasted_iota(jnp.int32, (STAGE_CHUNK, tm), 0) + r_lo).astype(F32)
        sel = jnp.zeros((STAGE_CHUNK, tm), F32)
        for k in range(TOP_K):
            sel = jnp.where(row_f == pos[k], 1.0, sel)
        moved = _dot(sel.astype(BF16), hb)
        lo = pltpu.bitcast(moved[:, 0:half], jnp.int32)
        hi = pltpu.bitcast(moved[:, half:2 * half], jnp.int32)
        stg_ref[r_lo:r_lo + STAGE_CHUNK, :] = lax.shift_right_logical(lo, 16) | (hi & hi_mask)

    always = TOP_K * tm // STAGE_CHUNK
    for rc in range(STAGE_ROWS // STAGE_CHUNK):
        if rc < always:
            stage_chunk(rc * STAGE_CHUNK)
        else:
            pl.when(rc * STAGE_CHUNK < used_rows)(functools.partial(stage_chunk, rc * STAGE_CHUNK))

    def finish_run(e, cr):
        stage_group = o_ref[base + e]
        first = pl.multiple_of(stage_group * SUBLANES, SUBLANES)
        stg_ref[pl.ds(first, SUBLANES), :] = stg_ref[pl.ds(first, SUBLANES), :] | tails_ref[e]
        part = pl.multiple_of((stage_group + f_ref[base + e]) * SUBLANES, SUBLANES)
        tails_ref[e] = jnp.where(rem_ref[base + e] > 0, stg_ref[pl.ds(part, SUBLANES), :], 0)
        _group_copies(fl_ref[base + e], stg_ref, stage_group, xs_ref, hg_ref[base + e], sem, False)
        return cr

    lax.fori_loop(0, N_EXPERTS, finish_run, 0)

    @pl.when(i == pl.num_programs(0) - 1)
    def _():
        zero_ref[...] = jnp.zeros_like(zero_ref)
        tail_group = fs_ref[N_EXPERTS]
        n_tail = (n_slots // SUBLANES - tail_group) // (MOE_BLK // SUBLANES)

        def tail_copy(j, wait):
            off = pl.multiple_of(tail_group * SUBLANES + j * MOE_BLK, MOE_BLK)
            cp = pltpu.make_async_copy(zero_ref, xs_ref.at[pl.ds(off, MOE_BLK)], fill_sem)
            cp.wait() if wait else cp.start()

        for wait in (False, True):
            def pad_body(e, cr):
                _group_copies(flen_ref[e], zero_ref, 0, xs_ref, fs_ref[e], fill_sem, wait)
                return cr
            lax.fori_loop(0, N_EXPERTS, pad_body, 0)
            lax.fori_loop(0, n_tail, lambda j, cr: (tail_copy(j, wait), cr)[1], 0)

    @pl.when(i > 0)
    def _():
        _wait_groups(tot_ref[jnp.maximum(i - 1, 0)], stg2_ref.at[1 - slot], xs_ref, sem2.at[1 - slot])

    @pl.when(i == pl.num_programs(0) - 1)
    def _():
        _wait_groups(tot_ref[i], stg_ref, xs_ref, sem)


def _dispatch(tables, fill_groups, fill_len_groups, h, rt, stage_base, n_slots):
    t, d = h.shape
    tm = DISP_TM
    rt_per = rt.shape[2] // tm
    grid_spec = pltpu.PrefetchScalarGridSpec(
        num_scalar_prefetch=len(tables) + 2,
        grid=(t // tm,),
        in_specs=[pl.BlockSpec((tm, d), lambda i, *_: (i, 0)),
                  pl.BlockSpec((1, RT_ROWS, tm), lambda i, *_: (i // rt_per, 0, i % rt_per)),
                  pl.BlockSpec((1, LANES, 1), lambda i, *_: (i, 0, 0)),
                  pl.BlockSpec((tm, tm), lambda i, *_: (0, 0))],
        out_specs=pl.BlockSpec(memory_space=pl.ANY),
        scratch_shapes=[pltpu.VMEM((2, STAGE_ROWS, XS_W), jnp.int32),
                        pltpu.VMEM((N_EXPERTS, SUBLANES, XS_W), jnp.int32),
                        pltpu.VMEM((MOE_BLK, XS_W), jnp.int32),
                        pltpu.SemaphoreType.DMA((2,)), pltpu.SemaphoreType.DMA(())],
    )
    return pl.pallas_call(
        _dispatch_kernel,
        grid_spec=grid_spec,
        out_shape=jax.ShapeDtypeStruct((n_slots, XS_W), jnp.int32),
        compiler_params=_cparams(("arbitrary",)),
    )(*tables, fill_groups, fill_len_groups, h, rt, stage_base, _strict_upper(tm))


def _expert_kernel(be_ref, nused_ref, xs_ref, w1_ref, b1_ref, w2_ref, b2_ref, perm_ref, y_ref,
                   w1s_ref, w2s_ref):
    i = pl.program_id(0)
    prev = be_ref[jnp.maximum(i - 1, 0)]
    changed = (i == 0) | (be_ref[i] != prev)
    dff2 = w1_ref.shape[2]
    tile = 2 * LANES

    @pl.when(changed & (i < nused_ref[0]))
    def _():
        for j in range(dff2 // tile):
            wj = w1_ref[0, :, j * tile:(j + 1) * tile].astype(BF16)
            w1s_ref[:, j * tile:(j + 1) * tile] = _dot(wj, perm_ref[...]).astype(BF16)
        w2s_ref[...] = w2_ref[0].astype(BF16)

    @pl.when(i < nused_ref[0])
    def _():
        hi_mask = jnp.int32(-65536)
        words = xs_ref[...]
        x = jnp.concatenate(
            [pltpu.bitcast(lax.shift_left(words, 16), F32).astype(BF16),
             pltpu.bitcast(words & hi_mask, F32).astype(BF16)], axis=1)
        acts = []
        for j in range(dff2 // tile):
            hb = _dot(x, w1s_ref[:, j * tile:(j + 1) * tile]) + b1_ref[0, :, j * tile:(j + 1) * tile]
            gp = jnp.minimum(hb[:, 0:LANES], SWIGLU_LIMIT)
            up = jnp.clip(hb[:, LANES:tile], -SWIGLU_LIMIT, SWIGLU_LIMIT)
            acts.append((gp * _sigmoid(SWIGLU_ALPHA * gp) * (up + 1.0)).astype(BF16))
        act = jnp.concatenate(acts, axis=1)
        y = _dot(act, w2s_ref[...]) + b2_ref[0]
        lo = pltpu.bitcast(y[:, 0:PACK_W].astype(BF16).astype(F32), jnp.int32)
        hi = pltpu.bitcast(y[:, PACK_W:2 * PACK_W].astype(BF16).astype(F32), jnp.int32)
        y_ref[...] = lax.shift_right_logical(lo, 16) | (hi & hi_mask)

    @pl.when(i >= nused_ref[0])
    def _():
        y_ref[...] = jnp.zeros_like(y_ref)


def _experts(block_exp, n_used, xs, w1, b1p, w2, b2, perm, n_blocks):
    blk = MOE_BLK
    d = w1.shape[1]
    dff2 = w1.shape[2]
    dff = w2.shape[1]

    def x_idx(i, be, nu):
        return (jnp.minimum(i, nu[0] - 1), 0)

    grid_spec = pltpu.PrefetchScalarGridSpec(
        num_scalar_prefetch=2,
        grid=(n_blocks,),
        in_specs=[pl.BlockSpec((blk, XS_W), x_idx),
                  pl.BlockSpec((1, d, dff2), lambda i, be, nu: (be[i], 0, 0)),
                  pl.BlockSpec((1, 1, dff2), lambda i, be, nu: (be[i], 0, 0)),
                  pl.BlockSpec((1, dff, d), lambda i, be, nu: (be[i], 0, 0)),
                  pl.BlockSpec((1, 1, d), lambda i, be, nu: (be[i], 0, 0)),
                  pl.BlockSpec(perm.shape, lambda i, be, nu: (0, 0))],
        out_specs=pl.BlockSpec((blk, PACK_W), lambda i, be, nu: (i, 0)),
        scratch_shapes=[pltpu.VMEM((d, dff2), BF16), pltpu.VMEM((dff, d), BF16)],
    )
    return pl.pallas_call(
        _expert_kernel,
        grid_spec=grid_spec,
        out_shape=jax.ShapeDtypeStruct((n_blocks * blk, PACK_W), jnp.int32),
        compiler_params=_cparams(("arbitrary",)),
    )(block_exp, n_used, xs, w1, b1p, w2, b2, perm)


def _combine_kernel(o_ref, g_ref, hg_ref, tot_ref, x1_ref, rt_ref, sb_ref, tri_ref, g2_ref, fnw_ref,
                    ys_ref, out_ref, stg_ref, sem):
    tm = x1_ref.shape[1]
    step = pl.program_id(0) * pl.num_programs(1) + pl.program_id(1)
    n_steps = pl.num_programs(0) * pl.num_programs(1)
    slot = step % 2
    hi_mask = jnp.int32(-65536)

    def fetch(tile, into):
        def body(e, cr):
            _group_copies(g_ref[tile * N_EXPERTS + e], ys_ref, hg_ref[tile * N_EXPERTS + e],
                          stg_ref.at[into], o_ref[tile * N_EXPERTS + e], sem.at[into], False)
            return cr
        lax.fori_loop(0, N_EXPERTS, body, 0)

    @pl.when(step == 0)
    def _():
        stg_ref[...] = jnp.zeros_like(stg_ref)
        fetch(0, 0)

    @pl.when(step + 1 < n_steps)
    def _():
        fetch(step + 1, 1 - slot)

    _wait_groups(tot_ref[step], ys_ref, stg_ref.at[slot], sem.at[slot])

    rt_t = rt_ref[0]
    pos_rows = _stage_positions(rt_t, sb_ref[0], tri_ref[...])
    expert_i = lax.broadcasted_iota(jnp.int32, (LANES, tm), 0)
    stacked = jnp.zeros((LANES, tm), F32)
    for k in range(TOP_K):
        stacked = jnp.where(expert_i == k, pos_rows[k], stacked)
        stacked = jnp.where(expert_i == TOP_K + k, rt_t[GATE_LANE + k:GATE_LANE + k + 1, :], stacked)
    cols = stacked.T

    def unpermute(n_rows):
        col_f = lax.broadcasted_iota(jnp.int32, (tm, n_rows), 1).astype(F32)
        sel = jnp.zeros((tm, n_rows), F32)
        for k in range(TOP_K):
            sel = jnp.where(col_f == cols[:, k:k + 1], cols[:, TOP_K + k:TOP_K + k + 1], sel)
        sel = sel.astype(BF16)
        words = stg_ref[slot, 0:n_rows, :]
        lo = pltpu.bitcast(lax.shift_left(words, 16), F32).astype(BF16)
        hi = pltpu.bitcast(words & hi_mask, F32).astype(BF16)
        moe = jnp.concatenate([_dot(sel, lo), _dot(sel, hi)], axis=1)
        x2 = x1_ref[0] + g2_ref[0] * moe
        ms = jnp.mean(x2 * x2, axis=-1, keepdims=True)
        out_ref[0] = x2 * lax.rsqrt(ms + RMS_EPS) * fnw_ref[...]

    used_rows = tot_ref[step] * SUBLANES
    short = STAGE_ROWS - STAGE_CHUNK
    pl.when(used_rows <= short)(functools.partial(unpermute, short))
    pl.when(used_rows > short)(functools.partial(unpermute, STAGE_ROWS))


def _combine(tables, x1, rt, stage_base, g2, fnw, ys):
    b, l, d = x1.shape
    tm = DISP_TM
    per_l = l // tm
    rt_per = rt.shape[2] // tm
    tok = lambda n: pl.BlockSpec((1, tm, n), lambda i, j, *_: (i, j, 0))
    grid_spec = pltpu.PrefetchScalarGridSpec(
        num_scalar_prefetch=len(tables),
        grid=(b, per_l),
        in_specs=[tok(d),
                  pl.BlockSpec((1, RT_ROWS, tm),
                               lambda i, j, *_: ((i * per_l + j) // rt_per, 0, (i * per_l + j) % rt_per)),
                  pl.BlockSpec((1, LANES, 1), lambda i, j, *_: (i * per_l + j, 0, 0)),
                  pl.BlockSpec((tm, tm), lambda i, j, *_: (0, 0)),
                  pl.BlockSpec((1, 1, d), lambda i, j, *_: (i, 0, 0)),
                  pl.BlockSpec(fnw.shape, lambda i, j, *_: (0, 0)),
                  pl.BlockSpec(memory_space=pl.ANY)],
        out_specs=tok(d),
        scratch_shapes=[pltpu.VMEM((2, STAGE_ROWS, PACK_W), jnp.int32),
                        pltpu.SemaphoreType.DMA((2,))],
    )
    return pl.pallas_call(
        _combine_kernel,
        grid_spec=grid_spec,
        out_shape=jax.ShapeDtypeStruct((b, l, d), F32),
        compiler_params=_cparams(("arbitrary", "arbitrary")),
    )(*tables, x1, rt, stage_base, _strict_upper(tm), g2, fnw, ys)


def _pool_constants():
    i = np.arange(POOL_TM)[:, None]
    j = np.arange(POOL_TM)[None, :]
    same_row = (i // GRID_W) == (j // GRID_W)
    mats, cnts = [], []
    for w in POOL_WINDOWS:
        band = same_row & (j - i >= -(w // 2)) & (j - i < w - w // 2)
        mats.append(band)
        cnts.append(np.broadcast_to(band.sum(axis=1, keepdims=True), (POOL_TM, POOL_GW)))
    return (jnp.asarray(np.stack(mats), BF16), jnp.asarray(np.stack(cnts), F32))


def _head_expand():
    e = np.zeros((LANES, SSD_W), np.float32)
    for h in range(SSD_HEADS):
        e[h, h * SSD_HEADDIM:(h + 1) * SSD_HEADDIM] = 1.0
    return jnp.asarray(e, BF16)


def _deinterleave_perm():
    n = 2 * LANES
    p = np.zeros((n, n), np.float32)
    for k in range(LANES):
        p[2 * k, k] = 1.0
        p[2 * k + 1, LANES + k] = 1.0
    return jnp.asarray(p, BF16)


def _strict_upper(n):
    return jnp.asarray(np.triu(np.ones((n, n), np.float32), 1), BF16)


def _pad_lanes(a, n):
    return jnp.pad(a, [(0, 0)] * (a.ndim - 1) + [(0, n - a.shape[-1])])


def kernel(x, c, ctx, c_ctx, w_mod, b_mod, norm1_w, norm2_w, w_in, conv_w, conv_b, dt_bias, a_log,
           d_skip, ssd_norm_w, pool_w, pool_scale, w_out, router_w, router_b, w1, b1, w2, b2,
           final_norm_w):
    depth = w_mod.shape[0]
    assert depth == 1, "single-layer problem"
    b, l, d = x.shape
    lc = ctx.shape[1]
    xbcdt = CONV_CH + 2 * SSD_HEADS

    mod_rows = 2 * SUBLANES
    cc = jnp.zeros((mod_rows, d), F32).at[0:b].set(c).at[b].set(c_ctx)
    mod = _modulation(cc, w_mod[0], b_mod[0])
    sh1, sc1, g1, sh2, sc2, g2 = [m.reshape(b, 1, d) for m in jnp.split(mod[0:b], 6, axis=-1)]
    csh1, csc1 = [jnp.broadcast_to(m.reshape(1, 1, d), (b, 1, d))
                  for m in jnp.split(mod[b:b + 1], 6, axis=-1)[0:2]]

    wi = w_in[0]
    wx = wi[:, 0:CONV_CH].astype(BF16)
    wd = jnp.concatenate([_pad_lanes(wi[:, CONV_CH:CONV_CH + SSD_HEADS], LANES),
                          _pad_lanes(wi[:, CONV_CH + SSD_HEADS:xbcdt], LANES)], axis=1).astype(BF16)
    wz = wi[:, xbcdt:xbcdt + SSD_W].astype(BF16)
    wp = wi[:, xbcdt + SSD_W:].astype(BF16)
    n1 = norm1_w[0].reshape(1, d)
    conv_w8 = jnp.pad(conv_w[0], ((0, SUBLANES - SSD_CONV), (0, 0)))
    conv_b1 = conv_b[0].reshape(1, CONV_CH)
    dtb = _pad_lanes(dt_bias[0], LANES).reshape(2, 1, LANES)
    alog = _pad_lanes(a_log[0], LANES).reshape(2, 1, LANES)
    dsk = jnp.repeat(d_skip[0], SSD_HEADDIM).reshape(1, SSD_W)
    snw = ssd_norm_w[0].reshape(1, SSD_W)
    expand = _head_expand()

    xbc_c, dt_c = _inproj(ctx, n1, csh1, csc1, (wx, wd), (BF16, F32), min(PROJ_TM, lc))
    zero_state = jnp.zeros((b, 2, SSD_GROUPS, SSD_STATE, GROUP_W), F32)
    _, ctx_states = _ssd(xbc_c, dt_c, None, conv_w8, conv_b1, dtb, alog, dsk, snw, expand, zero_state)

    xbc, dt, z, u_pool = _inproj(x, n1, sh1, sc1, (wx, wd, wz, wp), (BF16, F32, BF16, BF16), PROJ_TM)
    y_ssd, _ = _ssd(xbc, dt, z, conv_w8, conv_b1, dtb, alog, dsk, snw, expand, ctx_states)

    pool_a, pool_cnt = _pool_constants()
    pw = pool_w[0].astype(BF16)
    zero_gw = jnp.zeros((POOL_GW, POOL_GW), BF16)
    pw_pairs = jnp.stack([jnp.block([[pw[2 * gp], zero_gw], [zero_gw, pw[2 * gp + 1]]])
                          for gp in range(len(POOL_WINDOWS) // 2)])
    x1, h2, rt, tcnt = _mix(
        y_ssd, u_pool, x, pool_a, pool_cnt, pw_pairs, pool_scale[0].reshape(1, POOL_W),
        w_out[0].astype(BF16), g1, norm2_w[0].reshape(1, d), sh2, sc2,
        _pad_lanes(router_w[0], LANES), _pad_lanes(router_b[0].reshape(1, N_EXPERTS), LANES))

    i32 = jnp.int32
    t = b * l
    n_tiles = t // DISP_TM
    tc = tcnt.reshape(n_tiles, N_EXPERTS).astype(i32)
    counts = jnp.sum(tc, axis=0)
    run_start = jnp.cumsum(tc, axis=0) - tc
    padded = (counts + MOE_BLK - 1) // MOE_BLK * MOE_BLK
    pad_end = jnp.cumsum(padded)
    pad_start = pad_end - padded
    carried = run_start % SUBLANES
    span = carried + tc
    groups = (span + SUBLANES - 1) // SUBLANES
    full = span // SUBLANES
    stage_group = jnp.cumsum(groups, axis=1) - groups
    slot_group = (pad_start[None, :] + run_start - carried) // SUBLANES
    flush = full.at[n_tiles - 1].set(groups[n_tiles - 1])
    flat = lambda a: a.reshape(n_tiles * N_EXPERTS).astype(i32)
    stage_base = _pad_lanes((stage_group * SUBLANES + carried).astype(F32), LANES)
    stage_base = stage_base.reshape(n_tiles, LANES, 1)
    n_blocks = (t * TOP_K) // MOE_BLK + N_EXPERTS
    n_used = (pad_end[-1] // MOE_BLK).astype(i32).reshape(1)
    blk_start = jnp.minimum(jnp.arange(n_blocks, dtype=i32), n_used[0] - 1) * MOE_BLK
    block_exp = jnp.minimum(jnp.sum(blk_start[:, None] >= pad_end[None, :], axis=1),
                            N_EXPERTS - 1).astype(i32)
    written = (counts + SUBLANES - 1) // SUBLANES * SUBLANES
    fill_group = (jnp.concatenate([pad_start + written, pad_end[-1:]]) // SUBLANES).astype(i32)
    fill_groups = (jnp.concatenate([padded - written, jnp.zeros((1,), i32)]) // SUBLANES).astype(i32)

    xs = _dispatch((flat(stage_group), flat(full), flat(flush), flat(span % SUBLANES), flat(slot_group),
                    jnp.sum(flush, axis=1).astype(i32), jnp.sum(groups, axis=1).astype(i32)),
                   fill_group, fill_groups, h2.reshape(t, d), rt, stage_base,
                   n_blocks * MOE_BLK)
    dff2 = w1.shape[-1]
    b1p = jnp.concatenate(
        [b1[0].reshape(N_EXPERTS, dff2 // (2 * LANES), LANES, 2)[..., 0],
         b1[0].reshape(N_EXPERTS, dff2 // (2 * LANES), LANES, 2)[..., 1]], axis=-1
    ).reshape(N_EXPERTS, 1, dff2)
    ys = _experts(block_exp, n_used, xs, w1[0], b1p, w2[0], b2[0].reshape(N_EXPERTS, 1, d),
                  _deinterleave_perm(), n_blocks)
    return _combine((flat(stage_group), flat(groups), flat(slot_group),
                     jnp.sum(groups, axis=1).astype(i32)), x1, rt, stage_base, g2,
                    final_norm_w.reshape(1, d), ys)
```
